```python
import math
import jax
import jax.numpy as jnp
from jax import lax
import numpy as np

D_MODEL = 1024
BATCH = 16
SEQ = 4096
DEPTH = 1
DEC_BATCH = 128
DEC_SEQ = 8
PAST_LEN = 8192
PAGE_SIZE = 128

A_HEADS = 8
A_HEAD_DIM = 64
A_WIDTH = A_HEADS * A_HEAD_DIM
DILATED_PATTERNS = ((128, 1), (512, 4), (2048, 16))
MAX_WINDOW = max(w for w, _ in DILATED_PATTERNS)
ATTN_SCALE = A_HEAD_DIM ** -0.5
B_HEADS = 4
B_KEY_DIM = 128
B_VAL_DIM = 128
B_WIDTH = B_HEADS * B_VAL_DIM
CONV_WIDTH = 4
CONV_DIM = B_HEADS * (2 * B_KEY_DIM + B_VAL_DIM)
CHUNK = 64
MIX_WIDTH = A_WIDTH + B_WIDTH
IN_SPLITS = (A_WIDTH, 2 * A_WIDTH, 3 * A_WIDTH, 3 * A_WIDTH + CONV_DIM,
             3 * A_WIDTH + CONV_DIM + B_WIDTH, 3 * A_WIDTH + CONV_DIM + B_WIDTH + B_HEADS)
N_IN = 3 * A_WIDTH + CONV_DIM + B_WIDTH + 2 * B_HEADS
N_GROUPS = 4
EXPERTS_PER_GROUP = 4
N_EXPERTS = N_GROUPS * EXPERTS_PER_GROUP
TOP_K = 2
D_EXPERT = 512
RMS_EPS = 1e-6
NEG_INF = -1e30

kernel_name = 'dilated_attn_gdn_hier_moe_step'


def _rmsnorm(x, w):
    xf = x.astype(jnp.float32)
    y = xf * lax.rsqrt(jnp.mean(xf * xf, -1, keepdims=True) + RMS_EPS)
    return (y * w.astype(jnp.float32)).astype(x.dtype)


def _l2norm(x):
    return x * lax.rsqrt(jnp.sum(x * x, -1, keepdims=True) + RMS_EPS)


def _in_proj(x, norm1_w, w_in):
    p = _rmsnorm(x, norm1_w) @ w_in
    return jnp.split(p, IN_SPLITS, axis=-1)


def _a_heads(aq, ak, av, qnorm_w, knorm_w):
    b, s, _ = aq.shape
    r = lambda t: t.reshape(b, s, A_HEADS, A_HEAD_DIM)
    return _rmsnorm(r(aq), qnorm_w), _rmsnorm(r(ak), knorm_w), r(av)


def _attend(scores, mask, v, eq):
    s = jnp.where(mask, scores, NEG_INF)
    m = jnp.max(s, -1, keepdims=True)
    p = jnp.exp(s - m)
    l = jnp.sum(p, -1)
    o = jnp.einsum(eq, p, v.astype(jnp.float32)) / l[..., None]
    return o, m[..., 0] + jnp.log(l)


def _dilated_prompt(q, k, v, window, dilation):
    b, s, h, dh = q.shape
    L = s // dilation
    wc = window // dilation
    bd = b * dilation
    fold = lambda t: t.reshape(b, L, dilation, h, dh).transpose(0, 2, 1, 3, 4).reshape(bd, L, h, dh)
    nb = -(-L // wc)
    pad = nb * wc - L
    blk = lambda t: jnp.pad(fold(t), ((0, 0), (0, pad), (0, 0), (0, 0))).reshape(bd, nb, wc, h, dh)
    qb, kb, vb = blk(q), blk(k), blk(v)
    prev = lambda t: jnp.concatenate([jnp.zeros_like(t[:, :1]), t[:, :-1]], axis=1)
    kk = jnp.concatenate([prev(kb), kb], axis=2)
    vv = jnp.concatenate([prev(vb), vb], axis=2)
    scores = jnp.einsum('bnqhd,bnkhd->bnhqk', qb, kk).astype(jnp.float32) * ATTN_SCALE
    ki = jnp.arange(2 * wc)
    dist = (jnp.arange(wc) + wc)[:, None] - ki[None, :]
    band = (dist >= 0) & (dist <= wc)
    key_ok = (jnp.arange(nb)[:, None] > 0) | (ki[None, :] >= wc)
    mask = band[None] & key_ok[:, None, :]
    o, lse = _attend(scores, mask[None, :, None], vv, 'bnhqk,bnkhd->bnhqd')
    o = o.transpose(0, 1, 3, 2, 4).reshape(bd, nb * wc, h, dh)[:, :L]
    lse = lse.transpose(0, 1, 3, 2).reshape(bd, nb * wc, h)[:, :L]
    o = o.reshape(b, dilation, L, h, dh).transpose(0, 2, 1, 3, 4).reshape(b, s, h, dh)
    lse = lse.reshape(b, dilation, L, h).transpose(0, 2, 1, 3).reshape(b, s, h)
    return o, lse


def _dilated_sample(q, k_all, v_all, window, dilation, n_buf):
    t = q.shape[1]
    j = jnp.arange(window // dilation + 1)
    idx = n_buf + jnp.arange(t)[:, None] - j[None, :] * dilation
    valid = idx >= 0
    idx = jnp.maximum(idx, 0)
    kg = k_all[:, idx]
    vg = v_all[:, idx]
    scores = jnp.einsum('bthd,btjhd->bthj', q, kg).astype(jnp.float32) * ATTN_SCALE
    return _attend(scores, valid[None, :, None, :], vg, 'bthj,btjhd->bthd')


def _combine(parts):
    o = jnp.stack([p[0] for p in parts], 0)
    lse = jnp.stack([p[1] for p in parts], 0)
    w = jax.nn.softmax(lse, axis=0)
    out = jnp.sum(w[..., None] * o, axis=0)
    return out.reshape(out.shape[0], out.shape[1], A_WIDTH)


def _short_conv(x, state, conv_w):
    s = x.shape[1]
    xp = jnp.concatenate([state.astype(x.dtype), x], axis=1)
    y = xp[:, 0:s] * conv_w[0]
    for i in range(1, CONV_WIDTH):
        y = y + xp[:, i:i + s] * conv_w[i]
    return jax.nn.silu(y), xp[:, -(CONV_WIDTH - 1):]


def _gated_delta_chunked(q, k, v, g, beta, s0):
    b, L, h, dk = q.shape
    dv = v.shape[-1]
    n = -(-L // CHUNK)
    pad = n * CHUNK - L

    def blk(t):
        t = jnp.pad(t, ((0, 0), (0, pad)) + ((0, 0),) * (t.ndim - 2))
        t = t.reshape((b, n, CHUNK) + t.shape[2:])
        return jnp.moveaxis(t, 3, 1)

    qc, kc, vc, gc, bc = blk(q), blk(k), blk(v), blk(g), blk(beta)
    decay = jnp.cumsum(gc, axis=-1)
    ci = jnp.arange(CHUNK)
    lower = ci[:, None] >= ci[None, :]
    strict = ci[:, None] > ci[None, :]
    gam = jnp.exp(jnp.where(lower, decay[..., :, None] - decay[..., None, :], NEG_INF))
    kb = kc * bc[..., None]
    vb = vc * bc[..., None]
    a_mat = jnp.where(strict, jnp.einsum('bhncd,bhnjd->bhncj', kb, kc) * gam, 0.0)
    m = a_mat + jnp.eye(CHUNK, dtype=jnp.float32)
    u = lax.linalg.triangular_solve(m, vb, left_side=True, lower=True, unit_diagonal=True)
    w = lax.linalg.triangular_solve(m, kb * jnp.exp(decay)[..., None], left_side=True, lower=True,
                                    unit_diagonal=True)
    attn = jnp.where(lower, jnp.einsum('bhncd,bhnjd->bhncj', qc, kc) * gam, 0.0)
    qd = qc * jnp.exp(decay)[..., None]
    last = decay[..., -1]
    kd = kc * jnp.exp(last[..., None] - decay)[..., None]

    def step(S, xs):
        qd_n, kd_n, u_n, w_n, attn_n, last_n = xs
        v_new = u_n - jnp.einsum('bhcd,bhde->bhce', w_n, S)
        o = jnp.einsum('bhcd,bhde->bhce', qd_n, S) + jnp.einsum('bhcj,bhje->bhce', attn_n, v_new)
        S = S * jnp.exp(last_n)[..., None, None] + jnp.einsum('bhcd,bhce->bhde', kd_n, v_new)
        return S, o

    xs = tuple(jnp.moveaxis(t, 2, 0) for t in (qd, kd, u, w, attn, last))
    S, o = lax.scan(step, s0, xs)
    o = jnp.moveaxis(o, 0, 2).reshape(b, h, n * CHUNK, dv)[:, :, :L].transpose(0, 2, 1, 3)
    return o, S


def _mixer_b(qkv, z, b_logit, a_logit, conv_state, ssm_state, conv_w, a_log, dt_bias, onorm_w):
    b, s, _ = qkv.shape
    f32 = jnp.float32
    c, new_conv = _short_conv(qkv, conv_state, conv_w)
    q, k, v = jnp.split(c, [B_HEADS * B_KEY_DIM, 2 * B_HEADS * B_KEY_DIM], axis=-1)
    q = _l2norm(q.reshape(b, s, B_HEADS, B_KEY_DIM).astype(f32)) * (B_KEY_DIM ** -0.5)
    k = _l2norm(k.reshape(b, s, B_HEADS, B_KEY_DIM).astype(f32))
    v = v.reshape(b, s, B_HEADS, B_VAL_DIM).astype(f32)
    beta = jax.nn.sigmoid(b_logit.astype(f32))
    g = -jnp.exp(a_log.astype(f32)) * jax.nn.softplus(a_logit.astype(f32) + dt_bias.astype(f32))
    o, s_new = _gated_delta_chunked(q, k, v, g, beta, ssm_state.astype(f32))
    o = _rmsnorm(o, onorm_w) * jax.nn.silu(z.reshape(b, s, B_HEADS, B_VAL_DIM).astype(f32))
    return o.reshape(b, s, B_WIDTH).astype(qkv.dtype), new_conv, s_new


def _hier_moe(x, w_group, b_group, w_expert_router, b_expert_router, w_gate_up, w_down):
    b, s, d = x.shape
    f32 = jnp.float32
    xt = x.reshape(b * s, d)
    g_logit = (xt @ w_group).astype(f32) + b_group.astype(f32)
    g_prob = jax.nn.softmax(g_logit, axis=-1)
    gi = jnp.argmax(g_logit, axis=-1)
    g_sel = jnp.take_along_axis(g_prob, gi[:, None], axis=1)[:, 0]
    e_logit = jnp.einsum('nd,gde->nge', xt, w_expert_router).astype(f32) + b_expert_router.astype(f32)
    e_logit = jnp.take_along_axis(e_logit, gi[:, None, None], axis=1)[:, 0]
    tv, ti = lax.top_k(e_logit, TOP_K)
    tw = jax.nn.softmax(tv, axis=-1) * g_sel[:, None]
    eid = gi[:, None] * EXPERTS_PER_GROUP + ti
    gates = jnp.sum(jax.nn.one_hot(eid, N_EXPERTS, dtype=f32) * tw[..., None], axis=1)
    out = jnp.zeros((b * s, d), f32)
    for e in range(N_EXPERTS):
        gate, up = jnp.split(xt @ w_gate_up[e], 2, axis=-1)
        ye = (jax.nn.silu(gate) * up) @ w_down[e]
        out = out + gates[:, e:e + 1] * ye.astype(f32)
    return out.reshape(b, s, d).astype(x.dtype)


def _finish(x, o_a, o_b, w_out, norm2_w, w_group, b_group, w_expert_router, b_expert_router, w_gate_up, w_down):
    h = x + jnp.concatenate([o_a.astype(x.dtype), o_b.astype(x.dtype)], axis=-1) @ w_out
    return h + _hier_moe(_rmsnorm(h, norm2_w), w_group, b_group, w_expert_router, b_expert_router,
                         w_gate_up, w_down)


def setup_inputs(seed: int = 0) -> dict:
    key = jax.random.key(seed)
    ks = jax.random.split(key, 24)
    f32 = jnp.float32
    n_buf = min(MAX_WINDOW, PAST_LEN)
    nrm = lambda k, shape, scale: scale * jax.random.normal(k, shape, f32)
    dt = jnp.exp(jax.random.uniform(ks[10], (B_HEADS,), f32, math.log(1e-3), math.log(1e-1)))
    return {
        'x_prompt': nrm(ks[0], (BATCH, SEQ, D_MODEL), 1.0),
        'x_sample': nrm(ks[1], (DEC_BATCH, DEC_SEQ, D_MODEL), 1.0),
        'cache_win_k': nrm(ks[2], (DEC_BATCH, n_buf, A_HEADS, A_HEAD_DIM), 1.0),
        'cache_win_v': nrm(ks[3], (DEC_BATCH, n_buf, A_HEADS, A_HEAD_DIM), 1.0),
        'state_conv': nrm(ks[4], (DEC_BATCH, CONV_WIDTH - 1, CONV_DIM), 1.0),
        'state_ssm': nrm(ks[5], (DEC_BATCH, B_HEADS, B_KEY_DIM, B_VAL_DIM), 0.1),
        'norm1_w': 1.0 + nrm(ks[6], (D_MODEL,), 0.02),
        'w_in': nrm(ks[7], (D_MODEL, N_IN), D_MODEL ** -0.5),
        'qnorm_w': 1.0 + nrm(ks[8], (A_HEAD_DIM,), 0.02),
        'knorm_w': 1.0 + nrm(ks[9], (A_HEAD_DIM,), 0.02),
        'conv_w': nrm(ks[11], (CONV_WIDTH, CONV_DIM), CONV_WIDTH ** -0.5),
        'a_log': jnp.log(jax.random.uniform(ks[12], (B_HEADS,), f32, 1.0, 16.0)),
        'dt_bias': dt + jnp.log(-jnp.expm1(-dt)),
        'onorm_w': 1.0 + nrm(ks[13], (B_VAL_DIM,), 0.02),
        'w_out': nrm(ks[14], (MIX_WIDTH, D_MODEL), MIX_WIDTH ** -0.5),
        'norm2_w': 1.0 + nrm(ks[15], (D_MODEL,), 0.02),
        'w_group': nrm(ks[16], (D_MODEL, N_GROUPS), D_MODEL ** -0.5),
        'b_group': nrm(ks[17], (N_GROUPS,), 0.01),
        'w_expert_router': nrm(ks[18], (N_GROUPS, D_MODEL, EXPERTS_PER_GROUP), D_MODEL ** -0.5),
        'b_expert_router': nrm(ks[19], (N_GROUPS, EXPERTS_PER_GROUP), 0.01),
        'w_gate_up': nrm(ks[20], (N_EXPERTS, D_MODEL, 2 * D_EXPERT), D_MODEL ** -0.5),
        'w_down': nrm(ks[21], (N_EXPERTS, D_EXPERT, D_MODEL), D_EXPERT ** -0.5),
    }


def reference(x_prompt, x_sample, cache_win_k, cache_win_v, state_conv, state_ssm, norm1_w, w_in,
              qnorm_w, knorm_w, conv_w, a_log, dt_bias, onorm_w, w_out, norm2_w, w_group, b_group,
              w_expert_router, b_expert_router, w_gate_up, w_down):
    aq, ak, av, bqkv, bz, bb, ba = _in_proj(x_prompt, norm1_w, w_in)
    q, k, v = _a_heads(aq, ak, av, qnorm_w, knorm_w)
    o_a = _combine([_dilated_prompt(q, k, v, w, d) for (w, d) in DILATED_PATTERNS])
    b, s, _ = x_prompt.shape
    conv0 = jnp.zeros((b, CONV_WIDTH - 1, CONV_DIM), x_prompt.dtype)
    ssm0 = jnp.zeros((b, B_HEADS, B_KEY_DIM, B_VAL_DIM), jnp.float32)
    o_b, conv_prompt, ssm_prompt = _mixer_b(bqkv, bz, bb, ba, conv0, ssm0, conv_w, a_log, dt_bias, onorm_w)
    y_prompt = _finish(x_prompt, o_a, o_b, w_out, norm2_w, w_group, b_group, w_expert_router,
                       b_expert_router, w_gate_up, w_down)
    keep = min(MAX_WINDOW, s)
    win_k_prompt = k[:, s - keep:]
    win_v_prompt = v[:, s - keep:]

    aq, ak, av, bqkv, bz, bb, ba = _in_proj(x_sample, norm1_w, w_in)
    qs, ksn, vsn = _a_heads(aq, ak, av, qnorm_w, knorm_w)
    n_buf = cache_win_k.shape[1]
    k_all = jnp.concatenate([cache_win_k.astype(ksn.dtype), ksn], axis=1)
    v_all = jnp.concatenate([cache_win_v.astype(vsn.dtype), vsn], axis=1)
    o_a = _combine([_dilated_sample(qs, k_all, v_all, w, d, n_buf) for (w, d) in DILATED_PATTERNS])
    o_b, conv_sample, ssm_sample = _mixer_b(bqkv, bz, bb, ba, state_conv, state_ssm, conv_w, a_log,
                                            dt_bias, onorm_w)
    y_sample = _finish(x_sample, o_a, o_b, w_out, norm2_w, w_group, b_group, w_expert_router,
                       b_expert_router, w_gate_up, w_down)
    win_k_sample = ksn
    win_v_sample = vsn
    return (y_prompt, y_sample, win_k_prompt, win_v_prompt, conv_prompt, ssm_prompt,
            win_k_sample, win_v_sample, conv_sample, ssm_sample)
```

```python
import functools

import jax
import jax.numpy as jnp
from jax import lax
from jax.experimental import pallas as pl
from jax.experimental.pallas import tpu as pltpu

F32 = jnp.float32
BF16 = jnp.bfloat16

A_HEADS = 8
A_HEAD_DIM = 64
A_WIDTH = A_HEADS * A_HEAD_DIM
DILATED_PATTERNS = ((128, 1), (512, 4), (2048, 16))
MAX_WINDOW = 2048
MAX_DILATION = 16
BAND = 128
ATTN_SCALE = A_HEAD_DIM ** -0.5
B_HEADS = 4
B_KEY_DIM = 128
B_VAL_DIM = 128
B_WIDTH = B_HEADS * B_VAL_DIM
CONV_WIDTH = 4
CONV_DIM = B_HEADS * (2 * B_KEY_DIM + B_VAL_DIM)
CHUNK = 64
N_GROUPS = 4
EXPERTS_PER_GROUP = 4
N_EXPERTS = N_GROUPS * EXPERTS_PER_GROUP
D_EXPERT = 512
RMS_EPS = 1e-6
NEG_INF = -1e30

LANES = 128
SUBLANES = 8
VMEM_LIMIT = 56 * 1024 * 1024


def _params(n_axes, vmem=VMEM_LIMIT):
    return pltpu.CompilerParams(dimension_semantics=("arbitrary",) * n_axes, vmem_limit_bytes=vmem)


def _split2(x):
    hi = x.astype(BF16)
    lo = (x - hi.astype(F32)).astype(BF16)
    return hi, lo


def _split3(x):
    hi = x.astype(BF16)
    r = x - hi.astype(F32)
    mid = r.astype(BF16)
    lo = (r - mid.astype(F32)).astype(BF16)
    return hi, mid, lo


def _dot(a, b):
    return jnp.dot(a, b, preferred_element_type=F32)


def _dot_nt(a, b):
    return lax.dot_general(a, b, (((1,), (1,)), ((), ())), preferred_element_type=F32)


def _dot_tn(a, b):
    return lax.dot_general(a, b, (((0,), (0,)), ((), ())), preferred_element_type=F32)


def _sigmoid(x):
    return 1.0 / (1.0 + jnp.exp(-x))


def _silu(x):
    return x * _sigmoid(x)


def _in_proj_kernel(x_ref, n1w_ref, w_ref, qw_ref, kw_ref, bd_ref,
                    q_ref, k_ref, v_ref, kwin_ref, vwin_ref, bqkv_ref, z_ref, bg_ref,
                    *scratch, fold):
    tm = x_ref.shape[0]
    x = x_ref[...]
    ms = jnp.mean(x * x, axis=-1, keepdims=True)
    n1 = (x * lax.rsqrt(ms + RMS_EPS) * n1w_ref[...]).astype(BF16)

    bd = bd_ref[...]

    def head_norm(t, w):
        hi, lo = _split2(t * t)
        ss = _dot(hi, bd) + _dot(lo, bd)
        return t * lax.rsqrt(ss * (1.0 / A_HEAD_DIM) + RMS_EPS) * w

    q = head_norm(_dot(n1, w_ref[:, 0:A_WIDTH]), qw_ref[...]) * ATTN_SCALE
    k = head_norm(_dot(n1, w_ref[:, A_WIDTH:2 * A_WIDTH]), kw_ref[...])
    v = _dot(n1, w_ref[:, 2 * A_WIDTH:3 * A_WIDTH])
    kwin_ref[...] = k
    vwin_ref[...] = v
    if fold:
        (scr,) = scratch
        rows = tm // MAX_DILATION
        for val, out in ((q, q_ref), (k, k_ref), (v, v_ref)):
            for c in range(A_WIDTH // LANES):
                cs = slice(c * LANES, (c + 1) * LANES)
                scr[c] = val[:, cs]
                for r in range(MAX_DILATION):
                    out[0, r, :, cs] = scr[c, pl.ds(r, rows, stride=MAX_DILATION), :]
    else:
        q_ref[...] = q
        k_ref[...] = k
        v_ref[...] = v
    c0 = 3 * A_WIDTH
    bqkv_ref[...] = _dot(n1, w_ref[:, c0:c0 + CONV_DIM])
    c1 = c0 + CONV_DIM
    z_ref[...] = _dot(n1, w_ref[:, c1:c1 + B_WIDTH]).astype(BF16)
    c2 = c1 + B_WIDTH
    bg_ref[...] = _dot(n1, w_ref[:, c2:c2 + LANES])


def _in_proj(x2d, n1w, w_cat, qw, kw, bd, *, seq, keep, fold, tm):
    m, d = x2d.shape
    nb = m // seq
    tiles_per_seq = seq // tm
    skip = (seq - keep) // tm
    keep_tiles = keep // tm
    n_cols = w_cat.shape[1]

    def win_map(i):
        return (i // tiles_per_seq) * keep_tiles + jnp.maximum(i % tiles_per_seq - skip, 0), 0

    row = lambda i: (i, 0)
    const = lambda i: (0, 0)
    if fold:
        rows = tm // MAX_DILATION
        qkv_shape = jax.ShapeDtypeStruct((nb, MAX_DILATION, seq // MAX_DILATION, A_WIDTH), F32)
        qkv_spec = pl.BlockSpec((1, MAX_DILATION, rows, A_WIDTH),
                                lambda i: (i // tiles_per_seq, 0, i % tiles_per_seq, 0))
        scratch = [pltpu.VMEM((A_WIDTH // LANES, tm, LANES), F32)]
    else:
        qkv_shape = jax.ShapeDtypeStruct((m, A_WIDTH), F32)
        qkv_spec = pl.BlockSpec((tm, A_WIDTH), row)
        scratch = []
    out_shape = (qkv_shape, qkv_shape, qkv_shape,
                 jax.ShapeDtypeStruct((nb * keep, A_WIDTH), F32),
                 jax.ShapeDtypeStruct((nb * keep, A_WIDTH), F32),
                 jax.ShapeDtypeStruct((m, CONV_DIM), F32),
                 jax.ShapeDtypeStruct((m, B_WIDTH), BF16),
                 jax.ShapeDtypeStruct((m, LANES), F32))
    out_specs = (qkv_spec, qkv_spec, qkv_spec,
                 pl.BlockSpec((tm, A_WIDTH), win_map),
                 pl.BlockSpec((tm, A_WIDTH), win_map),
                 pl.BlockSpec((tm, CONV_DIM), row),
                 pl.BlockSpec((tm, B_WIDTH), row),
                 pl.BlockSpec((tm, LANES), row))
    in_specs = [pl.BlockSpec((tm, d), row),
                pl.BlockSpec((1, d), const),
                pl.BlockSpec((d, n_cols), const),
                pl.BlockSpec((1, A_WIDTH), const),
                pl.BlockSpec((1, A_WIDTH), const),
                pl.BlockSpec((A_WIDTH, A_WIDTH), const)]
    return pl.pallas_call(
        functools.partial(_in_proj_kernel, fold=fold),
        grid=(m // tm,), in_specs=in_specs, out_specs=out_specs, out_shape=out_shape,
        scratch_shapes=scratch, compiler_params=_params(1), name="in_proj",
    )(x2d, n1w, w_cat, qw, kw, bd)


def _band_mask(pieces):
    sub = BAND // pieces
    r = lax.broadcasted_iota(jnp.int32, (BAND, 2 * BAND), 0)
    c = lax.broadcasted_iota(jnp.int32, (BAND, 2 * BAND), 1)
    qpos = (r % sub) * pieces + r // sub + BAND
    cc = c % BAND
    kpos = (cc % sub) * pieces + cc // sub + (c // BAND) * BAND
    dist = qpos - kpos
    return (dist >= 0) & (dist <= BAND), c >= BAND


def _attend_pair(qb, kb, vb, mask):
    lane = lax.broadcasted_iota(jnp.int32, qb.shape, 1)
    head0 = lane < A_HEAD_DIM
    kb16 = kb.astype(BF16)
    vb16 = vb.astype(BF16)
    res = []
    for h in range(2):
        hm = head0 if h == 0 else jnp.logical_not(head0)
        qh = jnp.where(hm, qb, 0.0).astype(BF16)
        s = _dot_nt(qh, kb16)
        s = jnp.where(mask, s, NEG_INF)
        m = jnp.max(s, axis=-1, keepdims=True)
        p = jnp.exp(s - m)
        l = jnp.sum(p, axis=-1, keepdims=True)
        o = _dot(p.astype(BF16), vb16) / l
        res.append((o, m + jnp.log(l)))
    o = jnp.where(head0, res[0][0], res[1][0])
    lse = jnp.where(head0, jnp.broadcast_to(res[0][1], qb.shape), jnp.broadcast_to(res[1][1], qb.shape))
    return o, lse


def _attn_prompt_kernel(q_ref, k_ref, v_ref, o_ref, op_scr, lse_scr, nat_scr):
    u_len = q_ref.shape[2]

    for pi, pieces in enumerate((16, 4, 1)):
        sub = BAND // pieces
        n_res = MAX_DILATION // pieces
        n_blk = u_len // sub
        band, is_cur = _band_mask(pieces)

        def body(it, carry, pieces=pieces, sub=sub, n_blk=n_blk, band=band, is_cur=is_cur, pi=pi):
            res = it // n_blk
            m = it % n_blk
            mp = jnp.maximum(m - 1, 0)
            cur = pl.multiple_of(m * sub, SUBLANES)
            prv = pl.multiple_of(mp * sub, SUBLANES)

            def gather(ref, start):
                parts = [ref[0, a * n_res + res, pl.ds(start, sub), :] for a in range(pieces)]
                return parts[0] if pieces == 1 else jnp.concatenate(parts, axis=0)

            qb = gather(q_ref, cur)
            kb = jnp.concatenate([gather(k_ref, prv), gather(k_ref, cur)], axis=0)
            vb = jnp.concatenate([gather(v_ref, prv), gather(v_ref, cur)], axis=0)
            mask = band & (is_cur | (m > 0))
            o, lse = _attend_pair(qb, kb, vb, mask)
            for a in range(pieces):
                op_scr[pi, a * n_res + res, pl.ds(cur, sub), :] = o[a * sub:(a + 1) * sub]
                lse_scr[pi, a * n_res + res, pl.ds(cur, sub), :] = lse[a * sub:(a + 1) * sub]
            return carry

        lax.fori_loop(0, n_res * n_blk, body, 0)

    for r in range(MAX_DILATION):
        l0, l1, l2 = lse_scr[0, r], lse_scr[1, r], lse_scr[2, r]
        mx = jnp.maximum(jnp.maximum(l0, l1), l2)
        e0, e1, e2 = jnp.exp(l0 - mx), jnp.exp(l1 - mx), jnp.exp(l2 - mx)
        out = (e0 * op_scr[0, r] + e1 * op_scr[1, r] + e2 * op_scr[2, r]) / (e0 + e1 + e2)
        nat_scr[pl.ds(r, u_len, stride=MAX_DILATION), :] = out
    o_ref[0] = nat_scr[...].astype(BF16)


def _attn_prompt(qf, kf, vf):
    nb, _, u_len, _ = qf.shape
    seq = u_len * MAX_DILATION
    n_pairs = A_WIDTH // LANES
    spec = pl.BlockSpec((1, MAX_DILATION, u_len, LANES), lambda b, p: (b, 0, 0, p))
    return pl.pallas_call(
        _attn_prompt_kernel,
        grid=(nb, n_pairs),
        in_specs=[spec, spec, spec],
        out_specs=pl.BlockSpec((1, seq, LANES), lambda b, p: (b, 0, p)),
        out_shape=jax.ShapeDtypeStruct((nb, seq, A_WIDTH), BF16),
        scratch_shapes=[pltpu.VMEM((3, MAX_DILATION, u_len, LANES), F32),
                        pltpu.VMEM((3, MAX_DILATION, u_len, LANES), F32),
                        pltpu.VMEM((seq, LANES), F32)],
        compiler_params=_params(2), name="attn_prompt",
    )(qf, kf, vf)


SAMPLE_TAIL = 512


def _multiplicity(t, dd):
    cnt = jnp.zeros(dd.shape, F32)
    for w, d in DILATED_PATTERNS:
        ok = (dd >= 0) & (dd <= w) & (dd % d == 0)
        cnt = cnt + jnp.where(ok, 1.0, 0.0)
    return cnt


def _attn_sample_kernel(q_ref, kn_ref, vn_ref, k1_ref, v1_ref, k2_ref, v2_ref, o_ref, *, n_buf):
    t_len = q_ref.shape[0]
    rows = A_HEADS * t_len
    q = q_ref[...]
    qt = jnp.concatenate([q] * A_HEADS, axis=0)
    r = lax.broadcasted_iota(jnp.int32, (rows, A_WIDTH), 0)
    lane = lax.broadcasted_iota(jnp.int32, (rows, A_WIDTH), 1)
    qblk = jnp.where(r // t_len == lane // A_HEAD_DIM, qt, 0.0).astype(BF16)

    n1 = k1_ref.shape[1] * k1_ref.shape[2]
    half = k1_ref.shape[2]
    k1 = k1_ref[0].reshape(n1, A_WIDTH)
    v1 = v1_ref[0].reshape(n1, A_WIDTH)
    k2 = k2_ref[0]
    v2 = v2_ref[0]
    n2 = k2.shape[0]
    kn = kn_ref[...]
    vn = vn_ref[...]

    def qpos(n):
        return n_buf + lax.broadcasted_iota(jnp.int32, (rows, n), 0) % t_len

    c1i = lax.broadcasted_iota(jnp.int32, (rows, n1), 1)
    pos1 = (c1i // half) * MAX_DILATION + c1i % half
    pos2 = (n_buf - n2) + lax.broadcasted_iota(jnp.int32, (rows, n2), 1)
    pos3 = n_buf + lax.broadcasted_iota(jnp.int32, (rows, t_len), 1)
    cnt = [_multiplicity(None, qpos(n1) - pos1), _multiplicity(None, qpos(n2) - pos2),
           _multiplicity(None, qpos(t_len) - pos3)]
    sc = [_dot_nt(qblk, k1.astype(BF16)), _dot_nt(qblk, k2.astype(BF16)), _dot_nt(qblk, kn.astype(BF16))]
    m = None
    for s, c in zip(sc, cnt):
        mi = jnp.max(jnp.where(c > 0, s, NEG_INF), axis=-1, keepdims=True)
        m = mi if m is None else jnp.maximum(m, mi)
    acc = jnp.zeros((rows, A_WIDTH), F32)
    l = jnp.zeros((rows, 1), F32)
    for s, c, vv in zip(sc, cnt, (v1, v2, vn)):
        p = jnp.where(c > 0, c * jnp.exp(jnp.where(c > 0, s, NEG_INF) - m), 0.0)
        l = l + jnp.sum(p, axis=-1, keepdims=True)
        acc = acc + _dot(p.astype(BF16), vv.astype(BF16))
    acc = acc / l
    out = jnp.zeros((t_len, A_WIDTH), F32)
    lane_t = lax.broadcasted_iota(jnp.int32, (t_len, A_WIDTH), 1)
    for h in range(A_HEADS):
        out = out + jnp.where(lane_t // A_HEAD_DIM == h, acc[h * t_len:(h + 1) * t_len], 0.0)
    o_ref[...] = out


def _attn_sample(q2d, kn2d, vn2d, cache_k, cache_v, *, t_len):
    nb, n_buf = cache_k.shape[0], cache_k.shape[1]
    far_u = (n_buf - SAMPLE_TAIL) // MAX_DILATION
    ck3 = cache_k.reshape(nb, n_buf, A_WIDTH)
    cv3 = cache_v.reshape(nb, n_buf, A_WIDTH)
    ck4 = cache_k.reshape(nb, n_buf // MAX_DILATION, MAX_DILATION, A_WIDTH)
    cv4 = cache_v.reshape(nb, n_buf // MAX_DILATION, MAX_DILATION, A_WIDTH)
    tok = pl.BlockSpec((t_len, A_WIDTH), lambda b: (b, 0))
    far = pl.BlockSpec((1, far_u, t_len, A_WIDTH), lambda b: (b, 0, 0, 0))
    tail = pl.BlockSpec((1, SAMPLE_TAIL, A_WIDTH), lambda b: (b, n_buf // SAMPLE_TAIL - 1, 0))
    return pl.pallas_call(
        functools.partial(_attn_sample_kernel, n_buf=n_buf),
        grid=(nb,),
        in_specs=[tok, tok, tok, far, far, tail, tail],
        out_specs=tok,
        out_shape=jax.ShapeDtypeStruct((nb * t_len, A_WIDTH), F32),
        compiler_params=_params(1), name="attn_sample",
    )(q2d, kn2d, vn2d, ck4, cv4, ck3, cv3)


def _unit_lower_inverse(a, n_chunk):
    n = a.shape[0]
    ri = lax.broadcasted_iota(jnp.int32, (n, n), 0)
    ci = lax.broadcasted_iota(jnp.int32, (n, n), 1)
    eye = jnp.where(ri == ci, 1.0, 0.0).astype(F32)
    t = eye - a
    pw = a
    k = 2
    while k < n_chunk:
        pw16 = pw.astype(BF16)
        pw = _dot(pw16, pw16)
        t = t + _dot(t.astype(BF16), pw.astype(BF16))
        k *= 2
    return t


def _gdn_intra(q, k, v, beta, dcol, drow, chunk):
    n = q.shape[0]
    ri = lax.broadcasted_iota(jnp.int32, (n, n), 0)
    ci = lax.broadcasted_iota(jnp.int32, (n, n), 1)
    same = (ri // chunk) == (ci // chunk)
    lower = same & (ri >= ci)
    strict = same & (ri > ci)
    gam = jnp.exp(jnp.where(lower, dcol - drow, NEG_INF))
    kb = k * beta
    vb = v * beta
    k16 = k.astype(BF16)
    a_mat = jnp.where(strict, _dot_nt(kb.astype(BF16), k16) * gam, 0.0)
    attn = jnp.where(lower, _dot_nt(q.astype(BF16), k16) * gam, 0.0)
    t_inv = _unit_lower_inverse(a_mat, chunk).astype(BF16)
    ed = jnp.exp(dcol)
    u = _dot(t_inv, vb.astype(BF16))
    w = _dot(t_inv, (kb * ed).astype(BF16))
    qd = q * ed
    return u, w, qd, attn


def _gdn_prep(c, bg, avec, dtb, chunk):
    n = c.shape[0]
    lane = lax.broadcasted_iota(jnp.int32, bg.shape, 1)
    xg = bg + dtb
    sp = jnp.maximum(xg, 0.0) + jnp.log1p(jnp.exp(-jnp.abs(xg)))
    g = jnp.where((lane >= B_HEADS) & (lane < 2 * B_HEADS), -avec * sp, 0.0)
    beta_all = _sigmoid(bg)
    ri = lax.broadcasted_iota(jnp.int32, (n, n), 0)
    ci = lax.broadcasted_iota(jnp.int32, (n, n), 1)
    same = (ri // chunk) == (ci // chunk)
    tri = jnp.where(same & (ri >= ci), 1.0, 0.0).astype(BF16)
    gh, gm, gl = _split3(g)
    dcol = _dot(tri, gh) + _dot(tri, gm) + _dot(tri, gl)
    gt = g.T
    th, tm_, tl = _split3(gt)
    drow = _dot_nt(th, tri) + _dot_nt(tm_, tri) + _dot_nt(tl, tri)
    heads = []
    kd_ = B_KEY_DIM
    for h in range(B_HEADS):
        qh = c[:, h * kd_:(h + 1) * kd_]
        kh = c[:, B_HEADS * kd_ + h * kd_:B_HEADS * kd_ + (h + 1) * kd_]
        vh = c[:, 2 * B_HEADS * kd_ + h * B_VAL_DIM:2 * B_HEADS * kd_ + (h + 1) * B_VAL_DIM]
        qh = qh * lax.rsqrt(jnp.sum(qh * qh, axis=-1, keepdims=True) + RMS_EPS) * (B_KEY_DIM ** -0.5)
        kh = kh * lax.rsqrt(jnp.sum(kh * kh, axis=-1, keepdims=True) + RMS_EPS)
        heads.append((qh, kh, vh, beta_all[:, h:h + 1], dcol[:, B_HEADS + h:B_HEADS + h + 1],
                      drow[B_HEADS + h:B_HEADS + h + 1, :]))
    return heads


def _gdn_out(o, onw, zh):
    ms = jnp.mean(o * o, axis=-1, keepdims=True)
    return o * lax.rsqrt(ms + RMS_EPS) * onw * _silu(zh)


def _gdn_prompt_kernel(x_ref, z_ref, bg_ref, cw_ref, cst_ref, sst_ref, avec_ref, dtb_ref, onw_ref,
                       o_ref, snew_ref, xp_scr, s_scr, vn_scr, *, block, chunk):
    ts = x_ref.shape[0]
    t = pl.program_id(1)
    pad = SUBLANES
    hist = CONV_WIDTH - 1

    @pl.when(t == 0)
    def _():
        xp_scr[0:pad, :] = jnp.zeros((pad, CONV_DIM), F32)
        xp_scr[pad - hist:pad, :] = cst_ref[0]
        s_scr[...] = sst_ref[0]

    xp_scr[pad:pad + ts, :] = x_ref[...]
    y = xp_scr[pl.ds(pad - hist, ts), :] * cw_ref[0:1, :]
    for i in range(1, CONV_WIDTH):
        y = y + xp_scr[pl.ds(pad - hist + i, ts), :] * cw_ref[i:i + 1, :]
    xp_scr[0:pad, :] = xp_scr[ts:ts + pad, :]
    c = _silu(y)
    avec = jnp.exp(avec_ref[...])
    onw = onw_ref[...]
    n_chunks = block // chunk

    for sb in range(ts // block):
        rs = slice(sb * block, (sb + 1) * block)
        heads = _gdn_prep(c[rs], bg_ref[rs, :], avec, dtb_ref[...], chunk)
        for h, (qh, kh, vh, beta, dcol, drow) in enumerate(heads):
            u, w, qd, attn = _gdn_intra(qh, kh, vh, beta, dcol, drow, chunk)
            attn16 = attn.astype(BF16)
            kt = kh.T
            vn_scr[...] = jnp.zeros(vn_scr.shape, F32)
            outs = []
            s = s_scr[h]
            for cc in range(n_chunks):
                cs = slice(cc * chunk, (cc + 1) * chunk)
                last = drow[:, (cc + 1) * chunk - 1:(cc + 1) * chunk]
                s16 = s.astype(BF16)
                wq = jnp.concatenate([w[cs], qd[cs]], axis=0).astype(BF16)
                ws_qs = _dot(wq, s16)
                v_new = u[cs] - ws_qs[:chunk]
                vn_scr[cs, :] = v_new
                o = ws_qs[chunk:] + _dot(attn16[cs], vn_scr[...].astype(BF16))
                kdt = kt[:, cs] * jnp.exp(last - drow[:, cs])
                s = s * jnp.exp(last) + _dot(kdt.astype(BF16), v_new.astype(BF16))
                outs.append(o)
            s_scr[h] = s
            o_all = jnp.concatenate(outs, axis=0) if n_chunks > 1 else outs[0]
            zh = z_ref[rs, h * B_VAL_DIM:(h + 1) * B_VAL_DIM].astype(F32)
            o_ref[rs, h * B_VAL_DIM:(h + 1) * B_VAL_DIM] = _gdn_out(o_all, onw, zh).astype(o_ref.dtype)

    @pl.when(t == pl.num_programs(1) - 1)
    def _():
        snew_ref[0] = s_scr[...]


def _gdn_prompt(bqkv, z, bg, conv_w, conv_state, ssm_state, a_log_v, dtb_v, onw, *, seq, ts, block, chunk):
    m = bqkv.shape[0]
    nb = m // seq
    tps = seq // ts
    row = lambda b, t: (b * tps + t, 0)
    const = lambda b, t: (0, 0)
    return pl.pallas_call(
        functools.partial(_gdn_prompt_kernel, block=block, chunk=chunk),
        grid=(nb, tps),
        in_specs=[pl.BlockSpec((ts, CONV_DIM), row),
                  pl.BlockSpec((ts, B_WIDTH), row),
                  pl.BlockSpec((ts, LANES), row),
                  pl.BlockSpec((CONV_WIDTH, CONV_DIM), const),
                  pl.BlockSpec((1, CONV_WIDTH - 1, CONV_DIM), lambda b, t: (b, 0, 0)),
                  pl.BlockSpec((1, B_HEADS, B_KEY_DIM, B_VAL_DIM), lambda b, t: (b, 0, 0, 0)),
                  pl.BlockSpec((1, LANES), const),
                  pl.BlockSpec((1, LANES), const),
                  pl.BlockSpec((1, B_VAL_DIM), const)],
        out_specs=(pl.BlockSpec((ts, B_WIDTH), row),
                   pl.BlockSpec((1, B_HEADS, B_KEY_DIM, B_VAL_DIM), lambda b, t: (b, 0, 0, 0))),
        out_shape=(jax.ShapeDtypeStruct((m, B_WIDTH), BF16),
                   jax.ShapeDtypeStruct(ssm_state.shape, F32)),
        scratch_shapes=[pltpu.VMEM((ts + 2 * SUBLANES, CONV_DIM), F32),
                        pltpu.VMEM((B_HEADS, B_KEY_DIM, B_VAL_DIM), F32),
                        pltpu.VMEM((block, B_VAL_DIM), F32)],
        compiler_params=_params(2), name="gdn_prompt",
    )(bqkv, z, bg, conv_w, conv_state, ssm_state, a_log_v, dtb_v, onw)


def _gdn_sample_kernel(xp_ref, z_ref, bg_ref, cw_ref, sst_ref, avec_ref, dtb_ref, onw_ref,
                       o_ref, snew_ref, vn_scr, *, t_len):
    nseq = xp_ref.shape[0]
    hist = CONV_WIDTH - 1
    ys = []
    for j in range(nseq):
        y = xp_ref[j, 0:t_len, :] * cw_ref[0:1, :]
        for i in range(1, CONV_WIDTH):
            y = y + xp_ref[j, i:i + t_len, :] * cw_ref[i:i + 1, :]
        ys.append(y)
    c = _silu(jnp.concatenate(ys, axis=0))
    avec = jnp.exp(avec_ref[...])
    onw = onw_ref[...]
    heads = _gdn_prep(c, bg_ref[...], avec, dtb_ref[...], t_len)
    for h, (qh, kh, vh, beta, dcol, drow) in enumerate(heads):
        u, w, qd, attn = _gdn_intra(qh, kh, vh, beta, dcol, drow, t_len)
        attn16 = attn.astype(BF16)
        kt = kh.T
        vn_scr[...] = jnp.zeros(vn_scr.shape, F32)
        outs = []
        for j in range(nseq):
            cs = slice(j * t_len, (j + 1) * t_len)
            last = drow[:, (j + 1) * t_len - 1:(j + 1) * t_len]
            s = sst_ref[j, h]
            wq = jnp.concatenate([w[cs], qd[cs]], axis=0).astype(BF16)
            ws_qs = _dot(wq, s.astype(BF16))
            v_new = u[cs] - ws_qs[:t_len]
            vn_scr[cs, :] = v_new
            o = ws_qs[t_len:] + _dot(attn16[cs], vn_scr[...].astype(BF16))
            kdt = kt[:, cs] * jnp.exp(last - drow[:, cs])
            snew_ref[j, h] = s * jnp.exp(last) + _dot(kdt.astype(BF16), v_new.astype(BF16))
            outs.append(o)
        o_all = jnp.concatenate(outs, axis=0)
        zh = z_ref[:, h * B_VAL_DIM:(h + 1) * B_VAL_DIM].astype(F32)
        o_ref[:, h * B_VAL_DIM:(h + 1) * B_VAL_DIM] = _gdn_out(o_all, onw, zh).astype(o_ref.dtype)


def _gdn_sample(xp, z, bg, conv_w, ssm_state, a_log_v, dtb_v, onw, *, t_len, nseq):
    nb = xp.shape[0]
    rows = nseq * t_len
    row = lambda i: (i, 0)
    const = lambda i: (0, 0)
    return pl.pallas_call(
        functools.partial(_gdn_sample_kernel, t_len=t_len),
        grid=(nb // nseq,),
        in_specs=[pl.BlockSpec((nseq, xp.shape[1], CONV_DIM), lambda i: (i, 0, 0)),
                  pl.BlockSpec((rows, B_WIDTH), row),
                  pl.BlockSpec((rows, LANES), row),
                  pl.BlockSpec((CONV_WIDTH, CONV_DIM), const),
                  pl.BlockSpec((nseq, B_HEADS, B_KEY_DIM, B_VAL_DIM), lambda i: (i, 0, 0, 0)),
                  pl.BlockSpec((1, LANES), const),
                  pl.BlockSpec((1, LANES), const),
                  pl.BlockSpec((1, B_VAL_DIM), const)],
        out_specs=(pl.BlockSpec((rows, B_WIDTH), row),
                   pl.BlockSpec((nseq, B_HEADS, B_KEY_DIM, B_VAL_DIM), lambda i: (i, 0, 0, 0))),
        out_shape=(jax.ShapeDtypeStruct((nb * t_len, B_WIDTH), BF16),
                   jax.ShapeDtypeStruct(ssm_state.shape, F32)),
        scratch_shapes=[pltpu.VMEM((rows, B_VAL_DIM), F32)],
        compiler_params=_params(1), name="gdn_sample",
    )(xp, z, bg, conv_w, ssm_state, a_log_v, dtb_v, onw)


def _out_proj_kernel(x_ref, oa_ref, ob_ref, wa_ref, wb_ref, n2w_ref, wr_ref, br_ref,
                     h_ref, n2_ref, gates_ref):
    h = x_ref[...] + _dot(oa_ref[...].astype(BF16), wa_ref[...]) + _dot(ob_ref[...].astype(BF16), wb_ref[...])
    h_ref[...] = h
    ms = jnp.mean(h * h, axis=-1, keepdims=True)
    n2 = h * lax.rsqrt(ms + RMS_EPS) * n2w_ref[...]
    n2_ref[...] = n2.astype(BF16)
    nh, nl = _split2(n2)
    wr = wr_ref[...]
    wh, wl = _split2(wr)
    logits = _dot(nh, wh) + _dot(nl, wh) + _dot(nh, wl) + br_ref[...]
    lane = lax.broadcasted_iota(jnp.int32, logits.shape, 1).astype(F32)
    big = 1e9
    gl = jnp.where(lane < N_GROUPS, logits, NEG_INF)
    gmax = jnp.max(gl, axis=-1, keepdims=True)
    gi = jnp.min(jnp.where(gl == gmax, lane, big), axis=-1, keepdims=True)
    g_sel = 1.0 / jnp.sum(jnp.exp(gl - gmax), axis=-1, keepdims=True)
    lo = N_GROUPS + EXPERTS_PER_GROUP * gi
    in_grp = (lane >= lo) & (lane < lo + EXPERTS_PER_GROUP)
    el = jnp.where(in_grp, logits, NEG_INF)
    v1 = jnp.max(el, axis=-1, keepdims=True)
    i1 = jnp.min(jnp.where(el == v1, lane, big), axis=-1, keepdims=True)
    el2 = jnp.where(lane == i1, NEG_INF, el)
    v2 = jnp.max(el2, axis=-1, keepdims=True)
    i2 = jnp.min(jnp.where(el2 == v2, lane, big), axis=-1, keepdims=True)
    e2 = jnp.exp(v2 - v1)
    w1 = g_sel / (1.0 + e2)
    w2 = g_sel * e2 / (1.0 + e2)
    gates_ref[...] = jnp.where(lane + N_GROUPS == i1, w1, 0.0) + jnp.where(lane + N_GROUPS == i2, w2, 0.0)


def _out_proj(x2d, oa, ob, wa, wb, n2w, wr, br, *, tm):
    m, d = x2d.shape
    row = lambda i: (i, 0)
    const = lambda i: (0, 0)
    return pl.pallas_call(
        _out_proj_kernel,
        grid=(m // tm,),
        in_specs=[pl.BlockSpec((tm, d), row),
                  pl.BlockSpec((tm, A_WIDTH), row),
                  pl.BlockSpec((tm, B_WIDTH), row),
                  pl.BlockSpec((A_WIDTH, d), const),
                  pl.BlockSpec((B_WIDTH, d), const),
                  pl.BlockSpec((1, d), const),
                  pl.BlockSpec((d, LANES), const),
                  pl.BlockSpec((1, LANES), const)],
        out_specs=(pl.BlockSpec((tm, d), row), pl.BlockSpec((tm, d), row), pl.BlockSpec((tm, LANES), row)),
        out_shape=(jax.ShapeDtypeStruct((m, d), F32), jax.ShapeDtypeStruct((m, d), BF16),
                   jax.ShapeDtypeStruct((m, LANES), F32)),
        compiler_params=_params(1), name="out_proj",
    )(x2d, oa, ob, wa, wb, n2w, wr, br)


def _moe_kernel(h_ref, n2_ref, gates_ref, wgu_ref, wd_ref, y_ref):
    e = pl.program_id(1)

    @pl.when(e == 0)
    def _():
        y_ref[...] = h_ref[...]

    n2 = n2_ref[...]
    gu = _dot(n2, wgu_ref[0])
    act = _silu(gu[:, :D_EXPERT]) * gu[:, D_EXPERT:]
    lane = lax.broadcasted_iota(jnp.int32, gates_ref.shape, 1)
    gate = jnp.sum(jnp.where(lane == e, gates_ref[...], 0.0), axis=-1, keepdims=True)
    y_ref[...] += gate * _dot(act.astype(BF16), wd_ref[0])


def _moe(h, n2, gates, wgu, wd, *, tm):
    m, d = h.shape
    row = lambda i, e: (i, 0)
    return pl.pallas_call(
        _moe_kernel,
        grid=(m // tm, N_EXPERTS),
        in_specs=[pl.BlockSpec((tm, d), row),
                  pl.BlockSpec((tm, d), row),
                  pl.BlockSpec((tm, LANES), row),
                  pl.BlockSpec((1, d, 2 * D_EXPERT), lambda i, e: (e, 0, 0)),
                  pl.BlockSpec((1, D_EXPERT, d), lambda i, e: (e, 0, 0))],
        out_specs=pl.BlockSpec((tm, d), row),
        out_shape=jax.ShapeDtypeStruct((m, d), F32),
        compiler_params=_params(2), name="moe",
    )(h, n2, gates, wgu, wd)


def _pad_lanes(v, offset):
    out = jnp.zeros((1, LANES), F32)
    return out.at[0, offset:offset + v.shape[0]].set(v.astype(F32))


def kernel(x_prompt, x_sample, cache_win_k, cache_win_v, state_conv, state_ssm, norm1_w, w_in, qnorm_w, knorm_w, conv_w, a_log, dt_bias, onorm_w, w_out, norm2_w, w_group, b_group, w_expert_router, b_expert_router, w_gate_up, w_down):
    nb, seq, d = x_prompt.shape
    db, t_len, _ = x_sample.shape
    n_buf = cache_win_k.shape[1]
    assert n_buf == MAX_WINDOW and seq % (MAX_DILATION * BAND) == 0 and t_len == SUBLANES
    keep = min(MAX_WINDOW, seq)

    c_bg = 3 * A_WIDTH + CONV_DIM + B_WIDTH
    w_cat = jnp.concatenate([w_in, jnp.zeros((d, LANES - 2 * B_HEADS), w_in.dtype)], axis=1).astype(BF16)
    assert w_cat.shape[1] == c_bg + LANES
    n1w = norm1_w.reshape(1, d).astype(F32)
    qw = jnp.tile(qnorm_w.astype(F32), A_HEADS).reshape(1, A_WIDTH)
    kw = jnp.tile(knorm_w.astype(F32), A_HEADS).reshape(1, A_WIDTH)
    hid = jnp.arange(A_WIDTH) // A_HEAD_DIM
    bd = (hid[:, None] == hid[None, :]).astype(BF16)
    a_log_v = _pad_lanes(a_log, B_HEADS)
    dtb_v = _pad_lanes(dt_bias, B_HEADS)
    onw = onorm_w.reshape(1, B_VAL_DIM).astype(F32)
    wa = w_out[:A_WIDTH].astype(BF16)
    wb = w_out[A_WIDTH:].astype(BF16)
    n2w = norm2_w.reshape(1, d).astype(F32)
    wr = jnp.concatenate([w_group, jnp.transpose(w_expert_router, (1, 0, 2)).reshape(d, N_EXPERTS),
                          jnp.zeros((d, LANES - N_GROUPS - N_EXPERTS), F32)], axis=1).astype(F32)
    br = jnp.zeros((1, LANES), F32).at[0, :N_GROUPS].set(b_group).at[0, N_GROUPS:N_GROUPS + N_EXPERTS].set(
        b_expert_router.reshape(-1))
    wgu = w_gate_up.astype(BF16)
    wd = w_down.astype(BF16)
    cw = conv_w.astype(F32)

    def finish(x2d, oa, ob, tm, tm_moe):
        h, n2, gates = _out_proj(x2d, oa, ob, wa, wb, n2w, wr, br, tm=tm)
        return _moe(h, n2, gates, wgu, wd, tm=tm_moe)

    xp2d = x_prompt.reshape(nb * seq, d)
    qf, kf, vf, kwin, vwin, bqkv, z, bg = _in_proj(xp2d, n1w, w_cat, qw, kw, bd, seq=seq, keep=keep,
                                                   fold=True, tm=512)
    oa = _attn_prompt(qf, kf, vf).reshape(nb * seq, A_WIDTH)
    conv0 = jnp.zeros((nb, CONV_WIDTH - 1, CONV_DIM), F32)
    ssm0 = jnp.zeros((nb, B_HEADS, B_KEY_DIM, B_VAL_DIM), F32)
    ob, ssm_prompt = _gdn_prompt(bqkv, z, bg, cw, conv0, ssm0, a_log_v, dtb_v, onw,
                                 seq=seq, ts=512, block=256, chunk=CHUNK)
    y_prompt = finish(xp2d, oa, ob, 512, 1024).reshape(nb, seq, d)
    win_k_prompt = kwin.reshape(nb, keep, A_HEADS, A_HEAD_DIM)
    win_v_prompt = vwin.reshape(nb, keep, A_HEADS, A_HEAD_DIM)
    conv_prompt = bqkv.reshape(nb, seq, CONV_DIM)[:, seq - (CONV_WIDTH - 1):]

    ms = db * t_len
    xs2d = x_sample.reshape(ms, d)
    tms = min(512, ms)
    qs, ksn, vsn, _, _, bqkv_s, z_s, bg_s = _in_proj(xs2d, n1w, w_cat, qw, kw, bd, seq=ms, keep=ms,
                                                     fold=False, tm=tms)
    oa_s = _attn_sample(qs, ksn, vsn, cache_win_k.astype(F32), cache_win_v.astype(F32), t_len=t_len)
    xpad = jnp.concatenate([state_conv.astype(F32), bqkv_s.reshape(db, t_len, CONV_DIM)], axis=1)
    nseq = 16 if db % 16 == 0 else 1
    ob_s, ssm_sample = _gdn_sample(xpad, z_s, bg_s, cw, state_ssm.astype(F32), a_log_v, dtb_v, onw,
                                   t_len=t_len, nseq=nseq)
    y_sample = finish(xs2d, oa_s, ob_s, tms, tms).reshape(db, t_len, d)
    win_k_sample = ksn.reshape(db, t_len, A_HEADS, A_HEAD_DIM)
    win_v_sample = vsn.reshape(db, t_len, A_HEADS, A_HEAD_DIM)
    conv_sample = xpad[:, t_len:]

    return (y_prompt, y_sample, win_k_prompt, win_v_prompt, conv_prompt, ssm_prompt,
            win_k_sample, win_v_sample, conv_sample, ssm_sample)
```

```python
import functools

import jax
import jax.numpy as jnp
from jax import lax
from jax.experimental import pallas as pl
from jax.experimental.pallas import tpu as pltpu

F32 = jnp.float32
BF16 = jnp.bfloat16

A_HEADS = 8
A_HEAD_DIM = 64
A_WIDTH = A_HEADS * A_HEAD_DIM
DILATED_PATTERNS = ((128, 1), (512, 4), (2048, 16))
MAX_WINDOW = 2048
MAX_DILATION = 16
BAND = 128
ATTN_SCALE = A_HEAD_DIM ** -0.5
B_HEADS = 4
B_KEY_DIM = 128
B_VAL_DIM = 128
B_WIDTH = B_HEADS * B_VAL_DIM
CONV_WIDTH = 4
CONV_DIM = B_HEADS * (2 * B_KEY_DIM + B_VAL_DIM)
CHUNK = 64
N_GROUPS = 4
EXPERTS_PER_GROUP = 4
N_EXPERTS = N_GROUPS * EXPERTS_PER_GROUP
D_EXPERT = 512
RMS_EPS = 1e-6
NEG_INF = -1e30

LANES = 128
SUBLANES = 8
VMEM_LIMIT = 56 * 1024 * 1024


def _params(n_axes, vmem=VMEM_LIMIT):
    return pltpu.CompilerParams(dimension_semantics=("arbitrary",) * n_axes, vmem_limit_bytes=vmem)


def _split2(x):
    hi = x.astype(BF16)
    lo = (x - hi.astype(F32)).astype(BF16)
    return hi, lo


def _split3(x):
    hi = x.astype(BF16)
    r = x - hi.astype(F32)
    mid = r.astype(BF16)
    lo = (r - mid.astype(F32)).astype(BF16)
    return hi, mid, lo


def _dot(a, b):
    return jnp.dot(a, b, preferred_element_type=F32)


def _dot_nt(a, b):
    return lax.dot_general(a, b, (((1,), (1,)), ((), ())), preferred_element_type=F32)


def _dot_tn(a, b):
    return lax.dot_general(a, b, (((0,), (0,)), ((), ())), preferred_element_type=F32)


def _sigmoid(x):
    return 1.0 / (1.0 + jnp.exp(-x))


def _silu(x):
    return x * _sigmoid(x)


def _in_proj_kernel(x_ref, n1w_ref, w_ref, qw_ref, kw_ref, bd_ref,
                    q_ref, k_ref, v_ref, kwin_ref, vwin_ref, bqkv_ref, z_ref, bg_ref,
                    *scratch, fold):
    tm = x_ref.shape[0]
    x = x_ref[...]
    ms = jnp.mean(x * x, axis=-1, keepdims=True)
    n1 = (x * lax.rsqrt(ms + RMS_EPS) * n1w_ref[...]).astype(BF16)

    bd = bd_ref[...]

    def head_norm(t, w):
        hi, lo = _split2(t * t)
        ss = _dot(hi, bd) + _dot(lo, bd)
        return t * lax.rsqrt(ss * (1.0 / A_HEAD_DIM) + RMS_EPS) * w

    q = head_norm(_dot(n1, w_ref[:, 0:A_WIDTH]), qw_ref[...]) * ATTN_SCALE
    k = head_norm(_dot(n1, w_ref[:, A_WIDTH:2 * A_WIDTH]), kw_ref[...])
    v = _dot(n1, w_ref[:, 2 * A_WIDTH:3 * A_WIDTH])
    kwin_ref[...] = k
    vwin_ref[...] = v
    if fold:
        (scr,) = scratch
        rows = tm // MAX_DILATION
        for val, out in ((q, q_ref), (k, k_ref), (v, v_ref)):
            for c in range(A_WIDTH // LANES):
                cs = slice(c * LANES, (c + 1) * LANES)
                scr[c] = val[:, cs]
                for r in range(MAX_DILATION):
                    out[0, r, :, cs] = scr[c, pl.ds(r, rows, stride=MAX_DILATION), :]
    else:
        q_ref[...] = q
        k_ref[...] = k
        v_ref[...] = v
    c0 = 3 * A_WIDTH
    bqkv_ref[...] = _dot(n1, w_ref[:, c0:c0 + CONV_DIM])
    c1 = c0 + CONV_DIM
    z_ref[...] = _dot(n1, w_ref[:, c1:c1 + B_WIDTH]).astype(BF16)
    c2 = c1 + B_WIDTH
    bg_ref[...] = _dot(n1, w_ref[:, c2:c2 + LANES])


def _in_proj(x2d, n1w, w_cat, qw, kw, bd, *, seq, keep, fold, tm):
    m, d = x2d.shape
    nb = m // seq
    tiles_per_seq = seq // tm
    skip = (seq - keep) // tm
    keep_tiles = keep // tm
    n_cols = w_cat.shape[1]

    def win_map(i):
        return (i // tiles_per_seq) * keep_tiles + jnp.maximum(i % tiles_per_seq - skip, 0), 0

    row = lambda i: (i, 0)
    const = lambda i: (0, 0)
    if fold:
        rows = tm // MAX_DILATION
        qkv_shape = jax.ShapeDtypeStruct((nb, MAX_DILATION, seq // MAX_DILATION, A_WIDTH), F32)
        qkv_spec = pl.BlockSpec((1, MAX_DILATION, rows, A_WIDTH),
                                lambda i: (i // tiles_per_seq, 0, i % tiles_per_seq, 0))
        scratch = [pltpu.VMEM((A_WIDTH // LANES, tm, LANES), F32)]
    else:
        qkv_shape = jax.ShapeDtypeStruct((m, A_WIDTH), F32)
        qkv_spec = pl.BlockSpec((tm, A_WIDTH), row)
        scratch = []
    out_shape = (qkv_shape, qkv_shape, qkv_shape,
                 jax.ShapeDtypeStruct((nb * keep, A_WIDTH), F32),
                 jax.ShapeDtypeStruct((nb * keep, A_WIDTH), F32),
                 jax.ShapeDtypeStruct((m, CONV_DIM), F32),
                 jax.ShapeDtypeStruct((m, B_WIDTH), BF16),
                 jax.ShapeDtypeStruct((m, LANES), F32))
    out_specs = (qkv_spec, qkv_spec, qkv_spec,
                 pl.BlockSpec((tm, A_WIDTH), win_map),
                 pl.BlockSpec((tm, A_WIDTH), win_map),
                 pl.BlockSpec((tm, CONV_DIM), row),
                 pl.BlockSpec((tm, B_WIDTH), row),
                 pl.BlockSpec((tm, LANES), row))
    in_specs = [pl.BlockSpec((tm, d), row),
                pl.BlockSpec((1, d), const),
                pl.BlockSpec((d, n_cols), const),
                pl.BlockSpec((1, A_WIDTH), const),
                pl.BlockSpec((1, A_WIDTH), const),
                pl.BlockSpec((A_WIDTH, A_WIDTH), const)]
    return pl.pallas_call(
        functools.partial(_in_proj_kernel, fold=fold),
        grid=(m // tm,), in_specs=in_specs, out_specs=out_specs, out_shape=out_shape,
        scratch_shapes=scratch, compiler_params=_params(1), name="in_proj",
    )(x2d, n1w, w_cat, qw, kw, bd)


def _band_mask(pieces):
    sub = BAND // pieces
    r = lax.broadcasted_iota(jnp.int32, (BAND, 2 * BAND), 0)
    c = lax.broadcasted_iota(jnp.int32, (BAND, 2 * BAND), 1)
    qpos = (r % sub) * pieces + r // sub + BAND
    cc = c % BAND
    kpos = (cc % sub) * pieces + cc // sub + (c // BAND) * BAND
    dist = qpos - kpos
    return (dist >= 0) & (dist <= BAND), c >= BAND


ATTN_UNROLL = 8


def _attn_prompt_kernel(q_ref, k_ref, v_ref, o_ref, p_scr, op_scr, lse_scr, nat_scr):
    u_len = q_ref.shape[2]
    lane = lax.broadcasted_iota(jnp.int32, (BAND, LANES), 1)
    head0 = lane < A_HEAD_DIM

    for pi, pieces in enumerate((16, 4, 1)):
        sub = BAND // pieces
        n_res = MAX_DILATION // pieces
        n_blk = u_len // sub
        band, is_cur = _band_mask(pieces)
        band2 = jnp.concatenate([band, band], axis=0)
        is_cur2 = jnp.concatenate([is_cur, is_cur], axis=0)

        def locate(blk, n_blk=n_blk, sub=sub):
            res = blk // n_blk
            m = blk % n_blk
            cur = pl.multiple_of(m * sub, SUBLANES)
            prv = pl.multiple_of(jnp.maximum(m - 1, 0) * sub, SUBLANES)
            return res, m, cur, prv

        def gather(ref, res, start, pieces=pieces, n_res=n_res, sub=sub):
            parts = [ref[0, a * n_res + res, pl.ds(start, sub), :] for a in range(pieces)]
            return parts[0] if pieces == 1 else jnp.concatenate(parts, axis=0)

        def scatter(ref, res, start, val, pi=pi, pieces=pieces, n_res=n_res, sub=sub):
            for a in range(pieces):
                ref[pi, a * n_res + res, pl.ds(start, sub), :] = val[a * sub:(a + 1) * sub]

        def probs(it, carry, band2=band2, is_cur2=is_cur2):
            for j in range(ATTN_UNROLL):
                blk = it * ATTN_UNROLL + j
                res, m, cur, prv = locate(blk)
                qb = gather(q_ref, res, cur)
                q2 = jnp.concatenate([jnp.where(head0, qb, 0.0), jnp.where(head0, 0.0, qb)], axis=0).astype(BF16)
                kb = jnp.concatenate([gather(k_ref, res, prv), gather(k_ref, res, cur)], axis=0).astype(BF16)
                s = _dot_nt(q2, kb)
                s = jnp.where(band2 & (is_cur2 | (m > 0)), s, NEG_INF)
                mx = jnp.max(s, axis=-1, keepdims=True)
                p = jnp.exp(s - mx)
                l = jnp.sum(p, axis=-1, keepdims=True)
                p_scr[blk] = (p * (1.0 / l)).astype(BF16)
                lse = mx + jnp.log(l)
                lse = jnp.where(head0, jnp.broadcast_to(lse[:BAND], (BAND, LANES)),
                                jnp.broadcast_to(lse[BAND:], (BAND, LANES)))
                scatter(lse_scr, res, cur, lse)
            return carry

        def values(it, carry):
            for j in range(ATTN_UNROLL):
                blk = it * ATTN_UNROLL + j
                res, m, cur, prv = locate(blk)
                vb = jnp.concatenate([gather(v_ref, res, prv), gather(v_ref, res, cur)], axis=0).astype(BF16)
                o2 = _dot(p_scr[blk], vb)
                scatter(op_scr, res, cur, jnp.where(head0, o2[:BAND], o2[BAND:]))
            return carry

        n_it = n_res * n_blk // ATTN_UNROLL
        lax.fori_loop(0, n_it, probs, 0)
        lax.fori_loop(0, n_it, values, 0)

    for r in range(MAX_DILATION):
        l0, l1, l2 = lse_scr[0, r], lse_scr[1, r], lse_scr[2, r]
        mx = jnp.maximum(jnp.maximum(l0, l1), l2)
        e0, e1, e2 = jnp.exp(l0 - mx), jnp.exp(l1 - mx), jnp.exp(l2 - mx)
        out = (e0 * op_scr[0, r] + e1 * op_scr[1, r] + e2 * op_scr[2, r]) / (e0 + e1 + e2)
        nat_scr[pl.ds(r, u_len, stride=MAX_DILATION), :] = out
    o_ref[0] = nat_scr[...].astype(BF16)


def _attn_prompt(qf, kf, vf):
    nb, _, u_len, _ = qf.shape
    seq = u_len * MAX_DILATION
    n_pairs = A_WIDTH // LANES
    spec = pl.BlockSpec((1, MAX_DILATION, u_len, LANES), lambda b, p: (b, 0, 0, p))
    return pl.pallas_call(
        _attn_prompt_kernel,
        grid=(nb, n_pairs),
        in_specs=[spec, spec, spec],
        out_specs=pl.BlockSpec((1, seq, LANES), lambda b, p: (b, 0, p)),
        out_shape=jax.ShapeDtypeStruct((nb, seq, A_WIDTH), BF16),
        scratch_shapes=[pltpu.VMEM((seq // BAND, 2 * BAND, 2 * BAND), BF16),
                        pltpu.VMEM((3, MAX_DILATION, u_len, LANES), F32),
                        pltpu.VMEM((3, MAX_DILATION, u_len, LANES), F32),
                        pltpu.VMEM((seq, LANES), F32)],
        compiler_params=_params(2), name="attn_prompt",
    )(qf, kf, vf)


def _multiplicity(t, dd):
    cnt = jnp.zeros(dd.shape, F32)
    for w, d in DILATED_PATTERNS:
        ok = (dd >= 0) & (dd <= w) & (dd % d == 0)
        cnt = cnt + jnp.where(ok, 1.0, 0.0)
    return cnt


def _attn_sample_kernel(q_ref, kn_ref, vn_ref, kt_ref, vt_ref, o_ref):
    t_len = q_ref.shape[0]
    n_buf = kt_ref.shape[3]
    q = q_ref[...]
    kn = kn_ref[...]
    vn = vn_ref[...]
    tq = n_buf + lax.broadcasted_iota(jnp.int32, (t_len, n_buf), 0)
    cnt = _multiplicity(None, tq - lax.broadcasted_iota(jnp.int32, (t_len, n_buf), 1))
    cnt_new = _multiplicity(None, lax.broadcasted_iota(jnp.int32, (t_len, t_len), 0)
                            - lax.broadcasted_iota(jnp.int32, (t_len, t_len), 1))
    outs = []
    for h in range(A_HEADS):
        hs = slice(h * A_HEAD_DIM, (h + 1) * A_HEAD_DIM)
        qh = q[:, hs].astype(BF16)
        s = jnp.where(cnt > 0, _dot(qh, kt_ref[0, h].astype(BF16)), NEG_INF)
        sn = jnp.where(cnt_new > 0, _dot_nt(qh, kn[:, hs].astype(BF16)), NEG_INF)
        m = jnp.maximum(jnp.max(s, axis=-1, keepdims=True), jnp.max(sn, axis=-1, keepdims=True))
        p = cnt * jnp.exp(s - m)
        pn = cnt_new * jnp.exp(sn - m)
        l = jnp.sum(p, axis=-1, keepdims=True) + jnp.sum(pn, axis=-1, keepdims=True)
        o = _dot_nt(p.astype(BF16), vt_ref[0, h].astype(BF16)) + _dot(pn.astype(BF16), vn[:, hs].astype(BF16))
        outs.append(o / l)
    o_ref[...] = jnp.concatenate(outs, axis=-1)


def _attn_sample(q2d, kn2d, vn2d, cache_kt, cache_vt, *, t_len):
    nb, _, _, n_buf = cache_kt.shape
    tok = pl.BlockSpec((t_len, A_WIDTH), lambda b: (b, 0))
    win = pl.BlockSpec((1, A_HEADS, A_HEAD_DIM, n_buf), lambda b: (b, 0, 0, 0))
    return pl.pallas_call(
        _attn_sample_kernel,
        grid=(nb,),
        in_specs=[tok, tok, tok, win, win],
        out_specs=tok,
        out_shape=jax.ShapeDtypeStruct((nb * t_len, A_WIDTH), F32),
        compiler_params=_params(1), name="attn_sample",
    )(q2d, kn2d, vn2d, cache_kt, cache_vt)


def _unit_lower_inverse(a, n_chunk):
    n = a.shape[0]
    ri = lax.broadcasted_iota(jnp.int32, (n, n), 0)
    ci = lax.broadcasted_iota(jnp.int32, (n, n), 1)
    eye = jnp.where(ri == ci, 1.0, 0.0).astype(F32)
    t = eye - a
    pw = a
    k = 2
    while k < n_chunk:
        pw16 = pw.astype(BF16)
        pw = _dot(pw16, pw16)
        t = t + _dot(t.astype(BF16), pw.astype(BF16))
        k *= 2
    return t


def _gdn_intra(q, k, v, beta, dcol, drow, chunk):
    n = q.shape[0]
    ri = lax.broadcasted_iota(jnp.int32, (n, n), 0)
    ci = lax.broadcasted_iota(jnp.int32, (n, n), 1)
    same = (ri // chunk) == (ci // chunk)
    lower = same & (ri >= ci)
    strict = same & (ri > ci)
    gam = jnp.exp(jnp.where(lower, dcol - drow, NEG_INF))
    kb = k * beta
    vb = v * beta
    k16 = k.astype(BF16)
    a_mat = jnp.where(strict, _dot_nt(kb.astype(BF16), k16) * gam, 0.0)
    attn = jnp.where(lower, _dot_nt(q.astype(BF16), k16) * gam, 0.0)
    t_inv = _unit_lower_inverse(a_mat, chunk).astype(BF16)
    ed = jnp.exp(dcol)
    u = _dot(t_inv, vb.astype(BF16))
    w = _dot(t_inv, (kb * ed).astype(BF16))
    qd = q * ed
    return u, w, qd, attn


def _gdn_prep(c, bg, avec, dtb, chunk):
    n = c.shape[0]
    lane = lax.broadcasted_iota(jnp.int32, bg.shape, 1)
    xg = bg + dtb
    sp = jnp.maximum(xg, 0.0) + jnp.log1p(jnp.exp(-jnp.abs(xg)))
    g = jnp.where((lane >= B_HEADS) & (lane < 2 * B_HEADS), -avec * sp, 0.0)
    beta_all = _sigmoid(bg)
    ri = lax.broadcasted_iota(jnp.int32, (n, n), 0)
    ci = lax.broadcasted_iota(jnp.int32, (n, n), 1)
    same = (ri // chunk) == (ci // chunk)
    tri = jnp.where(same & (ri >= ci), 1.0, 0.0).astype(BF16)
    gh, gm, gl = _split3(g)
    dcol = _dot(tri, gh) + _dot(tri, gm) + _dot(tri, gl)
    gt = g.T
    th, tm_, tl = _split3(gt)
    drow = _dot_nt(th, tri) + _dot_nt(tm_, tri) + _dot_nt(tl, tri)
    heads = []
    kd_ = B_KEY_DIM
    for h in range(B_HEADS):
        qh = c[:, h * kd_:(h + 1) * kd_]
        kh = c[:, B_HEADS * kd_ + h * kd_:B_HEADS * kd_ + (h + 1) * kd_]
        vh = c[:, 2 * B_HEADS * kd_ + h * B_VAL_DIM:2 * B_HEADS * kd_ + (h + 1) * B_VAL_DIM]
        qh = qh * lax.rsqrt(jnp.sum(qh * qh, axis=-1, keepdims=True) + RMS_EPS) * (B_KEY_DIM ** -0.5)
        kh = kh * lax.rsqrt(jnp.sum(kh * kh, axis=-1, keepdims=True) + RMS_EPS)
        heads.append((qh, kh, vh, beta_all[:, h:h + 1], dcol[:, B_HEADS + h:B_HEADS + h + 1],
                      drow[B_HEADS + h:B_HEADS + h + 1, :]))
    return heads


def _gdn_out(o, onw, zh):
    ms = jnp.mean(o * o, axis=-1, keepdims=True)
    return o * lax.rsqrt(ms + RMS_EPS) * onw * _silu(zh)


def _gdn_prompt_kernel(x_ref, z_ref, bg_ref, cw_ref, cst_ref, sst_ref, avec_ref, dtb_ref, onw_ref,
                       o_ref, snew_ref, xp_scr, s_scr, *, block, chunk):
    ts = x_ref.shape[0]
    t = pl.program_id(1)
    pad = SUBLANES
    hist = CONV_WIDTH - 1

    @pl.when(t == 0)
    def _():
        xp_scr[0:pad, :] = jnp.zeros((pad, CONV_DIM), F32)
        xp_scr[pad - hist:pad, :] = cst_ref[0]
        s_scr[...] = sst_ref[0]

    xp_scr[pad:pad + ts, :] = x_ref[...]
    y = xp_scr[pl.ds(pad - hist, ts), :] * cw_ref[0:1, :]
    for i in range(1, CONV_WIDTH):
        y = y + xp_scr[pl.ds(pad - hist + i, ts), :] * cw_ref[i:i + 1, :]
    xp_scr[0:pad, :] = xp_scr[ts:ts + pad, :]
    c = _silu(y)
    avec = jnp.exp(avec_ref[...])
    onw = onw_ref[...]
    n_chunks = block // chunk
    n_sb = ts // block
    ri = lax.broadcasted_iota(jnp.int32, (block, block), 0)
    ci = lax.broadcasted_iota(jnp.int32, (block, block), 1)
    same = (ri // chunk) == (ci // chunk)
    lower = same & (ri >= ci)
    strict = same & (ri > ci)
    eye = jnp.where(ri == ci, 1.0, 0.0).astype(F32)

    units = []
    for sb in range(n_sb):
        rs = slice(sb * block, (sb + 1) * block)
        heads = _gdn_prep(c[rs], bg_ref[rs, :], avec, dtb_ref[...], chunk)
        for h, (qh, kh, vh, beta, dcol, drow) in enumerate(heads):
            units.append(dict(sb=sb, h=h, q=qh, k=kh, v=vh, beta=beta, dcol=dcol, drow=drow))

    for u in units:
        gam = jnp.exp(jnp.where(lower, u["dcol"] - u["drow"], NEG_INF))
        kb = u["k"] * u["beta"]
        kq = _dot_nt(jnp.concatenate([kb, u["q"]], axis=0).astype(BF16), u["k"].astype(BF16))
        u["pw"] = jnp.where(strict, kq[:block] * gam, 0.0)
        u["t"] = eye - u["pw"]
        u["attn"] = jnp.where(lower, kq[block:] * gam, 0.0).astype(BF16)
        ed = jnp.exp(u["dcol"])
        u["rhs"] = jnp.concatenate([u["v"] * u["beta"], kb * ed], axis=1).astype(BF16)
        u["qd"] = u["q"] * ed
        u["kt"] = u["k"].T

    kk = 2
    while kk < chunk:
        for u in units:
            p16 = u["pw"].astype(BF16)
            u["pw"] = _dot(p16, p16)
        for u in units:
            u["t"] = u["t"] + _dot(u["t"].astype(BF16), u["pw"].astype(BF16))
        kk *= 2

    for u in units:
        u["uw"] = _dot(u["t"].astype(BF16), u["rhs"]).astype(BF16)
    for u in units:
        au_aw = _dot(u["attn"], u["uw"])
        u["op"] = au_aw[:, :B_VAL_DIM]
        u["qp"] = (u["qd"] - au_aw[:, B_VAL_DIM:]).astype(BF16)
    for u in units:
        u["n"], u["mw"], u["e"] = [], [], []
        for cc in range(n_chunks):
            cs = slice(cc * chunk, (cc + 1) * chunk)
            last = u["drow"][:, (cc + 1) * chunk - 1:(cc + 1) * chunk]
            kdt = (u["kt"][:, cs] * jnp.exp(last - u["drow"][:, cs])).astype(BF16)
            nm = _dot(kdt, u["uw"][cs])
            u["n"].append(nm[:, :B_VAL_DIM])
            u["mw"].append(nm[:, B_VAL_DIM:].astype(BF16))
            u["e"].append(jnp.exp(last))

    s = [s_scr[h] for h in range(B_HEADS)]
    for sb in range(n_sb):
        rs = slice(sb * block, (sb + 1) * block)
        outs = [[] for _ in range(B_HEADS)]
        for cc in range(n_chunks):
            cs = slice(cc * chunk, (cc + 1) * chunk)
            for h in range(B_HEADS):
                u = units[sb * B_HEADS + h]
                r = _dot(jnp.concatenate([u["mw"][cc], u["qp"][cs]], axis=0), s[h].astype(BF16))
                outs[h].append(r[B_KEY_DIM:] + u["op"][cs])
                s[h] = s[h] * u["e"][cc] - r[:B_KEY_DIM] + u["n"][cc]
        for h in range(B_HEADS):
            o_all = jnp.concatenate(outs[h], axis=0) if n_chunks > 1 else outs[h][0]
            zh = z_ref[rs, h * B_VAL_DIM:(h + 1) * B_VAL_DIM].astype(F32)
            o_ref[rs, h * B_VAL_DIM:(h + 1) * B_VAL_DIM] = _gdn_out(o_all, onw, zh).astype(o_ref.dtype)
    for h in range(B_HEADS):
        s_scr[h] = s[h]

    @pl.when(t == pl.num_programs(1) - 1)
    def _():
        snew_ref[0] = s_scr[...]


def _gdn_prompt(bqkv, z, bg, conv_w, conv_state, ssm_state, a_log_v, dtb_v, onw, *, seq, ts, block, chunk):
    m = bqkv.shape[0]
    nb = m // seq
    tps = seq // ts
    row = lambda b, t: (b * tps + t, 0)
    const = lambda b, t: (0, 0)
    return pl.pallas_call(
        functools.partial(_gdn_prompt_kernel, block=block, chunk=chunk),
        grid=(nb, tps),
        in_specs=[pl.BlockSpec((ts, CONV_DIM), row),
                  pl.BlockSpec((ts, B_WIDTH), row),
                  pl.BlockSpec((ts, LANES), row),
                  pl.BlockSpec((CONV_WIDTH, CONV_DIM), const),
                  pl.BlockSpec((1, CONV_WIDTH - 1, CONV_DIM), lambda b, t: (b, 0, 0)),
                  pl.BlockSpec((1, B_HEADS, B_KEY_DIM, B_VAL_DIM), lambda b, t: (b, 0, 0, 0)),
                  pl.BlockSpec((1, LANES), const),
                  pl.BlockSpec((1, LANES), const),
                  pl.BlockSpec((1, B_VAL_DIM), const)],
        out_specs=(pl.BlockSpec((ts, B_WIDTH), row),
                   pl.BlockSpec((1, B_HEADS, B_KEY_DIM, B_VAL_DIM), lambda b, t: (b, 0, 0, 0))),
        out_shape=(jax.ShapeDtypeStruct((m, B_WIDTH), BF16),
                   jax.ShapeDtypeStruct(ssm_state.shape, F32)),
        scratch_shapes=[pltpu.VMEM((ts + 2 * SUBLANES, CONV_DIM), F32),
                        pltpu.VMEM((B_HEADS, B_KEY_DIM, B_VAL_DIM), F32)],
        compiler_params=_params(2), name="gdn_prompt",
    )(bqkv, z, bg, conv_w, conv_state, ssm_state, a_log_v, dtb_v, onw)


def _gdn_sample_kernel(xp_ref, z_ref, bg_ref, cw_ref, sst_ref, avec_ref, dtb_ref, onw_ref,
                       o_ref, snew_ref, vn_scr, *, t_len):
    nseq = xp_ref.shape[0]
    hist = CONV_WIDTH - 1
    ys = []
    for j in range(nseq):
        y = xp_ref[j, 0:t_len, :] * cw_ref[0:1, :]
        for i in range(1, CONV_WIDTH):
            y = y + xp_ref[j, i:i + t_len, :] * cw_ref[i:i + 1, :]
        ys.append(y)
    c = _silu(jnp.concatenate(ys, axis=0))
    avec = jnp.exp(avec_ref[...])
    onw = onw_ref[...]
    heads = _gdn_prep(c, bg_ref[...], avec, dtb_ref[...], t_len)
    for h, (qh, kh, vh, beta, dcol, drow) in enumerate(heads):
        u, w, qd, attn = _gdn_intra(qh, kh, vh, beta, dcol, drow, t_len)
        attn16 = attn.astype(BF16)
        kt = kh.T
        vn_scr[...] = jnp.zeros(vn_scr.shape, F32)
        outs = []
        for j in range(nseq):
            cs = slice(j * t_len, (j + 1) * t_len)
            last = drow[:, (j + 1) * t_len - 1:(j + 1) * t_len]
            s = sst_ref[j, h]
            wq = jnp.concatenate([w[cs], qd[cs]], axis=0).astype(BF16)
            ws_qs = _dot(wq, s.astype(BF16))
            v_new = u[cs] - ws_qs[:t_len]
            vn_scr[cs, :] = v_new
            o = ws_qs[t_len:] + _dot(attn16[cs], vn_scr[...].astype(BF16))
            kdt = kt[:, cs] * jnp.exp(last - drow[:, cs])
            snew_ref[j, h] = s * jnp.exp(last) + _dot(kdt.astype(BF16), v_new.astype(BF16))
            outs.append(o)
        o_all = jnp.concatenate(outs, axis=0)
        zh = z_ref[:, h * B_VAL_DIM:(h + 1) * B_VAL_DIM].astype(F32)
        o_ref[:, h * B_VAL_DIM:(h + 1) * B_VAL_DIM] = _gdn_out(o_all, onw, zh).astype(o_ref.dtype)


def _gdn_sample(xp, z, bg, conv_w, ssm_state, a_log_v, dtb_v, onw, *, t_len, nseq):
    nb = xp.shape[0]
    rows = nseq * t_len
    row = lambda i: (i, 0)
    const = lambda i: (0, 0)
    return pl.pallas_call(
        functools.partial(_gdn_sample_kernel, t_len=t_len),
        grid=(nb // nseq,),
        in_specs=[pl.BlockSpec((nseq, xp.shape[1], CONV_DIM), lambda i: (i, 0, 0)),
                  pl.BlockSpec((rows, B_WIDTH), row),
                  pl.BlockSpec((rows, LANES), row),
                  pl.BlockSpec((CONV_WIDTH, CONV_DIM), const),
                  pl.BlockSpec((nseq, B_HEADS, B_KEY_DIM, B_VAL_DIM), lambda i: (i, 0, 0, 0)),
                  pl.BlockSpec((1, LANES), const),
                  pl.BlockSpec((1, LANES), const),
                  pl.BlockSpec((1, B_VAL_DIM), const)],
        out_specs=(pl.BlockSpec((rows, B_WIDTH), row),
                   pl.BlockSpec((nseq, B_HEADS, B_KEY_DIM, B_VAL_DIM), lambda i: (i, 0, 0, 0))),
        out_shape=(jax.ShapeDtypeStruct((nb * t_len, B_WIDTH), BF16),
                   jax.ShapeDtypeStruct(ssm_state.shape, F32)),
        scratch_shapes=[pltpu.VMEM((rows, B_VAL_DIM), F32)],
        compiler_params=_params(1), name="gdn_sample",
    )(xp, z, bg, conv_w, ssm_state, a_log_v, dtb_v, onw)


def _out_proj_kernel(x_ref, oa_ref, ob_ref, wa_ref, wb_ref, n2w_ref, wr_ref, br_ref,
                     h_ref, n2_ref, gates_ref):
    h = x_ref[...] + _dot(oa_ref[...].astype(BF16), wa_ref[...]) + _dot(ob_ref[...].astype(BF16), wb_ref[...])
    h_ref[...] = h
    ms = jnp.mean(h * h, axis=-1, keepdims=True)
    n2 = h * lax.rsqrt(ms + RMS_EPS) * n2w_ref[...]
    n2_ref[...] = n2.astype(BF16)
    nh, nl = _split2(n2)
    wr = wr_ref[...]
    wh, wl = _split2(wr)
    logits = _dot(nh, wh) + _dot(nl, wh) + _dot(nh, wl) + br_ref[...]
    lane = lax.broadcasted_iota(jnp.int32, logits.shape, 1).astype(F32)
    big = 1e9
    gl = jnp.where(lane < N_GROUPS, logits, NEG_INF)
    gmax = jnp.max(gl, axis=-1, keepdims=True)
    gi = jnp.min(jnp.where(gl == gmax, lane, big), axis=-1, keepdims=True)
    g_sel = 1.0 / jnp.sum(jnp.exp(gl - gmax), axis=-1, keepdims=True)
    lo = N_GROUPS + EXPERTS_PER_GROUP * gi
    in_grp = (lane >= lo) & (lane < lo + EXPERTS_PER_GROUP)
    el = jnp.where(in_grp, logits, NEG_INF)
    v1 = jnp.max(el, axis=-1, keepdims=True)
    i1 = jnp.min(jnp.where(el == v1, lane, big), axis=-1, keepdims=True)
    el2 = jnp.where(lane == i1, NEG_INF, el)
    v2 = jnp.max(el2, axis=-1, keepdims=True)
    i2 = jnp.min(jnp.where(el2 == v2, lane, big), axis=-1, keepdims=True)
    e2 = jnp.exp(v2 - v1)
    w1 = g_sel / (1.0 + e2)
    w2 = g_sel * e2 / (1.0 + e2)
    gates_ref[...] = jnp.where(lane + N_GROUPS == i1, w1, 0.0) + jnp.where(lane + N_GROUPS == i2, w2, 0.0)


def _out_proj(x2d, oa, ob, wa, wb, n2w, wr, br, *, tm):
    m, d = x2d.shape
    row = lambda i: (i, 0)
    const = lambda i: (0, 0)
    return pl.pallas_call(
        _out_proj_kernel,
        grid=(m // tm,),
        in_specs=[pl.BlockSpec((tm, d), row),
                  pl.BlockSpec((tm, A_WIDTH), row),
                  pl.BlockSpec((tm, B_WIDTH), row),
                  pl.BlockSpec((A_WIDTH, d), const),
                  pl.BlockSpec((B_WIDTH, d), const),
                  pl.BlockSpec((1, d), const),
                  pl.BlockSpec((d, LANES), const),
                  pl.BlockSpec((1, LANES), const)],
        out_specs=(pl.BlockSpec((tm, d), row), pl.BlockSpec((tm, d), row), pl.BlockSpec((tm, LANES), row)),
        out_shape=(jax.ShapeDtypeStruct((m, d), F32), jax.ShapeDtypeStruct((m, d), BF16),
                   jax.ShapeDtypeStruct((m, LANES), F32)),
        compiler_params=_params(1), name="out_proj",
    )(x2d, oa, ob, wa, wb, n2w, wr, br)


def _moe_kernel(h_ref, n2_ref, gates_ref, wgu_ref, wd_ref, y_ref):
    e = pl.program_id(1)

    @pl.when(e == 0)
    def _():
        y_ref[...] = h_ref[...]

    n2 = n2_ref[...]
    gu = _dot(n2, wgu_ref[0])
    act = _silu(gu[:, :D_EXPERT]) * gu[:, D_EXPERT:]
    lane = lax.broadcasted_iota(jnp.int32, gates_ref.shape, 1)
    gate = jnp.sum(jnp.where(lane == e, gates_ref[...], 0.0), axis=-1, keepdims=True)
    y_ref[...] += gate * _dot(act.astype(BF16), wd_ref[0])


def _moe(h, n2, gates, wgu, wd, *, tm):
    m, d = h.shape
    row = lambda i, e: (i, 0)
    return pl.pallas_call(
        _moe_kernel,
        grid=(m // tm, N_EXPERTS),
        in_specs=[pl.BlockSpec((tm, d), row),
                  pl.BlockSpec((tm, d), row),
                  pl.BlockSpec((tm, LANES), row),
                  pl.BlockSpec((1, d, 2 * D_EXPERT), lambda i, e: (e, 0, 0)),
                  pl.BlockSpec((1, D_EXPERT, d), lambda i, e: (e, 0, 0))],
        out_specs=pl.BlockSpec((tm, d), row),
        out_shape=jax.ShapeDtypeStruct((m, d), F32),
        compiler_params=_params(2), name="moe",
    )(h, n2, gates, wgu, wd)


def _pad_lanes(v, offset):
    out = jnp.zeros((1, LANES), F32)
    return out.at[0, offset:offset + v.shape[0]].set(v.astype(F32))


def kernel(x_prompt, x_sample, cache_win_k, cache_win_v, state_conv, state_ssm, norm1_w, w_in, qnorm_w, knorm_w, conv_w, a_log, dt_bias, onorm_w, w_out, norm2_w, w_group, b_group, w_expert_router, b_expert_router, w_gate_up, w_down):
    nb, seq, d = x_prompt.shape
    db, t_len, _ = x_sample.shape
    n_buf = cache_win_k.shape[1]
    assert n_buf == MAX_WINDOW and seq % (MAX_DILATION * BAND) == 0 and t_len == SUBLANES
    keep = min(MAX_WINDOW, seq)

    c_bg = 3 * A_WIDTH + CONV_DIM + B_WIDTH
    w_cat = jnp.concatenate([w_in, jnp.zeros((d, LANES - 2 * B_HEADS), w_in.dtype)], axis=1).astype(BF16)
    assert w_cat.shape[1] == c_bg + LANES
    n1w = norm1_w.reshape(1, d).astype(F32)
    qw = jnp.tile(qnorm_w.astype(F32), A_HEADS).reshape(1, A_WIDTH)
    kw = jnp.tile(knorm_w.astype(F32), A_HEADS).reshape(1, A_WIDTH)
    hid = jnp.arange(A_WIDTH) // A_HEAD_DIM
    bd = (hid[:, None] == hid[None, :]).astype(BF16)
    a_log_v = _pad_lanes(a_log, B_HEADS)
    dtb_v = _pad_lanes(dt_bias, B_HEADS)
    onw = onorm_w.reshape(1, B_VAL_DIM).astype(F32)
    wa = w_out[:A_WIDTH].astype(BF16)
    wb = w_out[A_WIDTH:].astype(BF16)
    n2w = norm2_w.reshape(1, d).astype(F32)
    wr = jnp.concatenate([w_group, jnp.transpose(w_expert_router, (1, 0, 2)).reshape(d, N_EXPERTS),
                          jnp.zeros((d, LANES - N_GROUPS - N_EXPERTS), F32)], axis=1).astype(F32)
    br = jnp.zeros((1, LANES), F32).at[0, :N_GROUPS].set(b_group).at[0, N_GROUPS:N_GROUPS + N_EXPERTS].set(
        b_expert_router.reshape(-1))
    wgu = w_gate_up.astype(BF16)
    wd = w_down.astype(BF16)
    cw = conv_w.astype(F32)

    def finish(x2d, oa, ob, tm, tm_moe):
        h, n2, gates = _out_proj(x2d, oa, ob, wa, wb, n2w, wr, br, tm=tm)
        return _moe(h, n2, gates, wgu, wd, tm=tm_moe)

    xp2d = x_prompt.reshape(nb * seq, d)
    qf, kf, vf, kwin, vwin, bqkv, z, bg = _in_proj(xp2d, n1w, w_cat, qw, kw, bd, seq=seq, keep=keep,
                                                   fold=True, tm=512)
    oa = _attn_prompt(qf, kf, vf).reshape(nb * seq, A_WIDTH)
    conv0 = jnp.zeros((nb, CONV_WIDTH - 1, CONV_DIM), F32)
    ssm0 = jnp.zeros((nb, B_HEADS, B_KEY_DIM, B_VAL_DIM), F32)
    ob, ssm_prompt = _gdn_prompt(bqkv, z, bg, cw, conv0, ssm0, a_log_v, dtb_v, onw,
                                 seq=seq, ts=512, block=128, chunk=CHUNK)
    y_prompt = finish(xp2d, oa, ob, 512, 1024).reshape(nb, seq, d)
    win_k_prompt = kwin.reshape(nb, keep, A_HEADS, A_HEAD_DIM)
    win_v_prompt = vwin.reshape(nb, keep, A_HEADS, A_HEAD_DIM)
    conv_prompt = bqkv.reshape(nb, seq, CONV_DIM)[:, seq - (CONV_WIDTH - 1):]

    ms = db * t_len
    xs2d = x_sample.reshape(ms, d)
    tms = min(512, ms)
    qs, ksn, vsn, _, _, bqkv_s, z_s, bg_s = _in_proj(xs2d, n1w, w_cat, qw, kw, bd, seq=ms, keep=ms,
                                                     fold=False, tm=tms)
    cache_kt = jnp.transpose(cache_win_k.astype(F32), (0, 2, 3, 1))
    cache_vt = jnp.transpose(cache_win_v.astype(F32), (0, 2, 3, 1))
    oa_s = _attn_sample(qs, ksn, vsn, cache_kt, cache_vt, t_len=t_len)
    xpad = jnp.concatenate([state_conv.astype(F32), bqkv_s.reshape(db, t_len, CONV_DIM)], axis=1)
    nseq = 16 if db % 16 == 0 else 1
    ob_s, ssm_sample = _gdn_sample(xpad, z_s, bg_s, cw, state_ssm.astype(F32), a_log_v, dtb_v, onw,
                                   t_len=t_len, nseq=nseq)
    y_sample = finish(xs2d, oa_s, ob_s, tms, tms).reshape(db, t_len, d)
    win_k_sample = ksn.reshape(db, t_len, A_HEADS, A_HEAD_DIM)
    win_v_sample = vsn.reshape(db, t_len, A_HEADS, A_HEAD_DIM)
    conv_sample = xpad[:, t_len:]

    return (y_prompt, y_sample, win_k_prompt, win_v_prompt, conv_prompt, ssm_prompt,
            win_k_sample, win_v_sample, conv_sample, ssm_sample)
```

```python
import functools

import jax
import jax.numpy as jnp
from jax import lax
from jax.experimental import pallas as pl
from jax.experimental.pallas import tpu as pltpu
from jax.experimental.pallas import tpu_sc as plsc

F32 = jnp.float32
BF16 = jnp.bfloat16

A_HEADS = 8
A_HEAD_DIM = 64
A_WIDTH = A_HEADS * A_HEAD_DIM
DILATED_PATTERNS = ((128, 1), (512, 4), (2048, 16))
MAX_WINDOW = 2048
MAX_DILATION = 16
BAND = 128
ATTN_SCALE = A_HEAD_DIM ** -0.5
B_HEADS = 4
B_KEY_DIM = 128
B_VAL_DIM = 128
B_WIDTH = B_HEADS * B_VAL_DIM
CONV_WIDTH = 4
CONV_DIM = B_HEADS * (2 * B_KEY_DIM + B_VAL_DIM)
CHUNK = 64
N_GROUPS = 4
EXPERTS_PER_GROUP = 4
N_EXPERTS = N_GROUPS * EXPERTS_PER_GROUP
D_EXPERT = 512
RMS_EPS = 1e-6
NEG_INF = -1e30

LANES = 128
SUBLANES = 8
VMEM_LIMIT = 56 * 1024 * 1024


def _params(n_axes, vmem=VMEM_LIMIT):
    return pltpu.CompilerParams(dimension_semantics=("arbitrary",) * n_axes, vmem_limit_bytes=vmem)


def _split2(x):
    hi = x.astype(BF16)
    lo = (x - hi.astype(F32)).astype(BF16)
    return hi, lo


def _split3(x):
    hi = x.astype(BF16)
    r = x - hi.astype(F32)
    mid = r.astype(BF16)
    lo = (r - mid.astype(F32)).astype(BF16)
    return hi, mid, lo


def _dot(a, b):
    return jnp.dot(a, b, preferred_element_type=F32)


def _dot_nt(a, b):
    return lax.dot_general(a, b, (((1,), (1,)), ((), ())), preferred_element_type=F32)


def _dot_tn(a, b):
    return lax.dot_general(a, b, (((0,), (0,)), ((), ())), preferred_element_type=F32)


def _sigmoid(x):
    return 1.0 / (1.0 + jnp.exp(-x))


def _silu(x):
    return x * _sigmoid(x)


def _in_proj_kernel(x_ref, n1w_ref, w_ref, qw_ref, kw_ref, bd_ref,
                    q_ref, k_ref, v_ref, kwin_ref, vwin_ref, bqkv_ref, z_ref, bg_ref,
                    *scratch, fold):
    tm = x_ref.shape[0]
    x = x_ref[...]
    ms = jnp.mean(x * x, axis=-1, keepdims=True)
    n1 = (x * lax.rsqrt(ms + RMS_EPS) * n1w_ref[...]).astype(BF16)

    bd = bd_ref[...]

    def head_norm(t, w):
        hi, lo = _split2(t * t)
        ss = _dot(hi, bd) + _dot(lo, bd)
        return t * lax.rsqrt(ss * (1.0 / A_HEAD_DIM) + RMS_EPS) * w

    q = head_norm(_dot(n1, w_ref[:, 0:A_WIDTH]), qw_ref[...]) * ATTN_SCALE
    k = head_norm(_dot(n1, w_ref[:, A_WIDTH:2 * A_WIDTH]), kw_ref[...])
    v = _dot(n1, w_ref[:, 2 * A_WIDTH:3 * A_WIDTH])
    kwin_ref[...] = k
    vwin_ref[...] = v
    if fold:
        (scr,) = scratch
        rows = tm // MAX_DILATION
        for val, out in ((q, q_ref), (k, k_ref), (v, v_ref)):
            for c in range(A_WIDTH // LANES):
                cs = slice(c * LANES, (c + 1) * LANES)
                scr[c] = val[:, cs]
                for r in range(MAX_DILATION):
                    out[0, r, :, cs] = scr[c, pl.ds(r, rows, stride=MAX_DILATION), :]
    else:
        q_ref[...] = q
        k_ref[...] = k
        v_ref[...] = v
    c0 = 3 * A_WIDTH
    bqkv_ref[...] = _dot(n1, w_ref[:, c0:c0 + CONV_DIM])
    c1 = c0 + CONV_DIM
    z_ref[...] = _dot(n1, w_ref[:, c1:c1 + B_WIDTH]).astype(BF16)
    c2 = c1 + B_WIDTH
    bg_ref[...] = _dot(n1, w_ref[:, c2:c2 + LANES])


def _in_proj(x2d, n1w, w_cat, qw, kw, bd, *, seq, keep, fold, tm):
    m, d = x2d.shape
    nb = m // seq
    tiles_per_seq = seq // tm
    skip = (seq - keep) // tm
    keep_tiles = keep // tm
    n_cols = w_cat.shape[1]

    def win_map(i):
        return (i // tiles_per_seq) * keep_tiles + jnp.maximum(i % tiles_per_seq - skip, 0), 0

    row = lambda i: (i, 0)
    const = lambda i: (0, 0)
    if fold:
        rows = tm // MAX_DILATION
        qkv_shape = jax.ShapeDtypeStruct((nb, MAX_DILATION, seq // MAX_DILATION, A_WIDTH), F32)
        qkv_spec = pl.BlockSpec((1, MAX_DILATION, rows, A_WIDTH),
                                lambda i: (i // tiles_per_seq, 0, i % tiles_per_seq, 0))
        scratch = [pltpu.VMEM((A_WIDTH // LANES, tm, LANES), F32)]
    else:
        qkv_shape = jax.ShapeDtypeStruct((m, A_WIDTH), F32)
        qkv_spec = pl.BlockSpec((tm, A_WIDTH), row)
        scratch = []
    out_shape = (qkv_shape, qkv_shape, qkv_shape,
                 jax.ShapeDtypeStruct((nb * keep, A_WIDTH), F32),
                 jax.ShapeDtypeStruct((nb * keep, A_WIDTH), F32),
                 jax.ShapeDtypeStruct((m, CONV_DIM), F32),
                 jax.ShapeDtypeStruct((m, B_WIDTH), BF16),
                 jax.ShapeDtypeStruct((m, LANES), F32))
    out_specs = (qkv_spec, qkv_spec, qkv_spec,
                 pl.BlockSpec((tm, A_WIDTH), win_map),
                 pl.BlockSpec((tm, A_WIDTH), win_map),
                 pl.BlockSpec((tm, CONV_DIM), row),
                 pl.BlockSpec((tm, B_WIDTH), row),
                 pl.BlockSpec((tm, LANES), row))
    in_specs = [pl.BlockSpec((tm, d), row),
                pl.BlockSpec((1, d), const),
                pl.BlockSpec((d, n_cols), const),
                pl.BlockSpec((1, A_WIDTH), const),
                pl.BlockSpec((1, A_WIDTH), const),
                pl.BlockSpec((A_WIDTH, A_WIDTH), const)]
    return pl.pallas_call(
        functools.partial(_in_proj_kernel, fold=fold),
        grid=(m // tm,), in_specs=in_specs, out_specs=out_specs, out_shape=out_shape,
        scratch_shapes=scratch, compiler_params=_params(1), name="in_proj",
    )(x2d, n1w, w_cat, qw, kw, bd)


def _band_mask(pieces):
    sub = BAND // pieces
    r = lax.broadcasted_iota(jnp.int32, (BAND, 2 * BAND), 0)
    c = lax.broadcasted_iota(jnp.int32, (BAND, 2 * BAND), 1)
    qpos = (r % sub) * pieces + r // sub + BAND
    cc = c % BAND
    kpos = (cc % sub) * pieces + cc // sub + (c // BAND) * BAND
    dist = qpos - kpos
    return (dist >= 0) & (dist <= BAND), c >= BAND


ATTN_UNROLL = 8


def _attn_prompt_kernel(q_ref, k_ref, v_ref, o_ref, p_scr, op_scr, lse_scr, nat_scr):
    u_len = q_ref.shape[2]
    lane = lax.broadcasted_iota(jnp.int32, (BAND, LANES), 1)
    head0 = lane < A_HEAD_DIM

    for pi, pieces in enumerate((16, 4, 1)):
        sub = BAND // pieces
        n_res = MAX_DILATION // pieces
        n_blk = u_len // sub
        band, is_cur = _band_mask(pieces)
        band2 = jnp.concatenate([band, band], axis=0)
        is_cur2 = jnp.concatenate([is_cur, is_cur], axis=0)

        def locate(blk, n_blk=n_blk, sub=sub):
            res = blk // n_blk
            m = blk % n_blk
            cur = pl.multiple_of(m * sub, SUBLANES)
            prv = pl.multiple_of(jnp.maximum(m - 1, 0) * sub, SUBLANES)
            return res, m, cur, prv

        def gather(ref, res, start, pieces=pieces, n_res=n_res, sub=sub):
            parts = [ref[0, a * n_res + res, pl.ds(start, sub), :] for a in range(pieces)]
            return parts[0] if pieces == 1 else jnp.concatenate(parts, axis=0)

        def scatter(ref, res, start, val, pi=pi, pieces=pieces, n_res=n_res, sub=sub):
            for a in range(pieces):
                ref[pi, a * n_res + res, pl.ds(start, sub), :] = val[a * sub:(a + 1) * sub]

        def probs(it, carry, band2=band2, is_cur2=is_cur2):
            for j in range(ATTN_UNROLL):
                blk = it * ATTN_UNROLL + j
                res, m, cur, prv = locate(blk)
                qb = gather(q_ref, res, cur)
                q2 = jnp.concatenate([jnp.where(head0, qb, 0.0), jnp.where(head0, 0.0, qb)], axis=0).astype(BF16)
                kb = jnp.concatenate([gather(k_ref, res, prv), gather(k_ref, res, cur)], axis=0).astype(BF16)
                s = _dot_nt(q2, kb)
                s = jnp.where(band2 & (is_cur2 | (m > 0)), s, NEG_INF)
                mx = jnp.max(s, axis=-1, keepdims=True)
                p = jnp.exp(s - mx)
                l = jnp.sum(p, axis=-1, keepdims=True)
                p_scr[blk] = (p * (1.0 / l)).astype(BF16)
                lse = mx + jnp.log(l)
                lse = jnp.where(head0, jnp.broadcast_to(lse[:BAND], (BAND, LANES)),
                                jnp.broadcast_to(lse[BAND:], (BAND, LANES)))
                scatter(lse_scr, res, cur, lse)
            return carry

        def values(it, carry):
            for j in range(ATTN_UNROLL):
                blk = it * ATTN_UNROLL + j
                res, m, cur, prv = locate(blk)
                vb = jnp.concatenate([gather(v_ref, res, prv), gather(v_ref, res, cur)], axis=0).astype(BF16)
                o2 = _dot(p_scr[blk], vb)
                scatter(op_scr, res, cur, jnp.where(head0, o2[:BAND], o2[BAND:]))
            return carry

        n_it = n_res * n_blk // ATTN_UNROLL
        lax.fori_loop(0, n_it, probs, 0)
        lax.fori_loop(0, n_it, values, 0)

    for r in range(MAX_DILATION):
        l0, l1, l2 = lse_scr[0, r], lse_scr[1, r], lse_scr[2, r]
        mx = jnp.maximum(jnp.maximum(l0, l1), l2)
        e0, e1, e2 = jnp.exp(l0 - mx), jnp.exp(l1 - mx), jnp.exp(l2 - mx)
        out = (e0 * op_scr[0, r] + e1 * op_scr[1, r] + e2 * op_scr[2, r]) / (e0 + e1 + e2)
        nat_scr[pl.ds(r, u_len, stride=MAX_DILATION), :] = out
    o_ref[0] = nat_scr[...].astype(BF16)


def _attn_prompt(qf, kf, vf):
    nb, _, u_len, _ = qf.shape
    seq = u_len * MAX_DILATION
    n_pairs = A_WIDTH // LANES
    spec = pl.BlockSpec((1, MAX_DILATION, u_len, LANES), lambda b, p: (b, 0, 0, p))
    return pl.pallas_call(
        _attn_prompt_kernel,
        grid=(nb, n_pairs),
        in_specs=[spec, spec, spec],
        out_specs=pl.BlockSpec((1, seq, LANES), lambda b, p: (b, 0, p)),
        out_shape=jax.ShapeDtypeStruct((nb, seq, A_WIDTH), BF16),
        scratch_shapes=[pltpu.VMEM((seq // BAND, 2 * BAND, 2 * BAND), BF16),
                        pltpu.VMEM((3, MAX_DILATION, u_len, LANES), F32),
                        pltpu.VMEM((3, MAX_DILATION, u_len, LANES), F32),
                        pltpu.VMEM((seq, LANES), F32)],
        compiler_params=_params(2), name="attn_prompt",
    )(qf, kf, vf)


def _multiplicity(t, dd):
    cnt = jnp.zeros(dd.shape, F32)
    for w, d in DILATED_PATTERNS:
        ok = (dd >= 0) & (dd <= w) & (dd % d == 0)
        cnt = cnt + jnp.where(ok, 1.0, 0.0)
    return cnt


def _attn_sample_kernel(q_ref, kn_ref, vn_ref, kt_ref, vt_ref, o_ref):
    t_len = q_ref.shape[0]
    n_buf = kt_ref.shape[3]
    q = q_ref[...]
    kn = kn_ref[...]
    vn = vn_ref[...]
    tq = n_buf + lax.broadcasted_iota(jnp.int32, (t_len, n_buf), 0)
    cnt = _multiplicity(None, tq - lax.broadcasted_iota(jnp.int32, (t_len, n_buf), 1))
    cnt_new = _multiplicity(None, lax.broadcasted_iota(jnp.int32, (t_len, t_len), 0)
                            - lax.broadcasted_iota(jnp.int32, (t_len, t_len), 1))
    outs = []
    for h in range(A_HEADS):
        hs = slice(h * A_HEAD_DIM, (h + 1) * A_HEAD_DIM)
        qh = q[:, hs].astype(BF16)
        s = jnp.where(cnt > 0, _dot(qh, kt_ref[0, h].astype(BF16)), NEG_INF)
        sn = jnp.where(cnt_new > 0, _dot_nt(qh, kn[:, hs].astype(BF16)), NEG_INF)
        m = jnp.maximum(jnp.max(s, axis=-1, keepdims=True), jnp.max(sn, axis=-1, keepdims=True))
        p = cnt * jnp.exp(s - m)
        pn = cnt_new * jnp.exp(sn - m)
        l = jnp.sum(p, axis=-1, keepdims=True) + jnp.sum(pn, axis=-1, keepdims=True)
        o = _dot_nt(p.astype(BF16), vt_ref[0, h].astype(BF16)) + _dot(pn.astype(BF16), vn[:, hs].astype(BF16))
        outs.append(o / l)
    o_ref[...] = jnp.concatenate(outs, axis=-1)


def _attn_sample(q2d, kn2d, vn2d, cache_kt, cache_vt, *, t_len):
    nb, _, _, n_buf = cache_kt.shape
    tok = pl.BlockSpec((t_len, A_WIDTH), lambda b: (b, 0))
    win = pl.BlockSpec((1, A_HEADS, A_HEAD_DIM, n_buf), lambda b: (b, 0, 0, 0))
    return pl.pallas_call(
        _attn_sample_kernel,
        grid=(nb,),
        in_specs=[tok, tok, tok, win, win],
        out_specs=tok,
        out_shape=jax.ShapeDtypeStruct((nb * t_len, A_WIDTH), F32),
        compiler_params=_params(1), name="attn_sample",
    )(q2d, kn2d, vn2d, cache_kt, cache_vt)


def _unit_lower_inverse(a, n_chunk):
    n = a.shape[0]
    ri = lax.broadcasted_iota(jnp.int32, (n, n), 0)
    ci = lax.broadcasted_iota(jnp.int32, (n, n), 1)
    eye = jnp.where(ri == ci, 1.0, 0.0).astype(F32)
    t = eye - a
    pw = a
    k = 2
    while k < n_chunk:
        pw16 = pw.astype(BF16)
        pw = _dot(pw16, pw16)
        t = t + _dot(t.astype(BF16), pw.astype(BF16))
        k *= 2
    return t


def _gdn_intra(q, k, v, beta, dcol, drow, chunk):
    n = q.shape[0]
    ri = lax.broadcasted_iota(jnp.int32, (n, n), 0)
    ci = lax.broadcasted_iota(jnp.int32, (n, n), 1)
    same = (ri // chunk) == (ci // chunk)
    lower = same & (ri >= ci)
    strict = same & (ri > ci)
    gam = jnp.exp(jnp.where(lower, dcol - drow, NEG_INF))
    kb = k * beta
    vb = v * beta
    k16 = k.astype(BF16)
    a_mat = jnp.where(strict, _dot_nt(kb.astype(BF16), k16) * gam, 0.0)
    attn = jnp.where(lower, _dot_nt(q.astype(BF16), k16) * gam, 0.0)
    t_inv = _unit_lower_inverse(a_mat, chunk).astype(BF16)
    ed = jnp.exp(dcol)
    u = _dot(t_inv, vb.astype(BF16))
    w = _dot(t_inv, (kb * ed).astype(BF16))
    qd = q * ed
    return u, w, qd, attn


def _gdn_prep(c, bg, avec, dtb, chunk):
    n = c.shape[0]
    lane = lax.broadcasted_iota(jnp.int32, bg.shape, 1)
    xg = bg + dtb
    sp = jnp.maximum(xg, 0.0) + jnp.log1p(jnp.exp(-jnp.abs(xg)))
    g = jnp.where((lane >= B_HEADS) & (lane < 2 * B_HEADS), -avec * sp, 0.0)
    beta_all = _sigmoid(bg)
    ri = lax.broadcasted_iota(jnp.int32, (n, n), 0)
    ci = lax.broadcasted_iota(jnp.int32, (n, n), 1)
    same = (ri // chunk) == (ci // chunk)
    tri = jnp.where(same & (ri >= ci), 1.0, 0.0).astype(BF16)
    gh, gm, gl = _split3(g)
    dcol = _dot(tri, gh) + _dot(tri, gm) + _dot(tri, gl)
    gt = g.T
    th, tm_, tl = _split3(gt)
    drow = _dot_nt(th, tri) + _dot_nt(tm_, tri) + _dot_nt(tl, tri)
    heads = []
    kd_ = B_KEY_DIM
    for h in range(B_HEADS):
        qh = c[:, h * kd_:(h + 1) * kd_]
        kh = c[:, B_HEADS * kd_ + h * kd_:B_HEADS * kd_ + (h + 1) * kd_]
        vh = c[:, 2 * B_HEADS * kd_ + h * B_VAL_DIM:2 * B_HEADS * kd_ + (h + 1) * B_VAL_DIM]
        qh = qh * lax.rsqrt(jnp.sum(qh * qh, axis=-1, keepdims=True) + RMS_EPS) * (B_KEY_DIM ** -0.5)
        kh = kh * lax.rsqrt(jnp.sum(kh * kh, axis=-1, keepdims=True) + RMS_EPS)
        heads.append((qh, kh, vh, beta_all[:, h:h + 1], dcol[:, B_HEADS + h:B_HEADS + h + 1],
                      drow[B_HEADS + h:B_HEADS + h + 1, :]))
    return heads


def _gdn_out(o, onw, zh):
    ms = jnp.mean(o * o, axis=-1, keepdims=True)
    return o * lax.rsqrt(ms + RMS_EPS) * onw * _silu(zh)


def _gdn_prompt_kernel(x_ref, z_ref, bg_ref, cw_ref, cst_ref, sst_ref, avec_ref, dtb_ref, onw_ref,
                       o_ref, snew_ref, xp_scr, s_scr, *, block, chunk):
    ts = x_ref.shape[0]
    t = pl.program_id(1)
    pad = SUBLANES
    hist = CONV_WIDTH - 1

    @pl.when(t == 0)
    def _():
        xp_scr[0:pad, :] = jnp.zeros((pad, CONV_DIM), F32)
        xp_scr[pad - hist:pad, :] = cst_ref[0]
        s_scr[...] = sst_ref[0]

    xp_scr[pad:pad + ts, :] = x_ref[...]
    y = xp_scr[pl.ds(pad - hist, ts), :] * cw_ref[0:1, :]
    for i in range(1, CONV_WIDTH):
        y = y + xp_scr[pl.ds(pad - hist + i, ts), :] * cw_ref[i:i + 1, :]
    xp_scr[0:pad, :] = xp_scr[ts:ts + pad, :]
    c = _silu(y)
    avec = jnp.exp(avec_ref[...])
    onw = onw_ref[...]
    n_chunks = block // chunk
    n_sb = ts // block
    ri = lax.broadcasted_iota(jnp.int32, (block, block), 0)
    ci = lax.broadcasted_iota(jnp.int32, (block, block), 1)
    same = (ri // chunk) == (ci // chunk)
    lower = same & (ri >= ci)
    strict = same & (ri > ci)
    eye = jnp.where(ri == ci, 1.0, 0.0).astype(F32)

    units = []
    for sb in range(n_sb):
        rs = slice(sb * block, (sb + 1) * block)
        heads = _gdn_prep(c[rs], bg_ref[rs, :], avec, dtb_ref[...], chunk)
        for h, (qh, kh, vh, beta, dcol, drow) in enumerate(heads):
            units.append(dict(sb=sb, h=h, q=qh, k=kh, v=vh, beta=beta, dcol=dcol, drow=drow))

    for u in units:
        gam = jnp.exp(jnp.where(lower, u["dcol"] - u["drow"], NEG_INF))
        kb = u["k"] * u["beta"]
        kq = _dot_nt(jnp.concatenate([kb, u["q"]], axis=0).astype(BF16), u["k"].astype(BF16))
        u["pw"] = jnp.where(strict, kq[:block] * gam, 0.0)
        u["t"] = eye - u["pw"]
        u["attn"] = jnp.where(lower, kq[block:] * gam, 0.0).astype(BF16)
        ed = jnp.exp(u["dcol"])
        u["rhs"] = jnp.concatenate([u["v"] * u["beta"], kb * ed], axis=1).astype(BF16)
        u["qd"] = u["q"] * ed
        u["kt"] = u["k"].T

    kk = 2
    while kk < chunk:
        for u in units:
            p16 = u["pw"].astype(BF16)
            u["pw"] = _dot(p16, p16)
        for u in units:
            u["t"] = u["t"] + _dot(u["t"].astype(BF16), u["pw"].astype(BF16))
        kk *= 2

    for u in units:
        u["uw"] = _dot(u["t"].astype(BF16), u["rhs"]).astype(BF16)
    for u in units:
        au_aw = _dot(u["attn"], u["uw"])
        u["op"] = au_aw[:, :B_VAL_DIM]
        u["qp"] = (u["qd"] - au_aw[:, B_VAL_DIM:]).astype(BF16)
    for u in units:
        u["n"], u["mw"], u["e"] = [], [], []
        for cc in range(n_chunks):
            cs = slice(cc * chunk, (cc + 1) * chunk)
            last = u["drow"][:, (cc + 1) * chunk - 1:(cc + 1) * chunk]
            kdt = (u["kt"][:, cs] * jnp.exp(last - u["drow"][:, cs])).astype(BF16)
            nm = _dot(kdt, u["uw"][cs])
            u["n"].append(nm[:, :B_VAL_DIM])
            u["mw"].append(nm[:, B_VAL_DIM:].astype(BF16))
            u["e"].append(jnp.exp(last))

    s = [s_scr[h] for h in range(B_HEADS)]
    for sb in range(n_sb):
        rs = slice(sb * block, (sb + 1) * block)
        outs = [[] for _ in range(B_HEADS)]
        for cc in range(n_chunks):
            cs = slice(cc * chunk, (cc + 1) * chunk)
            for h in range(B_HEADS):
                u = units[sb * B_HEADS + h]
                r = _dot(jnp.concatenate([u["mw"][cc], u["qp"][cs]], axis=0), s[h].astype(BF16))
                outs[h].append(r[B_KEY_DIM:] + u["op"][cs])
                s[h] = s[h] * u["e"][cc] - r[:B_KEY_DIM] + u["n"][cc]
        for h in range(B_HEADS):
            o_all = jnp.concatenate(outs[h], axis=0) if n_chunks > 1 else outs[h][0]
            zh = z_ref[rs, h * B_VAL_DIM:(h + 1) * B_VAL_DIM].astype(F32)
            o_ref[rs, h * B_VAL_DIM:(h + 1) * B_VAL_DIM] = _gdn_out(o_all, onw, zh).astype(o_ref.dtype)
    for h in range(B_HEADS):
        s_scr[h] = s[h]

    @pl.when(t == pl.num_programs(1) - 1)
    def _():
        snew_ref[0] = s_scr[...]


def _gdn_prompt(bqkv, z, bg, conv_w, conv_state, ssm_state, a_log_v, dtb_v, onw, *, seq, ts, block, chunk):
    m = bqkv.shape[0]
    nb = m // seq
    tps = seq // ts
    row = lambda b, t: (b * tps + t, 0)
    const = lambda b, t: (0, 0)
    return pl.pallas_call(
        functools.partial(_gdn_prompt_kernel, block=block, chunk=chunk),
        grid=(nb, tps),
        in_specs=[pl.BlockSpec((ts, CONV_DIM), row),
                  pl.BlockSpec((ts, B_WIDTH), row),
                  pl.BlockSpec((ts, LANES), row),
                  pl.BlockSpec((CONV_WIDTH, CONV_DIM), const),
                  pl.BlockSpec((1, CONV_WIDTH - 1, CONV_DIM), lambda b, t: (b, 0, 0)),
                  pl.BlockSpec((1, B_HEADS, B_KEY_DIM, B_VAL_DIM), lambda b, t: (b, 0, 0, 0)),
                  pl.BlockSpec((1, LANES), const),
                  pl.BlockSpec((1, LANES), const),
                  pl.BlockSpec((1, B_VAL_DIM), const)],
        out_specs=(pl.BlockSpec((ts, B_WIDTH), row),
                   pl.BlockSpec((1, B_HEADS, B_KEY_DIM, B_VAL_DIM), lambda b, t: (b, 0, 0, 0))),
        out_shape=(jax.ShapeDtypeStruct((m, B_WIDTH), BF16),
                   jax.ShapeDtypeStruct(ssm_state.shape, F32)),
        scratch_shapes=[pltpu.VMEM((ts + 2 * SUBLANES, CONV_DIM), F32),
                        pltpu.VMEM((B_HEADS, B_KEY_DIM, B_VAL_DIM), F32)],
        compiler_params=_params(2), name="gdn_prompt",
    )(bqkv, z, bg, conv_w, conv_state, ssm_state, a_log_v, dtb_v, onw)


def _gdn_sample_kernel(xp_ref, z_ref, bg_ref, cw_ref, sst_ref, avec_ref, dtb_ref, onw_ref,
                       o_ref, snew_ref, vn_scr, *, t_len):
    nseq = xp_ref.shape[0]
    hist = CONV_WIDTH - 1
    ys = []
    for j in range(nseq):
        y = xp_ref[j, 0:t_len, :] * cw_ref[0:1, :]
        for i in range(1, CONV_WIDTH):
            y = y + xp_ref[j, i:i + t_len, :] * cw_ref[i:i + 1, :]
        ys.append(y)
    c = _silu(jnp.concatenate(ys, axis=0))
    avec = jnp.exp(avec_ref[...])
    onw = onw_ref[...]
    heads = _gdn_prep(c, bg_ref[...], avec, dtb_ref[...], t_len)
    for h, (qh, kh, vh, beta, dcol, drow) in enumerate(heads):
        u, w, qd, attn = _gdn_intra(qh, kh, vh, beta, dcol, drow, t_len)
        attn16 = attn.astype(BF16)
        kt = kh.T
        vn_scr[...] = jnp.zeros(vn_scr.shape, F32)
        outs = []
        for j in range(nseq):
            cs = slice(j * t_len, (j + 1) * t_len)
            last = drow[:, (j + 1) * t_len - 1:(j + 1) * t_len]
            s = sst_ref[j, h]
            wq = jnp.concatenate([w[cs], qd[cs]], axis=0).astype(BF16)
            ws_qs = _dot(wq, s.astype(BF16))
            v_new = u[cs] - ws_qs[:t_len]
            vn_scr[cs, :] = v_new
            o = ws_qs[t_len:] + _dot(attn16[cs], vn_scr[...].astype(BF16))
            kdt = kt[:, cs] * jnp.exp(last - drow[:, cs])
            snew_ref[j, h] = s * jnp.exp(last) + _dot(kdt.astype(BF16), v_new.astype(BF16))
            outs.append(o)
        o_all = jnp.concatenate(outs, axis=0)
        zh = z_ref[:, h * B_VAL_DIM:(h + 1) * B_VAL_DIM].astype(F32)
        o_ref[:, h * B_VAL_DIM:(h + 1) * B_VAL_DIM] = _gdn_out(o_all, onw, zh).astype(o_ref.dtype)


def _gdn_sample(xp, z, bg, conv_w, ssm_state, a_log_v, dtb_v, onw, *, t_len, nseq):
    nb = xp.shape[0]
    rows = nseq * t_len
    row = lambda i: (i, 0)
    const = lambda i: (0, 0)
    return pl.pallas_call(
        functools.partial(_gdn_sample_kernel, t_len=t_len),
        grid=(nb // nseq,),
        in_specs=[pl.BlockSpec((nseq, xp.shape[1], CONV_DIM), lambda i: (i, 0, 0)),
                  pl.BlockSpec((rows, B_WIDTH), row),
                  pl.BlockSpec((rows, LANES), row),
                  pl.BlockSpec((CONV_WIDTH, CONV_DIM), const),
                  pl.BlockSpec((nseq, B_HEADS, B_KEY_DIM, B_VAL_DIM), lambda i: (i, 0, 0, 0)),
                  pl.BlockSpec((1, LANES), const),
                  pl.BlockSpec((1, LANES), const),
                  pl.BlockSpec((1, B_VAL_DIM), const)],
        out_specs=(pl.BlockSpec((rows, B_WIDTH), row),
                   pl.BlockSpec((nseq, B_HEADS, B_KEY_DIM, B_VAL_DIM), lambda i: (i, 0, 0, 0))),
        out_shape=(jax.ShapeDtypeStruct((nb * t_len, B_WIDTH), BF16),
                   jax.ShapeDtypeStruct(ssm_state.shape, F32)),
        scratch_shapes=[pltpu.VMEM((rows, B_VAL_DIM), F32)],
        compiler_params=_params(1), name="gdn_sample",
    )(xp, z, bg, conv_w, ssm_state, a_log_v, dtb_v, onw)


def _pack_bf16_pairs(x):
    n = x.shape[1] // 2
    bits = pltpu.bitcast(x.astype(BF16).astype(F32), jnp.int32)
    return lax.shift_right_logical(bits[:, :n], 16) | (bits[:, n:] & jnp.int32(-65536))


def _unpack_bf16_pairs(w):
    lo = pltpu.bitcast(lax.shift_left(w, 16), F32)
    hi = pltpu.bitcast(w & jnp.int32(-65536), F32)
    return jnp.concatenate([lo, hi], axis=1)


def _out_proj_kernel(x_ref, oa_ref, ob_ref, wa_ref, wb_ref, n2w_ref, wr_ref, br_ref, *rest, sparse):
    if sparse:
        tri_ref, h_ref, n2_ref, gates_ref, cnt_ref = rest
    else:
        h_ref, n2_ref, gates_ref = rest
    h = x_ref[...] + _dot(oa_ref[...].astype(BF16), wa_ref[...]) + _dot(ob_ref[...].astype(BF16), wb_ref[...])
    h_ref[...] = h
    ms = jnp.mean(h * h, axis=-1, keepdims=True)
    n2 = h * lax.rsqrt(ms + RMS_EPS) * n2w_ref[...]
    if sparse:
        n2_ref[...] = _pack_bf16_pairs(n2)
    else:
        n2_ref[...] = n2.astype(BF16)
    nh, nl = _split2(n2)
    wr = wr_ref[...]
    wh, wl = _split2(wr)
    logits = _dot(nh, wh) + _dot(nl, wh) + _dot(nh, wl) + br_ref[...]
    lane = lax.broadcasted_iota(jnp.int32, logits.shape, 1).astype(F32)
    big = 1e9
    gl = jnp.where(lane < N_GROUPS, logits, NEG_INF)
    gmax = jnp.max(gl, axis=-1, keepdims=True)
    gi = jnp.min(jnp.where(gl == gmax, lane, big), axis=-1, keepdims=True)
    g_sel = 1.0 / jnp.sum(jnp.exp(gl - gmax), axis=-1, keepdims=True)
    lo = N_GROUPS + EXPERTS_PER_GROUP * gi
    in_grp = (lane >= lo) & (lane < lo + EXPERTS_PER_GROUP)
    el = jnp.where(in_grp, logits, NEG_INF)
    v1 = jnp.max(el, axis=-1, keepdims=True)
    i1 = jnp.min(jnp.where(el == v1, lane, big), axis=-1, keepdims=True)
    el2 = jnp.where(lane == i1, NEG_INF, el)
    v2 = jnp.max(el2, axis=-1, keepdims=True)
    i2 = jnp.min(jnp.where(el2 == v2, lane, big), axis=-1, keepdims=True)
    e2 = jnp.exp(v2 - v1)
    w1 = g_sel / (1.0 + e2)
    w2 = g_sel * e2 / (1.0 + e2)
    if not sparse:
        gates_ref[...] = jnp.where(lane + N_GROUPS == i1, w1, 0.0) + jnp.where(lane + N_GROUPS == i2, w2, 0.0)
        return
    @pl.when(pl.program_id(0) == 0)
    def _():
        cnt_ref[...] = jnp.zeros(cnt_ref.shape, F32)

    e0 = i1 - N_GROUPS
    e1 = i2 - N_GROUPS
    onehot = jnp.where(lane == e0, 1.0, 0.0) + jnp.where(lane == e1, 1.0, 0.0)
    before = _dot(tri_ref[...], onehot.astype(BF16)) + cnt_ref[...]
    r0 = jnp.sum(jnp.where(lane == e0, before, 0.0), axis=-1, keepdims=True)
    r1 = jnp.sum(jnp.where(lane == e1, before, 0.0), axis=-1, keepdims=True)
    cnt_ref[...] += jnp.sum(onehot, axis=0, keepdims=True)
    rec = jnp.zeros(logits.shape, F32)
    for col, val in enumerate((e0, e1, w1, w2, r0, r1)):
        rec = jnp.where(lane == col, val, rec)
    gates_ref[...] = rec


def _out_proj(x2d, oa, ob, wa, wb, n2w, wr, br, *, tm, sparse):
    m, d = x2d.shape
    row = lambda i: (i, 0)
    const = lambda i: (0, 0)
    in_specs = [pl.BlockSpec((tm, d), row),
                pl.BlockSpec((tm, A_WIDTH), row),
                pl.BlockSpec((tm, B_WIDTH), row),
                pl.BlockSpec((A_WIDTH, d), const),
                pl.BlockSpec((B_WIDTH, d), const),
                pl.BlockSpec((1, d), const),
                pl.BlockSpec((d, LANES), const),
                pl.BlockSpec((1, LANES), const)]
    args = [x2d, oa, ob, wa, wb, n2w, wr, br]
    if sparse:
        idx = jnp.arange(tm)
        args.append((idx[:, None] > idx[None, :]).astype(BF16))
        in_specs.append(pl.BlockSpec((tm, tm), const))
        out_specs = (pl.BlockSpec((tm, d), row), pl.BlockSpec((tm, d // 2), row), pl.BlockSpec((tm, LANES), row),
                     pl.BlockSpec((1, LANES), const))
        out_shape = (jax.ShapeDtypeStruct((m, d), F32), jax.ShapeDtypeStruct((m, d // 2), jnp.int32),
                     jax.ShapeDtypeStruct((m, LANES), F32), jax.ShapeDtypeStruct((1, LANES), F32))
    else:
        out_specs = (pl.BlockSpec((tm, d), row), pl.BlockSpec((tm, d), row), pl.BlockSpec((tm, LANES), row))
        out_shape = (jax.ShapeDtypeStruct((m, d), F32), jax.ShapeDtypeStruct((m, d), BF16),
                     jax.ShapeDtypeStruct((m, LANES), F32))
    return pl.pallas_call(
        functools.partial(_out_proj_kernel, sparse=sparse),
        grid=(m // tm,), in_specs=in_specs, out_specs=out_specs, out_shape=out_shape,
        compiler_params=_params(1), name="out_proj",
    )(*args)


def _moe_kernel(h_ref, n2_ref, gates_ref, wgu_ref, wd_ref, y_ref):
    e = pl.program_id(1)

    @pl.when(e == 0)
    def _():
        y_ref[...] = h_ref[...]

    n2 = n2_ref[...]
    gu = _dot(n2, wgu_ref[0])
    act = _silu(gu[:, :D_EXPERT]) * gu[:, D_EXPERT:]
    lane = lax.broadcasted_iota(jnp.int32, gates_ref.shape, 1)
    gate = jnp.sum(jnp.where(lane == e, gates_ref[...], 0.0), axis=-1, keepdims=True)
    y_ref[...] += gate * _dot(act.astype(BF16), wd_ref[0])


def _moe(h, n2, gates, wgu, wd, *, tm):
    m, d = h.shape
    row = lambda i, e: (i, 0)
    return pl.pallas_call(
        _moe_kernel,
        grid=(m // tm, N_EXPERTS),
        in_specs=[pl.BlockSpec((tm, d), row),
                  pl.BlockSpec((tm, d), row),
                  pl.BlockSpec((tm, LANES), row),
                  pl.BlockSpec((1, d, 2 * D_EXPERT), lambda i, e: (e, 0, 0)),
                  pl.BlockSpec((1, D_EXPERT, d), lambda i, e: (e, 0, 0))],
        out_specs=pl.BlockSpec((tm, d), row),
        out_shape=jax.ShapeDtypeStruct((m, d), F32),
        compiler_params=_params(2), name="moe",
    )(h, n2, gates, wgu, wd)


SC_CORES = 2
SC_SUBCORES = 16
SC_WORKERS = SC_CORES * SC_SUBCORES
SC_CHUNK = 64
MOE_TILE = 512


def _sc_mesh():
    return plsc.VectorSubcoreMesh(core_axis_name="c", subcore_axis_name="s",
                                  num_cores=SC_CORES, num_subcores=SC_SUBCORES)


def _sc_dispatch(src, pos, n_rows):
    m, d = src.shape
    n_chunks = m // (SC_WORKERS * SC_CHUNK)
    pos4 = pos.reshape(2, SC_WORKERS, n_chunks, SC_CHUNK).transpose(1, 0, 2, 3)

    def body(src_hbm, pos_hbm, out_hbm, idx_v, rows_v):
        wid = lax.axis_index("s") * SC_CORES + lax.axis_index("c")
        pltpu.sync_copy(pos_hbm.at[wid], idx_v)

        @pl.loop(0, n_chunks)
        def _(j):
            start = pl.multiple_of((wid * n_chunks + j) * SC_CHUNK, SC_CHUNK)
            pltpu.sync_copy(src_hbm.at[pl.ds(start, SC_CHUNK)], rows_v)
            pltpu.sync_copy(rows_v, out_hbm.at[idx_v.at[0, j]])
            pltpu.sync_copy(rows_v, out_hbm.at[idx_v.at[1, j]])

    return pl.kernel(
        body, out_type=jax.ShapeDtypeStruct((n_rows, d), src.dtype), mesh=_sc_mesh(),
        scratch_types=[pltpu.VMEM((2, n_chunks, SC_CHUNK), jnp.int32), pltpu.VMEM((SC_CHUNK, d), src.dtype)],
        name="moe_dispatch",
    )(src, pos4)


def _sc_gather(table, idx):
    b = idx.shape[0]
    d = table.shape[1]
    n_chunks = b // (SC_WORKERS * SC_CHUNK)
    idx3 = idx.reshape(SC_WORKERS, n_chunks, SC_CHUNK)

    def body(table_hbm, idx_hbm, out_hbm, idx_v, rows_v):
        wid = lax.axis_index("s") * SC_CORES + lax.axis_index("c")
        pltpu.sync_copy(idx_hbm.at[wid], idx_v)

        @pl.loop(0, n_chunks)
        def _(j):
            start = pl.multiple_of((wid * n_chunks + j) * SC_CHUNK, SC_CHUNK)
            pltpu.sync_copy(table_hbm.at[idx_v.at[j]], rows_v)
            pltpu.sync_copy(rows_v, out_hbm.at[pl.ds(start, SC_CHUNK)])

    return pl.kernel(
        body, out_type=jax.ShapeDtypeStruct((b, d), table.dtype), mesh=_sc_mesh(),
        scratch_types=[pltpu.VMEM((n_chunks, SC_CHUNK), jnp.int32), pltpu.VMEM((SC_CHUNK, d), table.dtype)],
        name="moe_collect",
    )(table, idx3)


def _moe_grouped_kernel(te_ref, nu_ref, xs_ref, wgu_ref, wd_ref, ys_ref):
    i = pl.program_id(0)

    @pl.when(i < nu_ref[0])
    def _():
        x = _unpack_bf16_pairs(xs_ref[...]).astype(BF16)
        gu = _dot(x, wgu_ref[0])
        act = _silu(gu[:, :D_EXPERT]) * gu[:, D_EXPERT:]
        ys_ref[...] = _pack_bf16_pairs(_dot(act.astype(BF16), wd_ref[0]))

    @pl.when(i >= nu_ref[0])
    def _():
        ys_ref[...] = jnp.zeros(ys_ref.shape, ys_ref.dtype)


def _moe_grouped(xs, tile_expert, n_used, wgu, wd):
    r, half = xs.shape
    d = 2 * half
    grid_spec = pltpu.PrefetchScalarGridSpec(
        num_scalar_prefetch=2, grid=(r // MOE_TILE,),
        in_specs=[pl.BlockSpec((MOE_TILE, half), lambda i, te, nu: (i, 0)),
                  pl.BlockSpec((1, d, 2 * D_EXPERT), lambda i, te, nu: (te[i], 0, 0)),
                  pl.BlockSpec((1, D_EXPERT, d), lambda i, te, nu: (te[i], 0, 0))],
        out_specs=pl.BlockSpec((MOE_TILE, half), lambda i, te, nu: (i, 0)))
    return pl.pallas_call(
        _moe_grouped_kernel, grid_spec=grid_spec,
        out_shape=jax.ShapeDtypeStruct((r, half), jnp.int32),
        compiler_params=_params(1), name="moe_grouped",
    )(tile_expert, n_used, xs, wgu, wd)


def _moe_combine_kernel(h_ref, z0_ref, z1_ref, rec_ref, y_ref):
    rec = rec_ref[...]
    y_ref[...] = (h_ref[...] + rec[:, 2:3] * _unpack_bf16_pairs(z0_ref[...])
                  + rec[:, 3:4] * _unpack_bf16_pairs(z1_ref[...]))


def _moe_combine(h, z, rec, *, tm):
    m, d = h.shape
    nt = m // tm
    row = lambda i: (i, 0)
    return pl.pallas_call(
        _moe_combine_kernel, grid=(nt,),
        in_specs=[pl.BlockSpec((tm, d), row),
                  pl.BlockSpec((tm, d // 2), row),
                  pl.BlockSpec((tm, d // 2), lambda i: (i + nt, 0)),
                  pl.BlockSpec((tm, LANES), row)],
        out_specs=pl.BlockSpec((tm, d), row),
        out_shape=jax.ShapeDtypeStruct((m, d), F32),
        compiler_params=_params(1), name="moe_combine",
    )(h, z, z, rec)


def _moe_sparse(h, n2p, rec, cnt, wgu, wd):
    m = h.shape[0]
    n_tiles = 2 * m // MOE_TILE + N_EXPERTS
    counts = cnt[0, :N_EXPERTS].astype(jnp.int32)
    tiles = (counts + MOE_TILE - 1) // MOE_TILE
    tile_end = jnp.cumsum(tiles)
    row_start = (tile_end - tiles) * MOE_TILE
    eid = rec[:, 0:2].astype(jnp.int32).T
    rank = rec[:, 4:6].astype(jnp.int32).T
    pos = row_start[eid] + rank
    n_used = tile_end[-1:]
    tile_expert = jnp.minimum(jnp.searchsorted(tile_end, jnp.arange(n_tiles, dtype=jnp.int32), side="right"),
                              jnp.searchsorted(tile_end, n_used[0] - 1, side="right")).astype(jnp.int32)
    xs = _sc_dispatch(n2p, pos, n_tiles * MOE_TILE)
    ys = _moe_grouped(xs, tile_expert, n_used.astype(jnp.int32), wgu, wd)
    z = _sc_gather(ys, pos.reshape(-1))
    return _moe_combine(h, z, rec, tm=MOE_TILE)


def _pad_lanes(v, offset):
    out = jnp.zeros((1, LANES), F32)
    return out.at[0, offset:offset + v.shape[0]].set(v.astype(F32))


def kernel(x_prompt, x_sample, cache_win_k, cache_win_v, state_conv, state_ssm, norm1_w, w_in, qnorm_w, knorm_w, conv_w, a_log, dt_bias, onorm_w, w_out, norm2_w, w_group, b_group, w_expert_router, b_expert_router, w_gate_up, w_down):
    nb, seq, d = x_prompt.shape
    db, t_len, _ = x_sample.shape
    n_buf = cache_win_k.shape[1]
    assert n_buf == MAX_WINDOW and seq % (MAX_DILATION * BAND) == 0 and t_len == SUBLANES
    keep = min(MAX_WINDOW, seq)

    c_bg = 3 * A_WIDTH + CONV_DIM + B_WIDTH
    w_cat = jnp.concatenate([w_in, jnp.zeros((d, LANES - 2 * B_HEADS), w_in.dtype)], axis=1).astype(BF16)
    assert w_cat.shape[1] == c_bg + LANES
    n1w = norm1_w.reshape(1, d).astype(F32)
    qw = jnp.tile(qnorm_w.astype(F32), A_HEADS).reshape(1, A_WIDTH)
    kw = jnp.tile(knorm_w.astype(F32), A_HEADS).reshape(1, A_WIDTH)
    hid = jnp.arange(A_WIDTH) // A_HEAD_DIM
    bd = (hid[:, None] == hid[None, :]).astype(BF16)
    a_log_v = _pad_lanes(a_log, B_HEADS)
    dtb_v = _pad_lanes(dt_bias, B_HEADS)
    onw = onorm_w.reshape(1, B_VAL_DIM).astype(F32)
    wa = w_out[:A_WIDTH].astype(BF16)
    wb = w_out[A_WIDTH:].astype(BF16)
    n2w = norm2_w.reshape(1, d).astype(F32)
    wr = jnp.concatenate([w_group, jnp.transpose(w_expert_router, (1, 0, 2)).reshape(d, N_EXPERTS),
                          jnp.zeros((d, LANES - N_GROUPS - N_EXPERTS), F32)], axis=1).astype(F32)
    br = jnp.zeros((1, LANES), F32).at[0, :N_GROUPS].set(b_group).at[0, N_GROUPS:N_GROUPS + N_EXPERTS].set(
        b_expert_router.reshape(-1))
    wgu = w_gate_up.astype(BF16)
    wd = w_down.astype(BF16)
    cw = conv_w.astype(F32)

    def finish(x2d, oa, ob, tm):
        if x2d.shape[0] % (SC_WORKERS * SC_CHUNK) == 0 and x2d.shape[0] >= N_EXPERTS * MOE_TILE:
            h, n2p, rec, cnt = _out_proj(x2d, oa, ob, wa, wb, n2w, wr, br, tm=tm, sparse=True)
            return _moe_sparse(h, n2p, rec, cnt, wgu, wd)
        h, n2, gates = _out_proj(x2d, oa, ob, wa, wb, n2w, wr, br, tm=tm, sparse=False)
        return _moe(h, n2, gates, wgu, wd, tm=tm)

    xp2d = x_prompt.reshape(nb * seq, d)
    qf, kf, vf, kwin, vwin, bqkv, z, bg = _in_proj(xp2d, n1w, w_cat, qw, kw, bd, seq=seq, keep=keep,
                                                   fold=True, tm=512)
    oa = _attn_prompt(qf, kf, vf).reshape(nb * seq, A_WIDTH)
    conv0 = jnp.zeros((nb, CONV_WIDTH - 1, CONV_DIM), F32)
    ssm0 = jnp.zeros((nb, B_HEADS, B_KEY_DIM, B_VAL_DIM), F32)
    ob, ssm_prompt = _gdn_prompt(bqkv, z, bg, cw, conv0, ssm0, a_log_v, dtb_v, onw,
                                 seq=seq, ts=512, block=128, chunk=CHUNK)
    y_prompt = finish(xp2d, oa, ob, 512).reshape(nb, seq, d)
    win_k_prompt = kwin.reshape(nb, keep, A_HEADS, A_HEAD_DIM)
    win_v_prompt = vwin.reshape(nb, keep, A_HEADS, A_HEAD_DIM)
    conv_prompt = bqkv.reshape(nb, seq, CONV_DIM)[:, seq - (CONV_WIDTH - 1):]

    ms = db * t_len
    xs2d = x_sample.reshape(ms, d)
    tms = min(512, ms)
    qs, ksn, vsn, _, _, bqkv_s, z_s, bg_s = _in_proj(xs2d, n1w, w_cat, qw, kw, bd, seq=ms, keep=ms,
                                                     fold=False, tm=tms)
    cache_kt = jnp.transpose(cache_win_k.astype(F32), (0, 2, 3, 1))
    cache_vt = jnp.transpose(cache_win_v.astype(F32), (0, 2, 3, 1))
    oa_s = _attn_sample(qs, ksn, vsn, cache_kt, cache_vt, t_len=t_len)
    xpad = jnp.concatenate([state_conv.astype(F32), bqkv_s.reshape(db, t_len, CONV_DIM)], axis=1)
    nseq = 16 if db % 16 == 0 else 1
    ob_s, ssm_sample = _gdn_sample(xpad, z_s, bg_s, cw, state_ssm.astype(F32), a_log_v, dtb_v, onw,
                                   t_len=t_len, nseq=nseq)
    y_sample = finish(xs2d, oa_s, ob_s, tms).reshape(db, t_len, d)
    win_k_sample = ksn.reshape(db, t_len, A_HEADS, A_HEAD_DIM)
    win_v_sample = vsn.reshape(db, t_len, A_HEADS, A_HEAD_DIM)
    conv_sample = xpad[:, t_len:]

    return (y_prompt, y_sample, win_k_prompt, win_v_prompt, conv_prompt, ssm_prompt,
            win_k_sample, win_v_sample, conv_sample, ssm_sample)
```

```python
import functools

import jax
import jax.numpy as jnp
from jax import lax
from jax.experimental import pallas as pl
from jax.experimental.pallas import tpu as pltpu
from jax.experimental.pallas import tpu_sc as plsc

F32 = jnp.float32
BF16 = jnp.bfloat16

A_HEADS = 8
A_HEAD_DIM = 64
A_WIDTH = A_HEADS * A_HEAD_DIM
DILATED_PATTERNS = ((128, 1), (512, 4), (2048, 16))
MAX_WINDOW = 2048
MAX_DILATION = 16
BAND = 128
ATTN_SCALE = A_HEAD_DIM ** -0.5
LOG2_E = 1.4426950408889634
B_HEADS = 4
B_KEY_DIM = 128
B_VAL_DIM = 128
B_WIDTH = B_HEADS * B_VAL_DIM
CONV_WIDTH = 4
CONV_DIM = B_HEADS * (2 * B_KEY_DIM + B_VAL_DIM)
CHUNK = 64
N_GROUPS = 4
EXPERTS_PER_GROUP = 4
N_EXPERTS = N_GROUPS * EXPERTS_PER_GROUP
D_EXPERT = 512
RMS_EPS = 1e-6
NEG_INF = -1e30

LANES = 128
SUBLANES = 8
VMEM_LIMIT = 56 * 1024 * 1024


def _params(n_axes, vmem=VMEM_LIMIT):
    return pltpu.CompilerParams(dimension_semantics=("arbitrary",) * n_axes, vmem_limit_bytes=vmem)


def _split2(x):
    hi = x.astype(BF16)
    lo = (x - hi.astype(F32)).astype(BF16)
    return hi, lo


def _split3(x):
    hi = x.astype(BF16)
    r = x - hi.astype(F32)
    mid = r.astype(BF16)
    lo = (r - mid.astype(F32)).astype(BF16)
    return hi, mid, lo


def _dot(a, b):
    return jnp.dot(a, b, preferred_element_type=F32)


def _dot_nt(a, b):
    return lax.dot_general(a, b, (((1,), (1,)), ((), ())), preferred_element_type=F32)


def _dot_tn(a, b):
    return lax.dot_general(a, b, (((0,), (0,)), ((), ())), preferred_element_type=F32)


def _sigmoid(x):
    return 1.0 / (1.0 + jnp.exp(-x))


def _silu(x):
    return x * _sigmoid(x)


def _in_proj_kernel(x_ref, n1w_ref, w_ref, qw_ref, kw_ref, bd_ref,
                    q_ref, k_ref, v_ref, kwin_ref, vwin_ref, bqkv_ref, z_ref, bg_ref,
                    *scratch, fold, tiles_per_seq, skip):
    tm = x_ref.shape[0]
    x = x_ref[...]
    ms = jnp.mean(x * x, axis=-1, keepdims=True)
    n1 = (x * lax.rsqrt(ms + RMS_EPS) * n1w_ref[...]).astype(BF16)

    bd = bd_ref[...]

    def head_norm(t, w):
        hi, lo = _split2(t * t)
        ss = _dot(hi, bd) + _dot(lo, bd)
        return t * lax.rsqrt(ss * (1.0 / A_HEAD_DIM) + RMS_EPS) * w

    q_scale = ATTN_SCALE * LOG2_E if fold else ATTN_SCALE
    q = head_norm(_dot(n1, w_ref[:, 0:A_WIDTH]), qw_ref[...]) * q_scale
    k = head_norm(_dot(n1, w_ref[:, A_WIDTH:2 * A_WIDTH]), kw_ref[...])
    v = _dot(n1, w_ref[:, 2 * A_WIDTH:3 * A_WIDTH])
    if fold:
        @pl.when(pl.program_id(0) % tiles_per_seq >= skip)
        def _():
            kwin_ref[0] = k.T.reshape(A_HEADS, A_HEAD_DIM, tm)
            vwin_ref[0] = v.T.reshape(A_HEADS, A_HEAD_DIM, tm)

        (scr,) = scratch
        rows = tm // MAX_DILATION
        for val, out in ((q, q_ref), (k, k_ref), (v, v_ref)):
            for c in range(A_WIDTH // LANES):
                cs = slice(c * LANES, (c + 1) * LANES)
                scr[c] = val[:, cs]
                for r in range(MAX_DILATION):
                    out[0, r, :, cs] = scr[c, pl.ds(r, rows, stride=MAX_DILATION), :]
    else:
        kwin_ref[...] = k
        vwin_ref[...] = v
        q_ref[...] = q
        k_ref[...] = k
        v_ref[...] = v
    c0 = 3 * A_WIDTH
    bqkv_ref[...] = _dot(n1, w_ref[:, c0:c0 + CONV_DIM])
    c1 = c0 + CONV_DIM
    z_ref[...] = _dot(n1, w_ref[:, c1:c1 + B_WIDTH]).astype(BF16)
    c2 = c1 + B_WIDTH
    bg_ref[...] = _dot(n1, w_ref[:, c2:c2 + LANES])


def _in_proj(x2d, n1w, w_cat, qw, kw, bd, *, seq, keep, fold, tm):
    m, d = x2d.shape
    nb = m // seq
    tiles_per_seq = seq // tm
    skip = (seq - keep) // tm
    keep_tiles = keep // tm
    n_cols = w_cat.shape[1]

    def win_map(i):
        return (i // tiles_per_seq) * keep_tiles + jnp.maximum(i % tiles_per_seq - skip, 0), 0

    row = lambda i: (i, 0)
    const = lambda i: (0, 0)
    if fold:
        rows = tm // MAX_DILATION
        qkv_shape = jax.ShapeDtypeStruct((nb, MAX_DILATION, seq // MAX_DILATION, A_WIDTH), F32)
        qkv_spec = pl.BlockSpec((1, MAX_DILATION, rows, A_WIDTH),
                                lambda i: (i // tiles_per_seq, 0, i % tiles_per_seq, 0))
        scratch = [pltpu.VMEM((A_WIDTH // LANES, tm, LANES), F32)]
        win_shape = jax.ShapeDtypeStruct((nb, A_HEADS, A_HEAD_DIM, keep), F32)
        win_spec = pl.BlockSpec((1, A_HEADS, A_HEAD_DIM, tm),
                                lambda i: (i // tiles_per_seq, 0, 0, jnp.maximum(i % tiles_per_seq - skip, 0)))
    else:
        qkv_shape = jax.ShapeDtypeStruct((m, A_WIDTH), F32)
        qkv_spec = pl.BlockSpec((tm, A_WIDTH), row)
        scratch = []
        win_shape = jax.ShapeDtypeStruct((nb * keep, A_WIDTH), F32)
        win_spec = pl.BlockSpec((tm, A_WIDTH), win_map)
    out_shape = (qkv_shape, qkv_shape, qkv_shape, win_shape, win_shape,
                 jax.ShapeDtypeStruct((m, CONV_DIM), F32),
                 jax.ShapeDtypeStruct((m, B_WIDTH), BF16),
                 jax.ShapeDtypeStruct((m, LANES), F32))
    out_specs = (qkv_spec, qkv_spec, qkv_spec, win_spec, win_spec,
                 pl.BlockSpec((tm, CONV_DIM), row),
                 pl.BlockSpec((tm, B_WIDTH), row),
                 pl.BlockSpec((tm, LANES), row))
    in_specs = [pl.BlockSpec((tm, d), row),
                pl.BlockSpec((1, d), const),
                pl.BlockSpec((d, n_cols), const),
                pl.BlockSpec((1, A_WIDTH), const),
                pl.BlockSpec((1, A_WIDTH), const),
                pl.BlockSpec((A_WIDTH, A_WIDTH), const)]
    return pl.pallas_call(
        functools.partial(_in_proj_kernel, fold=fold, tiles_per_seq=tiles_per_seq, skip=skip),
        grid=(m // tm,), in_specs=in_specs, out_specs=out_specs, out_shape=out_shape,
        scratch_shapes=scratch, compiler_params=_params(1), name="in_proj",
    )(x2d, n1w, w_cat, qw, kw, bd)


def _band_mask(pieces):
    sub = BAND // pieces
    r = lax.broadcasted_iota(jnp.int32, (BAND, 2 * BAND), 0)
    c = lax.broadcasted_iota(jnp.int32, (BAND, 2 * BAND), 1)
    qpos = (r % sub) * pieces + r // sub + BAND
    cc = c % BAND
    kpos = (cc % sub) * pieces + cc // sub + (c // BAND) * BAND
    dist = qpos - kpos
    return (dist >= 0) & (dist <= BAND), c >= BAND


ATTN_UNROLL = 8


def _attn_prompt_kernel(q_ref, k_ref, v_ref, o_ref, p_scr, op_scr, mx_scr, l_scr, nat_scr):
    u_len = q_ref.shape[2]
    lane = lax.broadcasted_iota(jnp.int32, (BAND, LANES), 1)
    head0 = lane < A_HEAD_DIM

    def per_head(col):
        return jnp.where(head0, jnp.broadcast_to(col[:BAND], (BAND, LANES)),
                         jnp.broadcast_to(col[BAND:], (BAND, LANES)))

    for pi, pieces in enumerate((16, 4, 1)):
        sub = BAND // pieces
        n_res = MAX_DILATION // pieces
        n_blk = u_len // sub
        band, _ = _band_mask(pieces)
        bias = jnp.where(band, 0.0, NEG_INF)
        bias2 = jnp.concatenate([bias, bias], axis=0)
        bias2_cur = bias2[:, BAND:]

        def locate(blk, n_res=n_res, sub=sub):
            m = blk // n_res
            cur = pl.multiple_of(m * sub, SUBLANES)
            prv = pl.multiple_of(jnp.maximum(m - 1, 0) * sub, SUBLANES)
            return blk % n_res, cur, prv

        def gather(ref, res, start, pieces=pieces, n_res=n_res, sub=sub):
            parts = [ref[0, a * n_res + res, pl.ds(start, sub), :] for a in range(pieces)]
            return parts[0] if pieces == 1 else jnp.concatenate(parts, axis=0)

        def scatter(ref, res, start, val, pi=pi, pieces=pieces, n_res=n_res, sub=sub):
            for a in range(pieces):
                ref[pi, a * n_res + res, pl.ds(start, sub), :] = val[a * sub:(a + 1) * sub]

        def probs(it, carry, first, bias2=bias2, bias2_cur=bias2_cur):
            for j, is_first in enumerate(first):
                blk = it * ATTN_UNROLL + j
                res, cur, prv = locate(blk)
                qb = gather(q_ref, res, cur)
                q2 = jnp.concatenate([jnp.where(head0, qb, 0.0), jnp.where(head0, 0.0, qb)], axis=0).astype(BF16)
                kc = gather(k_ref, res, cur)
                if is_first:
                    s = _dot_nt(q2, kc.astype(BF16)) + bias2_cur
                else:
                    kb = jnp.concatenate([gather(k_ref, res, prv), kc], axis=0).astype(BF16)
                    s = _dot_nt(q2, kb) + bias2
                mx = jnp.max(s, axis=-1, keepdims=True)
                p = jnp.exp2(s - mx)
                l = jnp.sum(p, axis=-1, keepdims=True)
                if is_first:
                    p_scr[blk, :, BAND:] = p.astype(BF16)
                else:
                    p_scr[blk] = p.astype(BF16)
                scatter(mx_scr, res, cur, per_head(mx))
                scatter(l_scr, res, cur, per_head(l))
            return carry

        def values(it, carry, first):
            for j, is_first in enumerate(first):
                blk = it * ATTN_UNROLL + j
                res, cur, prv = locate(blk)
                vc = gather(v_ref, res, cur)
                if is_first:
                    o2 = _dot(p_scr[blk, :, BAND:], vc.astype(BF16))
                else:
                    vb = jnp.concatenate([gather(v_ref, res, prv), vc], axis=0).astype(BF16)
                    o2 = _dot(p_scr[blk], vb)
                scatter(op_scr, res, cur, jnp.where(head0, o2[:BAND], o2[BAND:]))
            return carry

        n_it = n_res * n_blk // ATTN_UNROLL
        flags = [tuple(g * ATTN_UNROLL + j < n_res for j in range(ATTN_UNROLL)) for g in range(n_it)]
        segments = []
        for g, f in enumerate(flags):
            if segments and segments[-1][2] == f:
                segments[-1][1] = g + 1
            else:
                segments.append([g, g + 1, f])
        for phase in (probs, values):
            for lo, hi, f in segments:
                lax.fori_loop(lo, hi, functools.partial(phase, first=f), 0)

    for r in range(MAX_DILATION):
        m0, m1, m2 = mx_scr[0, r], mx_scr[1, r], mx_scr[2, r]
        big = jnp.maximum(jnp.maximum(m0, m1), m2)
        e0, e1, e2 = jnp.exp2(m0 - big), jnp.exp2(m1 - big), jnp.exp2(m2 - big)
        num = e0 * op_scr[0, r] + e1 * op_scr[1, r] + e2 * op_scr[2, r]
        den = e0 * l_scr[0, r] + e1 * l_scr[1, r] + e2 * l_scr[2, r]
        nat_scr[pl.ds(r, u_len, stride=MAX_DILATION), :] = num / den
    o_ref[0] = nat_scr[...].astype(BF16)


def _attn_prompt(qf, kf, vf):
    nb, _, u_len, _ = qf.shape
    seq = u_len * MAX_DILATION
    n_pairs = A_WIDTH // LANES
    spec = pl.BlockSpec((1, MAX_DILATION, u_len, LANES), lambda b, p: (b, 0, 0, p))
    return pl.pallas_call(
        _attn_prompt_kernel,
        grid=(nb, n_pairs),
        in_specs=[spec, spec, spec],
        out_specs=pl.BlockSpec((1, seq, LANES), lambda b, p: (b, 0, p)),
        out_shape=jax.ShapeDtypeStruct((nb, seq, A_WIDTH), BF16),
        scratch_shapes=[pltpu.VMEM((seq // BAND, 2 * BAND, 2 * BAND), BF16),
                        pltpu.VMEM((3, MAX_DILATION, u_len, LANES), F32),
                        pltpu.VMEM((3, MAX_DILATION, u_len, LANES), F32),
                        pltpu.VMEM((3, MAX_DILATION, u_len, LANES), F32),
                        pltpu.VMEM((seq, LANES), F32)],
        compiler_params=_params(2), name="attn_prompt",
    )(qf, kf, vf)


def _multiplicity(t, dd):
    cnt = jnp.zeros(dd.shape, F32)
    for w, d in DILATED_PATTERNS:
        ok = (dd >= 0) & (dd <= w) & (dd % d == 0)
        cnt = cnt + jnp.where(ok, 1.0, 0.0)
    return cnt


def _attn_sample_kernel(q_ref, kn_ref, vn_ref, kt_ref, vt_ref, o_ref):
    t_len = q_ref.shape[0]
    n_buf = kt_ref.shape[3]
    q = q_ref[...]
    kn = kn_ref[...]
    vn = vn_ref[...]
    tq = n_buf + lax.broadcasted_iota(jnp.int32, (t_len, n_buf), 0)
    cnt = _multiplicity(None, tq - lax.broadcasted_iota(jnp.int32, (t_len, n_buf), 1))
    cnt_new = _multiplicity(None, lax.broadcasted_iota(jnp.int32, (t_len, t_len), 0)
                            - lax.broadcasted_iota(jnp.int32, (t_len, t_len), 1))
    outs = []
    for h in range(A_HEADS):
        hs = slice(h * A_HEAD_DIM, (h + 1) * A_HEAD_DIM)
        qh = q[:, hs].astype(BF16)
        s = jnp.where(cnt > 0, _dot(qh, kt_ref[0, h].astype(BF16)), NEG_INF)
        sn = jnp.where(cnt_new > 0, _dot_nt(qh, kn[:, hs].astype(BF16)), NEG_INF)
        m = jnp.maximum(jnp.max(s, axis=-1, keepdims=True), jnp.max(sn, axis=-1, keepdims=True))
        p = cnt * jnp.exp(s - m)
        pn = cnt_new * jnp.exp(sn - m)
        l = jnp.sum(p, axis=-1, keepdims=True) + jnp.sum(pn, axis=-1, keepdims=True)
        o = _dot_nt(p.astype(BF16), vt_ref[0, h].astype(BF16)) + _dot(pn.astype(BF16), vn[:, hs].astype(BF16))
        outs.append(o / l)
    o_ref[...] = jnp.concatenate(outs, axis=-1)


def _attn_sample(q2d, kn2d, vn2d, cache_kt, cache_vt, *, t_len):
    nb, _, _, n_buf = cache_kt.shape
    tok = pl.BlockSpec((t_len, A_WIDTH), lambda b: (b, 0))
    win = pl.BlockSpec((1, A_HEADS, A_HEAD_DIM, n_buf), lambda b: (b, 0, 0, 0))
    return pl.pallas_call(
        _attn_sample_kernel,
        grid=(nb,),
        in_specs=[tok, tok, tok, win, win],
        out_specs=tok,
        out_shape=jax.ShapeDtypeStruct((nb * t_len, A_WIDTH), F32),
        compiler_params=_params(1), name="attn_sample",
    )(q2d, kn2d, vn2d, cache_kt, cache_vt)


def _unit_lower_inverse(a, n_chunk):
    n = a.shape[0]
    ri = lax.broadcasted_iota(jnp.int32, (n, n), 0)
    ci = lax.broadcasted_iota(jnp.int32, (n, n), 1)
    eye = jnp.where(ri == ci, 1.0, 0.0).astype(F32)
    t = eye - a
    pw = a
    k = 2
    while k < n_chunk:
        pw16 = pw.astype(BF16)
        pw = _dot(pw16, pw16)
        t = t + _dot(t.astype(BF16), pw.astype(BF16))
        k *= 2
    return t


def _gdn_intra(q, k, v, beta, dcol, drow, chunk):
    n = q.shape[0]
    ri = lax.broadcasted_iota(jnp.int32, (n, n), 0)
    ci = lax.broadcasted_iota(jnp.int32, (n, n), 1)
    same = (ri // chunk) == (ci // chunk)
    lower = same & (ri >= ci)
    strict = same & (ri > ci)
    gam = jnp.exp(jnp.where(lower, dcol - drow, NEG_INF))
    kb = k * beta
    vb = v * beta
    k16 = k.astype(BF16)
    a_mat = jnp.where(strict, _dot_nt(kb.astype(BF16), k16) * gam, 0.0)
    attn = jnp.where(lower, _dot_nt(q.astype(BF16), k16) * gam, 0.0)
    t_inv = _unit_lower_inverse(a_mat, chunk).astype(BF16)
    ed = jnp.exp(dcol)
    u = _dot(t_inv, vb.astype(BF16))
    w = _dot(t_inv, (kb * ed).astype(BF16))
    qd = q * ed
    return u, w, qd, attn


def _gdn_prep(c, bg, a_col, dt_col, chunk):
    n = c.shape[0]
    bgt = bg.T[0:2 * B_HEADS]
    row = lax.broadcasted_iota(jnp.int32, bgt.shape, 0)
    xg = bgt + dt_col
    sp = jnp.maximum(xg, 0.0) + jnp.log1p(jnp.exp(-jnp.abs(xg)))
    gt = jnp.where(row >= B_HEADS, -a_col * sp, 0.0)
    ri = lax.broadcasted_iota(jnp.int32, (n, n), 0)
    ci = lax.broadcasted_iota(jnp.int32, (n, n), 1)
    tri = jnp.where(((ri // chunk) == (ci // chunk)) & (ri >= ci), 1.0, 0.0).astype(BF16)
    th, tm_, tl = _split3(gt)
    drow = _dot_nt(th, tri) + _dot_nt(tm_, tri) + _dot_nt(tl, tri)
    both = jnp.where(row >= B_HEADS, drow, _sigmoid(bgt))
    cols = jnp.concatenate([both, jnp.zeros((LANES - 2 * B_HEADS, n), F32)], axis=0).T
    beta_all = cols
    dcol = cols
    heads = []
    kd_ = B_KEY_DIM
    for h in range(B_HEADS):
        qh = c[:, h * kd_:(h + 1) * kd_]
        kh = c[:, B_HEADS * kd_ + h * kd_:B_HEADS * kd_ + (h + 1) * kd_]
        vh = c[:, 2 * B_HEADS * kd_ + h * B_VAL_DIM:2 * B_HEADS * kd_ + (h + 1) * B_VAL_DIM]
        qh = qh * lax.rsqrt(jnp.sum(qh * qh, axis=-1, keepdims=True) + RMS_EPS) * (B_KEY_DIM ** -0.5)
        kh = kh * lax.rsqrt(jnp.sum(kh * kh, axis=-1, keepdims=True) + RMS_EPS)
        heads.append((qh, kh, vh, beta_all[:, h:h + 1], dcol[:, B_HEADS + h:B_HEADS + h + 1],
                      drow[B_HEADS + h:B_HEADS + h + 1, :]))
    return heads


def _gdn_out(o, onw, zh):
    ms = jnp.mean(o * o, axis=-1, keepdims=True)
    return o * lax.rsqrt(ms + RMS_EPS) * onw * _silu(zh)


def _gdn_prompt_kernel(x_ref, z_ref, bg_ref, cw_ref, cst_ref, sst_ref, avec_ref, dtb_ref, onw_ref,
                       o_ref, snew_ref, xp_scr, s_scr, *, block, chunk):
    ts = x_ref.shape[0]
    t = pl.program_id(1)
    pad = SUBLANES
    hist = CONV_WIDTH - 1

    @pl.when(t == 0)
    def _():
        xp_scr[...] = jnp.zeros((pad, CONV_DIM), F32)
        xp_scr[pad - hist:pad, :] = cst_ref[0]
        s_scr[...] = sst_ref[0]

    x = x_ref[...]
    tail = xp_scr[...]
    sub = lax.broadcasted_iota(jnp.int32, (pad, CONV_DIM), 0)
    y = x * cw_ref[hist:hist + 1, :]
    for s in range(1, CONV_WIDTH):
        xs = pltpu.roll(x, s, axis=0)
        head = jnp.where(sub < s, pltpu.roll(tail, s, axis=0), xs[0:pad])
        xs = jnp.concatenate([head, xs[pad:]], axis=0)
        y = y + xs * cw_ref[hist - s:hist - s + 1, :]
    xp_scr[...] = x[ts - pad:ts]
    c = _silu(y)
    avec = jnp.exp(avec_ref[...])
    onw = onw_ref[...]
    n_chunks = block // chunk
    n_sb = ts // block
    ri = lax.broadcasted_iota(jnp.int32, (block, block), 0)
    ci = lax.broadcasted_iota(jnp.int32, (block, block), 1)
    same = (ri // chunk) == (ci // chunk)
    lower = same & (ri >= ci)
    strict = same & (ri > ci)
    eye = jnp.where(ri == ci, 1.0, 0.0).astype(F32)

    units = []
    for sb in range(n_sb):
        rs = slice(sb * block, (sb + 1) * block)
        heads = _gdn_prep(c[rs], bg_ref[rs, :], avec, dtb_ref[...], chunk)
        for h, (qh, kh, vh, beta, dcol, drow) in enumerate(heads):
            units.append(dict(sb=sb, h=h, q=qh, k=kh, v=vh, beta=beta, dcol=dcol, drow=drow))

    for u in units:
        gam = jnp.exp(jnp.where(lower, u["dcol"] - u["drow"], NEG_INF))
        kb = u["k"] * u["beta"]
        kq = _dot_nt(jnp.concatenate([kb, u["q"]], axis=0).astype(BF16), u["k"].astype(BF16))
        u["pw"] = jnp.where(strict, kq[:block] * gam, 0.0)
        u["t"] = eye - u["pw"]
        u["attn"] = jnp.where(lower, kq[block:] * gam, 0.0).astype(BF16)
        ed = jnp.exp(u["dcol"])
        u["rhs"] = jnp.concatenate([u["v"] * u["beta"], kb * ed], axis=1).astype(BF16)
        u["qd"] = u["q"] * ed
        u["kt"] = u["k"].T

    kk = 2
    while kk < chunk:
        for u in units:
            p16 = u["pw"].astype(BF16)
            u["pw"] = _dot(p16, p16)
        for u in units:
            u["t"] = u["t"] + _dot(u["t"].astype(BF16), u["pw"].astype(BF16))
        kk *= 2

    for u in units:
        u["uw"] = _dot(u["t"].astype(BF16), u["rhs"]).astype(BF16)
    for u in units:
        au_aw = _dot(u["attn"], u["uw"])
        u["op"] = au_aw[:, :B_VAL_DIM]
        u["qp"] = (u["qd"] - au_aw[:, B_VAL_DIM:]).astype(BF16)
    for u in units:
        u["n"], u["mw"], u["e"] = [], [], []
        for cc in range(n_chunks):
            cs = slice(cc * chunk, (cc + 1) * chunk)
            last = u["drow"][:, (cc + 1) * chunk - 1:(cc + 1) * chunk]
            kdt = (u["kt"][:, cs] * jnp.exp(last - u["drow"][:, cs])).astype(BF16)
            nm = _dot(kdt, u["uw"][cs])
            u["n"].append(nm[:, :B_VAL_DIM])
            u["mw"].append(nm[:, B_VAL_DIM:].astype(BF16))
            u["e"].append(jnp.exp(last))

    s = [s_scr[h] for h in range(B_HEADS)]
    for sb in range(n_sb):
        rs = slice(sb * block, (sb + 1) * block)
        outs = [[] for _ in range(B_HEADS)]
        for cc in range(n_chunks):
            cs = slice(cc * chunk, (cc + 1) * chunk)
            for h in range(B_HEADS):
                u = units[sb * B_HEADS + h]
                r = _dot(jnp.concatenate([u["mw"][cc], u["qp"][cs]], axis=0), s[h].astype(BF16))
                outs[h].append(r[B_KEY_DIM:] + u["op"][cs])
                s[h] = s[h] * u["e"][cc] - r[:B_KEY_DIM] + u["n"][cc]
        for h in range(B_HEADS):
            o_all = jnp.concatenate(outs[h], axis=0) if n_chunks > 1 else outs[h][0]
            zh = z_ref[rs, h * B_VAL_DIM:(h + 1) * B_VAL_DIM].astype(F32)
            o_ref[rs, h * B_VAL_DIM:(h + 1) * B_VAL_DIM] = _gdn_out(o_all, onw, zh).astype(o_ref.dtype)
    for h in range(B_HEADS):
        s_scr[h] = s[h]

    @pl.when(t == pl.num_programs(1) - 1)
    def _():
        snew_ref[0] = s_scr[...]


def _gdn_prompt(bqkv, z, bg, conv_w, conv_state, ssm_state, a_log_v, dtb_v, onw, *, seq, ts, block, chunk):
    m = bqkv.shape[0]
    nb = m // seq
    tps = seq // ts
    row = lambda b, t: (b * tps + t, 0)
    const = lambda b, t: (0, 0)
    return pl.pallas_call(
        functools.partial(_gdn_prompt_kernel, block=block, chunk=chunk),
        grid=(nb, tps),
        in_specs=[pl.BlockSpec((ts, CONV_DIM), row),
                  pl.BlockSpec((ts, B_WIDTH), row),
                  pl.BlockSpec((ts, LANES), row),
                  pl.BlockSpec((CONV_WIDTH, CONV_DIM), const),
                  pl.BlockSpec((1, CONV_WIDTH - 1, CONV_DIM), lambda b, t: (b, 0, 0)),
                  pl.BlockSpec((1, B_HEADS, B_KEY_DIM, B_VAL_DIM), lambda b, t: (b, 0, 0, 0)),
                  pl.BlockSpec((2 * B_HEADS, LANES), const),
                  pl.BlockSpec((2 * B_HEADS, LANES), const),
                  pl.BlockSpec((1, B_VAL_DIM), const)],
        out_specs=(pl.BlockSpec((ts, B_WIDTH), row),
                   pl.BlockSpec((1, B_HEADS, B_KEY_DIM, B_VAL_DIM), lambda b, t: (b, 0, 0, 0))),
        out_shape=(jax.ShapeDtypeStruct((m, B_WIDTH), BF16),
                   jax.ShapeDtypeStruct(ssm_state.shape, F32)),
        scratch_shapes=[pltpu.VMEM((SUBLANES, CONV_DIM), F32),
                        pltpu.VMEM((B_HEADS, B_KEY_DIM, B_VAL_DIM), F32)],
        compiler_params=_params(2), name="gdn_prompt",
    )(bqkv, z, bg, conv_w, conv_state, ssm_state, a_log_v, dtb_v, onw)


def _gdn_sample_kernel(xp_ref, z_ref, bg_ref, cw_ref, sst_ref, avec_ref, dtb_ref, onw_ref,
                       o_ref, snew_ref, vn_scr, *, t_len):
    nseq = xp_ref.shape[0]
    hist = CONV_WIDTH - 1
    ys = []
    for j in range(nseq):
        y = xp_ref[j, 0:t_len, :] * cw_ref[0:1, :]
        for i in range(1, CONV_WIDTH):
            y = y + xp_ref[j, i:i + t_len, :] * cw_ref[i:i + 1, :]
        ys.append(y)
    c = _silu(jnp.concatenate(ys, axis=0))
    avec = jnp.exp(avec_ref[...])
    onw = onw_ref[...]
    heads = _gdn_prep(c, bg_ref[...], avec, dtb_ref[...], t_len)
    for h, (qh, kh, vh, beta, dcol, drow) in enumerate(heads):
        u, w, qd, attn = _gdn_intra(qh, kh, vh, beta, dcol, drow, t_len)
        attn16 = attn.astype(BF16)
        kt = kh.T
        vn_scr[...] = jnp.zeros(vn_scr.shape, F32)
        outs = []
        for j in range(nseq):
            cs = slice(j * t_len, (j + 1) * t_len)
            last = drow[:, (j + 1) * t_len - 1:(j + 1) * t_len]
            s = sst_ref[j, h]
            wq = jnp.concatenate([w[cs], qd[cs]], axis=0).astype(BF16)
            ws_qs = _dot(wq, s.astype(BF16))
            v_new = u[cs] - ws_qs[:t_len]
            vn_scr[cs, :] = v_new
            o = ws_qs[t_len:] + _dot(attn16[cs], vn_scr[...].astype(BF16))
            kdt = kt[:, cs] * jnp.exp(last - drow[:, cs])
            snew_ref[j, h] = s * jnp.exp(last) + _dot(kdt.astype(BF16), v_new.astype(BF16))
            outs.append(o)
        o_all = jnp.concatenate(outs, axis=0)
        zh = z_ref[:, h * B_VAL_DIM:(h + 1) * B_VAL_DIM].astype(F32)
        o_ref[:, h * B_VAL_DIM:(h + 1) * B_VAL_DIM] = _gdn_out(o_all, onw, zh).astype(o_ref.dtype)


def _gdn_sample(xp, z, bg, conv_w, ssm_state, a_log_v, dtb_v, onw, *, t_len, nseq):
    nb = xp.shape[0]
    rows = nseq * t_len
    row = lambda i: (i, 0)
    const = lambda i: (0, 0)
    return pl.pallas_call(
        functools.partial(_gdn_sample_kernel, t_len=t_len),
        grid=(nb // nseq,),
        in_specs=[pl.BlockSpec((nseq, xp.shape[1], CONV_DIM), lambda i: (i, 0, 0)),
                  pl.BlockSpec((rows, B_WIDTH), row),
                  pl.BlockSpec((rows, LANES), row),
                  pl.BlockSpec((CONV_WIDTH, CONV_DIM), const),
                  pl.BlockSpec((nseq, B_HEADS, B_KEY_DIM, B_VAL_DIM), lambda i: (i, 0, 0, 0)),
                  pl.BlockSpec((2 * B_HEADS, LANES), const),
                  pl.BlockSpec((2 * B_HEADS, LANES), const),
                  pl.BlockSpec((1, B_VAL_DIM), const)],
        out_specs=(pl.BlockSpec((rows, B_WIDTH), row),
                   pl.BlockSpec((nseq, B_HEADS, B_KEY_DIM, B_VAL_DIM), lambda i: (i, 0, 0, 0))),
        out_shape=(jax.ShapeDtypeStruct((nb * t_len, B_WIDTH), BF16),
                   jax.ShapeDtypeStruct(ssm_state.shape, F32)),
        scratch_shapes=[pltpu.VMEM((rows, B_VAL_DIM), F32)],
        compiler_params=_params(1), name="gdn_sample",
    )(xp, z, bg, conv_w, ssm_state, a_log_v, dtb_v, onw)


def _pack_bf16_pairs(x):
    n = x.shape[1] // 2
    bits = pltpu.bitcast(x.astype(BF16).astype(F32), jnp.int32)
    return lax.shift_right_logical(bits[:, :n], 16) | (bits[:, n:] & jnp.int32(-65536))


def _unpack_bf16_pairs(w):
    lo = pltpu.bitcast(lax.shift_left(w, 16), F32)
    hi = pltpu.bitcast(w & jnp.int32(-65536), F32)
    return jnp.concatenate([lo, hi], axis=1)


def _out_proj_kernel(x_ref, oa_ref, ob_ref, wa_ref, wb_ref, n2w_ref, wr_ref, br_ref, *rest, sparse):
    if sparse:
        tri_ref, h_ref, n2_ref, gates_ref, cnt_ref = rest
    else:
        h_ref, n2_ref, gates_ref = rest
    h = x_ref[...] + _dot(oa_ref[...].astype(BF16), wa_ref[...]) + _dot(ob_ref[...].astype(BF16), wb_ref[...])
    h_ref[...] = h
    ms = jnp.mean(h * h, axis=-1, keepdims=True)
    n2 = h * lax.rsqrt(ms + RMS_EPS) * n2w_ref[...]
    if sparse:
        n2_ref[...] = _pack_bf16_pairs(n2)
    else:
        n2_ref[...] = n2.astype(BF16)
    nh, nl = _split2(n2)
    wr = wr_ref[...]
    wh, wl = _split2(wr)
    logits = _dot(nh, wh) + _dot(nl, wh) + _dot(nh, wl) + br_ref[...]
    lane = lax.broadcasted_iota(jnp.int32, logits.shape, 1).astype(F32)
    big = 1e9
    gl = jnp.where(lane < N_GROUPS, logits, NEG_INF)
    gmax = jnp.max(gl, axis=-1, keepdims=True)
    gi = jnp.min(jnp.where(gl == gmax, lane, big), axis=-1, keepdims=True)
    g_sel = 1.0 / jnp.sum(jnp.exp(gl - gmax), axis=-1, keepdims=True)
    lo = N_GROUPS + EXPERTS_PER_GROUP * gi
    in_grp = (lane >= lo) & (lane < lo + EXPERTS_PER_GROUP)
    el = jnp.where(in_grp, logits, NEG_INF)
    v1 = jnp.max(el, axis=-1, keepdims=True)
    i1 = jnp.min(jnp.where(el == v1, lane, big), axis=-1, keepdims=True)
    el2 = jnp.where(lane == i1, NEG_INF, el)
    v2 = jnp.max(el2, axis=-1, keepdims=True)
    i2 = jnp.min(jnp.where(el2 == v2, lane, big), axis=-1, keepdims=True)
    e2 = jnp.exp(v2 - v1)
    w1 = g_sel / (1.0 + e2)
    w2 = g_sel * e2 / (1.0 + e2)
    if not sparse:
        gates_ref[...] = jnp.where(lane + N_GROUPS == i1, w1, 0.0) + jnp.where(lane + N_GROUPS == i2, w2, 0.0)
        return
    @pl.when(pl.program_id(0) == 0)
    def _():
        cnt_ref[...] = jnp.zeros(cnt_ref.shape, F32)

    e0 = i1 - N_GROUPS
    e1 = i2 - N_GROUPS
    onehot = jnp.where(lane == e0, 1.0, 0.0) + jnp.where(lane == e1, 1.0, 0.0)
    before = _dot(tri_ref[...], onehot.astype(BF16)) + cnt_ref[...]
    r0 = jnp.sum(jnp.where(lane == e0, before, 0.0), axis=-1, keepdims=True)
    r1 = jnp.sum(jnp.where(lane == e1, before, 0.0), axis=-1, keepdims=True)
    cnt_ref[...] += jnp.sum(onehot, axis=0, keepdims=True)
    rec = jnp.zeros(logits.shape, F32)
    for col, val in enumerate((e0, e1, w1, w2, r0, r1)):
        rec = jnp.where(lane == col, val, rec)
    gates_ref[...] = rec


def _out_proj(x2d, oa, ob, wa, wb, n2w, wr, br, *, tm, sparse):
    m, d = x2d.shape
    row = lambda i: (i, 0)
    const = lambda i: (0, 0)
    in_specs = [pl.BlockSpec((tm, d), row),
                pl.BlockSpec((tm, A_WIDTH), row),
                pl.BlockSpec((tm, B_WIDTH), row),
                pl.BlockSpec((A_WIDTH, d), const),
                pl.BlockSpec((B_WIDTH, d), const),
                pl.BlockSpec((1, d), const),
                pl.BlockSpec((d, LANES), const),
                pl.BlockSpec((1, LANES), const)]
    args = [x2d, oa, ob, wa, wb, n2w, wr, br]
    if sparse:
        idx = jnp.arange(tm)
        args.append((idx[:, None] > idx[None, :]).astype(BF16))
        in_specs.append(pl.BlockSpec((tm, tm), const))
        out_specs = (pl.BlockSpec((tm, d), row), pl.BlockSpec((tm, d // 2), row), pl.BlockSpec((tm, LANES), row),
                     pl.BlockSpec((1, LANES), const))
        out_shape = (jax.ShapeDtypeStruct((m, d), F32), jax.ShapeDtypeStruct((m, d // 2), jnp.int32),
                     jax.ShapeDtypeStruct((m, LANES), F32), jax.ShapeDtypeStruct((1, LANES), F32))
    else:
        out_specs = (pl.BlockSpec((tm, d), row), pl.BlockSpec((tm, d), row), pl.BlockSpec((tm, LANES), row))
        out_shape = (jax.ShapeDtypeStruct((m, d), F32), jax.ShapeDtypeStruct((m, d), BF16),
                     jax.ShapeDtypeStruct((m, LANES), F32))
    return pl.pallas_call(
        functools.partial(_out_proj_kernel, sparse=sparse),
        grid=(m // tm,), in_specs=in_specs, out_specs=out_specs, out_shape=out_shape,
        compiler_params=_params(1), name="out_proj",
    )(*args)


def _moe_kernel(h_ref, n2_ref, gates_ref, wgu_ref, wd_ref, y_ref):
    e = pl.program_id(1)

    @pl.when(e == 0)
    def _():
        y_ref[...] = h_ref[...]

    n2 = n2_ref[...]
    gu = _dot(n2, wgu_ref[0])
    act = _silu(gu[:, :D_EXPERT]) * gu[:, D_EXPERT:]
    lane = lax.broadcasted_iota(jnp.int32, gates_ref.shape, 1)
    gate = jnp.sum(jnp.where(lane == e, gates_ref[...], 0.0), axis=-1, keepdims=True)
    y_ref[...] += gate * _dot(act.astype(BF16), wd_ref[0])


def _moe(h, n2, gates, wgu, wd, *, tm):
    m, d = h.shape
    row = lambda i, e: (i, 0)
    return pl.pallas_call(
        _moe_kernel,
        grid=(m // tm, N_EXPERTS),
        in_specs=[pl.BlockSpec((tm, d), row),
                  pl.BlockSpec((tm, d), row),
                  pl.BlockSpec((tm, LANES), row),
                  pl.BlockSpec((1, d, 2 * D_EXPERT), lambda i, e: (e, 0, 0)),
                  pl.BlockSpec((1, D_EXPERT, d), lambda i, e: (e, 0, 0))],
        out_specs=pl.BlockSpec((tm, d), row),
        out_shape=jax.ShapeDtypeStruct((m, d), F32),
        compiler_params=_params(2), name="moe",
    )(h, n2, gates, wgu, wd)


SC_CORES = 2
SC_SUBCORES = 16
SC_WORKERS = SC_CORES * SC_SUBCORES
SC_CHUNK = 64
MOE_TILE = 512


def _sc_mesh():
    return plsc.VectorSubcoreMesh(core_axis_name="c", subcore_axis_name="s",
                                  num_cores=SC_CORES, num_subcores=SC_SUBCORES)


def _sc_dispatch(src, pos, n_rows):
    m, d = src.shape
    n_chunks = m // (SC_WORKERS * SC_CHUNK)
    pos4 = pos.reshape(2, SC_WORKERS, n_chunks, SC_CHUNK).transpose(1, 0, 2, 3)

    def body(src_hbm, pos_hbm, out_hbm, idx_v, rows_v):
        wid = lax.axis_index("s") * SC_CORES + lax.axis_index("c")
        pltpu.sync_copy(pos_hbm.at[wid], idx_v)

        @pl.loop(0, n_chunks)
        def _(j):
            start = pl.multiple_of((wid * n_chunks + j) * SC_CHUNK, SC_CHUNK)
            pltpu.sync_copy(src_hbm.at[pl.ds(start, SC_CHUNK)], rows_v)
            pltpu.sync_copy(rows_v, out_hbm.at[idx_v.at[0, j]])
            pltpu.sync_copy(rows_v, out_hbm.at[idx_v.at[1, j]])

    return pl.kernel(
        body, out_type=jax.ShapeDtypeStruct((n_rows, d), src.dtype), mesh=_sc_mesh(),
        scratch_types=[pltpu.VMEM((2, n_chunks, SC_CHUNK), jnp.int32), pltpu.VMEM((SC_CHUNK, d), src.dtype)],
        name="moe_dispatch",
    )(src, pos4)


def _sc_gather(table, idx):
    b = idx.shape[0]
    d = table.shape[1]
    n_chunks = b // (SC_WORKERS * SC_CHUNK)
    idx3 = idx.reshape(SC_WORKERS, n_chunks, SC_CHUNK)

    def body(table_hbm, idx_hbm, out_hbm, idx_v, rows_v):
        wid = lax.axis_index("s") * SC_CORES + lax.axis_index("c")
        pltpu.sync_copy(idx_hbm.at[wid], idx_v)

        @pl.loop(0, n_chunks)
        def _(j):
            start = pl.multiple_of((wid * n_chunks + j) * SC_CHUNK, SC_CHUNK)
            pltpu.sync_copy(table_hbm.at[idx_v.at[j]], rows_v)
            pltpu.sync_copy(rows_v, out_hbm.at[pl.ds(start, SC_CHUNK)])

    return pl.kernel(
        body, out_type=jax.ShapeDtypeStruct((b, d), table.dtype), mesh=_sc_mesh(),
        scratch_types=[pltpu.VMEM((n_chunks, SC_CHUNK), jnp.int32), pltpu.VMEM((SC_CHUNK, d), table.dtype)],
        name="moe_collect",
    )(table, idx3)


def _moe_grouped_kernel(te_ref, nu_ref, xs_ref, wgu_ref, wd_ref, ys_ref):
    i = pl.program_id(0)

    @pl.when(i < nu_ref[0])
    def _():
        x = _unpack_bf16_pairs(xs_ref[...]).astype(BF16)
        gu = _dot(x, wgu_ref[0])
        act = _silu(gu[:, :D_EXPERT]) * gu[:, D_EXPERT:]
        ys_ref[...] = _pack_bf16_pairs(_dot(act.astype(BF16), wd_ref[0]))

    @pl.when(i >= nu_ref[0])
    def _():
        ys_ref[...] = jnp.zeros(ys_ref.shape, ys_ref.dtype)


def _moe_grouped(xs, tile_expert, n_used, wgu, wd):
    r, half = xs.shape
    d = 2 * half
    grid_spec = pltpu.PrefetchScalarGridSpec(
        num_scalar_prefetch=2, grid=(r // MOE_TILE,),
        in_specs=[pl.BlockSpec((MOE_TILE, half), lambda i, te, nu: (i, 0)),
                  pl.BlockSpec((1, d, 2 * D_EXPERT), lambda i, te, nu: (te[i], 0, 0)),
                  pl.BlockSpec((1, D_EXPERT, d), lambda i, te, nu: (te[i], 0, 0))],
        out_specs=pl.BlockSpec((MOE_TILE, half), lambda i, te, nu: (i, 0)))
    return pl.pallas_call(
        _moe_grouped_kernel, grid_spec=grid_spec,
        out_shape=jax.ShapeDtypeStruct((r, half), jnp.int32),
        compiler_params=_params(1), name="moe_grouped",
    )(tile_expert, n_used, xs, wgu, wd)


def _moe_combine_kernel(h_ref, z0_ref, z1_ref, rec_ref, y_ref):
    rec = rec_ref[...]
    y_ref[...] = (h_ref[...] + rec[:, 2:3] * _unpack_bf16_pairs(z0_ref[...])
                  + rec[:, 3:4] * _unpack_bf16_pairs(z1_ref[...]))


def _moe_combine(h, z, rec, *, tm):
    m, d = h.shape
    nt = m // tm
    row = lambda i: (i, 0)
    return pl.pallas_call(
        _moe_combine_kernel, grid=(nt,),
        in_specs=[pl.BlockSpec((tm, d), row),
                  pl.BlockSpec((tm, d // 2), row),
                  pl.BlockSpec((tm, d // 2), lambda i: (i + nt, 0)),
                  pl.BlockSpec((tm, LANES), row)],
        out_specs=pl.BlockSpec((tm, d), row),
        out_shape=jax.ShapeDtypeStruct((m, d), F32),
        compiler_params=_params(1), name="moe_combine",
    )(h, z, z, rec)


def _moe_sparse(h, n2p, rec, cnt, wgu, wd):
    m = h.shape[0]
    n_tiles = 2 * m // MOE_TILE + N_EXPERTS
    counts = cnt[0, :N_EXPERTS].astype(jnp.int32)
    tiles = (counts + MOE_TILE - 1) // MOE_TILE
    tile_end = jnp.cumsum(tiles)
    row_start = (tile_end - tiles) * MOE_TILE
    eid = rec[:, 0:2].astype(jnp.int32).T
    rank = rec[:, 4:6].astype(jnp.int32).T
    pos = rank
    for e in range(N_EXPERTS):
        pos = pos + jnp.where(eid == e, row_start[e], 0)
    n_used = tile_end[-1:]
    tile_expert = jnp.minimum(jnp.searchsorted(tile_end, jnp.arange(n_tiles, dtype=jnp.int32), side="right"),
                              jnp.searchsorted(tile_end, n_used[0] - 1, side="right")).astype(jnp.int32)
    xs = _sc_dispatch(n2p, pos, n_tiles * MOE_TILE)
    ys = _moe_grouped(xs, tile_expert, n_used.astype(jnp.int32), wgu, wd)
    z = _sc_gather(ys, pos.reshape(-1))
    return _moe_combine(h, z, rec, tm=MOE_TILE)


def _head_rows(v):
    rows = jnp.concatenate([jnp.zeros((B_HEADS,), F32), v.astype(F32)])
    return jnp.broadcast_to(rows[:, None], (2 * B_HEADS, LANES))


def kernel(x_prompt, x_sample, cache_win_k, cache_win_v, state_conv, state_ssm, norm1_w, w_in, qnorm_w, knorm_w, conv_w, a_log, dt_bias, onorm_w, w_out, norm2_w, w_group, b_group, w_expert_router, b_expert_router, w_gate_up, w_down):
    nb, seq, d = x_prompt.shape
    db, t_len, _ = x_sample.shape
    n_buf = cache_win_k.shape[1]
    assert n_buf == MAX_WINDOW and seq % (MAX_DILATION * BAND) == 0 and t_len == SUBLANES
    keep = min(MAX_WINDOW, seq)

    c_bg = 3 * A_WIDTH + CONV_DIM + B_WIDTH
    w_cat = jnp.concatenate([w_in, jnp.zeros((d, LANES - 2 * B_HEADS), w_in.dtype)], axis=1).astype(BF16)
    assert w_cat.shape[1] == c_bg + LANES
    n1w = norm1_w.reshape(1, d).astype(F32)
    qw = jnp.tile(qnorm_w.astype(F32), A_HEADS).reshape(1, A_WIDTH)
    kw = jnp.tile(knorm_w.astype(F32), A_HEADS).reshape(1, A_WIDTH)
    hid = jnp.arange(A_WIDTH) // A_HEAD_DIM
    bd = (hid[:, None] == hid[None, :]).astype(BF16)
    a_log_v = _head_rows(a_log)
    dtb_v = _head_rows(dt_bias)
    onw = onorm_w.reshape(1, B_VAL_DIM).astype(F32)
    wa = w_out[:A_WIDTH].astype(BF16)
    wb = w_out[A_WIDTH:].astype(BF16)
    n2w = norm2_w.reshape(1, d).astype(F32)
    wr = jnp.concatenate([w_group, jnp.transpose(w_expert_router, (1, 0, 2)).reshape(d, N_EXPERTS),
                          jnp.zeros((d, LANES - N_GROUPS - N_EXPERTS), F32)], axis=1).astype(F32)
    br = jnp.zeros((1, LANES), F32).at[0, :N_GROUPS].set(b_group).at[0, N_GROUPS:N_GROUPS + N_EXPERTS].set(
        b_expert_router.reshape(-1))
    wgu = w_gate_up.astype(BF16)
    wd = w_down.astype(BF16)
    cw = conv_w.astype(F32)

    def finish(x2d, oa, ob, tm):
        if x2d.shape[0] % (SC_WORKERS * SC_CHUNK) == 0 and x2d.shape[0] >= N_EXPERTS * MOE_TILE:
            h, n2p, rec, cnt = _out_proj(x2d, oa, ob, wa, wb, n2w, wr, br, tm=tm, sparse=True)
            return _moe_sparse(h, n2p, rec, cnt, wgu, wd)
        h, n2, gates = _out_proj(x2d, oa, ob, wa, wb, n2w, wr, br, tm=tm, sparse=False)
        return _moe(h, n2, gates, wgu, wd, tm=tm)

    xp2d = x_prompt.reshape(nb * seq, d)
    qf, kf, vf, kwin, vwin, bqkv, z, bg = _in_proj(xp2d, n1w, w_cat, qw, kw, bd, seq=seq, keep=keep,
                                                   fold=True, tm=512)
    oa = _attn_prompt(qf, kf, vf).reshape(nb * seq, A_WIDTH)
    conv0 = jnp.zeros((nb, CONV_WIDTH - 1, CONV_DIM), F32)
    ssm0 = jnp.zeros((nb, B_HEADS, B_KEY_DIM, B_VAL_DIM), F32)
    ob, ssm_prompt = _gdn_prompt(bqkv, z, bg, cw, conv0, ssm0, a_log_v, dtb_v, onw,
                                 seq=seq, ts=512, block=128, chunk=CHUNK)
    y_prompt = finish(xp2d, oa, ob, 512).reshape(nb, seq, d)
    win_k_prompt = jnp.transpose(kwin, (0, 3, 1, 2))
    win_v_prompt = jnp.transpose(vwin, (0, 3, 1, 2))
    conv_prompt = bqkv.reshape(nb, seq, CONV_DIM)[:, seq - (CONV_WIDTH - 1):]

    ms = db * t_len
    xs2d = x_sample.reshape(ms, d)
    tms = min(512, ms)
    qs, ksn, vsn, _, _, bqkv_s, z_s, bg_s = _in_proj(xs2d, n1w, w_cat, qw, kw, bd, seq=ms, keep=ms,
                                                     fold=False, tm=tms)
    cache_kt = jnp.transpose(cache_win_k.astype(F32), (0, 2, 3, 1))
    cache_vt = jnp.transpose(cache_win_v.astype(F32), (0, 2, 3, 1))
    oa_s = _attn_sample(qs, ksn, vsn, cache_kt, cache_vt, t_len=t_len)
    xpad = jnp.concatenate([state_conv.astype(F32), bqkv_s.reshape(db, t_len, CONV_DIM)], axis=1)
    nseq = 16 if db % 16 == 0 else 1
    ob_s, ssm_sample = _gdn_sample(xpad, z_s, bg_s, cw, state_ssm.astype(F32), a_log_v, dtb_v, onw,
                                   t_len=t_len, nseq=nseq)
    y_sample = finish(xs2d, oa_s, ob_s, tms).reshape(db, t_len, d)
    win_k_sample = ksn.reshape(db, t_len, A_HEADS, A_HEAD_DIM)
    win_v_sample = vsn.reshape(db, t_len, A_HEADS, A_HEAD_DIM)
    conv_sample = xpad[:, t_len:]

    return (y_prompt, y_sample, win_k_prompt, win_v_prompt, conv_prompt, ssm_prompt,
            win_k_sample, win_v_sample, conv_sample, ssm_sample)
```

```python
import functools

import jax
import jax.numpy as jnp
from jax import lax
from jax.experimental import pallas as pl
from jax.experimental.pallas import tpu as pltpu
from jax.experimental.pallas import tpu_sc as plsc

F32 = jnp.float32
BF16 = jnp.bfloat16

A_HEADS = 8
A_HEAD_DIM = 64
A_WIDTH = A_HEADS * A_HEAD_DIM
DILATED_PATTERNS = ((128, 1), (512, 4), (2048, 16))
MAX_WINDOW = 2048
MAX_DILATION = 16
BAND = 128
ATTN_SCALE = A_HEAD_DIM ** -0.5
LOG2_E = 1.4426950408889634
B_HEADS = 4
B_KEY_DIM = 128
B_VAL_DIM = 128
B_WIDTH = B_HEADS * B_VAL_DIM
CONV_WIDTH = 4
CONV_DIM = B_HEADS * (2 * B_KEY_DIM + B_VAL_DIM)
CHUNK = 64
N_GROUPS = 4
EXPERTS_PER_GROUP = 4
N_EXPERTS = N_GROUPS * EXPERTS_PER_GROUP
D_EXPERT = 512
RMS_EPS = 1e-6
NEG_INF = -1e30

LANES = 128
SUBLANES = 8
VMEM_LIMIT = 56 * 1024 * 1024


def _params(n_axes, vmem=VMEM_LIMIT):
    return pltpu.CompilerParams(dimension_semantics=("arbitrary",) * n_axes, vmem_limit_bytes=vmem)


def _split2(x):
    hi = x.astype(BF16)
    lo = (x - hi.astype(F32)).astype(BF16)
    return hi, lo


def _split3(x):
    hi = x.astype(BF16)
    r = x - hi.astype(F32)
    mid = r.astype(BF16)
    lo = (r - mid.astype(F32)).astype(BF16)
    return hi, mid, lo


def _dot(a, b):
    return jnp.dot(a, b, preferred_element_type=F32)


def _dot_nt(a, b):
    return lax.dot_general(a, b, (((1,), (1,)), ((), ())), preferred_element_type=F32)


def _dot_tn(a, b):
    return lax.dot_general(a, b, (((0,), (0,)), ((), ())), preferred_element_type=F32)


def _sigmoid(x):
    return 1.0 / (1.0 + jnp.exp(-x))


def _silu(x):
    return x * _sigmoid(x)


def _in_proj_kernel(x_ref, n1w_ref, w_ref, qw_ref, kw_ref,
                    q_ref, k_ref, v_ref, kwin_ref, vwin_ref, bqkv_ref, z_ref, bg_ref,
                    *scratch, fold, tiles_per_seq, skip):
    tm = x_ref.shape[0]
    x = x_ref[...]
    ms = jnp.mean(x * x, axis=-1, keepdims=True)
    n1 = (x * lax.rsqrt(ms + RMS_EPS) * n1w_ref[...]).astype(BF16)

    first_head = lax.broadcasted_iota(jnp.int32, (tm, LANES), 1) < A_HEAD_DIM

    def head_norm(t, w):
        cols = []
        for c in range(A_WIDTH // LANES):
            tc = t[:, c * LANES:(c + 1) * LANES]
            sq = tc * tc
            s0 = jnp.sum(jnp.where(first_head, sq, 0.0), axis=-1, keepdims=True)
            s1 = jnp.sum(jnp.where(first_head, 0.0, sq), axis=-1, keepdims=True)
            ms = jnp.where(first_head, s0, s1) * (1.0 / A_HEAD_DIM)
            cols.append(tc * lax.rsqrt(ms + RMS_EPS))
        return jnp.concatenate(cols, axis=1) * w

    q_scale = ATTN_SCALE * LOG2_E if fold else ATTN_SCALE
    q = head_norm(_dot(n1, w_ref[:, 0:A_WIDTH]), qw_ref[...]) * q_scale
    k = head_norm(_dot(n1, w_ref[:, A_WIDTH:2 * A_WIDTH]), kw_ref[...])
    v = _dot(n1, w_ref[:, 2 * A_WIDTH:3 * A_WIDTH])
    if fold:
        @pl.when(pl.program_id(0) % tiles_per_seq >= skip)
        def _():
            kwin_ref[0] = k.T.reshape(A_HEADS, A_HEAD_DIM, tm)
            vwin_ref[0] = v.T.reshape(A_HEADS, A_HEAD_DIM, tm)

        (scr,) = scratch
        rows = tm // MAX_DILATION
        for val, out in ((q, q_ref), (k, k_ref), (v, v_ref)):
            for c in range(A_WIDTH // LANES):
                cs = slice(c * LANES, (c + 1) * LANES)
                scr[c] = val[:, cs]
                for r in range(MAX_DILATION):
                    out[0, r, :, cs] = scr[c, pl.ds(r, rows, stride=MAX_DILATION), :]
    else:
        kwin_ref[...] = k
        vwin_ref[...] = v
        q_ref[...] = q
        k_ref[...] = k
        v_ref[...] = v
    c0 = 3 * A_WIDTH
    bqkv_ref[...] = _dot(n1, w_ref[:, c0:c0 + CONV_DIM])
    c1 = c0 + CONV_DIM
    z_ref[...] = _dot(n1, w_ref[:, c1:c1 + B_WIDTH]).astype(BF16)
    c2 = c1 + B_WIDTH
    bg_ref[...] = _dot(n1, w_ref[:, c2:c2 + LANES])


def _in_proj(x2d, n1w, w_cat, qw, kw, *, seq, keep, fold, tm):
    m, d = x2d.shape
    nb = m // seq
    tiles_per_seq = seq // tm
    skip = (seq - keep) // tm
    keep_tiles = keep // tm
    n_cols = w_cat.shape[1]

    def win_map(i):
        return (i // tiles_per_seq) * keep_tiles + jnp.maximum(i % tiles_per_seq - skip, 0), 0

    row = lambda i: (i, 0)
    const = lambda i: (0, 0)
    if fold:
        rows = tm // MAX_DILATION
        qkv_shape = jax.ShapeDtypeStruct((nb, MAX_DILATION, seq // MAX_DILATION, A_WIDTH), F32)
        qkv_spec = pl.BlockSpec((1, MAX_DILATION, rows, A_WIDTH),
                                lambda i: (i // tiles_per_seq, 0, i % tiles_per_seq, 0))
        scratch = [pltpu.VMEM((A_WIDTH // LANES, tm, LANES), F32)]
        win_shape = jax.ShapeDtypeStruct((nb, A_HEADS, A_HEAD_DIM, keep), F32)
        win_spec = pl.BlockSpec((1, A_HEADS, A_HEAD_DIM, tm),
                                lambda i: (i // tiles_per_seq, 0, 0, jnp.maximum(i % tiles_per_seq - skip, 0)))
    else:
        qkv_shape = jax.ShapeDtypeStruct((m, A_WIDTH), F32)
        qkv_spec = pl.BlockSpec((tm, A_WIDTH), row)
        scratch = []
        win_shape = jax.ShapeDtypeStruct((nb * keep, A_WIDTH), F32)
        win_spec = pl.BlockSpec((tm, A_WIDTH), win_map)
    out_shape = (qkv_shape, qkv_shape, qkv_shape, win_shape, win_shape,
                 jax.ShapeDtypeStruct((m, CONV_DIM), F32),
                 jax.ShapeDtypeStruct((m, B_WIDTH), BF16),
                 jax.ShapeDtypeStruct((m, LANES), F32))
    out_specs = (qkv_spec, qkv_spec, qkv_spec, win_spec, win_spec,
                 pl.BlockSpec((tm, CONV_DIM), row),
                 pl.BlockSpec((tm, B_WIDTH), row),
                 pl.BlockSpec((tm, LANES), row))
    in_specs = [pl.BlockSpec((tm, d), row),
                pl.BlockSpec((1, d), const),
                pl.BlockSpec((d, n_cols), const),
                pl.BlockSpec((1, A_WIDTH), const),
                pl.BlockSpec((1, A_WIDTH), const)]
    return pl.pallas_call(
        functools.partial(_in_proj_kernel, fold=fold, tiles_per_seq=tiles_per_seq, skip=skip),
        grid=(m // tm,), in_specs=in_specs, out_specs=out_specs, out_shape=out_shape,
        scratch_shapes=scratch, compiler_params=_params(1), name="in_proj",
    )(x2d, n1w, w_cat, qw, kw)


def _band_mask(pieces):
    sub = BAND // pieces
    r = lax.broadcasted_iota(jnp.int32, (BAND, 2 * BAND), 0)
    c = lax.broadcasted_iota(jnp.int32, (BAND, 2 * BAND), 1)
    qpos = (r % sub) * pieces + r // sub + BAND
    cc = c % BAND
    kpos = (cc % sub) * pieces + cc // sub + (c // BAND) * BAND
    dist = qpos - kpos
    return (dist >= 0) & (dist <= BAND), c >= BAND


ATTN_UNROLL = 8


def _attn_prompt_kernel(q_ref, k_ref, v_ref, o_ref, p_scr, op_scr, mx_scr, l_scr, nat_scr):
    u_len = q_ref.shape[2]
    lane = lax.broadcasted_iota(jnp.int32, (BAND, LANES), 1)
    head0 = lane < A_HEAD_DIM

    def per_head(col):
        return jnp.where(head0, jnp.broadcast_to(col[:BAND], (BAND, LANES)),
                         jnp.broadcast_to(col[BAND:], (BAND, LANES)))

    for pi, pieces in enumerate((16, 4, 1)):
        sub = BAND // pieces
        n_res = MAX_DILATION // pieces
        n_blk = u_len // sub
        band, _ = _band_mask(pieces)
        bias = jnp.where(band, 0.0, NEG_INF)
        bias2 = jnp.concatenate([bias, bias], axis=0)
        bias2_cur = bias2[:, BAND:]

        def locate(blk, n_res=n_res, sub=sub):
            m = blk // n_res
            cur = pl.multiple_of(m * sub, SUBLANES)
            prv = pl.multiple_of(jnp.maximum(m - 1, 0) * sub, SUBLANES)
            return blk % n_res, cur, prv

        def gather(ref, res, start, pieces=pieces, n_res=n_res, sub=sub):
            parts = [ref[0, a * n_res + res, pl.ds(start, sub), :] for a in range(pieces)]
            return parts[0] if pieces == 1 else jnp.concatenate(parts, axis=0)

        def scatter(ref, res, start, val, pi=pi, pieces=pieces, n_res=n_res, sub=sub):
            for a in range(pieces):
                ref[pi, a * n_res + res, pl.ds(start, sub), :] = val[a * sub:(a + 1) * sub]

        def probs(it, carry, first, bias2=bias2, bias2_cur=bias2_cur):
            for j, is_first in enumerate(first):
                blk = it * ATTN_UNROLL + j
                res, cur, prv = locate(blk)
                qb = gather(q_ref, res, cur)
                q2 = jnp.concatenate([jnp.where(head0, qb, 0.0), jnp.where(head0, 0.0, qb)], axis=0).astype(BF16)
                kc = gather(k_ref, res, cur)
                if is_first:
                    s = _dot_nt(q2, kc.astype(BF16)) + bias2_cur
                else:
                    kb = jnp.concatenate([gather(k_ref, res, prv), kc], axis=0).astype(BF16)
                    s = _dot_nt(q2, kb) + bias2
                mx = jnp.max(s, axis=-1, keepdims=True)
                p = jnp.exp2(s - mx)
                l = jnp.sum(p, axis=-1, keepdims=True)
                if is_first:
                    p_scr[blk, :, BAND:] = p.astype(BF16)
                else:
                    p_scr[blk] = p.astype(BF16)
                scatter(mx_scr, res, cur, per_head(mx))
                scatter(l_scr, res, cur, per_head(l))
            return carry

        def values(it, carry, first):
            for j, is_first in enumerate(first):
                blk = it * ATTN_UNROLL + j
                res, cur, prv = locate(blk)
                vc = gather(v_ref, res, cur)
                if is_first:
                    o2 = _dot(p_scr[blk, :, BAND:], vc.astype(BF16))
                else:
                    vb = jnp.concatenate([gather(v_ref, res, prv), vc], axis=0).astype(BF16)
                    o2 = _dot(p_scr[blk], vb)
                scatter(op_scr, res, cur, jnp.where(head0, o2[:BAND], o2[BAND:]))
            return carry

        n_it = n_res * n_blk // ATTN_UNROLL
        flags = [tuple(g * ATTN_UNROLL + j < n_res for j in range(ATTN_UNROLL)) for g in range(n_it)]
        segments = []
        for g, f in enumerate(flags):
            if segments and segments[-1][2] == f:
                segments[-1][1] = g + 1
            else:
                segments.append([g, g + 1, f])
        for phase in (probs, values):
            for lo, hi, f in segments:
                lax.fori_loop(lo, hi, functools.partial(phase, first=f), 0)

    for r in range(MAX_DILATION):
        m0, m1, m2 = mx_scr[0, r], mx_scr[1, r], mx_scr[2, r]
        big = jnp.maximum(jnp.maximum(m0, m1), m2)
        e0, e1, e2 = jnp.exp2(m0 - big), jnp.exp2(m1 - big), jnp.exp2(m2 - big)
        num = e0 * op_scr[0, r] + e1 * op_scr[1, r] + e2 * op_scr[2, r]
        den = e0 * l_scr[0, r] + e1 * l_scr[1, r] + e2 * l_scr[2, r]
        nat_scr[pl.ds(r, u_len, stride=MAX_DILATION), :] = num / den
    o_ref[0] = nat_scr[...].astype(BF16)


def _attn_prompt(qf, kf, vf):
    nb, _, u_len, _ = qf.shape
    seq = u_len * MAX_DILATION
    n_pairs = A_WIDTH // LANES
    spec = pl.BlockSpec((1, MAX_DILATION, u_len, LANES), lambda b, p: (b, 0, 0, p))
    return pl.pallas_call(
        _attn_prompt_kernel,
        grid=(nb, n_pairs),
        in_specs=[spec, spec, spec],
        out_specs=pl.BlockSpec((1, seq, LANES), lambda b, p: (b, 0, p)),
        out_shape=jax.ShapeDtypeStruct((nb, seq, A_WIDTH), BF16),
        scratch_shapes=[pltpu.VMEM((seq // BAND, 2 * BAND, 2 * BAND), BF16),
                        pltpu.VMEM((3, MAX_DILATION, u_len, LANES), F32),
                        pltpu.VMEM((3, MAX_DILATION, u_len, LANES), F32),
                        pltpu.VMEM((3, MAX_DILATION, u_len, LANES), F32),
                        pltpu.VMEM((seq, LANES), F32)],
        compiler_params=_params(2), name="attn_prompt",
    )(qf, kf, vf)


def _multiplicity(t, dd):
    cnt = jnp.zeros(dd.shape, F32)
    for w, d in DILATED_PATTERNS:
        ok = (dd >= 0) & (dd <= w) & (dd % d == 0)
        cnt = cnt + jnp.where(ok, 1.0, 0.0)
    return cnt


def _attn_sample_kernel(q_ref, kn_ref, vn_ref, kt_ref, vt_ref, o_ref):
    t_len = q_ref.shape[0]
    n_buf = kt_ref.shape[3]
    q = q_ref[...]
    kn = kn_ref[...]
    vn = vn_ref[...]
    tq = n_buf + lax.broadcasted_iota(jnp.int32, (t_len, n_buf), 0)
    cnt = _multiplicity(None, tq - lax.broadcasted_iota(jnp.int32, (t_len, n_buf), 1))
    cnt_new = _multiplicity(None, lax.broadcasted_iota(jnp.int32, (t_len, t_len), 0)
                            - lax.broadcasted_iota(jnp.int32, (t_len, t_len), 1))
    outs = []
    for h in range(A_HEADS):
        hs = slice(h * A_HEAD_DIM, (h + 1) * A_HEAD_DIM)
        qh = q[:, hs].astype(BF16)
        s = jnp.where(cnt > 0, _dot(qh, kt_ref[0, h].astype(BF16)), NEG_INF)
        sn = jnp.where(cnt_new > 0, _dot_nt(qh, kn[:, hs].astype(BF16)), NEG_INF)
        m = jnp.maximum(jnp.max(s, axis=-1, keepdims=True), jnp.max(sn, axis=-1, keepdims=True))
        p = cnt * jnp.exp(s - m)
        pn = cnt_new * jnp.exp(sn - m)
        l = jnp.sum(p, axis=-1, keepdims=True) + jnp.sum(pn, axis=-1, keepdims=True)
        o = _dot_nt(p.astype(BF16), vt_ref[0, h].astype(BF16)) + _dot(pn.astype(BF16), vn[:, hs].astype(BF16))
        outs.append(o / l)
    o_ref[...] = jnp.concatenate(outs, axis=-1)


def _attn_sample(q2d, kn2d, vn2d, cache_kt, cache_vt, *, t_len):
    nb, _, _, n_buf = cache_kt.shape
    tok = pl.BlockSpec((t_len, A_WIDTH), lambda b: (b, 0))
    win = pl.BlockSpec((1, A_HEADS, A_HEAD_DIM, n_buf), lambda b: (b, 0, 0, 0))
    return pl.pallas_call(
        _attn_sample_kernel,
        grid=(nb,),
        in_specs=[tok, tok, tok, win, win],
        out_specs=tok,
        out_shape=jax.ShapeDtypeStruct((nb * t_len, A_WIDTH), F32),
        compiler_params=_params(1), name="attn_sample",
    )(q2d, kn2d, vn2d, cache_kt, cache_vt)


def _unit_lower_inverse(a, n_chunk):
    n = a.shape[0]
    ri = lax.broadcasted_iota(jnp.int32, (n, n), 0)
    ci = lax.broadcasted_iota(jnp.int32, (n, n), 1)
    eye = jnp.where(ri == ci, 1.0, 0.0).astype(F32)
    t = eye - a
    pw = a
    k = 2
    while k < n_chunk:
        pw16 = pw.astype(BF16)
        pw = _dot(pw16, pw16)
        t = t + _dot(t.astype(BF16), pw.astype(BF16))
        k *= 2
    return t


def _gdn_intra(q, k, v, beta, dcol, drow, chunk):
    n = q.shape[0]
    ri = lax.broadcasted_iota(jnp.int32, (n, n), 0)
    ci = lax.broadcasted_iota(jnp.int32, (n, n), 1)
    same = (ri // chunk) == (ci // chunk)
    lower = same & (ri >= ci)
    strict = same & (ri > ci)
    gam = jnp.exp(jnp.where(lower, dcol - drow, NEG_INF))
    kb = k * beta
    vb = v * beta
    k16 = k.astype(BF16)
    a_mat = jnp.where(strict, _dot_nt(kb.astype(BF16), k16) * gam, 0.0)
    attn = jnp.where(lower, _dot_nt(q.astype(BF16), k16) * gam, 0.0)
    t_inv = _unit_lower_inverse(a_mat, chunk).astype(BF16)
    ed = jnp.exp(dcol)
    u = _dot(t_inv, vb.astype(BF16))
    w = _dot(t_inv, (kb * ed).astype(BF16))
    qd = q * ed
    return u, w, qd, attn


def _gdn_prep(c, bg, a_col, dt_col, chunk):
    n = c.shape[0]
    bgt = bg.T[0:2 * B_HEADS]
    row = lax.broadcasted_iota(jnp.int32, bgt.shape, 0)
    xg = bgt + dt_col
    sp = jnp.maximum(xg, 0.0) + jnp.log1p(jnp.exp(-jnp.abs(xg)))
    gt = jnp.where(row >= B_HEADS, -a_col * sp, 0.0)
    ri = lax.broadcasted_iota(jnp.int32, (n, n), 0)
    ci = lax.broadcasted_iota(jnp.int32, (n, n), 1)
    tri = jnp.where(((ri // chunk) == (ci // chunk)) & (ri >= ci), 1.0, 0.0).astype(BF16)
    th, tm_, tl = _split3(gt)
    drow = _dot_nt(th, tri) + _dot_nt(tm_, tri) + _dot_nt(tl, tri)
    both = jnp.where(row >= B_HEADS, drow, _sigmoid(bgt))
    cols = jnp.concatenate([both, jnp.zeros((LANES - 2 * B_HEADS, n), F32)], axis=0).T
    beta_all = cols
    dcol = cols
    heads = []
    kd_ = B_KEY_DIM
    for h in range(B_HEADS):
        qh = c[:, h * kd_:(h + 1) * kd_]
        kh = c[:, B_HEADS * kd_ + h * kd_:B_HEADS * kd_ + (h + 1) * kd_]
        vh = c[:, 2 * B_HEADS * kd_ + h * B_VAL_DIM:2 * B_HEADS * kd_ + (h + 1) * B_VAL_DIM]
        qh = qh * lax.rsqrt(jnp.sum(qh * qh, axis=-1, keepdims=True) + RMS_EPS) * (B_KEY_DIM ** -0.5)
        kh = kh * lax.rsqrt(jnp.sum(kh * kh, axis=-1, keepdims=True) + RMS_EPS)
        heads.append((qh, kh, vh, beta_all[:, h:h + 1], dcol[:, B_HEADS + h:B_HEADS + h + 1],
                      drow[B_HEADS + h:B_HEADS + h + 1, :]))
    return heads


def _gdn_out(o, onw, zh):
    ms = jnp.mean(o * o, axis=-1, keepdims=True)
    return o * lax.rsqrt(ms + RMS_EPS) * onw * _silu(zh)


def _gdn_prompt_kernel(x_ref, z_ref, bg_ref, cw_ref, cst_ref, sst_ref, avec_ref, dtb_ref, onw_ref,
                       o_ref, snew_ref, xp_scr, s_scr, *, block, chunk):
    ts = x_ref.shape[0]
    t = pl.program_id(1)
    pad = SUBLANES
    hist = CONV_WIDTH - 1

    @pl.when(t == 0)
    def _():
        xp_scr[...] = jnp.zeros((pad, CONV_DIM), F32)
        xp_scr[pad - hist:pad, :] = cst_ref[0]
        s_scr[...] = sst_ref[0]

    x = x_ref[...]
    tail = xp_scr[...]
    sub = lax.broadcasted_iota(jnp.int32, (pad, CONV_DIM), 0)
    y = x * cw_ref[hist:hist + 1, :]
    for s in range(1, CONV_WIDTH):
        xs = pltpu.roll(x, s, axis=0)
        head = jnp.where(sub < s, pltpu.roll(tail, s, axis=0), xs[0:pad])
        xs = jnp.concatenate([head, xs[pad:]], axis=0)
        y = y + xs * cw_ref[hist - s:hist - s + 1, :]
    xp_scr[...] = x[ts - pad:ts]
    c = _silu(y)
    avec = jnp.exp(avec_ref[...])
    onw = onw_ref[...]
    n_chunks = block // chunk
    n_sb = ts // block
    ri = lax.broadcasted_iota(jnp.int32, (block, block), 0)
    ci = lax.broadcasted_iota(jnp.int32, (block, block), 1)
    same = (ri // chunk) == (ci // chunk)
    lower = same & (ri >= ci)
    strict = same & (ri > ci)
    eye = jnp.where(ri == ci, 1.0, 0.0).astype(F32)

    units = []
    for sb in range(n_sb):
        rs = slice(sb * block, (sb + 1) * block)
        heads = _gdn_prep(c[rs], bg_ref[rs, :], avec, dtb_ref[...], chunk)
        for h, (qh, kh, vh, beta, dcol, drow) in enumerate(heads):
            units.append(dict(sb=sb, h=h, q=qh, k=kh, v=vh, beta=beta, dcol=dcol, drow=drow))

    for u in units:
        gam = jnp.exp(jnp.where(lower, u["dcol"] - u["drow"], NEG_INF))
        kb = u["k"] * u["beta"]
        kq = _dot_nt(jnp.concatenate([kb, u["q"]], axis=0).astype(BF16), u["k"].astype(BF16))
        u["pw"] = jnp.where(strict, kq[:block] * gam, 0.0)
        u["t"] = eye - u["pw"]
        u["attn"] = jnp.where(lower, kq[block:] * gam, 0.0).astype(BF16)
        ed = jnp.exp(u["dcol"])
        u["rhs"] = jnp.concatenate([u["v"] * u["beta"], kb * ed], axis=1).astype(BF16)
        u["qd"] = u["q"] * ed
        u["kt"] = u["k"].T

    kk = 2
    while kk < chunk:
        for u in units:
            p16 = u["pw"].astype(BF16)
            u["pw"] = _dot(p16, p16)
        for u in units:
            u["t"] = u["t"] + _dot(u["t"].astype(BF16), u["pw"].astype(BF16))
        kk *= 2

    for u in units:
        u["uw"] = _dot(u["t"].astype(BF16), u["rhs"]).astype(BF16)
    for u in units:
        au_aw = _dot(u["attn"], u["uw"])
        u["op"] = au_aw[:, :B_VAL_DIM]
        u["qp"] = (u["qd"] - au_aw[:, B_VAL_DIM:]).astype(BF16)
    for u in units:
        u["n"], u["mw"], u["e"] = [], [], []
        for cc in range(n_chunks):
            cs = slice(cc * chunk, (cc + 1) * chunk)
            last = u["drow"][:, (cc + 1) * chunk - 1:(cc + 1) * chunk]
            kdt = (u["kt"][:, cs] * jnp.exp(last - u["drow"][:, cs])).astype(BF16)
            nm = _dot(kdt, u["uw"][cs])
            u["n"].append(nm[:, :B_VAL_DIM])
            u["mw"].append(nm[:, B_VAL_DIM:].astype(BF16))
            u["e"].append(jnp.exp(last))

    s = [s_scr[h] for h in range(B_HEADS)]
    for sb in range(n_sb):
        rs = slice(sb * block, (sb + 1) * block)
        outs = [[] for _ in range(B_HEADS)]
        for cc in range(n_chunks):
            cs = slice(cc * chunk, (cc + 1) * chunk)
            for h in range(B_HEADS):
                u = units[sb * B_HEADS + h]
                r = _dot(jnp.concatenate([u["mw"][cc], u["qp"][cs]], axis=0), s[h].astype(BF16))
                outs[h].append(r[B_KEY_DIM:] + u["op"][cs])
                s[h] = s[h] * u["e"][cc] - r[:B_KEY_DIM] + u["n"][cc]
        for h in range(B_HEADS):
            o_all = jnp.concatenate(outs[h], axis=0) if n_chunks > 1 else outs[h][0]
            zh = z_ref[rs, h * B_VAL_DIM:(h + 1) * B_VAL_DIM].astype(F32)
            o_ref[rs, h * B_VAL_DIM:(h + 1) * B_VAL_DIM] = _gdn_out(o_all, onw, zh).astype(o_ref.dtype)
    for h in range(B_HEADS):
        s_scr[h] = s[h]

    @pl.when(t == pl.num_programs(1) - 1)
    def _():
        snew_ref[0] = s_scr[...]


def _gdn_prompt(bqkv, z, bg, conv_w, conv_state, ssm_state, a_log_v, dtb_v, onw, *, seq, ts, block, chunk):
    m = bqkv.shape[0]
    nb = m // seq
    tps = seq // ts
    row = lambda b, t: (b * tps + t, 0)
    const = lambda b, t: (0, 0)
    return pl.pallas_call(
        functools.partial(_gdn_prompt_kernel, block=block, chunk=chunk),
        grid=(nb, tps),
        in_specs=[pl.BlockSpec((ts, CONV_DIM), row),
                  pl.BlockSpec((ts, B_WIDTH), row),
                  pl.BlockSpec((ts, LANES), row),
                  pl.BlockSpec((CONV_WIDTH, CONV_DIM), const),
                  pl.BlockSpec((1, CONV_WIDTH - 1, CONV_DIM), lambda b, t: (b, 0, 0)),
                  pl.BlockSpec((1, B_HEADS, B_KEY_DIM, B_VAL_DIM), lambda b, t: (b, 0, 0, 0)),
                  pl.BlockSpec((2 * B_HEADS, LANES), const),
                  pl.BlockSpec((2 * B_HEADS, LANES), const),
                  pl.BlockSpec((1, B_VAL_DIM), const)],
        out_specs=(pl.BlockSpec((ts, B_WIDTH), row),
                   pl.BlockSpec((1, B_HEADS, B_KEY_DIM, B_VAL_DIM), lambda b, t: (b, 0, 0, 0))),
        out_shape=(jax.ShapeDtypeStruct((m, B_WIDTH), BF16),
                   jax.ShapeDtypeStruct(ssm_state.shape, F32)),
        scratch_shapes=[pltpu.VMEM((SUBLANES, CONV_DIM), F32),
                        pltpu.VMEM((B_HEADS, B_KEY_DIM, B_VAL_DIM), F32)],
        compiler_params=_params(2), name="gdn_prompt",
    )(bqkv, z, bg, conv_w, conv_state, ssm_state, a_log_v, dtb_v, onw)


def _gdn_sample_kernel(xp_ref, z_ref, bg_ref, cw_ref, sst_ref, avec_ref, dtb_ref, onw_ref,
                       o_ref, snew_ref, vn_scr, *, t_len):
    nseq = xp_ref.shape[0]
    hist = CONV_WIDTH - 1
    ys = []
    for j in range(nseq):
        y = xp_ref[j, 0:t_len, :] * cw_ref[0:1, :]
        for i in range(1, CONV_WIDTH):
            y = y + xp_ref[j, i:i + t_len, :] * cw_ref[i:i + 1, :]
        ys.append(y)
    c = _silu(jnp.concatenate(ys, axis=0))
    avec = jnp.exp(avec_ref[...])
    onw = onw_ref[...]
    heads = _gdn_prep(c, bg_ref[...], avec, dtb_ref[...], t_len)
    for h, (qh, kh, vh, beta, dcol, drow) in enumerate(heads):
        u, w, qd, attn = _gdn_intra(qh, kh, vh, beta, dcol, drow, t_len)
        attn16 = attn.astype(BF16)
        kt = kh.T
        vn_scr[...] = jnp.zeros(vn_scr.shape, F32)
        outs = []
        for j in range(nseq):
            cs = slice(j * t_len, (j + 1) * t_len)
            last = drow[:, (j + 1) * t_len - 1:(j + 1) * t_len]
            s = sst_ref[j, h]
            wq = jnp.concatenate([w[cs], qd[cs]], axis=0).astype(BF16)
            ws_qs = _dot(wq, s.astype(BF16))
            v_new = u[cs] - ws_qs[:t_len]
            vn_scr[cs, :] = v_new
            o = ws_qs[t_len:] + _dot(attn16[cs], vn_scr[...].astype(BF16))
            kdt = kt[:, cs] * jnp.exp(last - drow[:, cs])
            snew_ref[j, h] = s * jnp.exp(last) + _dot(kdt.astype(BF16), v_new.astype(BF16))
            outs.append(o)
        o_all = jnp.concatenate(outs, axis=0)
        zh = z_ref[:, h * B_VAL_DIM:(h + 1) * B_VAL_DIM].astype(F32)
        o_ref[:, h * B_VAL_DIM:(h + 1) * B_VAL_DIM] = _gdn_out(o_all, onw, zh).astype(o_ref.dtype)


def _gdn_sample(xp, z, bg, conv_w, ssm_state, a_log_v, dtb_v, onw, *, t_len, nseq):
    nb = xp.shape[0]
    rows = nseq * t_len
    row = lambda i: (i, 0)
    const = lambda i: (0, 0)
    return pl.pallas_call(
        functools.partial(_gdn_sample_kernel, t_len=t_len),
        grid=(nb // nseq,),
        in_specs=[pl.BlockSpec((nseq, xp.shape[1], CONV_DIM), lambda i: (i, 0, 0)),
                  pl.BlockSpec((rows, B_WIDTH), row),
                  pl.BlockSpec((rows, LANES), row),
                  pl.BlockSpec((CONV_WIDTH, CONV_DIM), const),
                  pl.BlockSpec((nseq, B_HEADS, B_KEY_DIM, B_VAL_DIM), lambda i: (i, 0, 0, 0)),
                  pl.BlockSpec((2 * B_HEADS, LANES), const),
                  pl.BlockSpec((2 * B_HEADS, LANES), const),
                  pl.BlockSpec((1, B_VAL_DIM), const)],
        out_specs=(pl.BlockSpec((rows, B_WIDTH), row),
                   pl.BlockSpec((nseq, B_HEADS, B_KEY_DIM, B_VAL_DIM), lambda i: (i, 0, 0, 0))),
        out_shape=(jax.ShapeDtypeStruct((nb * t_len, B_WIDTH), BF16),
                   jax.ShapeDtypeStruct(ssm_state.shape, F32)),
        scratch_shapes=[pltpu.VMEM((rows, B_VAL_DIM), F32)],
        compiler_params=_params(1), name="gdn_sample",
    )(xp, z, bg, conv_w, ssm_state, a_log_v, dtb_v, onw)


def _pack_bf16_pairs(x):
    n = x.shape[1] // 2
    bits = pltpu.bitcast(x.astype(BF16).astype(F32), jnp.int32)
    return lax.shift_right_logical(bits[:, :n], 16) | (bits[:, n:] & jnp.int32(-65536))


def _unpack_bf16_pairs(w):
    lo = pltpu.bitcast(lax.shift_left(w, 16), F32)
    hi = pltpu.bitcast(w & jnp.int32(-65536), F32)
    return jnp.concatenate([lo, hi], axis=1)


def _out_proj_kernel(x_ref, oa_ref, ob_ref, wa_ref, wb_ref, n2w_ref, wr_ref, br_ref, *rest, sparse):
    if sparse:
        tri_ref, h_ref, n2_ref, gates_ref, cnt_ref = rest
    else:
        h_ref, n2_ref, gates_ref = rest
    h = x_ref[...] + _dot(oa_ref[...].astype(BF16), wa_ref[...]) + _dot(ob_ref[...].astype(BF16), wb_ref[...])
    h_ref[...] = h
    ms = jnp.mean(h * h, axis=-1, keepdims=True)
    n2 = h * lax.rsqrt(ms + RMS_EPS) * n2w_ref[...]
    if sparse:
        n2_ref[...] = _pack_bf16_pairs(n2)
    else:
        n2_ref[...] = n2.astype(BF16)
    nh, nl = _split2(n2)
    wr = wr_ref[...]
    wh, wl = _split2(wr)
    logits = _dot(nh, wh) + _dot(nl, wh) + _dot(nh, wl) + br_ref[...]
    lane = lax.broadcasted_iota(jnp.int32, logits.shape, 1).astype(F32)
    big = 1e9
    gl = jnp.where(lane < N_GROUPS, logits, NEG_INF)
    gmax = jnp.max(gl, axis=-1, keepdims=True)
    gi = jnp.min(jnp.where(gl == gmax, lane, big), axis=-1, keepdims=True)
    g_sel = 1.0 / jnp.sum(jnp.exp(gl - gmax), axis=-1, keepdims=True)
    lo = N_GROUPS + EXPERTS_PER_GROUP * gi
    in_grp = (lane >= lo) & (lane < lo + EXPERTS_PER_GROUP)
    el = jnp.where(in_grp, logits, NEG_INF)
    v1 = jnp.max(el, axis=-1, keepdims=True)
    i1 = jnp.min(jnp.where(el == v1, lane, big), axis=-1, keepdims=True)
    el2 = jnp.where(lane == i1, NEG_INF, el)
    v2 = jnp.max(el2, axis=-1, keepdims=True)
    i2 = jnp.min(jnp.where(el2 == v2, lane, big), axis=-1, keepdims=True)
    e2 = jnp.exp(v2 - v1)
    w1 = g_sel / (1.0 + e2)
    w2 = g_sel * e2 / (1.0 + e2)
    if not sparse:
        gates_ref[...] = jnp.where(lane + N_GROUPS == i1, w1, 0.0) + jnp.where(lane + N_GROUPS == i2, w2, 0.0)
        return
    @pl.when(pl.program_id(0) == 0)
    def _():
        cnt_ref[...] = jnp.zeros(cnt_ref.shape, F32)

    e0 = i1 - N_GROUPS
    e1 = i2 - N_GROUPS
    onehot = jnp.where(lane == e0, 1.0, 0.0) + jnp.where(lane == e1, 1.0, 0.0)
    before = _dot(tri_ref[...], onehot.astype(BF16)) + cnt_ref[...]
    r0 = jnp.sum(jnp.where(lane == e0, before, 0.0), axis=-1, keepdims=True)
    r1 = jnp.sum(jnp.where(lane == e1, before, 0.0), axis=-1, keepdims=True)
    cnt_ref[...] += jnp.sum(onehot, axis=0, keepdims=True)
    rec = jnp.zeros(logits.shape, F32)
    for col, val in enumerate((e0, e1, w1, w2, r0, r1)):
        rec = jnp.where(lane == col, val, rec)
    gates_ref[...] = rec


def _out_proj(x2d, oa, ob, wa, wb, n2w, wr, br, *, tm, sparse):
    m, d = x2d.shape
    row = lambda i: (i, 0)
    const = lambda i: (0, 0)
    in_specs = [pl.BlockSpec((tm, d), row),
                pl.BlockSpec((tm, A_WIDTH), row),
                pl.BlockSpec((tm, B_WIDTH), row),
                pl.BlockSpec((A_WIDTH, d), const),
                pl.BlockSpec((B_WIDTH, d), const),
                pl.BlockSpec((1, d), const),
                pl.BlockSpec((d, LANES), const),
                pl.BlockSpec((1, LANES), const)]
    args = [x2d, oa, ob, wa, wb, n2w, wr, br]
    if sparse:
        idx = jnp.arange(tm)
        args.append((idx[:, None] > idx[None, :]).astype(BF16))
        in_specs.append(pl.BlockSpec((tm, tm), const))
        out_specs = (pl.BlockSpec((tm, d), row), pl.BlockSpec((tm, d // 2), row), pl.BlockSpec((tm, LANES), row),
                     pl.BlockSpec((1, LANES), const))
        out_shape = (jax.ShapeDtypeStruct((m, d), F32), jax.ShapeDtypeStruct((m, d // 2), jnp.int32),
                     jax.ShapeDtypeStruct((m, LANES), F32), jax.ShapeDtypeStruct((1, LANES), F32))
    else:
        out_specs = (pl.BlockSpec((tm, d), row), pl.BlockSpec((tm, d), row), pl.BlockSpec((tm, LANES), row))
        out_shape = (jax.ShapeDtypeStruct((m, d), F32), jax.ShapeDtypeStruct((m, d), BF16),
                     jax.ShapeDtypeStruct((m, LANES), F32))
    return pl.pallas_call(
        functools.partial(_out_proj_kernel, sparse=sparse),
        grid=(m // tm,), in_specs=in_specs, out_specs=out_specs, out_shape=out_shape,
        compiler_params=_params(1), name="out_proj",
    )(*args)


def _moe_kernel(h_ref, n2_ref, gates_ref, wgu_ref, wd_ref, y_ref):
    e = pl.program_id(1)

    @pl.when(e == 0)
    def _():
        y_ref[...] = h_ref[...]

    n2 = n2_ref[...]
    gu = _dot(n2, wgu_ref[0].astype(BF16))
    act = _silu(gu[:, :D_EXPERT]) * gu[:, D_EXPERT:]
    lane = lax.broadcasted_iota(jnp.int32, gates_ref.shape, 1)
    gate = jnp.sum(jnp.where(lane == e, gates_ref[...], 0.0), axis=-1, keepdims=True)
    y_ref[...] += gate * _dot(act.astype(BF16), wd_ref[0].astype(BF16))


def _moe(h, n2, gates, wgu, wd, *, tm):
    m, d = h.shape
    row = lambda i, e: (i, 0)
    return pl.pallas_call(
        _moe_kernel,
        grid=(m // tm, N_EXPERTS),
        in_specs=[pl.BlockSpec((tm, d), row),
                  pl.BlockSpec((tm, d), row),
                  pl.BlockSpec((tm, LANES), row),
                  pl.BlockSpec((1, d, 2 * D_EXPERT), lambda i, e: (e, 0, 0)),
                  pl.BlockSpec((1, D_EXPERT, d), lambda i, e: (e, 0, 0))],
        out_specs=pl.BlockSpec((tm, d), row),
        out_shape=jax.ShapeDtypeStruct((m, d), F32),
        compiler_params=_params(2), name="moe",
    )(h, n2, gates, wgu, wd)


SC_CORES = 2
SC_SUBCORES = 16
SC_WORKERS = SC_CORES * SC_SUBCORES
SC_CHUNK = 64
MOE_TILE = 512


def _sc_mesh():
    return plsc.VectorSubcoreMesh(core_axis_name="c", subcore_axis_name="s",
                                  num_cores=SC_CORES, num_subcores=SC_SUBCORES)


def _sc_dispatch(src, pos, n_rows):
    m, d = src.shape
    n_chunks = m // (SC_WORKERS * SC_CHUNK)
    pos4 = pos.reshape(2, SC_WORKERS, n_chunks, SC_CHUNK).transpose(1, 0, 2, 3)

    def body(src_hbm, pos_hbm, out_hbm, idx_v, rows_v):
        wid = lax.axis_index("s") * SC_CORES + lax.axis_index("c")
        pltpu.sync_copy(pos_hbm.at[wid], idx_v)

        @pl.loop(0, n_chunks)
        def _(j):
            start = pl.multiple_of((wid * n_chunks + j) * SC_CHUNK, SC_CHUNK)
            pltpu.sync_copy(src_hbm.at[pl.ds(start, SC_CHUNK)], rows_v)
            pltpu.sync_copy(rows_v, out_hbm.at[idx_v.at[0, j]])
            pltpu.sync_copy(rows_v, out_hbm.at[idx_v.at[1, j]])

    return pl.kernel(
        body, out_type=jax.ShapeDtypeStruct((n_rows, d), src.dtype), mesh=_sc_mesh(),
        scratch_types=[pltpu.VMEM((2, n_chunks, SC_CHUNK), jnp.int32), pltpu.VMEM((SC_CHUNK, d), src.dtype)],
        name="moe_dispatch",
    )(src, pos4)


def _sc_gather(table, idx):
    b = idx.shape[0]
    d = table.shape[1]
    n_chunks = b // (SC_WORKERS * SC_CHUNK)
    idx3 = idx.reshape(SC_WORKERS, n_chunks, SC_CHUNK)

    def body(table_hbm, idx_hbm, out_hbm, idx_v, rows_v):
        wid = lax.axis_index("s") * SC_CORES + lax.axis_index("c")
        pltpu.sync_copy(idx_hbm.at[wid], idx_v)

        @pl.loop(0, n_chunks)
        def _(j):
            start = pl.multiple_of((wid * n_chunks + j) * SC_CHUNK, SC_CHUNK)
            pltpu.sync_copy(table_hbm.at[idx_v.at[j]], rows_v)
            pltpu.sync_copy(rows_v, out_hbm.at[pl.ds(start, SC_CHUNK)])

    return pl.kernel(
        body, out_type=jax.ShapeDtypeStruct((b, d), table.dtype), mesh=_sc_mesh(),
        scratch_types=[pltpu.VMEM((n_chunks, SC_CHUNK), jnp.int32), pltpu.VMEM((SC_CHUNK, d), table.dtype)],
        name="moe_collect",
    )(table, idx3)


def _moe_grouped_kernel(te_ref, nu_ref, xs_ref, wgu_ref, wd_ref, ys_ref, wgu16, wd16):
    i = pl.program_id(0)

    @pl.when((i == 0) | (te_ref[i] != te_ref[jnp.maximum(i - 1, 0)]))
    def _():
        wgu16[...] = wgu_ref[0].astype(BF16)
        wd16[...] = wd_ref[0].astype(BF16)

    @pl.when(i < nu_ref[0])
    def _():
        x = _unpack_bf16_pairs(xs_ref[...]).astype(BF16)
        gu = _dot(x, wgu16[...])
        act = _silu(gu[:, :D_EXPERT]) * gu[:, D_EXPERT:]
        ys_ref[...] = _pack_bf16_pairs(_dot(act.astype(BF16), wd16[...]))

    @pl.when(i >= nu_ref[0])
    def _():
        ys_ref[...] = jnp.zeros(ys_ref.shape, ys_ref.dtype)


def _moe_grouped(xs, tile_expert, n_used, wgu, wd):
    r, half = xs.shape
    d = 2 * half
    grid_spec = pltpu.PrefetchScalarGridSpec(
        num_scalar_prefetch=2, grid=(r // MOE_TILE,),
        in_specs=[pl.BlockSpec((MOE_TILE, half), lambda i, te, nu: (i, 0)),
                  pl.BlockSpec((1, d, 2 * D_EXPERT), lambda i, te, nu: (te[i], 0, 0)),
                  pl.BlockSpec((1, D_EXPERT, d), lambda i, te, nu: (te[i], 0, 0))],
        out_specs=pl.BlockSpec((MOE_TILE, half), lambda i, te, nu: (i, 0)),
        scratch_shapes=[pltpu.VMEM((d, 2 * D_EXPERT), BF16), pltpu.VMEM((D_EXPERT, d), BF16)])
    return pl.pallas_call(
        _moe_grouped_kernel, grid_spec=grid_spec,
        out_shape=jax.ShapeDtypeStruct((r, half), jnp.int32),
        compiler_params=_params(1), name="moe_grouped",
    )(tile_expert, n_used, xs, wgu, wd)


def _moe_combine_kernel(h_ref, z0_ref, z1_ref, rec_ref, y_ref):
    rec = rec_ref[...]
    y_ref[...] = (h_ref[...] + rec[:, 2:3] * _unpack_bf16_pairs(z0_ref[...])
                  + rec[:, 3:4] * _unpack_bf16_pairs(z1_ref[...]))


def _moe_combine(h, z, rec, *, tm):
    m, d = h.shape
    nt = m // tm
    row = lambda i: (i, 0)
    return pl.pallas_call(
        _moe_combine_kernel, grid=(nt,),
        in_specs=[pl.BlockSpec((tm, d), row),
                  pl.BlockSpec((tm, d // 2), row),
                  pl.BlockSpec((tm, d // 2), lambda i: (i + nt, 0)),
                  pl.BlockSpec((tm, LANES), row)],
        out_specs=pl.BlockSpec((tm, d), row),
        out_shape=jax.ShapeDtypeStruct((m, d), F32),
        compiler_params=_params(1), name="moe_combine",
    )(h, z, z, rec)


def _moe_sparse(h, n2p, rec, cnt, wgu, wd):
    m = h.shape[0]
    n_tiles = 2 * m // MOE_TILE + N_EXPERTS
    counts = cnt[0, :N_EXPERTS].astype(jnp.int32)
    tiles = (counts + MOE_TILE - 1) // MOE_TILE
    tile_end = jnp.cumsum(tiles)
    row_start = (tile_end - tiles) * MOE_TILE
    eid = rec[:, 0:2].astype(jnp.int32).T
    rank = rec[:, 4:6].astype(jnp.int32).T
    pos = rank
    for e in range(N_EXPERTS):
        pos = pos + jnp.where(eid == e, row_start[e], 0)
    n_used = tile_end[-1:]
    tile_id = jnp.minimum(jnp.arange(n_tiles, dtype=jnp.int32), n_used[0] - 1)
    tile_expert = jnp.sum((tile_end[None, :] <= tile_id[:, None]).astype(jnp.int32), axis=1)
    xs = _sc_dispatch(n2p, pos, n_tiles * MOE_TILE)
    ys = _moe_grouped(xs, tile_expert, n_used.astype(jnp.int32), wgu, wd)
    z = _sc_gather(ys, pos.reshape(-1))
    return _moe_combine(h, z, rec, tm=MOE_TILE)


def _head_rows(v):
    rows = jnp.concatenate([jnp.zeros((B_HEADS,), F32), v.astype(F32)])
    return jnp.broadcast_to(rows[:, None], (2 * B_HEADS, LANES))


def kernel(x_prompt, x_sample, cache_win_k, cache_win_v, state_conv, state_ssm, norm1_w, w_in, qnorm_w, knorm_w, conv_w, a_log, dt_bias, onorm_w, w_out, norm2_w, w_group, b_group, w_expert_router, b_expert_router, w_gate_up, w_down):
    nb, seq, d = x_prompt.shape
    db, t_len, _ = x_sample.shape
    n_buf = cache_win_k.shape[1]
    assert n_buf == MAX_WINDOW and seq % (MAX_DILATION * BAND) == 0 and t_len == SUBLANES
    keep = min(MAX_WINDOW, seq)

    c_bg = 3 * A_WIDTH + CONV_DIM + B_WIDTH
    w_cat = jnp.concatenate([w_in, jnp.zeros((d, LANES - 2 * B_HEADS), w_in.dtype)], axis=1).astype(BF16)
    assert w_cat.shape[1] == c_bg + LANES
    n1w = norm1_w.reshape(1, d).astype(F32)
    qw = jnp.tile(qnorm_w.astype(F32), A_HEADS).reshape(1, A_WIDTH)
    kw = jnp.tile(knorm_w.astype(F32), A_HEADS).reshape(1, A_WIDTH)
    a_log_v = _head_rows(a_log)
    dtb_v = _head_rows(dt_bias)
    onw = onorm_w.reshape(1, B_VAL_DIM).astype(F32)
    wa = w_out[:A_WIDTH].astype(BF16)
    wb = w_out[A_WIDTH:].astype(BF16)
    n2w = norm2_w.reshape(1, d).astype(F32)
    wr = jnp.concatenate([w_group, jnp.transpose(w_expert_router, (1, 0, 2)).reshape(d, N_EXPERTS),
                          jnp.zeros((d, LANES - N_GROUPS - N_EXPERTS), F32)], axis=1).astype(F32)
    br = jnp.zeros((1, LANES), F32).at[0, :N_GROUPS].set(b_group).at[0, N_GROUPS:N_GROUPS + N_EXPERTS].set(
        b_expert_router.reshape(-1))
    wgu = w_gate_up.astype(F32)
    wd = w_down.astype(F32)
    cw = conv_w.astype(F32)

    def finish(x2d, oa, ob, tm):
        if x2d.shape[0] % (SC_WORKERS * SC_CHUNK) == 0 and x2d.shape[0] >= N_EXPERTS * MOE_TILE:
            h, n2p, rec, cnt = _out_proj(x2d, oa, ob, wa, wb, n2w, wr, br, tm=tm, sparse=True)
            return _moe_sparse(h, n2p, rec, cnt, wgu, wd)
        h, n2, gates = _out_proj(x2d, oa, ob, wa, wb, n2w, wr, br, tm=tm, sparse=False)
        return _moe(h, n2, gates, wgu, wd, tm=tm)

    xp2d = x_prompt.reshape(nb * seq, d)
    qf, kf, vf, kwin, vwin, bqkv, z, bg = _in_proj(xp2d, n1w, w_cat, qw, kw, seq=seq, keep=keep,
                                                   fold=True, tm=512)
    oa = _attn_prompt(qf, kf, vf).reshape(nb * seq, A_WIDTH)
    conv0 = jnp.zeros((nb, CONV_WIDTH - 1, CONV_DIM), F32)
    ssm0 = jnp.zeros((nb, B_HEADS, B_KEY_DIM, B_VAL_DIM), F32)
    ob, ssm_prompt = _gdn_prompt(bqkv, z, bg, cw, conv0, ssm0, a_log_v, dtb_v, onw,
                                 seq=seq, ts=512, block=128, chunk=CHUNK)
    y_prompt = finish(xp2d, oa, ob, 512).reshape(nb, seq, d)
    win_k_prompt = jnp.transpose(kwin, (0, 3, 1, 2))
    win_v_prompt = jnp.transpose(vwin, (0, 3, 1, 2))
    conv_prompt = bqkv.reshape(nb, seq, CONV_DIM)[:, seq - (CONV_WIDTH - 1):]

    ms = db * t_len
    xs2d = x_sample.reshape(ms, d)
    tms = min(512, ms)
    qs, ksn, vsn, _, _, bqkv_s, z_s, bg_s = _in_proj(xs2d, n1w, w_cat, qw, kw, seq=ms, keep=ms,
                                                     fold=False, tm=tms)
    cache_kt = jnp.transpose(cache_win_k.astype(F32), (0, 2, 3, 1))
    cache_vt = jnp.transpose(cache_win_v.astype(F32), (0, 2, 3, 1))
    oa_s = _attn_sample(qs, ksn, vsn, cache_kt, cache_vt, t_len=t_len)
    xpad = jnp.concatenate([state_conv.astype(F32), bqkv_s.reshape(db, t_len, CONV_DIM)], axis=1)
    nseq = 16 if db % 16 == 0 else 1
    ob_s, ssm_sample = _gdn_sample(xpad, z_s, bg_s, cw, state_ssm.astype(F32), a_log_v, dtb_v, onw,
                                   t_len=t_len, nseq=nseq)
    y_sample = finish(xs2d, oa_s, ob_s, tms).reshape(db, t_len, d)
    win_k_sample = ksn.reshape(db, t_len, A_HEADS, A_HEAD_DIM)
    win_v_sample = vsn.reshape(db, t_len, A_HEADS, A_HEAD_DIM)
    conv_sample = xpad[:, t_len:]

    return (y_prompt, y_sample, win_k_prompt, win_v_prompt, conv_prompt, ssm_prompt,
            win_k_sample, win_v_sample, conv_sample, ssm_sample)
```

```python
import functools

import jax
import jax.numpy as jnp
from jax import lax
from jax.experimental import pallas as pl
from jax.experimental.pallas import tpu as pltpu
from jax.experimental.pallas import tpu_sc as plsc

F32 = jnp.float32
BF16 = jnp.bfloat16

A_HEADS = 8
A_HEAD_DIM = 64
A_WIDTH = A_HEADS * A_HEAD_DIM
DILATED_PATTERNS = ((128, 1), (512, 4), (2048, 16))
MAX_WINDOW = 2048
MAX_DILATION = 16
BAND = 128
ATTN_SCALE = A_HEAD_DIM ** -0.5
LOG2_E = 1.4426950408889634
B_HEADS = 4
B_KEY_DIM = 128
B_VAL_DIM = 128
B_WIDTH = B_HEADS * B_VAL_DIM
CONV_WIDTH = 4
CONV_DIM = B_HEADS * (2 * B_KEY_DIM + B_VAL_DIM)
CHUNK = 64
N_GROUPS = 4
EXPERTS_PER_GROUP = 4
N_EXPERTS = N_GROUPS * EXPERTS_PER_GROUP
D_EXPERT = 512
RMS_EPS = 1e-6
NEG_INF = -1e30

LANES = 128
SUBLANES = 8
VMEM_LIMIT = 56 * 1024 * 1024


def _params(n_axes, vmem=VMEM_LIMIT):
    return pltpu.CompilerParams(dimension_semantics=("arbitrary",) * n_axes, vmem_limit_bytes=vmem)


def _split2(x):
    hi = x.astype(BF16)
    lo = (x - hi.astype(F32)).astype(BF16)
    return hi, lo


def _split3(x):
    hi = x.astype(BF16)
    r = x - hi.astype(F32)
    mid = r.astype(BF16)
    lo = (r - mid.astype(F32)).astype(BF16)
    return hi, mid, lo


def _dot(a, b):
    return jnp.dot(a, b, preferred_element_type=F32)


def _dot_nt(a, b):
    return lax.dot_general(a, b, (((1,), (1,)), ((), ())), preferred_element_type=F32)


def _dot_tn(a, b):
    return lax.dot_general(a, b, (((0,), (0,)), ((), ())), preferred_element_type=F32)


def _sigmoid(x):
    return 1.0 / (1.0 + jnp.exp(-x))


def _silu(x):
    return x * _sigmoid(x)


def _in_proj_kernel(x_ref, n1w_ref, w_ref, qw_ref, kw_ref,
                    q_ref, k_ref, v_ref, kwin_ref, vwin_ref, bqkv_ref, z_ref, bg_ref,
                    *scratch, fold, tiles_per_seq, skip):
    tm = x_ref.shape[0]
    x = x_ref[...]
    ms = jnp.mean(x * x, axis=-1, keepdims=True)
    n1 = (x * lax.rsqrt(ms + RMS_EPS) * n1w_ref[...]).astype(BF16)

    first_head = lax.broadcasted_iota(jnp.int32, (tm, LANES), 1) < A_HEAD_DIM

    def head_norm(t, w):
        cols = []
        for c in range(A_WIDTH // LANES):
            tc = t[:, c * LANES:(c + 1) * LANES]
            sq = tc * tc
            s0 = jnp.sum(jnp.where(first_head, sq, 0.0), axis=-1, keepdims=True)
            s1 = jnp.sum(jnp.where(first_head, 0.0, sq), axis=-1, keepdims=True)
            ms = jnp.where(first_head, s0, s1) * (1.0 / A_HEAD_DIM)
            cols.append(tc * lax.rsqrt(ms + RMS_EPS))
        return jnp.concatenate(cols, axis=1) * w

    q_scale = ATTN_SCALE * LOG2_E if fold else ATTN_SCALE
    q = head_norm(_dot(n1, w_ref[:, 0:A_WIDTH]), qw_ref[...]) * q_scale
    k = head_norm(_dot(n1, w_ref[:, A_WIDTH:2 * A_WIDTH]), kw_ref[...])
    v = _dot(n1, w_ref[:, 2 * A_WIDTH:3 * A_WIDTH])
    if fold:
        kwin_ref[0] = k.T.reshape(A_HEADS, A_HEAD_DIM, tm)
        vwin_ref[0] = v.T.reshape(A_HEADS, A_HEAD_DIM, tm)

        (scr,) = scratch
        rows = tm // MAX_DILATION
        for val, out in ((q, q_ref), (k, k_ref), (v, v_ref)):
            for c in range(A_WIDTH // LANES):
                cs = slice(c * LANES, (c + 1) * LANES)
                scr[c] = val[:, cs]
                for r in range(MAX_DILATION):
                    out[0, r, :, cs] = scr[c, pl.ds(r, rows, stride=MAX_DILATION), :]
    else:
        kwin_ref[...] = k
        vwin_ref[...] = v
        q_ref[...] = q
        k_ref[...] = k
        v_ref[...] = v
    c0 = 3 * A_WIDTH
    bqkv_ref[...] = _dot(n1, w_ref[:, c0:c0 + CONV_DIM])
    c1 = c0 + CONV_DIM
    z_ref[...] = _dot(n1, w_ref[:, c1:c1 + B_WIDTH]).astype(BF16)
    c2 = c1 + B_WIDTH
    bg_ref[...] = _dot(n1, w_ref[:, c2:c2 + LANES])


def _in_proj(x2d, n1w, w_cat, qw, kw, *, seq, keep, fold, tm):
    m, d = x2d.shape
    nb = m // seq
    tiles_per_seq = seq // tm
    skip = (seq - keep) // tm
    keep_tiles = keep // tm
    n_cols = w_cat.shape[1]

    def win_map(i):
        return (i // tiles_per_seq) * keep_tiles + jnp.maximum(i % tiles_per_seq - skip, 0), 0

    row = lambda i: (i, 0)
    const = lambda i: (0, 0)
    if fold:
        rows = tm // MAX_DILATION
        qkv_shape = jax.ShapeDtypeStruct((nb, MAX_DILATION, seq // MAX_DILATION, A_WIDTH), F32)
        qkv_spec = pl.BlockSpec((1, MAX_DILATION, rows, A_WIDTH),
                                lambda i: (i // tiles_per_seq, 0, i % tiles_per_seq, 0))
        scratch = [pltpu.VMEM((A_WIDTH // LANES, tm, LANES), F32)]
        win_shape = jax.ShapeDtypeStruct((nb, A_HEADS, A_HEAD_DIM, keep), F32)
        win_spec = pl.BlockSpec((1, A_HEADS, A_HEAD_DIM, tm),
                                lambda i: (i // tiles_per_seq, 0, 0, jnp.maximum(i % tiles_per_seq - skip, 0)))
    else:
        qkv_shape = jax.ShapeDtypeStruct((m, A_WIDTH), F32)
        qkv_spec = pl.BlockSpec((tm, A_WIDTH), row)
        scratch = []
        win_shape = jax.ShapeDtypeStruct((nb * keep, A_WIDTH), F32)
        win_spec = pl.BlockSpec((tm, A_WIDTH), win_map)
    out_shape = (qkv_shape, qkv_shape, qkv_shape, win_shape, win_shape,
                 jax.ShapeDtypeStruct((m, CONV_DIM), F32),
                 jax.ShapeDtypeStruct((m, B_WIDTH), BF16),
                 jax.ShapeDtypeStruct((m, LANES), F32))
    out_specs = (qkv_spec, qkv_spec, qkv_spec, win_spec, win_spec,
                 pl.BlockSpec((tm, CONV_DIM), row),
                 pl.BlockSpec((tm, B_WIDTH), row),
                 pl.BlockSpec((tm, LANES), row))
    in_specs = [pl.BlockSpec((tm, d), row),
                pl.BlockSpec((1, d), const),
                pl.BlockSpec((d, n_cols), const),
                pl.BlockSpec((1, A_WIDTH), const),
                pl.BlockSpec((1, A_WIDTH), const)]
    return pl.pallas_call(
        functools.partial(_in_proj_kernel, fold=fold, tiles_per_seq=tiles_per_seq, skip=skip),
        grid=(m // tm,), in_specs=in_specs, out_specs=out_specs, out_shape=out_shape,
        scratch_shapes=scratch, compiler_params=_params(1), name="in_proj",
    )(x2d, n1w, w_cat, qw, kw)


def _band_mask(pieces):
    sub = BAND // pieces
    r = lax.broadcasted_iota(jnp.int32, (BAND, 2 * BAND), 0)
    c = lax.broadcasted_iota(jnp.int32, (BAND, 2 * BAND), 1)
    qpos = (r % sub) * pieces + r // sub + BAND
    cc = c % BAND
    kpos = (cc % sub) * pieces + cc // sub + (c // BAND) * BAND
    dist = qpos - kpos
    return (dist >= 0) & (dist <= BAND), c >= BAND


ATTN_UNROLL = 8


def _attn_prompt_kernel(q_ref, k_ref, v_ref, o_ref, p_scr, op_scr, mx_scr, l_scr, nat_scr):
    u_len = q_ref.shape[2]
    lane = lax.broadcasted_iota(jnp.int32, (BAND, LANES), 1)
    head0 = lane < A_HEAD_DIM

    def per_head(col):
        return jnp.where(head0, jnp.broadcast_to(col[:BAND], (BAND, LANES)),
                         jnp.broadcast_to(col[BAND:], (BAND, LANES)))

    for pi, pieces in enumerate((16, 4, 1)):
        sub = BAND // pieces
        n_res = MAX_DILATION // pieces
        n_blk = u_len // sub
        band, _ = _band_mask(pieces)
        bias = jnp.where(band, 0.0, NEG_INF)
        bias2 = jnp.concatenate([bias, bias], axis=0)
        bias2_cur = bias2[:, BAND:]

        def locate(blk, n_res=n_res, sub=sub):
            m = blk // n_res
            cur = pl.multiple_of(m * sub, SUBLANES)
            prv = pl.multiple_of(jnp.maximum(m - 1, 0) * sub, SUBLANES)
            return blk % n_res, cur, prv

        def gather(ref, res, start, pieces=pieces, n_res=n_res, sub=sub):
            parts = [ref[0, a * n_res + res, pl.ds(start, sub), :] for a in range(pieces)]
            return parts[0] if pieces == 1 else jnp.concatenate(parts, axis=0)

        def scatter(ref, res, start, val, pi=pi, pieces=pieces, n_res=n_res, sub=sub):
            for a in range(pieces):
                ref[pi, a * n_res + res, pl.ds(start, sub), :] = val[a * sub:(a + 1) * sub]

        def probs(it, carry, first, bias2=bias2, bias2_cur=bias2_cur):
            for j, is_first in enumerate(first):
                blk = it * ATTN_UNROLL + j
                res, cur, prv = locate(blk)
                qb = gather(q_ref, res, cur)
                q2 = jnp.concatenate([jnp.where(head0, qb, 0.0), jnp.where(head0, 0.0, qb)], axis=0).astype(BF16)
                kc = gather(k_ref, res, cur)
                if is_first:
                    s = _dot_nt(q2, kc.astype(BF16)) + bias2_cur
                else:
                    kb = jnp.concatenate([gather(k_ref, res, prv), kc], axis=0).astype(BF16)
                    s = _dot_nt(q2, kb) + bias2
                mx = jnp.max(s, axis=-1, keepdims=True)
                p = jnp.exp2(s - mx)
                l = jnp.sum(p, axis=-1, keepdims=True)
                if is_first:
                    p_scr[blk, :, BAND:] = p.astype(BF16)
                else:
                    p_scr[blk] = p.astype(BF16)
                scatter(mx_scr, res, cur, per_head(mx))
                scatter(l_scr, res, cur, per_head(l))
            return carry

        def values(it, carry, first):
            for j, is_first in enumerate(first):
                blk = it * ATTN_UNROLL + j
                res, cur, prv = locate(blk)
                vc = gather(v_ref, res, cur)
                if is_first:
                    o2 = _dot(p_scr[blk, :, BAND:], vc.astype(BF16))
                else:
                    vb = jnp.concatenate([gather(v_ref, res, prv), vc], axis=0).astype(BF16)
                    o2 = _dot(p_scr[blk], vb)
                scatter(op_scr, res, cur, jnp.where(head0, o2[:BAND], o2[BAND:]))
            return carry

        n_it = n_res * n_blk // ATTN_UNROLL
        flags = [tuple(g * ATTN_UNROLL + j < n_res for j in range(ATTN_UNROLL)) for g in range(n_it)]
        segments = []
        for g, f in enumerate(flags):
            if segments and segments[-1][2] == f:
                segments[-1][1] = g + 1
            else:
                segments.append([g, g + 1, f])
        for phase in (probs, values):
            for lo, hi, f in segments:
                lax.fori_loop(lo, hi, functools.partial(phase, first=f), 0)

    for r in range(MAX_DILATION):
        m0, m1, m2 = mx_scr[0, r], mx_scr[1, r], mx_scr[2, r]
        big = jnp.maximum(jnp.maximum(m0, m1), m2)
        e0, e1, e2 = jnp.exp2(m0 - big), jnp.exp2(m1 - big), jnp.exp2(m2 - big)
        num = e0 * op_scr[0, r] + e1 * op_scr[1, r] + e2 * op_scr[2, r]
        den = e0 * l_scr[0, r] + e1 * l_scr[1, r] + e2 * l_scr[2, r]
        nat_scr[pl.ds(r, u_len, stride=MAX_DILATION), :] = num / den
    o_ref[0] = nat_scr[...].astype(BF16)


def _attn_prompt(qf, kf, vf):
    nb, _, u_len, _ = qf.shape
    seq = u_len * MAX_DILATION
    n_pairs = A_WIDTH // LANES
    spec = pl.BlockSpec((1, MAX_DILATION, u_len, LANES), lambda b, p: (b, 0, 0, p))
    return pl.pallas_call(
        _attn_prompt_kernel,
        grid=(nb, n_pairs),
        in_specs=[spec, spec, spec],
        out_specs=pl.BlockSpec((1, seq, LANES), lambda b, p: (b, 0, p)),
        out_shape=jax.ShapeDtypeStruct((nb, seq, A_WIDTH), BF16),
        scratch_shapes=[pltpu.VMEM((seq // BAND, 2 * BAND, 2 * BAND), BF16),
                        pltpu.VMEM((3, MAX_DILATION, u_len, LANES), F32),
                        pltpu.VMEM((3, MAX_DILATION, u_len, LANES), F32),
                        pltpu.VMEM((3, MAX_DILATION, u_len, LANES), F32),
                        pltpu.VMEM((seq, LANES), F32)],
        compiler_params=_params(2), name="attn_prompt",
    )(qf, kf, vf)


def _multiplicity(t, dd):
    cnt = jnp.zeros(dd.shape, F32)
    for w, d in DILATED_PATTERNS:
        ok = (dd >= 0) & (dd <= w) & (dd % d == 0)
        cnt = cnt + jnp.where(ok, 1.0, 0.0)
    return cnt


def _attn_sample_kernel(q_ref, kn_ref, vn_ref, kt_ref, vt_ref, o_ref):
    t_len = q_ref.shape[0]
    n_buf = kt_ref.shape[3]
    q = q_ref[...]
    kn = kn_ref[...]
    vn = vn_ref[...]
    tq = n_buf + lax.broadcasted_iota(jnp.int32, (t_len, n_buf), 0)
    cnt = _multiplicity(None, tq - lax.broadcasted_iota(jnp.int32, (t_len, n_buf), 1))
    cnt_new = _multiplicity(None, lax.broadcasted_iota(jnp.int32, (t_len, t_len), 0)
                            - lax.broadcasted_iota(jnp.int32, (t_len, t_len), 1))
    outs = []
    for h in range(A_HEADS):
        hs = slice(h * A_HEAD_DIM, (h + 1) * A_HEAD_DIM)
        qh = q[:, hs].astype(BF16)
        s = jnp.where(cnt > 0, _dot(qh, kt_ref[0, h].astype(BF16)), NEG_INF)
        sn = jnp.where(cnt_new > 0, _dot_nt(qh, kn[:, hs].astype(BF16)), NEG_INF)
        m = jnp.maximum(jnp.max(s, axis=-1, keepdims=True), jnp.max(sn, axis=-1, keepdims=True))
        p = cnt * jnp.exp(s - m)
        pn = cnt_new * jnp.exp(sn - m)
        l = jnp.sum(p, axis=-1, keepdims=True) + jnp.sum(pn, axis=-1, keepdims=True)
        o = _dot_nt(p.astype(BF16), vt_ref[0, h].astype(BF16)) + _dot(pn.astype(BF16), vn[:, hs].astype(BF16))
        outs.append(o / l)
    o_ref[...] = jnp.concatenate(outs, axis=-1)


def _attn_sample(q2d, kn2d, vn2d, cache_kt, cache_vt, *, t_len):
    nb, _, _, n_buf = cache_kt.shape
    tok = pl.BlockSpec((t_len, A_WIDTH), lambda b: (b, 0))
    win = pl.BlockSpec((1, A_HEADS, A_HEAD_DIM, n_buf), lambda b: (b, 0, 0, 0))
    return pl.pallas_call(
        _attn_sample_kernel,
        grid=(nb,),
        in_specs=[tok, tok, tok, win, win],
        out_specs=tok,
        out_shape=jax.ShapeDtypeStruct((nb * t_len, A_WIDTH), F32),
        compiler_params=_params(1), name="attn_sample",
    )(q2d, kn2d, vn2d, cache_kt, cache_vt)


def _unit_lower_inverse(a, n_chunk):
    n = a.shape[0]
    ri = lax.broadcasted_iota(jnp.int32, (n, n), 0)
    ci = lax.broadcasted_iota(jnp.int32, (n, n), 1)
    eye = jnp.where(ri == ci, 1.0, 0.0).astype(F32)
    t = eye - a
    pw = a
    k = 2
    while k < n_chunk:
        pw16 = pw.astype(BF16)
        pw = _dot(pw16, pw16)
        t = t + _dot(t.astype(BF16), pw.astype(BF16))
        k *= 2
    return t


def _gdn_intra(q, k, v, beta, dcol, drow, chunk):
    n = q.shape[0]
    ri = lax.broadcasted_iota(jnp.int32, (n, n), 0)
    ci = lax.broadcasted_iota(jnp.int32, (n, n), 1)
    same = (ri // chunk) == (ci // chunk)
    lower = same & (ri >= ci)
    strict = same & (ri > ci)
    gam = jnp.exp(jnp.where(lower, dcol - drow, NEG_INF))
    kb = k * beta
    vb = v * beta
    k16 = k.astype(BF16)
    a_mat = jnp.where(strict, _dot_nt(kb.astype(BF16), k16) * gam, 0.0)
    attn = jnp.where(lower, _dot_nt(q.astype(BF16), k16) * gam, 0.0)
    t_inv = _unit_lower_inverse(a_mat, chunk).astype(BF16)
    ed = jnp.exp(dcol)
    u = _dot(t_inv, vb.astype(BF16))
    w = _dot(t_inv, (kb * ed).astype(BF16))
    qd = q * ed
    return u, w, qd, attn


def _gdn_prep(c, bg, a_col, dt_col, chunk):
    n = c.shape[0]
    bgt = bg.T[0:2 * B_HEADS]
    row = lax.broadcasted_iota(jnp.int32, bgt.shape, 0)
    xg = bgt + dt_col
    sp = jnp.maximum(xg, 0.0) + jnp.log1p(jnp.exp(-jnp.abs(xg)))
    gt = jnp.where(row >= B_HEADS, -a_col * sp, 0.0)
    ri = lax.broadcasted_iota(jnp.int32, (n, n), 0)
    ci = lax.broadcasted_iota(jnp.int32, (n, n), 1)
    tri = jnp.where(((ri // chunk) == (ci // chunk)) & (ri >= ci), 1.0, 0.0).astype(BF16)
    th, tm_, tl = _split3(gt)
    drow = _dot_nt(th, tri) + _dot_nt(tm_, tri) + _dot_nt(tl, tri)
    both = jnp.where(row >= B_HEADS, drow, _sigmoid(bgt))
    cols = jnp.concatenate([both, jnp.zeros((LANES - 2 * B_HEADS, n), F32)], axis=0).T
    beta_all = cols
    dcol = cols
    heads = []
    kd_ = B_KEY_DIM
    for h in range(B_HEADS):
        qh = c[:, h * kd_:(h + 1) * kd_]
        kh = c[:, B_HEADS * kd_ + h * kd_:B_HEADS * kd_ + (h + 1) * kd_]
        vh = c[:, 2 * B_HEADS * kd_ + h * B_VAL_DIM:2 * B_HEADS * kd_ + (h + 1) * B_VAL_DIM]
        qh = qh * lax.rsqrt(jnp.sum(qh * qh, axis=-1, keepdims=True) + RMS_EPS) * (B_KEY_DIM ** -0.5)
        kh = kh * lax.rsqrt(jnp.sum(kh * kh, axis=-1, keepdims=True) + RMS_EPS)
        heads.append((qh, kh, vh, beta_all[:, h:h + 1], dcol[:, B_HEADS + h:B_HEADS + h + 1],
                      drow[B_HEADS + h:B_HEADS + h + 1, :]))
    return heads


def _gdn_out(o, onw, zh):
    ms = jnp.mean(o * o, axis=-1, keepdims=True)
    return o * lax.rsqrt(ms + RMS_EPS) * onw * _silu(zh)


def _gdn_prompt_kernel(x_ref, z_ref, bg_ref, cw_ref, cst_ref, sst_ref, avec_ref, dtb_ref, onw_ref,
                       o_ref, snew_ref, xp_scr, s_scr, *, block, chunk):
    ts = x_ref.shape[0]
    t = pl.program_id(1)
    pad = SUBLANES
    hist = CONV_WIDTH - 1

    @pl.when(t == 0)
    def _():
        xp_scr[...] = jnp.zeros((pad, CONV_DIM), F32)
        xp_scr[pad - hist:pad, :] = cst_ref[0]
        s_scr[...] = sst_ref[0]

    x = x_ref[...]
    tail = xp_scr[...]
    sub = lax.broadcasted_iota(jnp.int32, (pad, CONV_DIM), 0)
    y = x * cw_ref[hist:hist + 1, :]
    for s in range(1, CONV_WIDTH):
        xs = pltpu.roll(x, s, axis=0)
        head = jnp.where(sub < s, pltpu.roll(tail, s, axis=0), xs[0:pad])
        xs = jnp.concatenate([head, xs[pad:]], axis=0)
        y = y + xs * cw_ref[hist - s:hist - s + 1, :]
    xp_scr[...] = x[ts - pad:ts]
    c = _silu(y)
    avec = jnp.exp(avec_ref[...])
    onw = onw_ref[...]
    n_chunks = block // chunk
    n_sb = ts // block
    ri = lax.broadcasted_iota(jnp.int32, (block, block), 0)
    ci = lax.broadcasted_iota(jnp.int32, (block, block), 1)
    same = (ri // chunk) == (ci // chunk)
    lower = same & (ri >= ci)
    strict = same & (ri > ci)
    eye = jnp.where(ri == ci, 1.0, 0.0).astype(F32)

    units = []
    for sb in range(n_sb):
        rs = slice(sb * block, (sb + 1) * block)
        heads = _gdn_prep(c[rs], bg_ref[rs, :], avec, dtb_ref[...], chunk)
        for h, (qh, kh, vh, beta, dcol, drow) in enumerate(heads):
            units.append(dict(sb=sb, h=h, q=qh, k=kh, v=vh, beta=beta, dcol=dcol, drow=drow))

    for u in units:
        gam = jnp.exp(jnp.where(lower, u["dcol"] - u["drow"], NEG_INF))
        kb = u["k"] * u["beta"]
        kq = _dot_nt(jnp.concatenate([kb, u["q"]], axis=0).astype(BF16), u["k"].astype(BF16))
        u["pw"] = jnp.where(strict, kq[:block] * gam, 0.0)
        u["t"] = eye - u["pw"]
        u["attn"] = jnp.where(lower, kq[block:] * gam, 0.0).astype(BF16)
        ed = jnp.exp(u["dcol"])
        u["rhs"] = jnp.concatenate([u["v"] * u["beta"], kb * ed], axis=1).astype(BF16)
        u["qd"] = u["q"] * ed
        u["kt"] = u["k"].T

    kk = 2
    while kk < chunk:
        for u in units:
            p16 = u["pw"].astype(BF16)
            u["pw"] = _dot(p16, p16)
        for u in units:
            u["t"] = u["t"] + _dot(u["t"].astype(BF16), u["pw"].astype(BF16))
        kk *= 2

    for u in units:
        u["uw"] = _dot(u["t"].astype(BF16), u["rhs"]).astype(BF16)
    for u in units:
        au_aw = _dot(u["attn"], u["uw"])
        u["op"] = au_aw[:, :B_VAL_DIM]
        u["qp"] = (u["qd"] - au_aw[:, B_VAL_DIM:]).astype(BF16)
    for u in units:
        u["n"], u["mw"], u["e"] = [], [], []
        for cc in range(n_chunks):
            cs = slice(cc * chunk, (cc + 1) * chunk)
            last = u["drow"][:, (cc + 1) * chunk - 1:(cc + 1) * chunk]
            kdt = (u["kt"][:, cs] * jnp.exp(last - u["drow"][:, cs])).astype(BF16)
            nm = _dot(kdt, u["uw"][cs])
            u["n"].append(nm[:, :B_VAL_DIM])
            u["mw"].append(nm[:, B_VAL_DIM:].astype(BF16))
            u["e"].append(jnp.exp(last))

    s = [s_scr[h] for h in range(B_HEADS)]
    for sb in range(n_sb):
        rs = slice(sb * block, (sb + 1) * block)
        outs = [[] for _ in range(B_HEADS)]
        for cc in range(n_chunks):
            cs = slice(cc * chunk, (cc + 1) * chunk)
            for h in range(B_HEADS):
                u = units[sb * B_HEADS + h]
                r = _dot(jnp.concatenate([u["mw"][cc], u["qp"][cs]], axis=0), s[h].astype(BF16))
                outs[h].append(r[B_KEY_DIM:] + u["op"][cs])
                s[h] = s[h] * u["e"][cc] - r[:B_KEY_DIM] + u["n"][cc]
        for h in range(B_HEADS):
            o_all = jnp.concatenate(outs[h], axis=0) if n_chunks > 1 else outs[h][0]
            zh = z_ref[rs, h * B_VAL_DIM:(h + 1) * B_VAL_DIM].astype(F32)
            o_ref[rs, h * B_VAL_DIM:(h + 1) * B_VAL_DIM] = _gdn_out(o_all, onw, zh).astype(o_ref.dtype)
    for h in range(B_HEADS):
        s_scr[h] = s[h]

    @pl.when(t == pl.num_programs(1) - 1)
    def _():
        snew_ref[0] = s_scr[...]


def _gdn_prompt(bqkv, z, bg, conv_w, conv_state, ssm_state, a_log_v, dtb_v, onw, *, seq, ts, block, chunk):
    m = bqkv.shape[0]
    nb = m // seq
    tps = seq // ts
    row = lambda b, t: (b * tps + t, 0)
    const = lambda b, t: (0, 0)
    return pl.pallas_call(
        functools.partial(_gdn_prompt_kernel, block=block, chunk=chunk),
        grid=(nb, tps),
        in_specs=[pl.BlockSpec((ts, CONV_DIM), row),
                  pl.BlockSpec((ts, B_WIDTH), row),
                  pl.BlockSpec((ts, LANES), row),
                  pl.BlockSpec((CONV_WIDTH, CONV_DIM), const),
                  pl.BlockSpec((1, CONV_WIDTH - 1, CONV_DIM), lambda b, t: (b, 0, 0)),
                  pl.BlockSpec((1, B_HEADS, B_KEY_DIM, B_VAL_DIM), lambda b, t: (b, 0, 0, 0)),
                  pl.BlockSpec((2 * B_HEADS, LANES), const),
                  pl.BlockSpec((2 * B_HEADS, LANES), const),
                  pl.BlockSpec((1, B_VAL_DIM), const)],
        out_specs=(pl.BlockSpec((ts, B_WIDTH), row),
                   pl.BlockSpec((1, B_HEADS, B_KEY_DIM, B_VAL_DIM), lambda b, t: (b, 0, 0, 0))),
        out_shape=(jax.ShapeDtypeStruct((m, B_WIDTH), BF16),
                   jax.ShapeDtypeStruct(ssm_state.shape, F32)),
        scratch_shapes=[pltpu.VMEM((SUBLANES, CONV_DIM), F32),
                        pltpu.VMEM((B_HEADS, B_KEY_DIM, B_VAL_DIM), F32)],
        compiler_params=_params(2), name="gdn_prompt",
    )(bqkv, z, bg, conv_w, conv_state, ssm_state, a_log_v, dtb_v, onw)


def _gdn_sample_kernel(xp_ref, z_ref, bg_ref, cw_ref, sst_ref, avec_ref, dtb_ref, onw_ref,
                       o_ref, snew_ref, vn_scr, *, t_len):
    nseq = xp_ref.shape[0]
    hist = CONV_WIDTH - 1
    ys = []
    for j in range(nseq):
        y = xp_ref[j, 0:t_len, :] * cw_ref[0:1, :]
        for i in range(1, CONV_WIDTH):
            y = y + xp_ref[j, i:i + t_len, :] * cw_ref[i:i + 1, :]
        ys.append(y)
    c = _silu(jnp.concatenate(ys, axis=0))
    avec = jnp.exp(avec_ref[...])
    onw = onw_ref[...]
    heads = _gdn_prep(c, bg_ref[...], avec, dtb_ref[...], t_len)
    for h, (qh, kh, vh, beta, dcol, drow) in enumerate(heads):
        u, w, qd, attn = _gdn_intra(qh, kh, vh, beta, dcol, drow, t_len)
        attn16 = attn.astype(BF16)
        kt = kh.T
        vn_scr[...] = jnp.zeros(vn_scr.shape, F32)
        outs = []
        for j in range(nseq):
            cs = slice(j * t_len, (j + 1) * t_len)
            last = drow[:, (j + 1) * t_len - 1:(j + 1) * t_len]
            s = sst_ref[j, h]
            wq = jnp.concatenate([w[cs], qd[cs]], axis=0).astype(BF16)
            ws_qs = _dot(wq, s.astype(BF16))
            v_new = u[cs] - ws_qs[:t_len]
            vn_scr[cs, :] = v_new
            o = ws_qs[t_len:] + _dot(attn16[cs], vn_scr[...].astype(BF16))
            kdt = kt[:, cs] * jnp.exp(last - drow[:, cs])
            snew_ref[j, h] = s * jnp.exp(last) + _dot(kdt.astype(BF16), v_new.astype(BF16))
            outs.append(o)
        o_all = jnp.concatenate(outs, axis=0)
        zh = z_ref[:, h * B_VAL_DIM:(h + 1) * B_VAL_DIM].astype(F32)
        o_ref[:, h * B_VAL_DIM:(h + 1) * B_VAL_DIM] = _gdn_out(o_all, onw, zh).astype(o_ref.dtype)


def _gdn_sample(xp, z, bg, conv_w, ssm_state, a_log_v, dtb_v, onw, *, t_len, nseq):
    nb = xp.shape[0]
    rows = nseq * t_len
    row = lambda i: (i, 0)
    const = lambda i: (0, 0)
    return pl.pallas_call(
        functools.partial(_gdn_sample_kernel, t_len=t_len),
        grid=(nb // nseq,),
        in_specs=[pl.BlockSpec((nseq, xp.shape[1], CONV_DIM), lambda i: (i, 0, 0)),
                  pl.BlockSpec((rows, B_WIDTH), row),
                  pl.BlockSpec((rows, LANES), row),
                  pl.BlockSpec((CONV_WIDTH, CONV_DIM), const),
                  pl.BlockSpec((nseq, B_HEADS, B_KEY_DIM, B_VAL_DIM), lambda i: (i, 0, 0, 0)),
                  pl.BlockSpec((2 * B_HEADS, LANES), const),
                  pl.BlockSpec((2 * B_HEADS, LANES), const),
                  pl.BlockSpec((1, B_VAL_DIM), const)],
        out_specs=(pl.BlockSpec((rows, B_WIDTH), row),
                   pl.BlockSpec((nseq, B_HEADS, B_KEY_DIM, B_VAL_DIM), lambda i: (i, 0, 0, 0))),
        out_shape=(jax.ShapeDtypeStruct((nb * t_len, B_WIDTH), BF16),
                   jax.ShapeDtypeStruct(ssm_state.shape, F32)),
        scratch_shapes=[pltpu.VMEM((rows, B_VAL_DIM), F32)],
        compiler_params=_params(1), name="gdn_sample",
    )(xp, z, bg, conv_w, ssm_state, a_log_v, dtb_v, onw)


def _pack_bf16_pairs(x):
    n = x.shape[1] // 2
    bits = pltpu.bitcast(x.astype(BF16).astype(F32), jnp.int32)
    return lax.shift_right_logical(bits[:, :n], 16) | (bits[:, n:] & jnp.int32(-65536))


def _unpack_bf16_pairs(w):
    lo = pltpu.bitcast(lax.shift_left(w, 16), F32)
    hi = pltpu.bitcast(w & jnp.int32(-65536), F32)
    return jnp.concatenate([lo, hi], axis=1)


def _out_proj_kernel(x_ref, oa_ref, ob_ref, wa_ref, wb_ref, n2w_ref, wr_ref, br_ref, *rest, sparse):
    if sparse:
        tri_ref, h_ref, n2_ref, gates_ref, cnt_ref = rest
    else:
        h_ref, n2_ref, gates_ref = rest
    h = x_ref[...] + _dot(oa_ref[...].astype(BF16), wa_ref[...]) + _dot(ob_ref[...].astype(BF16), wb_ref[...])
    h_ref[...] = h
    ms = jnp.mean(h * h, axis=-1, keepdims=True)
    n2 = h * lax.rsqrt(ms + RMS_EPS) * n2w_ref[...]
    if sparse:
        n2_ref[...] = _pack_bf16_pairs(n2)
    else:
        n2_ref[...] = n2.astype(BF16)
    nh, nl = _split2(n2)
    wr = wr_ref[...]
    wh, wl = _split2(wr)
    logits = _dot(nh, wh) + _dot(nl, wh) + _dot(nh, wl) + br_ref[...]
    lane = lax.broadcasted_iota(jnp.int32, logits.shape, 1).astype(F32)
    big = 1e9
    gl = jnp.where(lane < N_GROUPS, logits, NEG_INF)
    gmax = jnp.max(gl, axis=-1, keepdims=True)
    gi = jnp.min(jnp.where(gl == gmax, lane, big), axis=-1, keepdims=True)
    g_sel = 1.0 / jnp.sum(jnp.exp(gl - gmax), axis=-1, keepdims=True)
    lo = N_GROUPS + EXPERTS_PER_GROUP * gi
    in_grp = (lane >= lo) & (lane < lo + EXPERTS_PER_GROUP)
    el = jnp.where(in_grp, logits, NEG_INF)
    v1 = jnp.max(el, axis=-1, keepdims=True)
    i1 = jnp.min(jnp.where(el == v1, lane, big), axis=-1, keepdims=True)
    el2 = jnp.where(lane == i1, NEG_INF, el)
    v2 = jnp.max(el2, axis=-1, keepdims=True)
    i2 = jnp.min(jnp.where(el2 == v2, lane, big), axis=-1, keepdims=True)
    e2 = jnp.exp(v2 - v1)
    w1 = g_sel / (1.0 + e2)
    w2 = g_sel * e2 / (1.0 + e2)
    if not sparse:
        gates_ref[...] = jnp.where(lane + N_GROUPS == i1, w1, 0.0) + jnp.where(lane + N_GROUPS == i2, w2, 0.0)
        return
    @pl.when(pl.program_id(0) == 0)
    def _():
        cnt_ref[...] = jnp.zeros(cnt_ref.shape, F32)

    e0 = i1 - N_GROUPS
    e1 = i2 - N_GROUPS
    onehot = jnp.where(lane == e0, 1.0, 0.0) + jnp.where(lane == e1, 1.0, 0.0)
    before = _dot(tri_ref[...], onehot.astype(BF16)) + cnt_ref[...]
    r0 = jnp.sum(jnp.where(lane == e0, before, 0.0), axis=-1, keepdims=True)
    r1 = jnp.sum(jnp.where(lane == e1, before, 0.0), axis=-1, keepdims=True)
    cnt_ref[...] += jnp.sum(onehot, axis=0, keepdims=True)
    rec = jnp.zeros(logits.shape, F32)
    for col, val in enumerate((e0, e1, w1, w2, r0, r1)):
        rec = jnp.where(lane == col, val, rec)
    gates_ref[...] = rec


def _out_proj(x2d, oa, ob, wa, wb, n2w, wr, br, *, tm, sparse):
    m, d = x2d.shape
    row = lambda i: (i, 0)
    const = lambda i: (0, 0)
    in_specs = [pl.BlockSpec((tm, d), row),
                pl.BlockSpec((tm, A_WIDTH), row),
                pl.BlockSpec((tm, B_WIDTH), row),
                pl.BlockSpec((A_WIDTH, d), const),
                pl.BlockSpec((B_WIDTH, d), const),
                pl.BlockSpec((1, d), const),
                pl.BlockSpec((d, LANES), const),
                pl.BlockSpec((1, LANES), const)]
    args = [x2d, oa, ob, wa, wb, n2w, wr, br]
    if sparse:
        idx = jnp.arange(tm)
        args.append((idx[:, None] > idx[None, :]).astype(BF16))
        in_specs.append(pl.BlockSpec((tm, tm), const))
        out_specs = (pl.BlockSpec((tm, d), row), pl.BlockSpec((tm, d // 2), row), pl.BlockSpec((tm, LANES), row),
                     pl.BlockSpec((1, LANES), const))
        out_shape = (jax.ShapeDtypeStruct((m, d), F32), jax.ShapeDtypeStruct((m, d // 2), jnp.int32),
                     jax.ShapeDtypeStruct((m, LANES), F32), jax.ShapeDtypeStruct((1, LANES), F32))
    else:
        out_specs = (pl.BlockSpec((tm, d), row), pl.BlockSpec((tm, d), row), pl.BlockSpec((tm, LANES), row))
        out_shape = (jax.ShapeDtypeStruct((m, d), F32), jax.ShapeDtypeStruct((m, d), BF16),
                     jax.ShapeDtypeStruct((m, LANES), F32))
    return pl.pallas_call(
        functools.partial(_out_proj_kernel, sparse=sparse),
        grid=(m // tm,), in_specs=in_specs, out_specs=out_specs, out_shape=out_shape,
        compiler_params=_params(1), name="out_proj",
    )(*args)


def _moe_kernel(h_ref, n2_ref, gates_ref, wgu_ref, wd_ref, y_ref):
    e = pl.program_id(1)

    @pl.when(e == 0)
    def _():
        y_ref[...] = h_ref[...]

    n2 = n2_ref[...]
    gu = _dot(n2, wgu_ref[0].astype(BF16))
    act = _silu(gu[:, :D_EXPERT]) * gu[:, D_EXPERT:]
    lane = lax.broadcasted_iota(jnp.int32, gates_ref.shape, 1)
    gate = jnp.sum(jnp.where(lane == e, gates_ref[...], 0.0), axis=-1, keepdims=True)
    y_ref[...] += gate * _dot(act.astype(BF16), wd_ref[0].astype(BF16))


def _moe(h, n2, gates, wgu, wd, *, tm):
    m, d = h.shape
    row = lambda i, e: (i, 0)
    return pl.pallas_call(
        _moe_kernel,
        grid=(m // tm, N_EXPERTS),
        in_specs=[pl.BlockSpec((tm, d), row),
                  pl.BlockSpec((tm, d), row),
                  pl.BlockSpec((tm, LANES), row),
                  pl.BlockSpec((1, d, 2 * D_EXPERT), lambda i, e: (e, 0, 0)),
                  pl.BlockSpec((1, D_EXPERT, d), lambda i, e: (e, 0, 0))],
        out_specs=pl.BlockSpec((tm, d), row),
        out_shape=jax.ShapeDtypeStruct((m, d), F32),
        compiler_params=_params(2), name="moe",
    )(h, n2, gates, wgu, wd)


SC_CORES = 2
SC_SUBCORES = 16
SC_WORKERS = SC_CORES * SC_SUBCORES
SC_CHUNK = 64
MOE_TILE = 512


def _sc_mesh():
    return plsc.VectorSubcoreMesh(core_axis_name="c", subcore_axis_name="s",
                                  num_cores=SC_CORES, num_subcores=SC_SUBCORES)


def _sc_dispatch(src, pos, n_rows):
    m, d = src.shape
    n_chunks = m // (SC_WORKERS * SC_CHUNK)
    pos4 = pos.reshape(2, SC_WORKERS, n_chunks, SC_CHUNK).transpose(1, 0, 2, 3)

    def body(src_hbm, pos_hbm, out_hbm, idx_v, rows_v):
        wid = lax.axis_index("s") * SC_CORES + lax.axis_index("c")
        pltpu.sync_copy(pos_hbm.at[wid], idx_v)

        @pl.loop(0, n_chunks)
        def _(j):
            start = pl.multiple_of((wid * n_chunks + j) * SC_CHUNK, SC_CHUNK)
            pltpu.sync_copy(src_hbm.at[pl.ds(start, SC_CHUNK)], rows_v)
            pltpu.sync_copy(rows_v, out_hbm.at[idx_v.at[0, j]])
            pltpu.sync_copy(rows_v, out_hbm.at[idx_v.at[1, j]])

    return pl.kernel(
        body, out_type=jax.ShapeDtypeStruct((n_rows, d), src.dtype), mesh=_sc_mesh(),
        scratch_types=[pltpu.VMEM((2, n_chunks, SC_CHUNK), jnp.int32), pltpu.VMEM((SC_CHUNK, d), src.dtype)],
        name="moe_dispatch",
    )(src, pos4)


def _sc_gather(table, idx):
    b = idx.shape[0]
    d = table.shape[1]
    n_chunks = b // (SC_WORKERS * SC_CHUNK)
    idx3 = idx.reshape(SC_WORKERS, n_chunks, SC_CHUNK)

    def body(table_hbm, idx_hbm, out_hbm, idx_v, rows_v):
        wid = lax.axis_index("s") * SC_CORES + lax.axis_index("c")
        pltpu.sync_copy(idx_hbm.at[wid], idx_v)

        @pl.loop(0, n_chunks)
        def _(j):
            start = pl.multiple_of((wid * n_chunks + j) * SC_CHUNK, SC_CHUNK)
            pltpu.sync_copy(table_hbm.at[idx_v.at[j]], rows_v)
            pltpu.sync_copy(rows_v, out_hbm.at[pl.ds(start, SC_CHUNK)])

    return pl.kernel(
        body, out_type=jax.ShapeDtypeStruct((b, d), table.dtype), mesh=_sc_mesh(),
        scratch_types=[pltpu.VMEM((n_chunks, SC_CHUNK), jnp.int32), pltpu.VMEM((SC_CHUNK, d), table.dtype)],
        name="moe_collect",
    )(table, idx3)


def _moe_grouped_kernel(te_ref, nu_ref, xs_ref, wgu_ref, wd_ref, ys_ref, wgu16, wd16):
    i = pl.program_id(0)

    @pl.when((i == 0) | (te_ref[i] != te_ref[jnp.maximum(i - 1, 0)]))
    def _():
        wgu16[...] = wgu_ref[0].astype(BF16)
        wd16[...] = wd_ref[0].astype(BF16)

    @pl.when(i < nu_ref[0])
    def _():
        x = _unpack_bf16_pairs(xs_ref[...]).astype(BF16)
        gu = _dot(x, wgu16[...])
        act = _silu(gu[:, :D_EXPERT]) * gu[:, D_EXPERT:]
        ys_ref[...] = _pack_bf16_pairs(_dot(act.astype(BF16), wd16[...]))

    @pl.when(i >= nu_ref[0])
    def _():
        ys_ref[...] = jnp.zeros(ys_ref.shape, ys_ref.dtype)


def _moe_grouped(xs, tile_expert, n_used, wgu, wd):
    r, half = xs.shape
    d = 2 * half
    grid_spec = pltpu.PrefetchScalarGridSpec(
        num_scalar_prefetch=2, grid=(r // MOE_TILE,),
        in_specs=[pl.BlockSpec((MOE_TILE, half), lambda i, te, nu: (i, 0)),
                  pl.BlockSpec((1, d, 2 * D_EXPERT), lambda i, te, nu: (te[i], 0, 0)),
                  pl.BlockSpec((1, D_EXPERT, d), lambda i, te, nu: (te[i], 0, 0))],
        out_specs=pl.BlockSpec((MOE_TILE, half), lambda i, te, nu: (i, 0)),
        scratch_shapes=[pltpu.VMEM((d, 2 * D_EXPERT), BF16), pltpu.VMEM((D_EXPERT, d), BF16)])
    return pl.pallas_call(
        _moe_grouped_kernel, grid_spec=grid_spec,
        out_shape=jax.ShapeDtypeStruct((r, half), jnp.int32),
        compiler_params=_params(1), name="moe_grouped",
    )(tile_expert, n_used, xs, wgu, wd)


def _moe_combine_kernel(h_ref, z0_ref, z1_ref, rec_ref, y_ref):
    rec = rec_ref[...]
    y_ref[...] = (h_ref[...] + rec[:, 2:3] * _unpack_bf16_pairs(z0_ref[...])
                  + rec[:, 3:4] * _unpack_bf16_pairs(z1_ref[...]))


def _moe_combine(h, z, rec, *, tm):
    m, d = h.shape
    nt = m // tm
    row = lambda i: (i, 0)
    return pl.pallas_call(
        _moe_combine_kernel, grid=(nt,),
        in_specs=[pl.BlockSpec((tm, d), row),
                  pl.BlockSpec((tm, d // 2), row),
                  pl.BlockSpec((tm, d // 2), lambda i: (i + nt, 0)),
                  pl.BlockSpec((tm, LANES), row)],
        out_specs=pl.BlockSpec((tm, d), row),
        out_shape=jax.ShapeDtypeStruct((m, d), F32),
        compiler_params=_params(1), name="moe_combine",
    )(h, z, z, rec)


def _moe_route(rec, cnt):
    m = rec.shape[0]
    n_tiles = 2 * m // MOE_TILE + N_EXPERTS
    counts = cnt[0, :N_EXPERTS].astype(jnp.int32)
    tiles = (counts + MOE_TILE - 1) // MOE_TILE
    tile_end = jnp.cumsum(tiles)
    row_start = (tile_end - tiles) * MOE_TILE
    eid = rec[:, 0:2].astype(jnp.int32).T
    rank = rec[:, 4:6].astype(jnp.int32).T
    pos = rank
    for e in range(N_EXPERTS):
        pos = pos + jnp.where(eid == e, row_start[e], 0)
    n_used = tile_end[-1:]
    tile_id = jnp.minimum(jnp.arange(n_tiles, dtype=jnp.int32), n_used[0] - 1)
    tile_expert = jnp.sum((tile_end[None, :] <= tile_id[:, None]).astype(jnp.int32), axis=1)
    return pos, tile_expert, n_used.astype(jnp.int32), n_tiles * MOE_TILE


def _head_rows(v):
    rows = jnp.concatenate([jnp.zeros((B_HEADS,), F32), v.astype(F32)])
    return jnp.broadcast_to(rows[:, None], (2 * B_HEADS, LANES))


def kernel(x_prompt, x_sample, cache_win_k, cache_win_v, state_conv, state_ssm, norm1_w, w_in, qnorm_w, knorm_w, conv_w, a_log, dt_bias, onorm_w, w_out, norm2_w, w_group, b_group, w_expert_router, b_expert_router, w_gate_up, w_down):
    nb, seq, d = x_prompt.shape
    db, t_len, _ = x_sample.shape
    n_buf = cache_win_k.shape[1]
    assert n_buf == MAX_WINDOW and seq % (MAX_DILATION * BAND) == 0 and t_len == SUBLANES
    keep = min(MAX_WINDOW, seq)

    c_bg = 3 * A_WIDTH + CONV_DIM + B_WIDTH
    w_cat = jnp.concatenate([w_in, jnp.zeros((d, LANES - 2 * B_HEADS), w_in.dtype)], axis=1).astype(BF16)
    assert w_cat.shape[1] == c_bg + LANES
    n1w = norm1_w.reshape(1, d).astype(F32)
    qw = jnp.tile(qnorm_w.astype(F32), A_HEADS).reshape(1, A_WIDTH)
    kw = jnp.tile(knorm_w.astype(F32), A_HEADS).reshape(1, A_WIDTH)
    a_log_v = _head_rows(a_log)
    dtb_v = _head_rows(dt_bias)
    onw = onorm_w.reshape(1, B_VAL_DIM).astype(F32)
    wa = w_out[:A_WIDTH].astype(BF16)
    wb = w_out[A_WIDTH:].astype(BF16)
    n2w = norm2_w.reshape(1, d).astype(F32)
    wr = jnp.concatenate([w_group, jnp.transpose(w_expert_router, (1, 0, 2)).reshape(d, N_EXPERTS),
                          jnp.zeros((d, LANES - N_GROUPS - N_EXPERTS), F32)], axis=1).astype(F32)
    br = jnp.zeros((1, LANES), F32).at[0, :N_GROUPS].set(b_group).at[0, N_GROUPS:N_GROUPS + N_EXPERTS].set(
        b_expert_router.reshape(-1))
    wgu = w_gate_up.astype(F32)
    wd = w_down.astype(F32)
    cw = conv_w.astype(F32)

    xp2d = x_prompt.reshape(nb * seq, d)
    qf, kf, vf, kwin, vwin, bqkv, z, bg = _in_proj(xp2d, n1w, w_cat, qw, kw, seq=seq, keep=keep,
                                                   fold=True, tm=512)
    oa = _attn_prompt(qf, kf, vf).reshape(nb * seq, A_WIDTH)
    conv0 = jnp.zeros((nb, CONV_WIDTH - 1, CONV_DIM), F32)
    ssm0 = jnp.zeros((nb, B_HEADS, B_KEY_DIM, B_VAL_DIM), F32)
    ob, ssm_prompt = _gdn_prompt(bqkv, z, bg, cw, conv0, ssm0, a_log_v, dtb_v, onw,
                                 seq=seq, ts=512, block=128, chunk=CHUNK)
    assert (nb * seq) % (SC_WORKERS * SC_CHUNK) == 0 and nb * seq >= N_EXPERTS * MOE_TILE
    h_p, n2p, rec, cnt = _out_proj(xp2d, oa, ob, wa, wb, n2w, wr, br, tm=512, sparse=True)
    pos, tile_expert, n_used, n_rows = _moe_route(rec, cnt)
    xs = _sc_dispatch(n2p, pos, n_rows)
    win_k_prompt = jnp.transpose(kwin, (0, 3, 1, 2))
    win_v_prompt = jnp.transpose(vwin, (0, 3, 1, 2))
    conv_prompt = bqkv.reshape(nb, seq, CONV_DIM)[:, seq - (CONV_WIDTH - 1):]

    ms = db * t_len
    xs2d = x_sample.reshape(ms, d)
    tms = min(512, ms)
    qs, ksn, vsn, _, _, bqkv_s, z_s, bg_s = _in_proj(xs2d, n1w, w_cat, qw, kw, seq=ms, keep=ms,
                                                     fold=False, tm=tms)
    cache_kt = jnp.transpose(cache_win_k.astype(F32), (0, 2, 3, 1))
    cache_vt = jnp.transpose(cache_win_v.astype(F32), (0, 2, 3, 1))
    oa_s = _attn_sample(qs, ksn, vsn, cache_kt, cache_vt, t_len=t_len)
    xs, oa_s = lax.optimization_barrier((xs, oa_s))
    ys = _moe_grouped(xs, tile_expert, n_used, wgu, wd)
    zs = _sc_gather(ys, pos.reshape(-1))
    xpad = jnp.concatenate([state_conv.astype(F32), bqkv_s.reshape(db, t_len, CONV_DIM)], axis=1)
    nseq = 16 if db % 16 == 0 else 1
    ob_s, ssm_sample = _gdn_sample(xpad, z_s, bg_s, cw, state_ssm.astype(F32), a_log_v, dtb_v, onw,
                                   t_len=t_len, nseq=nseq)
    h_s, n2_s, gates_s = _out_proj(xs2d, oa_s, ob_s, wa, wb, n2w, wr, br, tm=tms, sparse=False)
    y_sample = _moe(h_s, n2_s, gates_s, wgu, wd, tm=tms)
    zs, y_sample = lax.optimization_barrier((zs, y_sample))
    y_prompt = _moe_combine(h_p, zs, rec, tm=MOE_TILE).reshape(nb, seq, d)
    y_sample = y_sample.reshape(db, t_len, d)
    win_k_sample = ksn.reshape(db, t_len, A_HEADS, A_HEAD_DIM)
    win_v_sample = vsn.reshape(db, t_len, A_HEADS, A_HEAD_DIM)
    conv_sample = xpad[:, t_len:]

    return (y_prompt, y_sample, win_k_prompt, win_v_prompt, conv_prompt, ssm_prompt,
            win_k_sample, win_v_sample, conv_sample, ssm_sample)
```

```python
import functools

import jax
import jax.numpy as jnp
from jax import lax
from jax.experimental import pallas as pl
from jax.experimental.pallas import tpu as pltpu
from jax.experimental.pallas import tpu_sc as plsc

F32 = jnp.float32
BF16 = jnp.bfloat16

A_HEADS = 8
A_HEAD_DIM = 64
A_WIDTH = A_HEADS * A_HEAD_DIM
DILATED_PATTERNS = ((128, 1), (512, 4), (2048, 16))
MAX_WINDOW = 2048
MAX_DILATION = 16
BAND = 128
ATTN_SCALE = A_HEAD_DIM ** -0.5
LOG2_E = 1.4426950408889634
B_HEADS = 4
B_KEY_DIM = 128
B_VAL_DIM = 128
B_WIDTH = B_HEADS * B_VAL_DIM
CONV_WIDTH = 4
CONV_DIM = B_HEADS * (2 * B_KEY_DIM + B_VAL_DIM)
CHUNK = 64
N_GROUPS = 4
EXPERTS_PER_GROUP = 4
N_EXPERTS = N_GROUPS * EXPERTS_PER_GROUP
D_EXPERT = 512
RMS_EPS = 1e-6
NEG_INF = -1e30

LANES = 128
SUBLANES = 8
VMEM_LIMIT = 56 * 1024 * 1024


def _params(n_axes, vmem=VMEM_LIMIT):
    return pltpu.CompilerParams(dimension_semantics=("arbitrary",) * n_axes, vmem_limit_bytes=vmem)


def _split2(x):
    hi = x.astype(BF16)
    lo = (x - hi.astype(F32)).astype(BF16)
    return hi, lo


def _split3(x):
    hi = x.astype(BF16)
    r = x - hi.astype(F32)
    mid = r.astype(BF16)
    lo = (r - mid.astype(F32)).astype(BF16)
    return hi, mid, lo


def _dot(a, b):
    return jnp.dot(a, b, preferred_element_type=F32)


def _dot_nt(a, b):
    return lax.dot_general(a, b, (((1,), (1,)), ((), ())), preferred_element_type=F32)


def _dot_tn(a, b):
    return lax.dot_general(a, b, (((0,), (0,)), ((), ())), preferred_element_type=F32)


def _sigmoid(x):
    return 1.0 / (1.0 + jnp.exp(-x))


def _silu(x):
    return x * _sigmoid(x)


def _in_proj_kernel(x_ref, n1w_ref, w_ref, qw_ref, kw_ref,
                    q_ref, k_ref, v_ref, kwin_ref, vwin_ref, bqkv_ref, z_ref, bg_ref,
                    *scratch, fold):
    tm = x_ref.shape[0]
    x = x_ref[...]
    ms = jnp.mean(x * x, axis=-1, keepdims=True)
    n1 = (x * lax.rsqrt(ms + RMS_EPS) * n1w_ref[...]).astype(BF16)

    first_head = lax.broadcasted_iota(jnp.int32, (tm, LANES), 1) < A_HEAD_DIM

    def head_norm(t, w):
        cols = []
        for c in range(A_WIDTH // LANES):
            tc = t[:, c * LANES:(c + 1) * LANES]
            sq = tc * tc
            s0 = jnp.sum(jnp.where(first_head, sq, 0.0), axis=-1, keepdims=True)
            s1 = jnp.sum(jnp.where(first_head, 0.0, sq), axis=-1, keepdims=True)
            ms = jnp.where(first_head, s0, s1) * (1.0 / A_HEAD_DIM)
            cols.append(tc * lax.rsqrt(ms + RMS_EPS))
        return jnp.concatenate(cols, axis=1) * w

    q_scale = ATTN_SCALE * LOG2_E if fold else ATTN_SCALE
    q = head_norm(_dot(n1, w_ref[:, 0:A_WIDTH]), qw_ref[...]) * q_scale
    k = head_norm(_dot(n1, w_ref[:, A_WIDTH:2 * A_WIDTH]), kw_ref[...])
    v = _dot(n1, w_ref[:, 2 * A_WIDTH:3 * A_WIDTH])
    if fold:
        kwin_ref[0] = k.T.reshape(A_HEADS, A_HEAD_DIM, tm)
        vwin_ref[0] = v.T.reshape(A_HEADS, A_HEAD_DIM, tm)

        (scr,) = scratch
        rows = tm // MAX_DILATION
        for val, out in ((q, q_ref), (k, k_ref), (v, v_ref)):
            for c in range(A_WIDTH // LANES):
                cs = slice(c * LANES, (c + 1) * LANES)
                scr[c] = val[:, cs]
                for r in range(MAX_DILATION):
                    out[0, r, :, cs] = scr[c, pl.ds(r, rows, stride=MAX_DILATION), :]
    else:
        kwin_ref[...] = k
        vwin_ref[...] = v
        q_ref[...] = q
        k_ref[...] = k
        v_ref[...] = v
    c0 = 3 * A_WIDTH
    bqkv_ref[...] = _dot(n1, w_ref[:, c0:c0 + CONV_DIM])
    c1 = c0 + CONV_DIM
    z_ref[...] = _dot(n1, w_ref[:, c1:c1 + B_WIDTH]).astype(BF16)
    c2 = c1 + B_WIDTH
    bg_ref[...] = _dot(n1, w_ref[:, c2:c2 + LANES])


def _in_proj(x2d, n1w, w_cat, qw, kw, *, seq, keep, fold, tm):
    m, d = x2d.shape
    nb = m // seq
    tiles_per_seq = seq // tm
    skip = (seq - keep) // tm
    keep_tiles = keep // tm
    n_cols = w_cat.shape[1]

    def win_map(i):
        return (i // tiles_per_seq) * keep_tiles + jnp.maximum(i % tiles_per_seq - skip, 0), 0

    row = lambda i: (i, 0)
    const = lambda i: (0, 0)
    if fold:
        rows = tm // MAX_DILATION
        qkv_shape = jax.ShapeDtypeStruct((nb, MAX_DILATION, seq // MAX_DILATION, A_WIDTH), F32)
        qkv_spec = pl.BlockSpec((1, MAX_DILATION, rows, A_WIDTH),
                                lambda i: (i // tiles_per_seq, 0, i % tiles_per_seq, 0))
        scratch = [pltpu.VMEM((A_WIDTH // LANES, tm, LANES), F32)]
        win_shape = jax.ShapeDtypeStruct((nb, A_HEADS, A_HEAD_DIM, keep), F32)
        win_spec = pl.BlockSpec((1, A_HEADS, A_HEAD_DIM, tm),
                                lambda i: (i // tiles_per_seq, 0, 0, jnp.maximum(i % tiles_per_seq - skip, 0)))
    else:
        qkv_shape = jax.ShapeDtypeStruct((m, A_WIDTH), F32)
        qkv_spec = pl.BlockSpec((tm, A_WIDTH), row)
        scratch = []
        win_shape = jax.ShapeDtypeStruct((nb * keep, A_WIDTH), F32)
        win_spec = pl.BlockSpec((tm, A_WIDTH), win_map)
    out_shape = (qkv_shape, qkv_shape, qkv_shape, win_shape, win_shape,
                 jax.ShapeDtypeStruct((m, CONV_DIM), F32),
                 jax.ShapeDtypeStruct((m, B_WIDTH), BF16),
                 jax.ShapeDtypeStruct((m, LANES), F32))
    out_specs = (qkv_spec, qkv_spec, qkv_spec, win_spec, win_spec,
                 pl.BlockSpec((tm, CONV_DIM), row),
                 pl.BlockSpec((tm, B_WIDTH), row),
                 pl.BlockSpec((tm, LANES), row))
    in_specs = [pl.BlockSpec((tm, d), row),
                pl.BlockSpec((1, d), const),
                pl.BlockSpec((d, n_cols), const),
                pl.BlockSpec((1, A_WIDTH), const),
                pl.BlockSpec((1, A_WIDTH), const)]
    return pl.pallas_call(
        functools.partial(_in_proj_kernel, fold=fold),
        grid=(m // tm,), in_specs=in_specs, out_specs=out_specs, out_shape=out_shape,
        scratch_shapes=scratch, compiler_params=_params(1), name="in_proj",
    )(x2d, n1w, w_cat, qw, kw)


def _band_mask(pieces):
    sub = BAND // pieces
    r = lax.broadcasted_iota(jnp.int32, (BAND, 2 * BAND), 0)
    c = lax.broadcasted_iota(jnp.int32, (BAND, 2 * BAND), 1)
    qpos = (r % sub) * pieces + r // sub + BAND
    cc = c % BAND
    kpos = (cc % sub) * pieces + cc // sub + (c // BAND) * BAND
    dist = qpos - kpos
    return (dist >= 0) & (dist <= BAND), c >= BAND


ATTN_UNROLL = 8


def _attn_prompt_kernel(q_ref, k_ref, v_ref, o_ref, p_scr, op_scr, mx_scr, l_scr, nat_scr):
    u_len = q_ref.shape[2]
    lane = lax.broadcasted_iota(jnp.int32, (BAND, LANES), 1)
    head0 = lane < A_HEAD_DIM

    def per_head(col):
        return jnp.where(head0, jnp.broadcast_to(col[:BAND], (BAND, LANES)),
                         jnp.broadcast_to(col[BAND:], (BAND, LANES)))

    for pi, pieces in enumerate((16, 4, 1)):
        sub = BAND // pieces
        n_res = MAX_DILATION // pieces
        n_blk = u_len // sub
        band, _ = _band_mask(pieces)
        bias = jnp.where(band, 0.0, NEG_INF)
        bias2 = jnp.concatenate([bias, bias], axis=0)
        bias2_cur = bias2[:, BAND:]

        def locate(blk, n_res=n_res, sub=sub):
            m = blk // n_res
            cur = pl.multiple_of(m * sub, SUBLANES)
            prv = pl.multiple_of(jnp.maximum(m - 1, 0) * sub, SUBLANES)
            return blk % n_res, cur, prv

        def gather(ref, res, start, pieces=pieces, n_res=n_res, sub=sub):
            parts = [ref[0, a * n_res + res, pl.ds(start, sub), :] for a in range(pieces)]
            return parts[0] if pieces == 1 else jnp.concatenate(parts, axis=0)

        def scatter(ref, res, start, val, pi=pi, pieces=pieces, n_res=n_res, sub=sub):
            for a in range(pieces):
                ref[pi, a * n_res + res, pl.ds(start, sub), :] = val[a * sub:(a + 1) * sub]

        def probs(it, carry, first, bias2=bias2, bias2_cur=bias2_cur):
            for j, is_first in enumerate(first):
                blk = it * ATTN_UNROLL + j
                res, cur, prv = locate(blk)
                qb = gather(q_ref, res, cur)
                q2 = jnp.concatenate([jnp.where(head0, qb, 0.0), jnp.where(head0, 0.0, qb)], axis=0).astype(BF16)
                kc = gather(k_ref, res, cur)
                if is_first:
                    s = _dot_nt(q2, kc.astype(BF16)) + bias2_cur
                else:
                    kb = jnp.concatenate([gather(k_ref, res, prv), kc], axis=0).astype(BF16)
                    s = _dot_nt(q2, kb) + bias2
                mx = jnp.max(s, axis=-1, keepdims=True)
                p = jnp.exp2(s - mx)
                if is_first:
                    p_scr[blk, :, BAND:] = p.astype(BF16)
                else:
                    p_scr[blk] = p.astype(BF16)
                scatter(mx_scr, res, cur, per_head(mx))
            return carry

        def values(it, carry, first):
            for j, is_first in enumerate(first):
                blk = it * ATTN_UNROLL + j
                res, cur, prv = locate(blk)
                vc = gather(v_ref, res, cur)
                if is_first:
                    p = p_scr[blk, :, BAND:]
                    vb = vc.astype(BF16)
                else:
                    p = p_scr[blk]
                    vb = jnp.concatenate([gather(v_ref, res, prv), vc], axis=0).astype(BF16)
                o2 = _dot(p, jnp.concatenate([vb, jnp.ones(vb.shape, BF16)], axis=1))
                scatter(op_scr, res, cur, jnp.where(head0, o2[:BAND, :LANES], o2[BAND:, :LANES]))
                scatter(l_scr, res, cur, jnp.where(head0, o2[:BAND, LANES:], o2[BAND:, LANES:]))
            return carry

        n_it = n_res * n_blk // ATTN_UNROLL
        flags = [tuple(g * ATTN_UNROLL + j < n_res for j in range(ATTN_UNROLL)) for g in range(n_it)]
        segments = []
        for g, f in enumerate(flags):
            if segments and segments[-1][2] == f:
                segments[-1][1] = g + 1
            else:
                segments.append([g, g + 1, f])
        for phase in (probs, values):
            for lo, hi, f in segments:
                lax.fori_loop(lo, hi, functools.partial(phase, first=f), 0)

    for r in range(MAX_DILATION):
        m0, m1, m2 = mx_scr[0, r], mx_scr[1, r], mx_scr[2, r]
        big = jnp.maximum(jnp.maximum(m0, m1), m2)
        e0, e1, e2 = jnp.exp2(m0 - big), jnp.exp2(m1 - big), jnp.exp2(m2 - big)
        num = e0 * op_scr[0, r] + e1 * op_scr[1, r] + e2 * op_scr[2, r]
        den = e0 * l_scr[0, r] + e1 * l_scr[1, r] + e2 * l_scr[2, r]
        nat_scr[pl.ds(r, u_len, stride=MAX_DILATION), :] = num / den
    o_ref[0] = nat_scr[...].astype(BF16)


def _attn_prompt(qf, kf, vf):
    nb, _, u_len, _ = qf.shape
    seq = u_len * MAX_DILATION
    n_pairs = A_WIDTH // LANES
    spec = pl.BlockSpec((1, MAX_DILATION, u_len, LANES), lambda b, p: (b, 0, 0, p))
    return pl.pallas_call(
        _attn_prompt_kernel,
        grid=(nb, n_pairs),
        in_specs=[spec, spec, spec],
        out_specs=pl.BlockSpec((1, seq, LANES), lambda b, p: (b, 0, p)),
        out_shape=jax.ShapeDtypeStruct((nb, seq, A_WIDTH), BF16),
        scratch_shapes=[pltpu.VMEM((seq // BAND, 2 * BAND, 2 * BAND), BF16),
                        pltpu.VMEM((3, MAX_DILATION, u_len, LANES), F32),
                        pltpu.VMEM((3, MAX_DILATION, u_len, LANES), F32),
                        pltpu.VMEM((3, MAX_DILATION, u_len, LANES), F32),
                        pltpu.VMEM((seq, LANES), F32)],
        compiler_params=_params(2), name="attn_prompt",
    )(qf, kf, vf)


def _multiplicity(t, dd):
    cnt = jnp.zeros(dd.shape, F32)
    for w, d in DILATED_PATTERNS:
        ok = (dd >= 0) & (dd <= w) & (dd % d == 0)
        cnt = cnt + jnp.where(ok, 1.0, 0.0)
    return cnt


def _attn_sample_kernel(q_ref, kn_ref, vn_ref, kt_ref, vt_ref, o_ref):
    t_len = q_ref.shape[0]
    n_buf = kt_ref.shape[3]
    q = q_ref[...]
    kn = kn_ref[...]
    vn = vn_ref[...]
    tq = n_buf + lax.broadcasted_iota(jnp.int32, (t_len, n_buf), 0)
    cnt = _multiplicity(None, tq - lax.broadcasted_iota(jnp.int32, (t_len, n_buf), 1))
    cnt_new = _multiplicity(None, lax.broadcasted_iota(jnp.int32, (t_len, t_len), 0)
                            - lax.broadcasted_iota(jnp.int32, (t_len, t_len), 1))
    outs = []
    for h in range(A_HEADS):
        hs = slice(h * A_HEAD_DIM, (h + 1) * A_HEAD_DIM)
        qh = q[:, hs].astype(BF16)
        s = jnp.where(cnt > 0, _dot(qh, kt_ref[0, h].astype(BF16)), NEG_INF)
        sn = jnp.where(cnt_new > 0, _dot_nt(qh, kn[:, hs].astype(BF16)), NEG_INF)
        m = jnp.maximum(jnp.max(s, axis=-1, keepdims=True), jnp.max(sn, axis=-1, keepdims=True))
        p = cnt * jnp.exp(s - m)
        pn = cnt_new * jnp.exp(sn - m)
        l = jnp.sum(p, axis=-1, keepdims=True) + jnp.sum(pn, axis=-1, keepdims=True)
        o = _dot_nt(p.astype(BF16), vt_ref[0, h].astype(BF16)) + _dot(pn.astype(BF16), vn[:, hs].astype(BF16))
        outs.append(o / l)
    o_ref[...] = jnp.concatenate(outs, axis=-1)


def _attn_sample(q2d, kn2d, vn2d, cache_kt, cache_vt, *, t_len):
    nb, _, _, n_buf = cache_kt.shape
    tok = pl.BlockSpec((t_len, A_WIDTH), lambda b: (b, 0))
    win = pl.BlockSpec((1, A_HEADS, A_HEAD_DIM, n_buf), lambda b: (b, 0, 0, 0))
    return pl.pallas_call(
        _attn_sample_kernel,
        grid=(nb,),
        in_specs=[tok, tok, tok, win, win],
        out_specs=tok,
        out_shape=jax.ShapeDtypeStruct((nb * t_len, A_WIDTH), F32),
        compiler_params=_params(1), name="attn_sample",
    )(q2d, kn2d, vn2d, cache_kt, cache_vt)


def _unit_lower_inverse(a, n_chunk):
    n = a.shape[0]
    ri = lax.broadcasted_iota(jnp.int32, (n, n), 0)
    ci = lax.broadcasted_iota(jnp.int32, (n, n), 1)
    eye = jnp.where(ri == ci, 1.0, 0.0).astype(F32)
    t = eye - a
    pw = a
    k = 2
    while k < n_chunk:
        pw16 = pw.astype(BF16)
        pw = _dot(pw16, pw16)
        t = t + _dot(t.astype(BF16), pw.astype(BF16))
        k *= 2
    return t


def _gdn_intra(q, k, v, beta, dcol, drow, chunk):
    n = q.shape[0]
    ri = lax.broadcasted_iota(jnp.int32, (n, n), 0)
    ci = lax.broadcasted_iota(jnp.int32, (n, n), 1)
    same = (ri // chunk) == (ci // chunk)
    lower = same & (ri >= ci)
    strict = same & (ri > ci)
    gam = jnp.exp(jnp.where(lower, dcol - drow, NEG_INF))
    kb = k * beta
    vb = v * beta
    k16 = k.astype(BF16)
    a_mat = jnp.where(strict, _dot_nt(kb.astype(BF16), k16) * gam, 0.0)
    attn = jnp.where(lower, _dot_nt(q.astype(BF16), k16) * gam, 0.0)
    t_inv = _unit_lower_inverse(a_mat, chunk).astype(BF16)
    ed = jnp.exp(dcol)
    u = _dot(t_inv, vb.astype(BF16))
    w = _dot(t_inv, (kb * ed).astype(BF16))
    qd = q * ed
    return u, w, qd, attn


def _gdn_prep(c, bg, a_col, dt_col, chunk):
    n = c.shape[0]
    bgt = bg.T[0:2 * B_HEADS]
    row = lax.broadcasted_iota(jnp.int32, bgt.shape, 0)
    xg = bgt + dt_col
    sp = jnp.maximum(xg, 0.0) + jnp.log1p(jnp.exp(-jnp.abs(xg)))
    gt = jnp.where(row >= B_HEADS, -a_col * sp, 0.0)
    ri = lax.broadcasted_iota(jnp.int32, (n, n), 0)
    ci = lax.broadcasted_iota(jnp.int32, (n, n), 1)
    tri = jnp.where(((ri // chunk) == (ci // chunk)) & (ri >= ci), 1.0, 0.0).astype(BF16)
    th, tm_, tl = _split3(gt)
    drow = _dot_nt(th, tri) + _dot_nt(tm_, tri) + _dot_nt(tl, tri)
    both = jnp.where(row >= B_HEADS, drow, _sigmoid(bgt))
    cols = jnp.concatenate([both, jnp.zeros((LANES - 2 * B_HEADS, n), F32)], axis=0).T
    beta_all = cols
    dcol = cols
    heads = []
    kd_ = B_KEY_DIM
    for h in range(B_HEADS):
        qh = c[:, h * kd_:(h + 1) * kd_]
        kh = c[:, B_HEADS * kd_ + h * kd_:B_HEADS * kd_ + (h + 1) * kd_]
        vh = c[:, 2 * B_HEADS * kd_ + h * B_VAL_DIM:2 * B_HEADS * kd_ + (h + 1) * B_VAL_DIM]
        qh = qh * lax.rsqrt(jnp.sum(qh * qh, axis=-1, keepdims=True) + RMS_EPS) * (B_KEY_DIM ** -0.5)
        kh = kh * lax.rsqrt(jnp.sum(kh * kh, axis=-1, keepdims=True) + RMS_EPS)
        heads.append((qh, kh, vh, beta_all[:, h:h + 1], dcol[:, B_HEADS + h:B_HEADS + h + 1],
                      drow[B_HEADS + h:B_HEADS + h + 1, :]))
    return heads


def _gdn_out(o, onw, zh):
    ms = jnp.mean(o * o, axis=-1, keepdims=True)
    return o * lax.rsqrt(ms + RMS_EPS) * onw * _silu(zh)


def _gdn_prompt_kernel(x_ref, z_ref, bg_ref, cw_ref, cst_ref, sst_ref, avec_ref, dtb_ref, onw_ref,
                       o_ref, snew_ref, xp_scr, s_scr, *, block, chunk):
    ts = x_ref.shape[0]
    t = pl.program_id(1)
    pad = SUBLANES
    hist = CONV_WIDTH - 1

    @pl.when(t == 0)
    def _():
        xp_scr[...] = jnp.zeros((pad, CONV_DIM), F32)
        xp_scr[pad - hist:pad, :] = cst_ref[0]
        s_scr[...] = sst_ref[0]

    assert CONV_WIDTH == 4
    x = x_ref[...]
    tail = xp_scr[...]
    sub = lax.broadcasted_iota(jnp.int32, (pad, CONV_DIM), 0)

    def shift(cur, prev_tail, s):
        rolled = pltpu.roll(cur, s, axis=0)
        head = jnp.where(sub < s, pltpu.roll(prev_tail, s, axis=0), rolled[0:pad])
        return jnp.concatenate([head, rolled[pad:]], axis=0)

    w0, w1, w2, w3 = (cw_ref[i:i + 1, :] for i in range(CONV_WIDTH))
    x1 = shift(x, tail, 1)
    a = x * w1 + x1 * w0
    a_tail = tail * w1 + pltpu.roll(tail, 1, axis=0) * w0
    y = x * w3 + x1 * w2 + shift(a, a_tail, 2)
    xp_scr[...] = x[ts - pad:ts]
    c = _silu(y)
    avec = jnp.exp(avec_ref[...])
    onw = onw_ref[...]
    n_chunks = block // chunk
    n_sb = ts // block
    ri = lax.broadcasted_iota(jnp.int32, (block, block), 0)
    ci = lax.broadcasted_iota(jnp.int32, (block, block), 1)
    same = (ri // chunk) == (ci // chunk)
    lower = same & (ri >= ci)
    strict = same & (ri > ci)
    eye = jnp.where(ri == ci, 1.0, 0.0).astype(F32)

    units = []
    for sb in range(n_sb):
        rs = slice(sb * block, (sb + 1) * block)
        heads = _gdn_prep(c[rs], bg_ref[rs, :], avec, dtb_ref[...], chunk)
        for h, (qh, kh, vh, beta, dcol, drow) in enumerate(heads):
            units.append(dict(sb=sb, h=h, q=qh, k=kh, v=vh, beta=beta, dcol=dcol, drow=drow))

    for u in units:
        gam = jnp.exp(jnp.where(lower, u["dcol"] - u["drow"], NEG_INF))
        kb = u["k"] * u["beta"]
        kq = _dot_nt(jnp.concatenate([kb, u["q"]], axis=0).astype(BF16), u["k"].astype(BF16))
        u["pw"] = jnp.where(strict, kq[:block] * gam, 0.0)
        u["t"] = eye - u["pw"]
        u["attn"] = jnp.where(lower, kq[block:] * gam, 0.0).astype(BF16)
        ed = jnp.exp(u["dcol"])
        u["rhs"] = jnp.concatenate([u["v"] * u["beta"], kb * ed], axis=1).astype(BF16)
        u["qd"] = u["q"] * ed
        u["kt"] = u["k"].T

    kk = 2
    while kk < chunk:
        for u in units:
            p16 = u["pw"].astype(BF16)
            u["pw"] = _dot(p16, p16)
        for u in units:
            u["t"] = u["t"] + _dot(u["t"].astype(BF16), u["pw"].astype(BF16))
        kk *= 2

    for u in units:
        u["uw"] = _dot(u["t"].astype(BF16), u["rhs"]).astype(BF16)
    for u in units:
        au_aw = _dot(u["attn"], u["uw"])
        u["op"] = au_aw[:, :B_VAL_DIM]
        u["qp"] = (u["qd"] - au_aw[:, B_VAL_DIM:]).astype(BF16)
    for u in units:
        u["n"], u["mw"], u["e"] = [], [], []
        for cc in range(n_chunks):
            cs = slice(cc * chunk, (cc + 1) * chunk)
            last = u["drow"][:, (cc + 1) * chunk - 1:(cc + 1) * chunk]
            kdt = (u["kt"][:, cs] * jnp.exp(last - u["drow"][:, cs])).astype(BF16)
            nm = _dot(kdt, u["uw"][cs])
            u["n"].append(nm[:, :B_VAL_DIM])
            u["mw"].append(nm[:, B_VAL_DIM:].astype(BF16))
            u["e"].append(jnp.exp(last))

    s = [s_scr[h] for h in range(B_HEADS)]
    for sb in range(n_sb):
        rs = slice(sb * block, (sb + 1) * block)
        outs = [[] for _ in range(B_HEADS)]
        for cc in range(n_chunks):
            cs = slice(cc * chunk, (cc + 1) * chunk)
            for h in range(B_HEADS):
                u = units[sb * B_HEADS + h]
                r = _dot(jnp.concatenate([u["mw"][cc], u["qp"][cs]], axis=0), s[h].astype(BF16))
                outs[h].append(r[B_KEY_DIM:] + u["op"][cs])
                s[h] = s[h] * u["e"][cc] - r[:B_KEY_DIM] + u["n"][cc]
        for h in range(B_HEADS):
            o_all = jnp.concatenate(outs[h], axis=0) if n_chunks > 1 else outs[h][0]
            zh = z_ref[rs, h * B_VAL_DIM:(h + 1) * B_VAL_DIM].astype(F32)
            o_ref[rs, h * B_VAL_DIM:(h + 1) * B_VAL_DIM] = _gdn_out(o_all, onw, zh).astype(o_ref.dtype)
    for h in range(B_HEADS):
        s_scr[h] = s[h]

    @pl.when(t == pl.num_programs(1) - 1)
    def _():
        snew_ref[0] = s_scr[...]


def _gdn_prompt(bqkv, z, bg, conv_w, conv_state, ssm_state, a_log_v, dtb_v, onw, *, seq, ts, block, chunk):
    m = bqkv.shape[0]
    nb = m // seq
    tps = seq // ts
    row = lambda b, t: (b * tps + t, 0)
    const = lambda b, t: (0, 0)
    return pl.pallas_call(
        functools.partial(_gdn_prompt_kernel, block=block, chunk=chunk),
        grid=(nb, tps),
        in_specs=[pl.BlockSpec((ts, CONV_DIM), row),
                  pl.BlockSpec((ts, B_WIDTH), row),
                  pl.BlockSpec((ts, LANES), row),
                  pl.BlockSpec((CONV_WIDTH, CONV_DIM), const),
                  pl.BlockSpec((1, CONV_WIDTH - 1, CONV_DIM), lambda b, t: (b, 0, 0)),
                  pl.BlockSpec((1, B_HEADS, B_KEY_DIM, B_VAL_DIM), lambda b, t: (b, 0, 0, 0)),
                  pl.BlockSpec((2 * B_HEADS, LANES), const),
                  pl.BlockSpec((2 * B_HEADS, LANES), const),
                  pl.BlockSpec((1, B_VAL_DIM), const)],
        out_specs=(pl.BlockSpec((ts, B_WIDTH), row),
                   pl.BlockSpec((1, B_HEADS, B_KEY_DIM, B_VAL_DIM), lambda b, t: (b, 0, 0, 0))),
        out_shape=(jax.ShapeDtypeStruct((m, B_WIDTH), BF16),
                   jax.ShapeDtypeStruct(ssm_state.shape, F32)),
        scratch_shapes=[pltpu.VMEM((SUBLANES, CONV_DIM), F32),
                        pltpu.VMEM((B_HEADS, B_KEY_DIM, B_VAL_DIM), F32)],
        compiler_params=_params(2), name="gdn_prompt",
    )(bqkv, z, bg, conv_w, conv_state, ssm_state, a_log_v, dtb_v, onw)


def _gdn_sample_kernel(xp_ref, z_ref, bg_ref, cw_ref, sst_ref, avec_ref, dtb_ref, onw_ref,
                       o_ref, snew_ref, vn_scr, *, t_len):
    nseq = xp_ref.shape[0]
    hist = CONV_WIDTH - 1
    ys = []
    for j in range(nseq):
        y = xp_ref[j, 0:t_len, :] * cw_ref[0:1, :]
        for i in range(1, CONV_WIDTH):
            y = y + xp_ref[j, i:i + t_len, :] * cw_ref[i:i + 1, :]
        ys.append(y)
    c = _silu(jnp.concatenate(ys, axis=0))
    avec = jnp.exp(avec_ref[...])
    onw = onw_ref[...]
    heads = _gdn_prep(c, bg_ref[...], avec, dtb_ref[...], t_len)
    for h, (qh, kh, vh, beta, dcol, drow) in enumerate(heads):
        u, w, qd, attn = _gdn_intra(qh, kh, vh, beta, dcol, drow, t_len)
        attn16 = attn.astype(BF16)
        kt = kh.T
        vn_scr[...] = jnp.zeros(vn_scr.shape, F32)
        outs = []
        for j in range(nseq):
            cs = slice(j * t_len, (j + 1) * t_len)
            last = drow[:, (j + 1) * t_len - 1:(j + 1) * t_len]
            s = sst_ref[j, h]
            wq = jnp.concatenate([w[cs], qd[cs]], axis=0).astype(BF16)
            ws_qs = _dot(wq, s.astype(BF16))
            v_new = u[cs] - ws_qs[:t_len]
            vn_scr[cs, :] = v_new
            o = ws_qs[t_len:] + _dot(attn16[cs], vn_scr[...].astype(BF16))
            kdt = kt[:, cs] * jnp.exp(last - drow[:, cs])
            snew_ref[j, h] = s * jnp.exp(last) + _dot(kdt.astype(BF16), v_new.astype(BF16))
            outs.append(o)
        o_all = jnp.concatenate(outs, axis=0)
        zh = z_ref[:, h * B_VAL_DIM:(h + 1) * B_VAL_DIM].astype(F32)
        o_ref[:, h * B_VAL_DIM:(h + 1) * B_VAL_DIM] = _gdn_out(o_all, onw, zh).astype(o_ref.dtype)


def _gdn_sample(xp, z, bg, conv_w, ssm_state, a_log_v, dtb_v, onw, *, t_len, nseq):
    nb = xp.shape[0]
    rows = nseq * t_len
    row = lambda i: (i, 0)
    const = lambda i: (0, 0)
    return pl.pallas_call(
        functools.partial(_gdn_sample_kernel, t_len=t_len),
        grid=(nb // nseq,),
        in_specs=[pl.BlockSpec((nseq, xp.shape[1], CONV_DIM), lambda i: (i, 0, 0)),
                  pl.BlockSpec((rows, B_WIDTH), row),
                  pl.BlockSpec((rows, LANES), row),
                  pl.BlockSpec((CONV_WIDTH, CONV_DIM), const),
                  pl.BlockSpec((nseq, B_HEADS, B_KEY_DIM, B_VAL_DIM), lambda i: (i, 0, 0, 0)),
                  pl.BlockSpec((2 * B_HEADS, LANES), const),
                  pl.BlockSpec((2 * B_HEADS, LANES), const),
                  pl.BlockSpec((1, B_VAL_DIM), const)],
        out_specs=(pl.BlockSpec((rows, B_WIDTH), row),
                   pl.BlockSpec((nseq, B_HEADS, B_KEY_DIM, B_VAL_DIM), lambda i: (i, 0, 0, 0))),
        out_shape=(jax.ShapeDtypeStruct((nb * t_len, B_WIDTH), BF16),
                   jax.ShapeDtypeStruct(ssm_state.shape, F32)),
        scratch_shapes=[pltpu.VMEM((rows, B_VAL_DIM), F32)],
        compiler_params=_params(1), name="gdn_sample",
    )(xp, z, bg, conv_w, ssm_state, a_log_v, dtb_v, onw)


def _pack_bf16_pairs(x):
    n = x.shape[1] // 2
    bits = pltpu.bitcast(x.astype(BF16).astype(F32), jnp.int32)
    return lax.shift_right_logical(bits[:, :n], 16) | (bits[:, n:] & jnp.int32(-65536))


def _unpack_bf16_pairs(w):
    lo = pltpu.bitcast(lax.shift_left(w, 16), F32)
    hi = pltpu.bitcast(w & jnp.int32(-65536), F32)
    return jnp.concatenate([lo, hi], axis=1)


def _out_proj_kernel(x_ref, oa_ref, ob_ref, wa_ref, wb_ref, n2w_ref, wr_ref, br_ref, *rest, sparse):
    if sparse:
        tri_ref, h_ref, n2_ref, gates_ref, cnt_ref = rest
    else:
        h_ref, n2_ref, gates_ref = rest
    h = x_ref[...] + _dot(oa_ref[...].astype(BF16), wa_ref[...]) + _dot(ob_ref[...].astype(BF16), wb_ref[...])
    h_ref[...] = h
    ms = jnp.mean(h * h, axis=-1, keepdims=True)
    n2 = h * lax.rsqrt(ms + RMS_EPS) * n2w_ref[...]
    if sparse:
        n2_ref[...] = _pack_bf16_pairs(n2)
    else:
        n2_ref[...] = n2.astype(BF16)
    nh, nl = _split2(n2)
    wh, wl = _split2(wr_ref[...])
    tm = n2.shape[0]
    parts = _dot(jnp.concatenate([nh, nl], axis=0), jnp.concatenate([wh, wl], axis=1))
    logits = parts[:tm, :LANES] + parts[:tm, LANES:] + parts[tm:, :LANES] + br_ref[...]
    lane = lax.broadcasted_iota(jnp.int32, logits.shape, 1).astype(F32)
    big = 1e9
    gl = jnp.where(lane < N_GROUPS, logits, NEG_INF)
    gmax = jnp.max(gl, axis=-1, keepdims=True)
    gi = jnp.min(jnp.where(gl == gmax, lane, big), axis=-1, keepdims=True)
    g_sel = 1.0 / jnp.sum(jnp.exp(gl - gmax), axis=-1, keepdims=True)
    lo = N_GROUPS + EXPERTS_PER_GROUP * gi
    in_grp = (lane >= lo) & (lane < lo + EXPERTS_PER_GROUP)
    el = jnp.where(in_grp, logits, NEG_INF)
    v1 = jnp.max(el, axis=-1, keepdims=True)
    i1 = jnp.min(jnp.where(el == v1, lane, big), axis=-1, keepdims=True)
    el2 = jnp.where(lane == i1, NEG_INF, el)
    v2 = jnp.max(el2, axis=-1, keepdims=True)
    i2 = jnp.min(jnp.where(el2 == v2, lane, big), axis=-1, keepdims=True)
    e2 = jnp.exp(v2 - v1)
    w1 = g_sel / (1.0 + e2)
    w2 = g_sel * e2 / (1.0 + e2)
    if not sparse:
        gates_ref[...] = jnp.where(lane + N_GROUPS == i1, w1, 0.0) + jnp.where(lane + N_GROUPS == i2, w2, 0.0)
        return
    @pl.when(pl.program_id(0) == 0)
    def _():
        cnt_ref[...] = jnp.zeros(cnt_ref.shape, F32)

    e0 = i1 - N_GROUPS
    e1 = i2 - N_GROUPS
    onehot = jnp.where(lane == e0, 1.0, 0.0) + jnp.where(lane == e1, 1.0, 0.0)
    before = _dot(tri_ref[...], onehot.astype(BF16)) + cnt_ref[...]
    r0 = jnp.sum(jnp.where(lane == e0, before, 0.0), axis=-1, keepdims=True)
    r1 = jnp.sum(jnp.where(lane == e1, before, 0.0), axis=-1, keepdims=True)
    cnt_ref[...] += jnp.sum(onehot, axis=0, keepdims=True)
    rec = jnp.zeros(logits.shape, F32)
    for col, val in enumerate((e0, e1, w1, w2, r0, r1)):
        rec = jnp.where(lane == col, val, rec)
    gates_ref[...] = rec


def _out_proj(x2d, oa, ob, wa, wb, n2w, wr, br, *, tm, sparse):
    m, d = x2d.shape
    row = lambda i: (i, 0)
    const = lambda i: (0, 0)
    in_specs = [pl.BlockSpec((tm, d), row),
                pl.BlockSpec((tm, A_WIDTH), row),
                pl.BlockSpec((tm, B_WIDTH), row),
                pl.BlockSpec((A_WIDTH, d), const),
                pl.BlockSpec((B_WIDTH, d), const),
                pl.BlockSpec((1, d), const),
                pl.BlockSpec((d, LANES), const),
                pl.BlockSpec((1, LANES), const)]
    args = [x2d, oa, ob, wa, wb, n2w, wr, br]
    if sparse:
        idx = jnp.arange(tm)
        args.append((idx[:, None] > idx[None, :]).astype(BF16))
        in_specs.append(pl.BlockSpec((tm, tm), const))
        out_specs = (pl.BlockSpec((tm, d), row), pl.BlockSpec((tm, d // 2), row), pl.BlockSpec((tm, LANES), row),
                     pl.BlockSpec((1, LANES), const))
        out_shape = (jax.ShapeDtypeStruct((m, d), F32), jax.ShapeDtypeStruct((m, d // 2), jnp.int32),
                     jax.ShapeDtypeStruct((m, LANES), F32), jax.ShapeDtypeStruct((1, LANES), F32))
    else:
        out_specs = (pl.BlockSpec((tm, d), row), pl.BlockSpec((tm, d), row), pl.BlockSpec((tm, LANES), row))
        out_shape = (jax.ShapeDtypeStruct((m, d), F32), jax.ShapeDtypeStruct((m, d), BF16),
                     jax.ShapeDtypeStruct((m, LANES), F32))
    return pl.pallas_call(
        functools.partial(_out_proj_kernel, sparse=sparse),
        grid=(m // tm,), in_specs=in_specs, out_specs=out_specs, out_shape=out_shape,
        compiler_params=_params(1), name="out_proj",
    )(*args)


def _moe_kernel(h_ref, n2_ref, gates_ref, wgu_ref, wd_ref, y_ref):
    e = pl.program_id(1)

    @pl.when(e == 0)
    def _():
        y_ref[...] = h_ref[...]

    n2 = n2_ref[...]
    gu = _dot(n2, wgu_ref[0].astype(BF16))
    act = _silu(gu[:, :D_EXPERT]) * gu[:, D_EXPERT:]
    lane = lax.broadcasted_iota(jnp.int32, gates_ref.shape, 1)
    gate = jnp.sum(jnp.where(lane == e, gates_ref[...], 0.0), axis=-1, keepdims=True)
    y_ref[...] += gate * _dot(act.astype(BF16), wd_ref[0].astype(BF16))


def _moe(h, n2, gates, wgu, wd, *, tm):
    m, d = h.shape
    row = lambda i, e: (i, 0)
    return pl.pallas_call(
        _moe_kernel,
        grid=(m // tm, N_EXPERTS),
        in_specs=[pl.BlockSpec((tm, d), row),
                  pl.BlockSpec((tm, d), row),
                  pl.BlockSpec((tm, LANES), row),
                  pl.BlockSpec((1, d, 2 * D_EXPERT), lambda i, e: (e, 0, 0)),
                  pl.BlockSpec((1, D_EXPERT, d), lambda i, e: (e, 0, 0))],
        out_specs=pl.BlockSpec((tm, d), row),
        out_shape=jax.ShapeDtypeStruct((m, d), F32),
        compiler_params=_params(2), name="moe",
    )(h, n2, gates, wgu, wd)


SC_CORES = 2
SC_SUBCORES = 16
SC_WORKERS = SC_CORES * SC_SUBCORES
SC_CHUNK = 64
MOE_TILE = 512


def _sc_mesh():
    return plsc.VectorSubcoreMesh(core_axis_name="c", subcore_axis_name="s",
                                  num_cores=SC_CORES, num_subcores=SC_SUBCORES)


def _sc_dispatch(src, pos, n_rows):
    m, d = src.shape
    n_chunks = m // (SC_WORKERS * SC_CHUNK)
    pos4 = pos.reshape(2, SC_WORKERS, n_chunks, SC_CHUNK).transpose(1, 0, 2, 3)

    def body(src_hbm, pos_hbm, out_hbm, idx_v, rows_v):
        wid = lax.axis_index("s") * SC_CORES + lax.axis_index("c")
        pltpu.sync_copy(pos_hbm.at[wid], idx_v)

        @pl.loop(0, n_chunks)
        def _(j):
            start = pl.multiple_of((wid * n_chunks + j) * SC_CHUNK, SC_CHUNK)
            pltpu.sync_copy(src_hbm.at[pl.ds(start, SC_CHUNK)], rows_v)
            pltpu.sync_copy(rows_v, out_hbm.at[idx_v.at[0, j]])
            pltpu.sync_copy(rows_v, out_hbm.at[idx_v.at[1, j]])

    return pl.kernel(
        body, out_type=jax.ShapeDtypeStruct((n_rows, d), src.dtype), mesh=_sc_mesh(),
        scratch_types=[pltpu.VMEM((2, n_chunks, SC_CHUNK), jnp.int32), pltpu.VMEM((SC_CHUNK, d), src.dtype)],
        name="moe_dispatch",
    )(src, pos4)


def _sc_gather(table, idx):
    b = idx.shape[0]
    d = table.shape[1]
    n_chunks = b // (SC_WORKERS * SC_CHUNK)
    idx3 = idx.reshape(SC_WORKERS, n_chunks, SC_CHUNK)

    def body(table_hbm, idx_hbm, out_hbm, idx_v, rows_v):
        wid = lax.axis_index("s") * SC_CORES + lax.axis_index("c")
        pltpu.sync_copy(idx_hbm.at[wid], idx_v)

        @pl.loop(0, n_chunks)
        def _(j):
            start = pl.multiple_of((wid * n_chunks + j) * SC_CHUNK, SC_CHUNK)
            pltpu.sync_copy(table_hbm.at[idx_v.at[j]], rows_v)
            pltpu.sync_copy(rows_v, out_hbm.at[pl.ds(start, SC_CHUNK)])

    return pl.kernel(
        body, out_type=jax.ShapeDtypeStruct((b, d), table.dtype), mesh=_sc_mesh(),
        scratch_types=[pltpu.VMEM((n_chunks, SC_CHUNK), jnp.int32), pltpu.VMEM((SC_CHUNK, d), table.dtype)],
        name="moe_collect",
    )(table, idx3)


def _moe_grouped_kernel(te_ref, nu_ref, xs_ref, wgu_ref, wd_ref, ys_ref, wgu16, wd16):
    i = pl.program_id(0)

    @pl.when((i == 0) | (te_ref[i] != te_ref[jnp.maximum(i - 1, 0)]))
    def _():
        wgu16[...] = wgu_ref[0].astype(BF16)
        wd16[...] = wd_ref[0].astype(BF16)

    @pl.when(i < nu_ref[0])
    def _():
        x = _unpack_bf16_pairs(xs_ref[...]).astype(BF16)
        gu = _dot(x, wgu16[...])
        act = _silu(gu[:, :D_EXPERT]) * gu[:, D_EXPERT:]
        ys_ref[...] = _pack_bf16_pairs(_dot(act.astype(BF16), wd16[...]))

    @pl.when(i >= nu_ref[0])
    def _():
        ys_ref[...] = jnp.zeros(ys_ref.shape, ys_ref.dtype)


def _moe_grouped(xs, tile_expert, n_used, wgu, wd):
    r, half = xs.shape
    d = 2 * half
    grid_spec = pltpu.PrefetchScalarGridSpec(
        num_scalar_prefetch=2, grid=(r // MOE_TILE,),
        in_specs=[pl.BlockSpec((MOE_TILE, half), lambda i, te, nu: (i, 0)),
                  pl.BlockSpec((1, d, 2 * D_EXPERT), lambda i, te, nu: (te[i], 0, 0)),
                  pl.BlockSpec((1, D_EXPERT, d), lambda i, te, nu: (te[i], 0, 0))],
        out_specs=pl.BlockSpec((MOE_TILE, half), lambda i, te, nu: (i, 0)),
        scratch_shapes=[pltpu.VMEM((d, 2 * D_EXPERT), BF16), pltpu.VMEM((D_EXPERT, d), BF16)])
    return pl.pallas_call(
        _moe_grouped_kernel, grid_spec=grid_spec,
        out_shape=jax.ShapeDtypeStruct((r, half), jnp.int32),
        compiler_params=_params(1), name="moe_grouped",
    )(tile_expert, n_used, xs, wgu, wd)


def _moe_combine_kernel(h_ref, z0_ref, z1_ref, rec_ref, y_ref):
    rec = rec_ref[...]
    y_ref[...] = (h_ref[...] + rec[:, 2:3] * _unpack_bf16_pairs(z0_ref[...])
                  + rec[:, 3:4] * _unpack_bf16_pairs(z1_ref[...]))


def _moe_combine(h, z, rec, *, tm):
    m, d = h.shape
    nt = m // tm
    row = lambda i: (i, 0)
    return pl.pallas_call(
        _moe_combine_kernel, grid=(nt,),
        in_specs=[pl.BlockSpec((tm, d), row),
                  pl.BlockSpec((tm, d // 2), row),
                  pl.BlockSpec((tm, d // 2), lambda i: (i + nt, 0)),
                  pl.BlockSpec((tm, LANES), row)],
        out_specs=pl.BlockSpec((tm, d), row),
        out_shape=jax.ShapeDtypeStruct((m, d), F32),
        compiler_params=_params(1), name="moe_combine",
    )(h, z, z, rec)


def _moe_route(rec, cnt):
    m = rec.shape[0]
    n_tiles = 2 * m // MOE_TILE + N_EXPERTS
    counts = cnt[0, :N_EXPERTS].astype(jnp.int32)
    tiles = (counts + MOE_TILE - 1) // MOE_TILE
    tile_end = jnp.cumsum(tiles)
    row_start = (tile_end - tiles) * MOE_TILE
    eid = rec[:, 0:2].astype(jnp.int32).T
    rank = rec[:, 4:6].astype(jnp.int32).T
    pos = rank
    for e in range(N_EXPERTS):
        pos = pos + jnp.where(eid == e, row_start[e], 0)
    n_used = tile_end[-1:]
    tile_id = jnp.minimum(jnp.arange(n_tiles, dtype=jnp.int32), n_used[0] - 1)
    tile_expert = jnp.sum((tile_end[None, :] <= tile_id[:, None]).astype(jnp.int32), axis=1)
    return pos, tile_expert, n_used.astype(jnp.int32), n_tiles * MOE_TILE


def _head_rows(v):
    rows = jnp.concatenate([jnp.zeros((B_HEADS,), F32), v.astype(F32)])
    return jnp.broadcast_to(rows[:, None], (2 * B_HEADS, LANES))


def kernel(x_prompt, x_sample, cache_win_k, cache_win_v, state_conv, state_ssm, norm1_w, w_in, qnorm_w, knorm_w, conv_w, a_log, dt_bias, onorm_w, w_out, norm2_w, w_group, b_group, w_expert_router, b_expert_router, w_gate_up, w_down):
    nb, seq, d = x_prompt.shape
    db, t_len, _ = x_sample.shape
    n_buf = cache_win_k.shape[1]
    assert n_buf == MAX_WINDOW and seq % (MAX_DILATION * BAND) == 0 and t_len == SUBLANES
    keep = min(MAX_WINDOW, seq)

    c_bg = 3 * A_WIDTH + CONV_DIM + B_WIDTH
    w_cat = jnp.concatenate([w_in, jnp.zeros((d, LANES - 2 * B_HEADS), w_in.dtype)], axis=1).astype(BF16)
    assert w_cat.shape[1] == c_bg + LANES
    n1w = norm1_w.reshape(1, d).astype(F32)
    qw = jnp.tile(qnorm_w.astype(F32), A_HEADS).reshape(1, A_WIDTH)
    kw = jnp.tile(knorm_w.astype(F32), A_HEADS).reshape(1, A_WIDTH)
    a_log_v = _head_rows(a_log)
    dtb_v = _head_rows(dt_bias)
    onw = onorm_w.reshape(1, B_VAL_DIM).astype(F32)
    wa = w_out[:A_WIDTH].astype(BF16)
    wb = w_out[A_WIDTH:].astype(BF16)
    n2w = norm2_w.reshape(1, d).astype(F32)
    wr = jnp.concatenate([w_group, jnp.transpose(w_expert_router, (1, 0, 2)).reshape(d, N_EXPERTS),
                          jnp.zeros((d, LANES - N_GROUPS - N_EXPERTS), F32)], axis=1).astype(F32)
    br = jnp.zeros((1, LANES), F32).at[0, :N_GROUPS].set(b_group).at[0, N_GROUPS:N_GROUPS + N_EXPERTS].set(
        b_expert_router.reshape(-1))
    wgu = w_gate_up.astype(F32)
    wd = w_down.astype(F32)
    cw = conv_w.astype(F32)

    xp2d = x_prompt.reshape(nb * seq, d)
    qf, kf, vf, kwin, vwin, bqkv, z, bg = _in_proj(xp2d, n1w, w_cat, qw, kw, seq=seq, keep=keep,
                                                   fold=True, tm=512)
    oa = _attn_prompt(qf, kf, vf).reshape(nb * seq, A_WIDTH)
    conv0 = jnp.zeros((nb, CONV_WIDTH - 1, CONV_DIM), F32)
    ssm0 = jnp.zeros((nb, B_HEADS, B_KEY_DIM, B_VAL_DIM), F32)
    ob, ssm_prompt = _gdn_prompt(bqkv, z, bg, cw, conv0, ssm0, a_log_v, dtb_v, onw,
                                 seq=seq, ts=512, block=128, chunk=CHUNK)
    assert (nb * seq) % (SC_WORKERS * SC_CHUNK) == 0 and nb * seq >= N_EXPERTS * MOE_TILE
    h_p, n2p, rec, cnt = _out_proj(xp2d, oa, ob, wa, wb, n2w, wr, br, tm=512, sparse=True)
    pos, tile_expert, n_used, n_rows = _moe_route(rec, cnt)
    xs = _sc_dispatch(n2p, pos, n_rows)
    win_k_prompt = jnp.transpose(kwin, (0, 3, 1, 2))
    win_v_prompt = jnp.transpose(vwin, (0, 3, 1, 2))
    conv_prompt = bqkv.reshape(nb, seq, CONV_DIM)[:, seq - (CONV_WIDTH - 1):]

    ms = db * t_len
    xs2d = x_sample.reshape(ms, d)
    tms = min(512, ms)
    qs, ksn, vsn, _, _, bqkv_s, z_s, bg_s = _in_proj(xs2d, n1w, w_cat, qw, kw, seq=ms, keep=ms,
                                                     fold=False, tm=tms)
    cache_kt = jnp.transpose(cache_win_k.astype(F32), (0, 2, 3, 1))
    cache_vt = jnp.transpose(cache_win_v.astype(F32), (0, 2, 3, 1))
    oa_s = _attn_sample(qs, ksn, vsn, cache_kt, cache_vt, t_len=t_len)
    xs, oa_s = lax.optimization_barrier((xs, oa_s))
    ys = _moe_grouped(xs, tile_expert, n_used, wgu, wd)
    zs = _sc_gather(ys, pos.reshape(-1))
    xpad = jnp.concatenate([state_conv.astype(F32), bqkv_s.reshape(db, t_len, CONV_DIM)], axis=1)
    nseq = 16 if db % 16 == 0 else 1
    ob_s, ssm_sample = _gdn_sample(xpad, z_s, bg_s, cw, state_ssm.astype(F32), a_log_v, dtb_v, onw,
                                   t_len=t_len, nseq=nseq)
    h_s, n2_s, gates_s = _out_proj(xs2d, oa_s, ob_s, wa, wb, n2w, wr, br, tm=tms, sparse=False)
    y_sample = _moe(h_s, n2_s, gates_s, wgu, wd, tm=ms)
    zs, y_sample = lax.optimization_barrier((zs, y_sample))
    y_prompt = _moe_combine(h_p, zs, rec, tm=MOE_TILE).reshape(nb, seq, d)
    y_sample = y_sample.reshape(db, t_len, d)
    win_k_sample = ksn.reshape(db, t_len, A_HEADS, A_HEAD_DIM)
    win_v_sample = vsn.reshape(db, t_len, A_HEADS, A_HEAD_DIM)
    conv_sample = xpad[:, t_len:]

    return (y_prompt, y_sample, win_k_prompt, win_v_prompt, conv_prompt, ssm_prompt,
            win_k_sample, win_v_sample, conv_sample, ssm_sample)
```

```python
import functools

import jax
import jax.numpy as jnp
from jax import lax
from jax.experimental import pallas as pl
from jax.experimental.pallas import tpu as pltpu
from jax.experimental.pallas import tpu_sc as plsc

F32 = jnp.float32
BF16 = jnp.bfloat16

A_HEADS = 8
A_HEAD_DIM = 64
A_WIDTH = A_HEADS * A_HEAD_DIM
DILATED_PATTERNS = ((128, 1), (512, 4), (2048, 16))
MAX_WINDOW = 2048
MAX_DILATION = 16
BAND = 128
ATTN_SCALE = A_HEAD_DIM ** -0.5
LOG2_E = 1.4426950408889634
B_HEADS = 4
B_KEY_DIM = 128
B_VAL_DIM = 128
B_WIDTH = B_HEADS * B_VAL_DIM
CONV_WIDTH = 4
CONV_DIM = B_HEADS * (2 * B_KEY_DIM + B_VAL_DIM)
CHUNK = 64
N_GROUPS = 4
EXPERTS_PER_GROUP = 4
N_EXPERTS = N_GROUPS * EXPERTS_PER_GROUP
D_EXPERT = 512
RMS_EPS = 1e-6
NEG_INF = -1e30

LANES = 128
SUBLANES = 8
VMEM_LIMIT = 56 * 1024 * 1024


def _params(n_axes, vmem=VMEM_LIMIT):
    return pltpu.CompilerParams(dimension_semantics=("arbitrary",) * n_axes, vmem_limit_bytes=vmem)


def _split2(x):
    hi = x.astype(BF16)
    lo = (x - hi.astype(F32)).astype(BF16)
    return hi, lo


def _split3(x):
    hi = x.astype(BF16)
    r = x - hi.astype(F32)
    mid = r.astype(BF16)
    lo = (r - mid.astype(F32)).astype(BF16)
    return hi, mid, lo


def _dot(a, b):
    return jnp.dot(a, b, preferred_element_type=F32)


def _dot_nt(a, b):
    return lax.dot_general(a, b, (((1,), (1,)), ((), ())), preferred_element_type=F32)


def _dot_tn(a, b):
    return lax.dot_general(a, b, (((0,), (0,)), ((), ())), preferred_element_type=F32)


def _sigmoid(x):
    return 1.0 / (1.0 + jnp.exp(-x))


def _silu(x):
    return x * _sigmoid(x)


def _in_proj_kernel(x_ref, n1w_ref, w_ref, qw_ref, kw_ref,
                    q_ref, k_ref, v_ref, kwin_ref, vwin_ref, bqkv_ref, z_ref, bg_ref,
                    *scratch, fold):
    tm = x_ref.shape[0]
    x = x_ref[...]
    ms = jnp.mean(x * x, axis=-1, keepdims=True)
    n1 = (x * lax.rsqrt(ms + RMS_EPS) * n1w_ref[...]).astype(BF16)

    first_head = lax.broadcasted_iota(jnp.int32, (tm, LANES), 1) < A_HEAD_DIM

    def head_norm(t, w):
        cols = []
        for c in range(A_WIDTH // LANES):
            tc = t[:, c * LANES:(c + 1) * LANES]
            sq = tc * tc
            s0 = jnp.sum(jnp.where(first_head, sq, 0.0), axis=-1, keepdims=True)
            s1 = jnp.sum(jnp.where(first_head, 0.0, sq), axis=-1, keepdims=True)
            ms = jnp.where(first_head, s0, s1) * (1.0 / A_HEAD_DIM)
            cols.append(tc * lax.rsqrt(ms + RMS_EPS))
        return jnp.concatenate(cols, axis=1) * w

    q_scale = ATTN_SCALE * LOG2_E if fold else ATTN_SCALE
    q = head_norm(_dot(n1, w_ref[:, 0:A_WIDTH]), qw_ref[...]) * q_scale
    k = head_norm(_dot(n1, w_ref[:, A_WIDTH:2 * A_WIDTH]), kw_ref[...])
    v = _dot(n1, w_ref[:, 2 * A_WIDTH:3 * A_WIDTH])
    if fold:
        kwin_ref[0] = k.T.reshape(A_HEADS, A_HEAD_DIM, tm)
        vwin_ref[0] = v.T.reshape(A_HEADS, A_HEAD_DIM, tm)

        (scr,) = scratch
        rows = tm // MAX_DILATION
        for val, out in ((q, q_ref), (k, k_ref), (v, v_ref)):
            for c in range(A_WIDTH // LANES):
                cs = slice(c * LANES, (c + 1) * LANES)
                scr[c] = val[:, cs]
                for r in range(MAX_DILATION):
                    out[0, r, :, cs] = scr[c, pl.ds(r, rows, stride=MAX_DILATION), :]
    else:
        kwin_ref[...] = k
        vwin_ref[...] = v
        q_ref[...] = q
        k_ref[...] = k
        v_ref[...] = v
    c0 = 3 * A_WIDTH
    bqkv_ref[...] = _dot(n1, w_ref[:, c0:c0 + CONV_DIM])
    c1 = c0 + CONV_DIM
    z_ref[...] = _dot(n1, w_ref[:, c1:c1 + B_WIDTH]).astype(BF16)
    c2 = c1 + B_WIDTH
    bg_ref[...] = _dot(n1, w_ref[:, c2:c2 + LANES])


def _in_proj(x2d, n1w, w_cat, qw, kw, *, seq, keep, fold, tm):
    m, d = x2d.shape
    nb = m // seq
    tiles_per_seq = seq // tm
    skip = (seq - keep) // tm
    keep_tiles = keep // tm
    n_cols = w_cat.shape[1]

    def win_map(i):
        return (i // tiles_per_seq) * keep_tiles + jnp.maximum(i % tiles_per_seq - skip, 0), 0

    row = lambda i: (i, 0)
    const = lambda i: (0, 0)
    if fold:
        rows = tm // MAX_DILATION
        qkv_shape = jax.ShapeDtypeStruct((nb, MAX_DILATION, seq // MAX_DILATION, A_WIDTH), F32)
        qkv_spec = pl.BlockSpec((1, MAX_DILATION, rows, A_WIDTH),
                                lambda i: (i // tiles_per_seq, 0, i % tiles_per_seq, 0))
        scratch = [pltpu.VMEM((A_WIDTH // LANES, tm, LANES), F32)]
        win_shape = jax.ShapeDtypeStruct((nb, A_HEADS, A_HEAD_DIM, keep), F32)
        win_spec = pl.BlockSpec((1, A_HEADS, A_HEAD_DIM, tm),
                                lambda i: (i // tiles_per_seq, 0, 0, jnp.maximum(i % tiles_per_seq - skip, 0)))
    else:
        qkv_shape = jax.ShapeDtypeStruct((m, A_WIDTH), F32)
        qkv_spec = pl.BlockSpec((tm, A_WIDTH), row)
        scratch = []
        win_shape = jax.ShapeDtypeStruct((nb * keep, A_WIDTH), F32)
        win_spec = pl.BlockSpec((tm, A_WIDTH), win_map)
    out_shape = (qkv_shape, qkv_shape, qkv_shape, win_shape, win_shape,
                 jax.ShapeDtypeStruct((m, CONV_DIM), F32),
                 jax.ShapeDtypeStruct((m, B_WIDTH), BF16),
                 jax.ShapeDtypeStruct((m, LANES), F32))
    out_specs = (qkv_spec, qkv_spec, qkv_spec, win_spec, win_spec,
                 pl.BlockSpec((tm, CONV_DIM), row),
                 pl.BlockSpec((tm, B_WIDTH), row),
                 pl.BlockSpec((tm, LANES), row))
    in_specs = [pl.BlockSpec((tm, d), row),
                pl.BlockSpec((1, d), const),
                pl.BlockSpec((d, n_cols), const),
                pl.BlockSpec((1, A_WIDTH), const),
                pl.BlockSpec((1, A_WIDTH), const)]
    return pl.pallas_call(
        functools.partial(_in_proj_kernel, fold=fold),
        grid=(m // tm,), in_specs=in_specs, out_specs=out_specs, out_shape=out_shape,
        scratch_shapes=scratch, compiler_params=_params(1), name="in_proj",
    )(x2d, n1w, w_cat, qw, kw)


def _band_mask(pieces):
    sub = BAND // pieces
    r = lax.broadcasted_iota(jnp.int32, (BAND, 2 * BAND), 0)
    c = lax.broadcasted_iota(jnp.int32, (BAND, 2 * BAND), 1)
    qpos = (r % sub) * pieces + r // sub + BAND
    cc = c % BAND
    kpos = (cc % sub) * pieces + cc // sub + (c // BAND) * BAND
    dist = qpos - kpos
    return (dist >= 0) & (dist <= BAND), c >= BAND


ATTN_UNROLL = 16


def _attn_prompt_kernel(q_ref, k_ref, v_ref, o_ref, p_scr, op_scr, mx_scr, l_scr, nat_scr):
    u_len = q_ref.shape[2]
    lane = lax.broadcasted_iota(jnp.int32, (BAND, LANES), 1)
    head0 = lane < A_HEAD_DIM

    def per_head(col):
        return jnp.where(head0, jnp.broadcast_to(col[:BAND], (BAND, LANES)),
                         jnp.broadcast_to(col[BAND:], (BAND, LANES)))

    for pi, pieces in enumerate((16, 4, 1)):
        sub = BAND // pieces
        n_res = MAX_DILATION // pieces
        n_blk = u_len // sub
        band, _ = _band_mask(pieces)
        bias = jnp.where(band, 0.0, NEG_INF)
        bias2 = jnp.concatenate([bias, bias], axis=0)
        bias2_cur = bias2[:, BAND:]

        def locate(blk, n_res=n_res, sub=sub):
            m = blk // n_res
            cur = pl.multiple_of(m * sub, SUBLANES)
            prv = pl.multiple_of(jnp.maximum(m - 1, 0) * sub, SUBLANES)
            return blk % n_res, cur, prv

        def gather(ref, res, start, pieces=pieces, n_res=n_res, sub=sub):
            parts = [ref[0, a * n_res + res, pl.ds(start, sub), :] for a in range(pieces)]
            return parts[0] if pieces == 1 else jnp.concatenate(parts, axis=0)

        def scatter(ref, res, start, val, pi=pi, pieces=pieces, n_res=n_res, sub=sub):
            for a in range(pieces):
                ref[pi, a * n_res + res, pl.ds(start, sub), :] = val[a * sub:(a + 1) * sub]

        def probs(it, carry, first, bias2=bias2, bias2_cur=bias2_cur):
            for j, is_first in enumerate(first):
                blk = it * ATTN_UNROLL + j
                res, cur, prv = locate(blk)
                qb = gather(q_ref, res, cur)
                q2 = jnp.concatenate([jnp.where(head0, qb, 0.0), jnp.where(head0, 0.0, qb)], axis=0).astype(BF16)
                kc = gather(k_ref, res, cur)
                if is_first:
                    s = _dot_nt(q2, kc.astype(BF16)) + bias2_cur
                else:
                    kb = jnp.concatenate([gather(k_ref, res, prv), kc], axis=0).astype(BF16)
                    s = _dot_nt(q2, kb) + bias2
                mx = jnp.max(s, axis=-1, keepdims=True)
                p = jnp.exp2(s - mx)
                if is_first:
                    p_scr[blk, :, BAND:] = p.astype(BF16)
                else:
                    p_scr[blk] = p.astype(BF16)
                scatter(mx_scr, res, cur, per_head(mx))
            return carry

        def values(it, carry, first):
            for j, is_first in enumerate(first):
                blk = it * ATTN_UNROLL + j
                res, cur, prv = locate(blk)
                vc = gather(v_ref, res, cur)
                if is_first:
                    p = p_scr[blk, :, BAND:]
                    vb = vc.astype(BF16)
                else:
                    p = p_scr[blk]
                    vb = jnp.concatenate([gather(v_ref, res, prv), vc], axis=0).astype(BF16)
                o2 = _dot(p, jnp.concatenate([vb, jnp.ones(vb.shape, BF16)], axis=1))
                scatter(op_scr, res, cur, jnp.where(head0, o2[:BAND, :LANES], o2[BAND:, :LANES]))
                scatter(l_scr, res, cur, jnp.where(head0, o2[:BAND, LANES:], o2[BAND:, LANES:]))
            return carry

        n_it = n_res * n_blk // ATTN_UNROLL
        flags = [tuple(g * ATTN_UNROLL + j < n_res for j in range(ATTN_UNROLL)) for g in range(n_it)]
        segments = []
        for g, f in enumerate(flags):
            if segments and segments[-1][2] == f:
                segments[-1][1] = g + 1
            else:
                segments.append([g, g + 1, f])
        for phase in (probs, values):
            for lo, hi, f in segments:
                lax.fori_loop(lo, hi, functools.partial(phase, first=f), 0)

    for r in range(MAX_DILATION):
        m0, m1, m2 = mx_scr[0, r], mx_scr[1, r], mx_scr[2, r]
        big = jnp.maximum(jnp.maximum(m0, m1), m2)
        e0, e1, e2 = jnp.exp2(m0 - big), jnp.exp2(m1 - big), jnp.exp2(m2 - big)
        num = e0 * op_scr[0, r] + e1 * op_scr[1, r] + e2 * op_scr[2, r]
        den = e0 * l_scr[0, r] + e1 * l_scr[1, r] + e2 * l_scr[2, r]
        nat_scr[pl.ds(r, u_len, stride=MAX_DILATION), :] = num / den
    o_ref[0] = nat_scr[...].astype(BF16)


def _attn_prompt(qf, kf, vf):
    nb, _, u_len, _ = qf.shape
    seq = u_len * MAX_DILATION
    n_pairs = A_WIDTH // LANES
    spec = pl.BlockSpec((1, MAX_DILATION, u_len, LANES), lambda b, p: (b, 0, 0, p))
    return pl.pallas_call(
        _attn_prompt_kernel,
        grid=(nb, n_pairs),
        in_specs=[spec, spec, spec],
        out_specs=pl.BlockSpec((1, seq, LANES), lambda b, p: (b, 0, p)),
        out_shape=jax.ShapeDtypeStruct((nb, seq, A_WIDTH), BF16),
        scratch_shapes=[pltpu.VMEM((seq // BAND, 2 * BAND, 2 * BAND), BF16),
                        pltpu.VMEM((3, MAX_DILATION, u_len, LANES), F32),
                        pltpu.VMEM((3, MAX_DILATION, u_len, LANES), F32),
                        pltpu.VMEM((3, MAX_DILATION, u_len, LANES), F32),
                        pltpu.VMEM((seq, LANES), F32)],
        compiler_params=_params(2), name="attn_prompt",
    )(qf, kf, vf)


def _multiplicity(t, dd):
    cnt = jnp.zeros(dd.shape, F32)
    for w, d in DILATED_PATTERNS:
        ok = (dd >= 0) & (dd <= w) & (dd % d == 0)
        cnt = cnt + jnp.where(ok, 1.0, 0.0)
    return cnt


def _attn_sample_kernel(q_ref, kn_ref, vn_ref, kt_ref, vt_ref, o_ref):
    t_len = q_ref.shape[0]
    n_buf = kt_ref.shape[3]
    q = q_ref[...]
    kn = kn_ref[...]
    vn = vn_ref[...]
    tq = n_buf + lax.broadcasted_iota(jnp.int32, (t_len, n_buf), 0)
    cnt = _multiplicity(None, tq - lax.broadcasted_iota(jnp.int32, (t_len, n_buf), 1))
    cnt_new = _multiplicity(None, lax.broadcasted_iota(jnp.int32, (t_len, t_len), 0)
                            - lax.broadcasted_iota(jnp.int32, (t_len, t_len), 1))
    outs = []
    for h in range(A_HEADS):
        hs = slice(h * A_HEAD_DIM, (h + 1) * A_HEAD_DIM)
        qh = q[:, hs].astype(BF16)
        s = jnp.where(cnt > 0, _dot(qh, kt_ref[0, h].astype(BF16)), NEG_INF)
        sn = jnp.where(cnt_new > 0, _dot_nt(qh, kn[:, hs].astype(BF16)), NEG_INF)
        m = jnp.maximum(jnp.max(s, axis=-1, keepdims=True), jnp.max(sn, axis=-1, keepdims=True))
        p = cnt * jnp.exp(s - m)
        pn = cnt_new * jnp.exp(sn - m)
        l = jnp.sum(p, axis=-1, keepdims=True) + jnp.sum(pn, axis=-1, keepdims=True)
        o = _dot_nt(p.astype(BF16), vt_ref[0, h].astype(BF16)) + _dot(pn.astype(BF16), vn[:, hs].astype(BF16))
        outs.append(o / l)
    o_ref[...] = jnp.concatenate(outs, axis=-1)


def _attn_sample(q2d, kn2d, vn2d, cache_kt, cache_vt, *, t_len):
    nb, _, _, n_buf = cache_kt.shape
    tok = pl.BlockSpec((t_len, A_WIDTH), lambda b: (b, 0))
    win = pl.BlockSpec((1, A_HEADS, A_HEAD_DIM, n_buf), lambda b: (b, 0, 0, 0))
    return pl.pallas_call(
        _attn_sample_kernel,
        grid=(nb,),
        in_specs=[tok, tok, tok, win, win],
        out_specs=tok,
        out_shape=jax.ShapeDtypeStruct((nb * t_len, A_WIDTH), F32),
        compiler_params=_params(1), name="attn_sample",
    )(q2d, kn2d, vn2d, cache_kt, cache_vt)


def _unit_lower_inverse(a, n_chunk):
    n = a.shape[0]
    ri = lax.broadcasted_iota(jnp.int32, (n, n), 0)
    ci = lax.broadcasted_iota(jnp.int32, (n, n), 1)
    eye = jnp.where(ri == ci, 1.0, 0.0).astype(F32)
    t = eye - a
    pw = a
    k = 2
    while k < n_chunk:
        pw16 = pw.astype(BF16)
        pw = _dot(pw16, pw16)
        t = t + _dot(t.astype(BF16), pw.astype(BF16))
        k *= 2
    return t


def _gdn_intra(q, k, v, beta, dcol, drow, chunk):
    n = q.shape[0]
    ri = lax.broadcasted_iota(jnp.int32, (n, n), 0)
    ci = lax.broadcasted_iota(jnp.int32, (n, n), 1)
    same = (ri // chunk) == (ci // chunk)
    lower = same & (ri >= ci)
    strict = same & (ri > ci)
    gam = jnp.exp(jnp.where(lower, dcol - drow, NEG_INF))
    kb = k * beta
    vb = v * beta
    k16 = k.astype(BF16)
    a_mat = jnp.where(strict, _dot_nt(kb.astype(BF16), k16) * gam, 0.0)
    attn = jnp.where(lower, _dot_nt(q.astype(BF16), k16) * gam, 0.0)
    t_inv = _unit_lower_inverse(a_mat, chunk).astype(BF16)
    ed = jnp.exp(dcol)
    u = _dot(t_inv, vb.astype(BF16))
    w = _dot(t_inv, (kb * ed).astype(BF16))
    qd = q * ed
    return u, w, qd, attn


def _gdn_prep(c, bg, a_col, dt_col, chunk):
    n = c.shape[0]
    bgt = bg.T[0:2 * B_HEADS]
    row = lax.broadcasted_iota(jnp.int32, bgt.shape, 0)
    xg = bgt + dt_col
    sp = jnp.maximum(xg, 0.0) + jnp.log1p(jnp.exp(-jnp.abs(xg)))
    gt = jnp.where(row >= B_HEADS, -a_col * sp, 0.0)
    ri = lax.broadcasted_iota(jnp.int32, (n, n), 0)
    ci = lax.broadcasted_iota(jnp.int32, (n, n), 1)
    tri = jnp.where(((ri // chunk) == (ci // chunk)) & (ri >= ci), 1.0, 0.0).astype(BF16)
    th, tm_, tl = _split3(gt)
    drow = _dot_nt(th, tri) + _dot_nt(tm_, tri) + _dot_nt(tl, tri)
    both = jnp.where(row >= B_HEADS, drow, _sigmoid(bgt))
    cols = jnp.concatenate([both, jnp.zeros((LANES - 2 * B_HEADS, n), F32)], axis=0).T
    beta_all = cols
    dcol = cols
    heads = []
    kd_ = B_KEY_DIM
    for h in range(B_HEADS):
        qh = c[:, h * kd_:(h + 1) * kd_]
        kh = c[:, B_HEADS * kd_ + h * kd_:B_HEADS * kd_ + (h + 1) * kd_]
        vh = c[:, 2 * B_HEADS * kd_ + h * B_VAL_DIM:2 * B_HEADS * kd_ + (h + 1) * B_VAL_DIM]
        qh = qh * lax.rsqrt(jnp.sum(qh * qh, axis=-1, keepdims=True) + RMS_EPS) * (B_KEY_DIM ** -0.5)
        kh = kh * lax.rsqrt(jnp.sum(kh * kh, axis=-1, keepdims=True) + RMS_EPS)
        heads.append((qh, kh, vh, beta_all[:, h:h + 1], dcol[:, B_HEADS + h:B_HEADS + h + 1],
                      drow[B_HEADS + h:B_HEADS + h + 1, :]))
    return heads


def _gdn_out(o, onw, zh):
    ms = jnp.mean(o * o, axis=-1, keepdims=True)
    return o * lax.rsqrt(ms + RMS_EPS) * onw * _silu(zh)


def _gdn_prompt_kernel(x_ref, z_ref, bg_ref, cw_ref, cst_ref, sst_ref, avec_ref, dtb_ref, onw_ref,
                       o_ref, snew_ref, xp_scr, s_scr, *, block, chunk):
    ts = x_ref.shape[0]
    t = pl.program_id(1)
    pad = SUBLANES
    hist = CONV_WIDTH - 1

    @pl.when(t == 0)
    def _():
        xp_scr[...] = jnp.zeros((pad, CONV_DIM), F32)
        xp_scr[pad - hist:pad, :] = cst_ref[0]
        s_scr[...] = sst_ref[0]

    assert CONV_WIDTH == 4
    x = x_ref[...]
    tail = xp_scr[...]
    sub = lax.broadcasted_iota(jnp.int32, (pad, CONV_DIM), 0)

    def shift(cur, prev_tail, s):
        rolled = pltpu.roll(cur, s, axis=0)
        head = jnp.where(sub < s, pltpu.roll(prev_tail, s, axis=0), rolled[0:pad])
        return jnp.concatenate([head, rolled[pad:]], axis=0)

    w0, w1, w2, w3 = (cw_ref[i:i + 1, :] for i in range(CONV_WIDTH))
    x1 = shift(x, tail, 1)
    a = x * w1 + x1 * w0
    a_tail = tail * w1 + pltpu.roll(tail, 1, axis=0) * w0
    y = x * w3 + x1 * w2 + shift(a, a_tail, 2)
    xp_scr[...] = x[ts - pad:ts]
    c = _silu(y)
    avec = jnp.exp(avec_ref[...])
    onw = onw_ref[...]
    n_chunks = block // chunk
    n_sb = ts // block
    ri = lax.broadcasted_iota(jnp.int32, (block, block), 0)
    ci = lax.broadcasted_iota(jnp.int32, (block, block), 1)
    same = (ri // chunk) == (ci // chunk)
    lower = same & (ri >= ci)
    strict = same & (ri > ci)
    eye = jnp.where(ri == ci, 1.0, 0.0).astype(F32)

    units = []
    for sb in range(n_sb):
        rs = slice(sb * block, (sb + 1) * block)
        heads = _gdn_prep(c[rs], bg_ref[rs, :], avec, dtb_ref[...], chunk)
        for h, (qh, kh, vh, beta, dcol, drow) in enumerate(heads):
            units.append(dict(sb=sb, h=h, q=qh, k=kh, v=vh, beta=beta, dcol=dcol, drow=drow))

    for u in units:
        gam = jnp.exp(jnp.where(lower, u["dcol"] - u["drow"], NEG_INF))
        kb = u["k"] * u["beta"]
        kq = _dot_nt(jnp.concatenate([kb, u["q"]], axis=0).astype(BF16), u["k"].astype(BF16))
        u["pw"] = jnp.where(strict, kq[:block] * gam, 0.0)
        u["t"] = eye - u["pw"]
        u["attn"] = jnp.where(lower, kq[block:] * gam, 0.0).astype(BF16)
        ed = jnp.exp(u["dcol"])
        u["rhs"] = jnp.concatenate([u["v"] * u["beta"], kb * ed], axis=1).astype(BF16)
        u["qd"] = u["q"] * ed
        u["kt"] = u["k"].T

    kk = 2
    while kk < chunk:
        for u in units:
            p16 = u["pw"].astype(BF16)
            u["pw"] = _dot(p16, p16)
        for u in units:
            u["t"] = u["t"] + _dot(u["t"].astype(BF16), u["pw"].astype(BF16))
        kk *= 2

    for u in units:
        u["uw"] = _dot(u["t"].astype(BF16), u["rhs"]).astype(BF16)
    for u in units:
        au_aw = _dot(u["attn"], u["uw"])
        u["op"] = au_aw[:, :B_VAL_DIM]
        u["qp"] = (u["qd"] - au_aw[:, B_VAL_DIM:]).astype(BF16)
    for u in units:
        u["n"], u["mw"], u["e"] = [], [], []
        for cc in range(n_chunks):
            cs = slice(cc * chunk, (cc + 1) * chunk)
            last = u["drow"][:, (cc + 1) * chunk - 1:(cc + 1) * chunk]
            kdt = (u["kt"][:, cs] * jnp.exp(last - u["drow"][:, cs])).astype(BF16)
            nm = _dot(kdt, u["uw"][cs])
            u["n"].append(nm[:, :B_VAL_DIM])
            u["mw"].append(nm[:, B_VAL_DIM:].astype(BF16))
            u["e"].append(jnp.exp(last))

    s = [s_scr[h] for h in range(B_HEADS)]
    for sb in range(n_sb):
        rs = slice(sb * block, (sb + 1) * block)
        outs = [[] for _ in range(B_HEADS)]
        for cc in range(n_chunks):
            cs = slice(cc * chunk, (cc + 1) * chunk)
            for h in range(B_HEADS):
                u = units[sb * B_HEADS + h]
                r = _dot(jnp.concatenate([u["mw"][cc], u["qp"][cs]], axis=0), s[h].astype(BF16))
                outs[h].append(r[B_KEY_DIM:] + u["op"][cs])
                s[h] = s[h] * u["e"][cc] - r[:B_KEY_DIM] + u["n"][cc]
        for h in range(B_HEADS):
            o_all = jnp.concatenate(outs[h], axis=0) if n_chunks > 1 else outs[h][0]
            zh = z_ref[rs, h * B_VAL_DIM:(h + 1) * B_VAL_DIM].astype(F32)
            o_ref[rs, h * B_VAL_DIM:(h + 1) * B_VAL_DIM] = _gdn_out(o_all, onw, zh).astype(o_ref.dtype)
    for h in range(B_HEADS):
        s_scr[h] = s[h]

    @pl.when(t == pl.num_programs(1) - 1)
    def _():
        snew_ref[0] = s_scr[...]


def _gdn_prompt(bqkv, z, bg, conv_w, conv_state, ssm_state, a_log_v, dtb_v, onw, *, seq, ts, block, chunk):
    m = bqkv.shape[0]
    nb = m // seq
    tps = seq // ts
    row = lambda b, t: (b * tps + t, 0)
    const = lambda b, t: (0, 0)
    return pl.pallas_call(
        functools.partial(_gdn_prompt_kernel, block=block, chunk=chunk),
        grid=(nb, tps),
        in_specs=[pl.BlockSpec((ts, CONV_DIM), row),
                  pl.BlockSpec((ts, B_WIDTH), row),
                  pl.BlockSpec((ts, LANES), row),
                  pl.BlockSpec((CONV_WIDTH, CONV_DIM), const),
                  pl.BlockSpec((1, CONV_WIDTH - 1, CONV_DIM), lambda b, t: (b, 0, 0)),
                  pl.BlockSpec((1, B_HEADS, B_KEY_DIM, B_VAL_DIM), lambda b, t: (b, 0, 0, 0)),
                  pl.BlockSpec((2 * B_HEADS, LANES), const),
                  pl.BlockSpec((2 * B_HEADS, LANES), const),
                  pl.BlockSpec((1, B_VAL_DIM), const)],
        out_specs=(pl.BlockSpec((ts, B_WIDTH), row),
                   pl.BlockSpec((1, B_HEADS, B_KEY_DIM, B_VAL_DIM), lambda b, t: (b, 0, 0, 0))),
        out_shape=(jax.ShapeDtypeStruct((m, B_WIDTH), BF16),
                   jax.ShapeDtypeStruct(ssm_state.shape, F32)),
        scratch_shapes=[pltpu.VMEM((SUBLANES, CONV_DIM), F32),
                        pltpu.VMEM((B_HEADS, B_KEY_DIM, B_VAL_DIM), F32)],
        compiler_params=_params(2), name="gdn_prompt",
    )(bqkv, z, bg, conv_w, conv_state, ssm_state, a_log_v, dtb_v, onw)


def _gdn_sample_kernel(xp_ref, z_ref, bg_ref, cw_ref, sst_ref, avec_ref, dtb_ref, onw_ref,
                       o_ref, snew_ref, vn_scr, *, t_len):
    nseq = xp_ref.shape[0]
    hist = CONV_WIDTH - 1
    ys = []
    for j in range(nseq):
        y = xp_ref[j, 0:t_len, :] * cw_ref[0:1, :]
        for i in range(1, CONV_WIDTH):
            y = y + xp_ref[j, i:i + t_len, :] * cw_ref[i:i + 1, :]
        ys.append(y)
    c = _silu(jnp.concatenate(ys, axis=0))
    avec = jnp.exp(avec_ref[...])
    onw = onw_ref[...]
    heads = _gdn_prep(c, bg_ref[...], avec, dtb_ref[...], t_len)
    for h, (qh, kh, vh, beta, dcol, drow) in enumerate(heads):
        u, w, qd, attn = _gdn_intra(qh, kh, vh, beta, dcol, drow, t_len)
        attn16 = attn.astype(BF16)
        kt = kh.T
        vn_scr[...] = jnp.zeros(vn_scr.shape, F32)
        outs = []
        for j in range(nseq):
            cs = slice(j * t_len, (j + 1) * t_len)
            last = drow[:, (j + 1) * t_len - 1:(j + 1) * t_len]
            s = sst_ref[j, h]
            wq = jnp.concatenate([w[cs], qd[cs]], axis=0).astype(BF16)
            ws_qs = _dot(wq, s.astype(BF16))
            v_new = u[cs] - ws_qs[:t_len]
            vn_scr[cs, :] = v_new
            o = ws_qs[t_len:] + _dot(attn16[cs], vn_scr[...].astype(BF16))
            kdt = kt[:, cs] * jnp.exp(last - drow[:, cs])
            snew_ref[j, h] = s * jnp.exp(last) + _dot(kdt.astype(BF16), v_new.astype(BF16))
            outs.append(o)
        o_all = jnp.concatenate(outs, axis=0)
        zh = z_ref[:, h * B_VAL_DIM:(h + 1) * B_VAL_DIM].astype(F32)
        o_ref[:, h * B_VAL_DIM:(h + 1) * B_VAL_DIM] = _gdn_out(o_all, onw, zh).astype(o_ref.dtype)


def _gdn_sample(xp, z, bg, conv_w, ssm_state, a_log_v, dtb_v, onw, *, t_len, nseq):
    nb = xp.shape[0]
    rows = nseq * t_len
    row = lambda i: (i, 0)
    const = lambda i: (0, 0)
    return pl.pallas_call(
        functools.partial(_gdn_sample_kernel, t_len=t_len),
        grid=(nb // nseq,),
        in_specs=[pl.BlockSpec((nseq, xp.shape[1], CONV_DIM), lambda i: (i, 0, 0)),
                  pl.BlockSpec((rows, B_WIDTH), row),
                  pl.BlockSpec((rows, LANES), row),
                  pl.BlockSpec((CONV_WIDTH, CONV_DIM), const),
                  pl.BlockSpec((nseq, B_HEADS, B_KEY_DIM, B_VAL_DIM), lambda i: (i, 0, 0, 0)),
                  pl.BlockSpec((2 * B_HEADS, LANES), const),
                  pl.BlockSpec((2 * B_HEADS, LANES), const),
                  pl.BlockSpec((1, B_VAL_DIM), const)],
        out_specs=(pl.BlockSpec((rows, B_WIDTH), row),
                   pl.BlockSpec((nseq, B_HEADS, B_KEY_DIM, B_VAL_DIM), lambda i: (i, 0, 0, 0))),
        out_shape=(jax.ShapeDtypeStruct((nb * t_len, B_WIDTH), BF16),
                   jax.ShapeDtypeStruct(ssm_state.shape, F32)),
        scratch_shapes=[pltpu.VMEM((rows, B_VAL_DIM), F32)],
        compiler_params=_params(1), name="gdn_sample",
    )(xp, z, bg, conv_w, ssm_state, a_log_v, dtb_v, onw)


def _pack_bf16_pairs(x):
    n = x.shape[1] // 2
    bits = pltpu.bitcast(x.astype(BF16).astype(F32), jnp.int32)
    return lax.shift_right_logical(bits[:, :n], 16) | (bits[:, n:] & jnp.int32(-65536))


def _unpack_bf16_pairs(w):
    lo = pltpu.bitcast(lax.shift_left(w, 16), F32)
    hi = pltpu.bitcast(w & jnp.int32(-65536), F32)
    return jnp.concatenate([lo, hi], axis=1)


def _out_proj_kernel(x_ref, oa_ref, ob_ref, wa_ref, wb_ref, n2w_ref, wr_ref, br_ref, *rest, sparse):
    if sparse:
        tri_ref, h_ref, n2_ref, gates_ref, cnt_ref = rest
    else:
        h_ref, n2_ref, gates_ref = rest
    h = x_ref[...] + _dot(oa_ref[...].astype(BF16), wa_ref[...]) + _dot(ob_ref[...].astype(BF16), wb_ref[...])
    h_ref[...] = h
    ms = jnp.mean(h * h, axis=-1, keepdims=True)
    n2 = h * lax.rsqrt(ms + RMS_EPS) * n2w_ref[...]
    if sparse:
        n2_ref[...] = _pack_bf16_pairs(n2)
    else:
        n2_ref[...] = n2.astype(BF16)
    nh, nl = _split2(n2)
    wh, wl = _split2(wr_ref[...])
    tm = n2.shape[0]
    parts = _dot(jnp.concatenate([nh, nl], axis=0), jnp.concatenate([wh, wl], axis=1))
    logits = parts[:tm, :LANES] + parts[:tm, LANES:] + parts[tm:, :LANES] + br_ref[...]
    n_rows = 3 * SUBLANES
    lt = logits.T[0:n_rows]
    row = lax.broadcasted_iota(jnp.int32, lt.shape, 0).astype(F32)
    big = 1e9
    gl = jnp.where(row < N_GROUPS, lt, NEG_INF)
    gmax = jnp.max(gl, axis=0, keepdims=True)
    gi = jnp.min(jnp.where(gl == gmax, row, big), axis=0, keepdims=True)
    g_sel = 1.0 / jnp.sum(jnp.exp(gl - gmax), axis=0, keepdims=True)
    lo = N_GROUPS + EXPERTS_PER_GROUP * gi
    el = jnp.where((row >= lo) & (row < lo + EXPERTS_PER_GROUP), lt, NEG_INF)
    v1 = jnp.max(el, axis=0, keepdims=True)
    i1 = jnp.min(jnp.where(el == v1, row, big), axis=0, keepdims=True)
    el2 = jnp.where(row == i1, NEG_INF, el)
    v2 = jnp.max(el2, axis=0, keepdims=True)
    i2 = jnp.min(jnp.where(el2 == v2, row, big), axis=0, keepdims=True)
    e2 = jnp.exp(v2 - v1)
    w1 = g_sel / (1.0 + e2)
    w2 = g_sel * e2 / (1.0 + e2)
    e0 = i1 - N_GROUPS
    e1 = i2 - N_GROUPS
    erow = lax.broadcasted_iota(jnp.int32, (N_EXPERTS, tm), 0).astype(F32)
    if not sparse:
        gt = jnp.where(erow == e0, w1, 0.0) + jnp.where(erow == e1, w2, 0.0)
        gates_ref[...] = jnp.concatenate([gt, jnp.zeros((LANES - N_EXPERTS, tm), F32)], axis=0).T
        return
    @pl.when(pl.program_id(0) == 0)
    def _():
        cnt_ref[...] = jnp.zeros(cnt_ref.shape, F32)

    cnt = cnt_ref[...]
    onehot = jnp.where(erow == e0, 1.0, 0.0) + jnp.where(erow == e1, 1.0, 0.0)
    before = _dot(onehot.astype(BF16), tri_ref[...]) + cnt[:, 0:1]
    r0 = jnp.sum(jnp.where(erow == e0, before, 0.0), axis=0, keepdims=True)
    r1 = jnp.sum(jnp.where(erow == e1, before, 0.0), axis=0, keepdims=True)
    cnt_ref[...] = cnt + jnp.sum(onehot, axis=1, keepdims=True)
    gates_ref[...] = jnp.concatenate([e0, e1, w1, w2, r0, r1, jnp.zeros((2, tm), F32)], axis=0)


def _out_proj(x2d, oa, ob, wa, wb, n2w, wr, br, *, tm, sparse):
    m, d = x2d.shape
    row = lambda i: (i, 0)
    const = lambda i: (0, 0)
    in_specs = [pl.BlockSpec((tm, d), row),
                pl.BlockSpec((tm, A_WIDTH), row),
                pl.BlockSpec((tm, B_WIDTH), row),
                pl.BlockSpec((A_WIDTH, d), const),
                pl.BlockSpec((B_WIDTH, d), const),
                pl.BlockSpec((1, d), const),
                pl.BlockSpec((d, LANES), const),
                pl.BlockSpec((1, LANES), const)]
    args = [x2d, oa, ob, wa, wb, n2w, wr, br]
    if sparse:
        idx = jnp.arange(tm)
        args.append((idx[:, None] < idx[None, :]).astype(BF16))
        in_specs.append(pl.BlockSpec((tm, tm), const))
        out_specs = (pl.BlockSpec((tm, d), row), pl.BlockSpec((tm, d // 2), row),
                     pl.BlockSpec((SUBLANES, tm), lambda i: (0, i)), pl.BlockSpec((N_EXPERTS, LANES), const))
        out_shape = (jax.ShapeDtypeStruct((m, d), F32), jax.ShapeDtypeStruct((m, d // 2), jnp.int32),
                     jax.ShapeDtypeStruct((SUBLANES, m), F32), jax.ShapeDtypeStruct((N_EXPERTS, LANES), F32))
    else:
        out_specs = (pl.BlockSpec((tm, d), row), pl.BlockSpec((tm, d), row), pl.BlockSpec((tm, LANES), row))
        out_shape = (jax.ShapeDtypeStruct((m, d), F32), jax.ShapeDtypeStruct((m, d), BF16),
                     jax.ShapeDtypeStruct((m, LANES), F32))
    return pl.pallas_call(
        functools.partial(_out_proj_kernel, sparse=sparse),
        grid=(m // tm,), in_specs=in_specs, out_specs=out_specs, out_shape=out_shape,
        compiler_params=_params(1), name="out_proj",
    )(*args)


def _moe_kernel(h_ref, n2_ref, gates_ref, wgu_ref, wd_ref, y_ref):
    e = pl.program_id(1)

    @pl.when(e == 0)
    def _():
        y_ref[...] = h_ref[...]

    n2 = n2_ref[...]
    gu = _dot(n2, wgu_ref[0].astype(BF16))
    act = _silu(gu[:, :D_EXPERT]) * gu[:, D_EXPERT:]
    lane = lax.broadcasted_iota(jnp.int32, gates_ref.shape, 1)
    gate = jnp.sum(jnp.where(lane == e, gates_ref[...], 0.0), axis=-1, keepdims=True)
    y_ref[...] += gate * _dot(act.astype(BF16), wd_ref[0].astype(BF16))


def _moe(h, n2, gates, wgu, wd, *, tm):
    m, d = h.shape
    row = lambda i, e: (i, 0)
    return pl.pallas_call(
        _moe_kernel,
        grid=(m // tm, N_EXPERTS),
        in_specs=[pl.BlockSpec((tm, d), row),
                  pl.BlockSpec((tm, d), row),
                  pl.BlockSpec((tm, LANES), row),
                  pl.BlockSpec((1, d, 2 * D_EXPERT), lambda i, e: (e, 0, 0)),
                  pl.BlockSpec((1, D_EXPERT, d), lambda i, e: (e, 0, 0))],
        out_specs=pl.BlockSpec((tm, d), row),
        out_shape=jax.ShapeDtypeStruct((m, d), F32),
        compiler_params=_params(2), name="moe",
    )(h, n2, gates, wgu, wd)


SC_CORES = 2
SC_SUBCORES = 16
SC_WORKERS = SC_CORES * SC_SUBCORES
SC_CHUNK = 64
MOE_TILE = 512


def _sc_mesh():
    return plsc.VectorSubcoreMesh(core_axis_name="c", subcore_axis_name="s",
                                  num_cores=SC_CORES, num_subcores=SC_SUBCORES)


def _sc_dispatch(src, pos, n_rows):
    m, d = src.shape
    n_chunks = m // (SC_WORKERS * SC_CHUNK)
    pos4 = pos.reshape(2, SC_WORKERS, n_chunks, SC_CHUNK).transpose(1, 0, 2, 3)

    def body(src_hbm, pos_hbm, out_hbm, idx_v, rows_v):
        wid = lax.axis_index("s") * SC_CORES + lax.axis_index("c")
        pltpu.sync_copy(pos_hbm.at[wid], idx_v)

        @pl.loop(0, n_chunks)
        def _(j):
            start = pl.multiple_of((wid * n_chunks + j) * SC_CHUNK, SC_CHUNK)
            pltpu.sync_copy(src_hbm.at[pl.ds(start, SC_CHUNK)], rows_v)
            pltpu.sync_copy(rows_v, out_hbm.at[idx_v.at[0, j]])
            pltpu.sync_copy(rows_v, out_hbm.at[idx_v.at[1, j]])

    return pl.kernel(
        body, out_type=jax.ShapeDtypeStruct((n_rows, d), src.dtype), mesh=_sc_mesh(),
        scratch_types=[pltpu.VMEM((2, n_chunks, SC_CHUNK), jnp.int32), pltpu.VMEM((SC_CHUNK, d), src.dtype)],
        name="moe_dispatch",
    )(src, pos4)


def _sc_gather(table, idx):
    b = idx.shape[0]
    d = table.shape[1]
    n_chunks = b // (SC_WORKERS * SC_CHUNK)
    idx3 = idx.reshape(SC_WORKERS, n_chunks, SC_CHUNK)

    def body(table_hbm, idx_hbm, out_hbm, idx_v, rows_v):
        wid = lax.axis_index("s") * SC_CORES + lax.axis_index("c")
        pltpu.sync_copy(idx_hbm.at[wid], idx_v)

        @pl.loop(0, n_chunks)
        def _(j):
            start = pl.multiple_of((wid * n_chunks + j) * SC_CHUNK, SC_CHUNK)
            pltpu.sync_copy(table_hbm.at[idx_v.at[j]], rows_v)
            pltpu.sync_copy(rows_v, out_hbm.at[pl.ds(start, SC_CHUNK)])

    return pl.kernel(
        body, out_type=jax.ShapeDtypeStruct((b, d), table.dtype), mesh=_sc_mesh(),
        scratch_types=[pltpu.VMEM((n_chunks, SC_CHUNK), jnp.int32), pltpu.VMEM((SC_CHUNK, d), table.dtype)],
        name="moe_collect",
    )(table, idx3)


def _moe_grouped_kernel(te_ref, nu_ref, xs_ref, wgu_ref, wd_ref, ys_ref, wgu16, wd16):
    i = pl.program_id(0)

    @pl.when((i == 0) | (te_ref[i] != te_ref[jnp.maximum(i - 1, 0)]))
    def _():
        wgu16[...] = wgu_ref[0].astype(BF16)
        wd16[...] = wd_ref[0].astype(BF16)

    @pl.when(i < nu_ref[0])
    def _():
        x = _unpack_bf16_pairs(xs_ref[...]).astype(BF16)
        gu = _dot(x, wgu16[...])
        act = _silu(gu[:, :D_EXPERT]) * gu[:, D_EXPERT:]
        ys_ref[...] = _pack_bf16_pairs(_dot(act.astype(BF16), wd16[...]))

    @pl.when(i >= nu_ref[0])
    def _():
        ys_ref[...] = jnp.zeros(ys_ref.shape, ys_ref.dtype)


def _moe_grouped(xs, tile_expert, n_used, wgu, wd):
    r, half = xs.shape
    d = 2 * half
    grid_spec = pltpu.PrefetchScalarGridSpec(
        num_scalar_prefetch=2, grid=(r // MOE_TILE,),
        in_specs=[pl.BlockSpec((MOE_TILE, half), lambda i, te, nu: (i, 0)),
                  pl.BlockSpec((1, d, 2 * D_EXPERT), lambda i, te, nu: (te[i], 0, 0)),
                  pl.BlockSpec((1, D_EXPERT, d), lambda i, te, nu: (te[i], 0, 0))],
        out_specs=pl.BlockSpec((MOE_TILE, half), lambda i, te, nu: (i, 0)),
        scratch_shapes=[pltpu.VMEM((d, 2 * D_EXPERT), BF16), pltpu.VMEM((D_EXPERT, d), BF16)])
    return pl.pallas_call(
        _moe_grouped_kernel, grid_spec=grid_spec,
        out_shape=jax.ShapeDtypeStruct((r, half), jnp.int32),
        compiler_params=_params(1), name="moe_grouped",
    )(tile_expert, n_used, xs, wgu, wd)


def _moe_combine_kernel(h_ref, z0_ref, z1_ref, rec_ref, y_ref):
    tm = h_ref.shape[0]
    rec = jnp.concatenate([rec_ref[...], jnp.zeros((LANES - SUBLANES, tm), F32)], axis=0).T
    y_ref[...] = (h_ref[...] + rec[:, 2:3] * _unpack_bf16_pairs(z0_ref[...])
                  + rec[:, 3:4] * _unpack_bf16_pairs(z1_ref[...]))


def _moe_combine(h, z, rec, *, tm):
    m, d = h.shape
    nt = m // tm
    row = lambda i: (i, 0)
    return pl.pallas_call(
        _moe_combine_kernel, grid=(nt,),
        in_specs=[pl.BlockSpec((tm, d), row),
                  pl.BlockSpec((tm, d // 2), row),
                  pl.BlockSpec((tm, d // 2), lambda i: (i + nt, 0)),
                  pl.BlockSpec((SUBLANES, tm), lambda i: (0, i))],
        out_specs=pl.BlockSpec((tm, d), row),
        out_shape=jax.ShapeDtypeStruct((m, d), F32),
        compiler_params=_params(1), name="moe_combine",
    )(h, z, z, rec)


def _moe_route(rec, cnt):
    m = rec.shape[1]
    n_tiles = 2 * m // MOE_TILE + N_EXPERTS
    counts = cnt[:, 0].astype(jnp.int32)
    tiles = (counts + MOE_TILE - 1) // MOE_TILE
    tile_end = jnp.cumsum(tiles)
    row_start = (tile_end - tiles) * MOE_TILE
    eid = rec[0:2].astype(jnp.int32)
    rank = rec[4:6].astype(jnp.int32)
    pos = rank
    for e in range(N_EXPERTS):
        pos = pos + jnp.where(eid == e, row_start[e], 0)
    n_used = tile_end[-1:]
    tile_id = jnp.minimum(jnp.arange(n_tiles, dtype=jnp.int32), n_used[0] - 1)
    tile_expert = jnp.sum((tile_end[None, :] <= tile_id[:, None]).astype(jnp.int32), axis=1)
    return pos, tile_expert, n_used.astype(jnp.int32), n_tiles * MOE_TILE


def _head_rows(v):
    rows = jnp.concatenate([jnp.zeros((B_HEADS,), F32), v.astype(F32)])
    return jnp.broadcast_to(rows[:, None], (2 * B_HEADS, LANES))


def kernel(x_prompt, x_sample, cache_win_k, cache_win_v, state_conv, state_ssm, norm1_w, w_in, qnorm_w, knorm_w, conv_w, a_log, dt_bias, onorm_w, w_out, norm2_w, w_group, b_group, w_expert_router, b_expert_router, w_gate_up, w_down):
    nb, seq, d = x_prompt.shape
    db, t_len, _ = x_sample.shape
    n_buf = cache_win_k.shape[1]
    assert n_buf == MAX_WINDOW and seq % (MAX_DILATION * BAND) == 0 and t_len == SUBLANES
    keep = min(MAX_WINDOW, seq)

    c_bg = 3 * A_WIDTH + CONV_DIM + B_WIDTH
    w_cat = jnp.concatenate([w_in, jnp.zeros((d, LANES - 2 * B_HEADS), w_in.dtype)], axis=1).astype(BF16)
    assert w_cat.shape[1] == c_bg + LANES
    n1w = norm1_w.reshape(1, d).astype(F32)
    qw = jnp.tile(qnorm_w.astype(F32), A_HEADS).reshape(1, A_WIDTH)
    kw = jnp.tile(knorm_w.astype(F32), A_HEADS).reshape(1, A_WIDTH)
    a_log_v = _head_rows(a_log)
    dtb_v = _head_rows(dt_bias)
    onw = onorm_w.reshape(1, B_VAL_DIM).astype(F32)
    wa = w_out[:A_WIDTH].astype(BF16)
    wb = w_out[A_WIDTH:].astype(BF16)
    n2w = norm2_w.reshape(1, d).astype(F32)
    wr = jnp.concatenate([w_group, jnp.transpose(w_expert_router, (1, 0, 2)).reshape(d, N_EXPERTS),
                          jnp.zeros((d, LANES - N_GROUPS - N_EXPERTS), F32)], axis=1).astype(F32)
    br = jnp.zeros((1, LANES), F32).at[0, :N_GROUPS].set(b_group).at[0, N_GROUPS:N_GROUPS + N_EXPERTS].set(
        b_expert_router.reshape(-1))
    wgu = w_gate_up.astype(F32)
    wd = w_down.astype(F32)
    cw = conv_w.astype(F32)

    xp2d = x_prompt.reshape(nb * seq, d)
    qf, kf, vf, kwin, vwin, bqkv, z, bg = _in_proj(xp2d, n1w, w_cat, qw, kw, seq=seq, keep=keep,
                                                   fold=True, tm=512)
    oa = _attn_prompt(qf, kf, vf).reshape(nb * seq, A_WIDTH)
    conv0 = jnp.zeros((nb, CONV_WIDTH - 1, CONV_DIM), F32)
    ssm0 = jnp.zeros((nb, B_HEADS, B_KEY_DIM, B_VAL_DIM), F32)
    ob, ssm_prompt = _gdn_prompt(bqkv, z, bg, cw, conv0, ssm0, a_log_v, dtb_v, onw,
                                 seq=seq, ts=512, block=128, chunk=CHUNK)
    assert (nb * seq) % (SC_WORKERS * SC_CHUNK) == 0 and nb * seq >= N_EXPERTS * MOE_TILE
    h_p, n2p, rec, cnt = _out_proj(xp2d, oa, ob, wa, wb, n2w, wr, br, tm=512, sparse=True)
    pos, tile_expert, n_used, n_rows = _moe_route(rec, cnt)
    xs = _sc_dispatch(n2p, pos, n_rows)
    win_k_prompt = jnp.transpose(kwin, (0, 3, 1, 2))
    win_v_prompt = jnp.transpose(vwin, (0, 3, 1, 2))
    conv_prompt = bqkv.reshape(nb, seq, CONV_DIM)[:, seq - (CONV_WIDTH - 1):]

    ms = db * t_len
    xs2d = x_sample.reshape(ms, d)
    tms = min(512, ms)
    qs, ksn, vsn, _, _, bqkv_s, z_s, bg_s = _in_proj(xs2d, n1w, w_cat, qw, kw, seq=ms, keep=ms,
                                                     fold=False, tm=tms)
    cache_kt = jnp.transpose(cache_win_k.astype(F32), (0, 2, 3, 1))
    cache_vt = jnp.transpose(cache_win_v.astype(F32), (0, 2, 3, 1))
    oa_s = _attn_sample(qs, ksn, vsn, cache_kt, cache_vt, t_len=t_len)
    xs, oa_s = lax.optimization_barrier((xs, oa_s))
    ys = _moe_grouped(xs, tile_expert, n_used, wgu, wd)
    zs = _sc_gather(ys, pos.reshape(-1))
    xpad = jnp.concatenate([state_conv.astype(F32), bqkv_s.reshape(db, t_len, CONV_DIM)], axis=1)
    nseq = 16 if db % 16 == 0 else 1
    ob_s, ssm_sample = _gdn_sample(xpad, z_s, bg_s, cw, state_ssm.astype(F32), a_log_v, dtb_v, onw,
                                   t_len=t_len, nseq=nseq)
    h_s, n2_s, gates_s = _out_proj(xs2d, oa_s, ob_s, wa, wb, n2w, wr, br, tm=tms, sparse=False)
    y_sample = _moe(h_s, n2_s, gates_s, wgu, wd, tm=ms)
    zs, y_sample = lax.optimization_barrier((zs, y_sample))
    y_prompt = _moe_combine(h_p, zs, rec, tm=MOE_TILE).reshape(nb, seq, d)
    y_sample = y_sample.reshape(db, t_len, d)
    win_k_sample = ksn.reshape(db, t_len, A_HEADS, A_HEAD_DIM)
    win_v_sample = vsn.reshape(db, t_len, A_HEADS, A_HEAD_DIM)
    conv_sample = xpad[:, t_len:]

    return (y_prompt, y_sample, win_k_prompt, win_v_prompt, conv_prompt, ssm_prompt,
            win_k_sample, win_v_sample, conv_sample, ssm_sample)
```

```python
import functools

import jax
import jax.numpy as jnp
from jax import lax
from jax.experimental import pallas as pl
from jax.experimental.pallas import tpu as pltpu
from jax.experimental.pallas import tpu_sc as plsc

F32 = jnp.float32
BF16 = jnp.bfloat16

A_HEADS = 8
A_HEAD_DIM = 64
A_WIDTH = A_HEADS * A_HEAD_DIM
DILATED_PATTERNS = ((128, 1), (512, 4), (2048, 16))
MAX_WINDOW = 2048
MAX_DILATION = 16
BAND = 128
ATTN_SCALE = A_HEAD_DIM ** -0.5
LOG2_E = 1.4426950408889634
B_HEADS = 4
B_KEY_DIM = 128
B_VAL_DIM = 128
B_WIDTH = B_HEADS * B_VAL_DIM
CONV_WIDTH = 4
CONV_DIM = B_HEADS * (2 * B_KEY_DIM + B_VAL_DIM)
CHUNK = 64
N_GROUPS = 4
EXPERTS_PER_GROUP = 4
N_EXPERTS = N_GROUPS * EXPERTS_PER_GROUP
D_EXPERT = 512
RMS_EPS = 1e-6
NEG_INF = -1e30

LANES = 128
SUBLANES = 8
VMEM_LIMIT = 56 * 1024 * 1024


def _params(n_axes, vmem=VMEM_LIMIT):
    return pltpu.CompilerParams(dimension_semantics=("arbitrary",) * n_axes, vmem_limit_bytes=vmem)


def _split2(x):
    hi = x.astype(BF16)
    lo = (x - hi.astype(F32)).astype(BF16)
    return hi, lo


def _split3(x):
    hi = x.astype(BF16)
    r = x - hi.astype(F32)
    mid = r.astype(BF16)
    lo = (r - mid.astype(F32)).astype(BF16)
    return hi, mid, lo


def _dot(a, b):
    return jnp.dot(a, b, preferred_element_type=F32)


def _dot_nt(a, b):
    return lax.dot_general(a, b, (((1,), (1,)), ((), ())), preferred_element_type=F32)


def _sigmoid(x):
    return 1.0 / (1.0 + jnp.exp(-x))


def _silu(x):
    return x * _sigmoid(x)


def _in_proj_kernel(x_ref, n1w_ref, w_ref, qw_ref, kw_ref,
                    q_ref, k_ref, v_ref, kwin_ref, vwin_ref, bqkv_ref, z_ref, bg_ref,
                    *scratch, fold):
    tm = x_ref.shape[0]
    x = x_ref[...]
    ms = jnp.mean(x * x, axis=-1, keepdims=True)
    n1 = (x * lax.rsqrt(ms + RMS_EPS) * n1w_ref[...]).astype(BF16)

    first_head = lax.broadcasted_iota(jnp.int32, (tm, LANES), 1) < A_HEAD_DIM

    def head_norm(t, w):
        cols = []
        for c in range(A_WIDTH // LANES):
            tc = t[:, c * LANES:(c + 1) * LANES]
            sq = tc * tc
            s0 = jnp.sum(jnp.where(first_head, sq, 0.0), axis=-1, keepdims=True)
            s1 = jnp.sum(jnp.where(first_head, 0.0, sq), axis=-1, keepdims=True)
            ms = jnp.where(first_head, s0, s1) * (1.0 / A_HEAD_DIM)
            cols.append(tc * lax.rsqrt(ms + RMS_EPS))
        return jnp.concatenate(cols, axis=1) * w

    q_scale = ATTN_SCALE * LOG2_E if fold else ATTN_SCALE
    q = head_norm(_dot(n1, w_ref[:, 0:A_WIDTH]), qw_ref[...]) * q_scale
    k = head_norm(_dot(n1, w_ref[:, A_WIDTH:2 * A_WIDTH]), kw_ref[...])
    v = _dot(n1, w_ref[:, 2 * A_WIDTH:3 * A_WIDTH])
    if fold:
        kwin_ref[0] = k.T.reshape(A_HEADS, A_HEAD_DIM, tm)
        vwin_ref[0] = v.T.reshape(A_HEADS, A_HEAD_DIM, tm)

        (scr,) = scratch
        rows = tm // MAX_DILATION
        for val, out in ((q, q_ref), (k, k_ref), (v, v_ref)):
            for c in range(A_WIDTH // LANES):
                cs = slice(c * LANES, (c + 1) * LANES)
                scr[c] = val[:, cs]
                for r in range(MAX_DILATION):
                    out[0, r, :, cs] = scr[c, pl.ds(r, rows, stride=MAX_DILATION), :]
    else:
        kwin_ref[...] = k
        vwin_ref[...] = v
        q_ref[...] = q
        k_ref[...] = k
        v_ref[...] = v
    c0 = 3 * A_WIDTH
    bqkv_ref[...] = _dot(n1, w_ref[:, c0:c0 + CONV_DIM])
    c1 = c0 + CONV_DIM
    z_ref[...] = _dot(n1, w_ref[:, c1:c1 + B_WIDTH]).astype(BF16)
    c2 = c1 + B_WIDTH
    bg_ref[...] = _dot(n1, w_ref[:, c2:c2 + LANES])


def _in_proj(x2d, n1w, w_cat, qw, kw, *, seq, keep, fold, tm):
    m, d = x2d.shape
    nb = m // seq
    tiles_per_seq = seq // tm
    skip = (seq - keep) // tm
    keep_tiles = keep // tm
    n_cols = w_cat.shape[1]

    def win_map(i):
        return (i // tiles_per_seq) * keep_tiles + jnp.maximum(i % tiles_per_seq - skip, 0), 0

    row = lambda i: (i, 0)
    const = lambda i: (0, 0)
    if fold:
        rows = tm // MAX_DILATION
        qkv_shape = jax.ShapeDtypeStruct((nb, MAX_DILATION, seq // MAX_DILATION, A_WIDTH), F32)
        qkv_spec = pl.BlockSpec((1, MAX_DILATION, rows, A_WIDTH),
                                lambda i: (i // tiles_per_seq, 0, i % tiles_per_seq, 0))
        scratch = [pltpu.VMEM((A_WIDTH // LANES, tm, LANES), F32)]
        win_shape = jax.ShapeDtypeStruct((nb, A_HEADS, A_HEAD_DIM, keep), F32)
        win_spec = pl.BlockSpec((1, A_HEADS, A_HEAD_DIM, tm),
                                lambda i: (i // tiles_per_seq, 0, 0, jnp.maximum(i % tiles_per_seq - skip, 0)))
    else:
        qkv_shape = jax.ShapeDtypeStruct((m, A_WIDTH), F32)
        qkv_spec = pl.BlockSpec((tm, A_WIDTH), row)
        scratch = []
        win_shape = jax.ShapeDtypeStruct((nb * keep, A_WIDTH), F32)
        win_spec = pl.BlockSpec((tm, A_WIDTH), win_map)
    out_shape = (qkv_shape, qkv_shape, qkv_shape, win_shape, win_shape,
                 jax.ShapeDtypeStruct((m, CONV_DIM), F32),
                 jax.ShapeDtypeStruct((m, B_WIDTH), BF16),
                 jax.ShapeDtypeStruct((m, LANES), F32))
    out_specs = (qkv_spec, qkv_spec, qkv_spec, win_spec, win_spec,
                 pl.BlockSpec((tm, CONV_DIM), row),
                 pl.BlockSpec((tm, B_WIDTH), row),
                 pl.BlockSpec((tm, LANES), row))
    in_specs = [pl.BlockSpec((tm, d), row),
                pl.BlockSpec((1, d), const),
                pl.BlockSpec((d, n_cols), const),
                pl.BlockSpec((1, A_WIDTH), const),
                pl.BlockSpec((1, A_WIDTH), const)]
    return pl.pallas_call(
        functools.partial(_in_proj_kernel, fold=fold),
        grid=(m // tm,), in_specs=in_specs, out_specs=out_specs, out_shape=out_shape,
        scratch_shapes=scratch, compiler_params=_params(1), name="in_proj",
    )(x2d, n1w, w_cat, qw, kw)


def _band_mask(pieces):
    sub = BAND // pieces
    r = lax.broadcasted_iota(jnp.int32, (BAND, 2 * BAND), 0)
    c = lax.broadcasted_iota(jnp.int32, (BAND, 2 * BAND), 1)
    qpos = (r % sub) * pieces + r // sub + BAND
    cc = c % BAND
    kpos = (cc % sub) * pieces + cc // sub + (c // BAND) * BAND
    dist = qpos - kpos
    return (dist >= 0) & (dist <= BAND), c >= BAND


ATTN_UNROLL = 16


def _attn_prompt_kernel(q_ref, k_ref, v_ref, o_ref, p_scr, op_scr, mx_scr, l_scr, nat_scr):
    u_len = q_ref.shape[2]
    lane = lax.broadcasted_iota(jnp.int32, (BAND, LANES), 1)
    head0 = lane < A_HEAD_DIM

    def per_head(col):
        return jnp.where(head0, jnp.broadcast_to(col[:BAND], (BAND, LANES)),
                         jnp.broadcast_to(col[BAND:], (BAND, LANES)))

    for pi, pieces in enumerate((16, 4, 1)):
        sub = BAND // pieces
        n_res = MAX_DILATION // pieces
        n_blk = u_len // sub
        band, _ = _band_mask(pieces)
        bias = jnp.where(band, 0.0, NEG_INF)
        bias2 = jnp.concatenate([bias, bias], axis=0)
        bias2_cur = bias2[:, BAND:]

        def locate(blk, n_res=n_res, sub=sub):
            m = blk // n_res
            cur = pl.multiple_of(m * sub, SUBLANES)
            prv = pl.multiple_of(jnp.maximum(m - 1, 0) * sub, SUBLANES)
            return blk % n_res, cur, prv

        def gather(ref, res, start, pieces=pieces, n_res=n_res, sub=sub):
            parts = [ref[0, a * n_res + res, pl.ds(start, sub), :] for a in range(pieces)]
            return parts[0] if pieces == 1 else jnp.concatenate(parts, axis=0)

        def scatter(ref, res, start, val, pi=pi, pieces=pieces, n_res=n_res, sub=sub):
            for a in range(pieces):
                ref[pi, a * n_res + res, pl.ds(start, sub), :] = val[a * sub:(a + 1) * sub]

        def probs(it, carry, first, bias2=bias2, bias2_cur=bias2_cur):
            for j, is_first in enumerate(first):
                blk = it * ATTN_UNROLL + j
                res, cur, prv = locate(blk)
                qb = gather(q_ref, res, cur)
                q2 = jnp.concatenate([jnp.where(head0, qb, 0.0), jnp.where(head0, 0.0, qb)], axis=0).astype(BF16)
                kc = gather(k_ref, res, cur)
                if is_first:
                    s = _dot_nt(q2, kc.astype(BF16)) + bias2_cur
                else:
                    kb = jnp.concatenate([gather(k_ref, res, prv), kc], axis=0).astype(BF16)
                    s = _dot_nt(q2, kb) + bias2
                mx = jnp.max(s, axis=-1, keepdims=True)
                p = jnp.exp2(s - mx)
                if is_first:
                    p_scr[blk, :, BAND:] = p.astype(BF16)
                else:
                    p_scr[blk] = p.astype(BF16)
                scatter(mx_scr, res, cur, per_head(mx))
            return carry

        def values(it, carry, first):
            for j, is_first in enumerate(first):
                blk = it * ATTN_UNROLL + j
                res, cur, prv = locate(blk)
                vc = gather(v_ref, res, cur)
                if is_first:
                    p = p_scr[blk, :, BAND:]
                    vb = vc.astype(BF16)
                else:
                    p = p_scr[blk]
                    vb = jnp.concatenate([gather(v_ref, res, prv), vc], axis=0).astype(BF16)
                o2 = _dot(p, jnp.concatenate([vb, jnp.ones(vb.shape, BF16)], axis=1))
                scatter(op_scr, res, cur, jnp.where(head0, o2[:BAND, :LANES], o2[BAND:, :LANES]))
                scatter(l_scr, res, cur, jnp.where(head0, o2[:BAND, LANES:], o2[BAND:, LANES:]))
            return carry

        n_it = n_res * n_blk // ATTN_UNROLL
        flags = [tuple(g * ATTN_UNROLL + j < n_res for j in range(ATTN_UNROLL)) for g in range(n_it)]
        segments = []
        for g, f in enumerate(flags):
            if segments and segments[-1][2] == f:
                segments[-1][1] = g + 1
            else:
                segments.append([g, g + 1, f])
        for phase in (probs, values):
            for lo, hi, f in segments:
                lax.fori_loop(lo, hi, functools.partial(phase, first=f), 0)

    for r in range(MAX_DILATION):
        m0, m1, m2 = mx_scr[0, r], mx_scr[1, r], mx_scr[2, r]
        big = jnp.maximum(jnp.maximum(m0, m1), m2)
        e0, e1, e2 = jnp.exp2(m0 - big), jnp.exp2(m1 - big), jnp.exp2(m2 - big)
        num = e0 * op_scr[0, r] + e1 * op_scr[1, r] + e2 * op_scr[2, r]
        den = e0 * l_scr[0, r] + e1 * l_scr[1, r] + e2 * l_scr[2, r]
        nat_scr[pl.ds(r, u_len, stride=MAX_DILATION), :] = num / den
    o_ref[0] = nat_scr[...].astype(BF16)


def _attn_prompt(qf, kf, vf):
    nb, _, u_len, _ = qf.shape
    seq = u_len * MAX_DILATION
    n_pairs = A_WIDTH // LANES
    spec = pl.BlockSpec((1, MAX_DILATION, u_len, LANES), lambda b, p: (b, 0, 0, p))
    return pl.pallas_call(
        _attn_prompt_kernel,
        grid=(nb, n_pairs),
        in_specs=[spec, spec, spec],
        out_specs=pl.BlockSpec((1, seq, LANES), lambda b, p: (b, 0, p)),
        out_shape=jax.ShapeDtypeStruct((nb, seq, A_WIDTH), BF16),
        scratch_shapes=[pltpu.VMEM((seq // BAND, 2 * BAND, 2 * BAND), BF16),
                        pltpu.VMEM((3, MAX_DILATION, u_len, LANES), F32),
                        pltpu.VMEM((3, MAX_DILATION, u_len, LANES), F32),
                        pltpu.VMEM((3, MAX_DILATION, u_len, LANES), F32),
                        pltpu.VMEM((seq, LANES), F32)],
        compiler_params=_params(2), name="attn_prompt",
    )(qf, kf, vf)


def _multiplicity(t, dd):
    cnt = jnp.zeros(dd.shape, F32)
    for w, d in DILATED_PATTERNS:
        ok = (dd >= 0) & (dd <= w) & (dd % d == 0)
        cnt = cnt + jnp.where(ok, 1.0, 0.0)
    return cnt


def _attn_sample_kernel(q_ref, kn_ref, vn_ref, kt_ref, vt_ref, o_ref):
    t_len = q_ref.shape[0]
    n_buf = kt_ref.shape[3]
    q = q_ref[...]
    kn = kn_ref[...]
    vn = vn_ref[...]
    tq = n_buf + lax.broadcasted_iota(jnp.int32, (t_len, n_buf), 0)
    cnt = _multiplicity(None, tq - lax.broadcasted_iota(jnp.int32, (t_len, n_buf), 1))
    cnt_new = _multiplicity(None, lax.broadcasted_iota(jnp.int32, (t_len, t_len), 0)
                            - lax.broadcasted_iota(jnp.int32, (t_len, t_len), 1))
    outs = []
    for h in range(A_HEADS):
        hs = slice(h * A_HEAD_DIM, (h + 1) * A_HEAD_DIM)
        qh = q[:, hs].astype(BF16)
        s = jnp.where(cnt > 0, _dot(qh, kt_ref[0, h].astype(BF16)), NEG_INF)
        sn = jnp.where(cnt_new > 0, _dot_nt(qh, kn[:, hs].astype(BF16)), NEG_INF)
        m = jnp.maximum(jnp.max(s, axis=-1, keepdims=True), jnp.max(sn, axis=-1, keepdims=True))
        p = cnt * jnp.exp(s - m)
        pn = cnt_new * jnp.exp(sn - m)
        l = jnp.sum(p, axis=-1, keepdims=True) + jnp.sum(pn, axis=-1, keepdims=True)
        o = _dot_nt(p.astype(BF16), vt_ref[0, h].astype(BF16)) + _dot(pn.astype(BF16), vn[:, hs].astype(BF16))
        outs.append(o / l)
    o_ref[...] = jnp.concatenate(outs, axis=-1)


def _attn_sample(q2d, kn2d, vn2d, cache_kt, cache_vt, *, t_len):
    nb, _, _, n_buf = cache_kt.shape
    tok = pl.BlockSpec((t_len, A_WIDTH), lambda b: (b, 0))
    win = pl.BlockSpec((1, A_HEADS, A_HEAD_DIM, n_buf), lambda b: (b, 0, 0, 0))
    return pl.pallas_call(
        _attn_sample_kernel,
        grid=(nb,),
        in_specs=[tok, tok, tok, win, win],
        out_specs=tok,
        out_shape=jax.ShapeDtypeStruct((nb * t_len, A_WIDTH), F32),
        compiler_params=_params(1), name="attn_sample",
    )(q2d, kn2d, vn2d, cache_kt, cache_vt)


def _gdn_prep(c, bg, a_col, dt_col, chunk):
    n = c.shape[0]
    bgt = bg.T[0:2 * B_HEADS]
    row = lax.broadcasted_iota(jnp.int32, bgt.shape, 0)
    xg = bgt + dt_col
    sp = jnp.maximum(xg, 0.0) + jnp.log1p(jnp.exp(-jnp.abs(xg)))
    gt = jnp.where(row >= B_HEADS, -a_col * sp, 0.0)
    ri = lax.broadcasted_iota(jnp.int32, (n, n), 0)
    ci = lax.broadcasted_iota(jnp.int32, (n, n), 1)
    tri = jnp.where(((ri // chunk) == (ci // chunk)) & (ri >= ci), 1.0, 0.0).astype(BF16)
    th, tm_, tl = _split3(gt)
    drow = _dot_nt(th, tri) + _dot_nt(tm_, tri) + _dot_nt(tl, tri)
    both = jnp.where(row >= B_HEADS, drow, _sigmoid(bgt))
    cols = jnp.concatenate([both, jnp.zeros((LANES - 2 * B_HEADS, n), F32)], axis=0).T
    beta_all = cols
    dcol = cols
    heads = []
    kd_ = B_KEY_DIM
    for h in range(B_HEADS):
        qh = c[:, h * kd_:(h + 1) * kd_]
        kh = c[:, B_HEADS * kd_ + h * kd_:B_HEADS * kd_ + (h + 1) * kd_]
        vh = c[:, 2 * B_HEADS * kd_ + h * B_VAL_DIM:2 * B_HEADS * kd_ + (h + 1) * B_VAL_DIM]
        qh = qh * lax.rsqrt(jnp.sum(qh * qh, axis=-1, keepdims=True) + RMS_EPS) * (B_KEY_DIM ** -0.5)
        kh = kh * lax.rsqrt(jnp.sum(kh * kh, axis=-1, keepdims=True) + RMS_EPS)
        heads.append((qh, kh, vh, beta_all[:, h:h + 1], dcol[:, B_HEADS + h:B_HEADS + h + 1],
                      drow[B_HEADS + h:B_HEADS + h + 1, :]))
    return heads


def _gdn_out(o, onw, zh):
    ms = jnp.mean(o * o, axis=-1, keepdims=True)
    return o * lax.rsqrt(ms + RMS_EPS) * onw * _silu(zh)


def _gdn_prompt_kernel(x_ref, z_ref, bg_ref, cw_ref, cst_ref, sst_ref, avec_ref, dtb_ref, onw_ref,
                       o_ref, snew_ref, xp_scr, s_scr, *, block, chunk):
    ts = x_ref.shape[0]
    t = pl.program_id(1)
    pad = SUBLANES
    hist = CONV_WIDTH - 1

    @pl.when(t == 0)
    def _():
        xp_scr[...] = jnp.zeros((pad, CONV_DIM), F32)
        xp_scr[pad - hist:pad, :] = cst_ref[0]
        s_scr[...] = sst_ref[0]

    assert CONV_WIDTH == 4
    x = x_ref[...]
    tail = xp_scr[...]
    sub = lax.broadcasted_iota(jnp.int32, (pad, CONV_DIM), 0)

    def shift(cur, prev_tail, s):
        rolled = pltpu.roll(cur, s, axis=0)
        head = jnp.where(sub < s, pltpu.roll(prev_tail, s, axis=0), rolled[0:pad])
        return jnp.concatenate([head, rolled[pad:]], axis=0)

    w0, w1, w2, w3 = (cw_ref[i:i + 1, :] for i in range(CONV_WIDTH))
    x1 = shift(x, tail, 1)
    a = x * w1 + x1 * w0
    a_tail = tail * w1 + pltpu.roll(tail, 1, axis=0) * w0
    y = x * w3 + x1 * w2 + shift(a, a_tail, 2)
    xp_scr[...] = x[ts - pad:ts]
    c = _silu(y)
    avec = jnp.exp(avec_ref[...])
    onw = onw_ref[...]
    n_chunks = block // chunk
    n_sb = ts // block
    ri = lax.broadcasted_iota(jnp.int32, (block, block), 0)
    ci = lax.broadcasted_iota(jnp.int32, (block, block), 1)
    same = (ri // chunk) == (ci // chunk)
    lower = same & (ri >= ci)
    strict = same & (ri > ci)
    eye = jnp.where(ri == ci, 1.0, 0.0).astype(F32)

    s = [s_scr[h] for h in range(B_HEADS)]
    for g0 in range(0, n_sb, GDN_GROUP):
        units = []
        for sb in range(g0, g0 + GDN_GROUP):
            rs = slice(sb * block, (sb + 1) * block)
            heads = _gdn_prep(c[rs], bg_ref[rs, :], avec, dtb_ref[...], chunk)
            for h, (qh, kh, vh, beta, dcol, drow) in enumerate(heads):
                units.append(dict(sb=sb, h=h, q=qh, k=kh, v=vh, beta=beta, dcol=dcol, drow=drow))
        s = _gdn_group(units, s, z_ref, o_ref, onw, lower, strict, eye, block, chunk)
    for h in range(B_HEADS):
        s_scr[h] = s[h]

    @pl.when(t == pl.num_programs(1) - 1)
    def _():
        snew_ref[0] = s_scr[...]


GDN_GROUP = 4


def _gdn_group(units, s, z_ref, o_ref, onw, lower, strict, eye, block, chunk, state_refs=None):
    n_chunks = block // chunk
    for u in units:
        gam = jnp.exp(jnp.where(lower, u["dcol"] - u["drow"], NEG_INF))
        kb = u["k"] * u["beta"]
        kq = _dot_nt(jnp.concatenate([kb, u["q"]], axis=0).astype(BF16), u["k"].astype(BF16))
        u["pw"] = jnp.where(strict, kq[:block] * gam, 0.0)
        u["t"] = eye - u["pw"]
        u["attn"] = jnp.where(lower, kq[block:] * gam, 0.0).astype(BF16)
        ed = jnp.exp(u["dcol"])
        u["rhs"] = jnp.concatenate([u["v"] * u["beta"], kb * ed], axis=1).astype(BF16)
        u["qd"] = u["q"] * ed
        u["kt"] = u["k"].T

    kk = 2
    while kk < chunk:
        for u in units:
            p16 = u["pw"].astype(BF16)
            u["pw"] = _dot(p16, p16)
        for u in units:
            u["t"] = u["t"] + _dot(u["t"].astype(BF16), u["pw"].astype(BF16))
        kk *= 2

    aligned = chunk % (2 * SUBLANES) == 0

    def chunk_rows(u, name, cs):
        return u[name + "16"][cs] if aligned else u[name][cs].astype(BF16)

    for u in units:
        u["uw"] = _dot(u["t"].astype(BF16), u["rhs"])
        u["uw16"] = u["uw"].astype(BF16)
    for u in units:
        au_aw = _dot(u["attn"], u["uw16"])
        u["op"] = au_aw[:, :B_VAL_DIM]
        u["qp"] = u["qd"] - au_aw[:, B_VAL_DIM:]
        u["qp16"] = u["qp"].astype(BF16)
    for u in units:
        u["n"], u["mw"], u["e"] = [], [], []
        for cc in range(n_chunks):
            cs = slice(cc * chunk, (cc + 1) * chunk)
            last = u["drow"][:, (cc + 1) * chunk - 1:(cc + 1) * chunk]
            kdt = (u["kt"][:, cs] * jnp.exp(last - u["drow"][:, cs])).astype(BF16)
            nm = _dot(kdt, chunk_rows(u, "uw", cs))
            u["n"].append(nm[:, :B_VAL_DIM])
            u["mw"].append(nm[:, B_VAL_DIM:].astype(BF16))
            u["e"].append(jnp.exp(last))

    s = None if s is None else list(s)
    for b0 in range(0, len(units), B_HEADS):
        sb = units[b0]["sb"]
        rs = slice(sb * block, (sb + 1) * block)
        outs = [[] for _ in range(B_HEADS)]
        for cc in range(n_chunks):
            cs = slice(cc * chunk, (cc + 1) * chunk)
            for h in range(B_HEADS):
                u = units[b0 + h]
                s_in = s[h] if state_refs is None else state_refs[0][cc, h]
                s16 = s_in.astype(BF16)
                if aligned:
                    r = _dot(jnp.concatenate([u["mw"][cc], chunk_rows(u, "qp", cs)], axis=0), s16)
                    r_m, r_q = r[:B_KEY_DIM], r[B_KEY_DIM:]
                else:
                    r_m, r_q = _dot(u["mw"][cc], s16), _dot(chunk_rows(u, "qp", cs), s16)
                outs[h].append(r_q + u["op"][cs])
                s_out = s_in * u["e"][cc] - r_m + u["n"][cc]
                if state_refs is None:
                    s[h] = s_out
                else:
                    state_refs[1][cc, h] = s_out
        for h in range(B_HEADS):
            o_all = jnp.concatenate(outs[h], axis=0) if n_chunks > 1 else outs[h][0]
            zh = z_ref[rs, h * B_VAL_DIM:(h + 1) * B_VAL_DIM].astype(F32)
            o_ref[rs, h * B_VAL_DIM:(h + 1) * B_VAL_DIM] = _gdn_out(o_all, onw, zh).astype(o_ref.dtype)
    return s


def _gdn_prompt(bqkv, z, bg, conv_w, conv_state, ssm_state, a_log_v, dtb_v, onw, *, seq, ts, block, chunk):
    m = bqkv.shape[0]
    nb = m // seq
    tps = seq // ts
    row = lambda b, t: (b * tps + t, 0)
    const = lambda b, t: (0, 0)
    return pl.pallas_call(
        functools.partial(_gdn_prompt_kernel, block=block, chunk=chunk),
        grid=(nb, tps),
        in_specs=[pl.BlockSpec((ts, CONV_DIM), row),
                  pl.BlockSpec((ts, B_WIDTH), row),
                  pl.BlockSpec((ts, LANES), row),
                  pl.BlockSpec((CONV_WIDTH, CONV_DIM), const),
                  pl.BlockSpec((1, CONV_WIDTH - 1, CONV_DIM), lambda b, t: (b, 0, 0)),
                  pl.BlockSpec((1, B_HEADS, B_KEY_DIM, B_VAL_DIM), lambda b, t: (b, 0, 0, 0)),
                  pl.BlockSpec((2 * B_HEADS, LANES), const),
                  pl.BlockSpec((2 * B_HEADS, LANES), const),
                  pl.BlockSpec((1, B_VAL_DIM), const)],
        out_specs=(pl.BlockSpec((ts, B_WIDTH), row),
                   pl.BlockSpec((1, B_HEADS, B_KEY_DIM, B_VAL_DIM), lambda b, t: (b, 0, 0, 0))),
        out_shape=(jax.ShapeDtypeStruct((m, B_WIDTH), BF16),
                   jax.ShapeDtypeStruct(ssm_state.shape, F32)),
        scratch_shapes=[pltpu.VMEM((SUBLANES, CONV_DIM), F32),
                        pltpu.VMEM((B_HEADS, B_KEY_DIM, B_VAL_DIM), F32)],
        compiler_params=_params(2), name="gdn_prompt",
    )(bqkv, z, bg, conv_w, conv_state, ssm_state, a_log_v, dtb_v, onw)


def _gdn_sample_kernel(xp_ref, z_ref, bg_ref, cw_ref, sst_ref, avec_ref, dtb_ref, onw_ref,
                       o_ref, snew_ref, *, t_len):
    nseq = xp_ref.shape[0]
    hist = CONV_WIDTH - 1
    ys = []
    for j in range(nseq):
        y = xp_ref[j, 0:t_len, :] * cw_ref[0:1, :]
        for i in range(1, CONV_WIDTH):
            y = y + xp_ref[j, i:i + t_len, :] * cw_ref[i:i + 1, :]
        ys.append(y)
    c = _silu(jnp.concatenate(ys, axis=0))
    avec = jnp.exp(avec_ref[...])
    onw = onw_ref[...]
    block = nseq * t_len
    ri = lax.broadcasted_iota(jnp.int32, (block, block), 0)
    ci = lax.broadcasted_iota(jnp.int32, (block, block), 1)
    same = (ri // t_len) == (ci // t_len)
    heads = _gdn_prep(c, bg_ref[...], avec, dtb_ref[...], t_len)
    units = [dict(sb=0, h=h, q=qh, k=kh, v=vh, beta=beta, dcol=dcol, drow=drow)
             for h, (qh, kh, vh, beta, dcol, drow) in enumerate(heads)]
    _gdn_group(units, None, z_ref, o_ref, onw, same & (ri >= ci), same & (ri > ci),
               jnp.where(ri == ci, 1.0, 0.0).astype(F32), block, t_len, state_refs=(sst_ref, snew_ref))


def _gdn_sample(xp, z, bg, conv_w, ssm_state, a_log_v, dtb_v, onw, *, t_len, nseq):
    nb = xp.shape[0]
    rows = nseq * t_len
    row = lambda i: (i, 0)
    const = lambda i: (0, 0)
    return pl.pallas_call(
        functools.partial(_gdn_sample_kernel, t_len=t_len),
        grid=(nb // nseq,),
        in_specs=[pl.BlockSpec((nseq, xp.shape[1], CONV_DIM), lambda i: (i, 0, 0)),
                  pl.BlockSpec((rows, B_WIDTH), row),
                  pl.BlockSpec((rows, LANES), row),
                  pl.BlockSpec((CONV_WIDTH, CONV_DIM), const),
                  pl.BlockSpec((nseq, B_HEADS, B_KEY_DIM, B_VAL_DIM), lambda i: (i, 0, 0, 0)),
                  pl.BlockSpec((2 * B_HEADS, LANES), const),
                  pl.BlockSpec((2 * B_HEADS, LANES), const),
                  pl.BlockSpec((1, B_VAL_DIM), const)],
        out_specs=(pl.BlockSpec((rows, B_WIDTH), row),
                   pl.BlockSpec((nseq, B_HEADS, B_KEY_DIM, B_VAL_DIM), lambda i: (i, 0, 0, 0))),
        out_shape=(jax.ShapeDtypeStruct((nb * t_len, B_WIDTH), BF16),
                   jax.ShapeDtypeStruct(ssm_state.shape, F32)),
        compiler_params=_params(1), name="gdn_sample",
    )(xp, z, bg, conv_w, ssm_state, a_log_v, dtb_v, onw)


def _pack_bf16_pairs(x):
    n = x.shape[1] // 2
    bits = pltpu.bitcast(x.astype(BF16).astype(F32), jnp.int32)
    return lax.shift_right_logical(bits[:, :n], 16) | (bits[:, n:] & jnp.int32(-65536))


def _unpack_bf16_pairs(w):
    lo = pltpu.bitcast(lax.shift_left(w, 16), F32)
    hi = pltpu.bitcast(w & jnp.int32(-65536), F32)
    return jnp.concatenate([lo, hi], axis=1)


def _out_proj_kernel(x_ref, oa_ref, ob_ref, wa_ref, wb_ref, n2w_ref, wr_ref, br_ref, *rest, sparse):
    if sparse:
        tri_ref, h_ref, n2_ref, gates_ref, cnt_ref = rest
    else:
        h_ref, n2_ref, gates_ref = rest
    h = x_ref[...] + _dot(oa_ref[...].astype(BF16), wa_ref[...]) + _dot(ob_ref[...].astype(BF16), wb_ref[...])
    h_ref[...] = h
    ms = jnp.mean(h * h, axis=-1, keepdims=True)
    n2 = h * lax.rsqrt(ms + RMS_EPS) * n2w_ref[...]
    if sparse:
        n2_ref[...] = _pack_bf16_pairs(n2)
    else:
        n2_ref[...] = n2.astype(BF16)
    nh, nl = _split2(n2)
    wh, wl = _split2(wr_ref[...])
    tm = n2.shape[0]
    parts = _dot(jnp.concatenate([nh, nl], axis=0), jnp.concatenate([wh, wl], axis=1))
    logits = parts[:tm, :LANES] + parts[:tm, LANES:] + parts[tm:, :LANES] + br_ref[...]
    n_rows = 3 * SUBLANES
    lt = logits.T[0:n_rows]
    row = lax.broadcasted_iota(jnp.int32, lt.shape, 0).astype(F32)
    big = 1e9
    gl = jnp.where(row < N_GROUPS, lt, NEG_INF)
    gmax = jnp.max(gl, axis=0, keepdims=True)
    gi = jnp.min(jnp.where(gl == gmax, row, big), axis=0, keepdims=True)
    g_sel = 1.0 / jnp.sum(jnp.exp(gl - gmax), axis=0, keepdims=True)
    lo = N_GROUPS + EXPERTS_PER_GROUP * gi
    el = jnp.where((row >= lo) & (row < lo + EXPERTS_PER_GROUP), lt, NEG_INF)
    v1 = jnp.max(el, axis=0, keepdims=True)
    i1 = jnp.min(jnp.where(el == v1, row, big), axis=0, keepdims=True)
    el2 = jnp.where(row == i1, NEG_INF, el)
    v2 = jnp.max(el2, axis=0, keepdims=True)
    i2 = jnp.min(jnp.where(el2 == v2, row, big), axis=0, keepdims=True)
    e2 = jnp.exp(v2 - v1)
    w1 = g_sel / (1.0 + e2)
    w2 = g_sel * e2 / (1.0 + e2)
    e0 = i1 - N_GROUPS
    e1 = i2 - N_GROUPS
    erow = lax.broadcasted_iota(jnp.int32, (N_EXPERTS, tm), 0).astype(F32)
    if not sparse:
        gt = jnp.where(erow == e0, w1, 0.0) + jnp.where(erow == e1, w2, 0.0)
        gates_ref[...] = jnp.concatenate([gt, jnp.zeros((LANES - N_EXPERTS, tm), F32)], axis=0).T
        return
    @pl.when(pl.program_id(0) == 0)
    def _():
        cnt_ref[...] = jnp.zeros(cnt_ref.shape, F32)

    cnt = cnt_ref[...]
    onehot = jnp.where(erow == e0, 1.0, 0.0) + jnp.where(erow == e1, 1.0, 0.0)
    before = _dot(onehot.astype(BF16), tri_ref[...]) + cnt[:, 0:1]
    r0 = jnp.sum(jnp.where(erow == e0, before, 0.0), axis=0, keepdims=True)
    r1 = jnp.sum(jnp.where(erow == e1, before, 0.0), axis=0, keepdims=True)
    cnt_ref[...] = cnt + jnp.sum(onehot, axis=1, keepdims=True)
    gates_ref[...] = jnp.concatenate([e0, e1, w1, w2, r0, r1, jnp.zeros((2, tm), F32)], axis=0)


def _out_proj(x2d, oa, ob, wa, wb, n2w, wr, br, *, tm, sparse):
    m, d = x2d.shape
    row = lambda i: (i, 0)
    const = lambda i: (0, 0)
    in_specs = [pl.BlockSpec((tm, d), row),
                pl.BlockSpec((tm, A_WIDTH), row),
                pl.BlockSpec((tm, B_WIDTH), row),
                pl.BlockSpec((A_WIDTH, d), const),
                pl.BlockSpec((B_WIDTH, d), const),
                pl.BlockSpec((1, d), const),
                pl.BlockSpec((d, LANES), const),
                pl.BlockSpec((1, LANES), const)]
    args = [x2d, oa, ob, wa, wb, n2w, wr, br]
    if sparse:
        idx = jnp.arange(tm)
        args.append((idx[:, None] < idx[None, :]).astype(BF16))
        in_specs.append(pl.BlockSpec((tm, tm), const))
        out_specs = (pl.BlockSpec((tm, d), row), pl.BlockSpec((tm, d // 2), row),
                     pl.BlockSpec((SUBLANES, tm), lambda i: (0, i)), pl.BlockSpec((N_EXPERTS, LANES), const))
        out_shape = (jax.ShapeDtypeStruct((m, d), F32), jax.ShapeDtypeStruct((m, d // 2), jnp.int32),
                     jax.ShapeDtypeStruct((SUBLANES, m), F32), jax.ShapeDtypeStruct((N_EXPERTS, LANES), F32))
    else:
        out_specs = (pl.BlockSpec((tm, d), row), pl.BlockSpec((tm, d), row), pl.BlockSpec((tm, LANES), row))
        out_shape = (jax.ShapeDtypeStruct((m, d), F32), jax.ShapeDtypeStruct((m, d), BF16),
                     jax.ShapeDtypeStruct((m, LANES), F32))
    return pl.pallas_call(
        functools.partial(_out_proj_kernel, sparse=sparse),
        grid=(m // tm,), in_specs=in_specs, out_specs=out_specs, out_shape=out_shape,
        compiler_params=_params(1), name="out_proj",
    )(*args)


def _moe_kernel(h_ref, n2_ref, gates_ref, wgu_ref, wd_ref, y_ref):
    e = pl.program_id(1)

    @pl.when(e == 0)
    def _():
        y_ref[...] = h_ref[...]

    n2 = n2_ref[...]
    gu = _dot(n2, wgu_ref[0].astype(BF16))
    act = _silu(gu[:, :D_EXPERT]) * gu[:, D_EXPERT:]
    lane = lax.broadcasted_iota(jnp.int32, gates_ref.shape, 1)
    gate = jnp.sum(jnp.where(lane == e, gates_ref[...], 0.0), axis=-1, keepdims=True)
    y_ref[...] += gate * _dot(act.astype(BF16), wd_ref[0].astype(BF16))


def _moe(h, n2, gates, wgu, wd, *, tm):
    m, d = h.shape
    row = lambda i, e: (i, 0)
    return pl.pallas_call(
        _moe_kernel,
        grid=(m // tm, N_EXPERTS),
        in_specs=[pl.BlockSpec((tm, d), row),
                  pl.BlockSpec((tm, d), row),
                  pl.BlockSpec((tm, LANES), row),
                  pl.BlockSpec((1, d, 2 * D_EXPERT), lambda i, e: (e, 0, 0)),
                  pl.BlockSpec((1, D_EXPERT, d), lambda i, e: (e, 0, 0))],
        out_specs=pl.BlockSpec((tm, d), row),
        out_shape=jax.ShapeDtypeStruct((m, d), F32),
        compiler_params=_params(2), name="moe",
    )(h, n2, gates, wgu, wd)


SC_CORES = 2
SC_SUBCORES = 16
SC_WORKERS = SC_CORES * SC_SUBCORES
SC_CHUNK = 64
MOE_TILE = 512


def _sc_mesh():
    return plsc.VectorSubcoreMesh(core_axis_name="c", subcore_axis_name="s",
                                  num_cores=SC_CORES, num_subcores=SC_SUBCORES)


def _sc_dispatch(src, pos, n_rows):
    m, d = src.shape
    n_chunks = m // (SC_WORKERS * SC_CHUNK)
    pos4 = pos.reshape(2, SC_WORKERS, n_chunks, SC_CHUNK).transpose(1, 0, 2, 3)

    def body(src_hbm, pos_hbm, out_hbm, idx_v, rows_v):
        wid = lax.axis_index("s") * SC_CORES + lax.axis_index("c")
        pltpu.sync_copy(pos_hbm.at[wid], idx_v)

        @pl.loop(0, n_chunks)
        def _(j):
            start = pl.multiple_of((wid * n_chunks + j) * SC_CHUNK, SC_CHUNK)
            pltpu.sync_copy(src_hbm.at[pl.ds(start, SC_CHUNK)], rows_v)
            pltpu.sync_copy(rows_v, out_hbm.at[idx_v.at[0, j]])
            pltpu.sync_copy(rows_v, out_hbm.at[idx_v.at[1, j]])

    return pl.kernel(
        body, out_type=jax.ShapeDtypeStruct((n_rows, d), src.dtype), mesh=_sc_mesh(),
        scratch_types=[pltpu.VMEM((2, n_chunks, SC_CHUNK), jnp.int32), pltpu.VMEM((SC_CHUNK, d), src.dtype)],
        name="moe_dispatch",
    )(src, pos4)


def _sc_gather(table, idx):
    b = idx.shape[0]
    d = table.shape[1]
    n_chunks = b // (SC_WORKERS * SC_CHUNK)
    idx3 = idx.reshape(SC_WORKERS, n_chunks, SC_CHUNK)

    def body(table_hbm, idx_hbm, out_hbm, idx_v, rows_v):
        wid = lax.axis_index("s") * SC_CORES + lax.axis_index("c")
        pltpu.sync_copy(idx_hbm.at[wid], idx_v)

        @pl.loop(0, n_chunks)
        def _(j):
            start = pl.multiple_of((wid * n_chunks + j) * SC_CHUNK, SC_CHUNK)
            pltpu.sync_copy(table_hbm.at[idx_v.at[j]], rows_v)
            pltpu.sync_copy(rows_v, out_hbm.at[pl.ds(start, SC_CHUNK)])

    return pl.kernel(
        body, out_type=jax.ShapeDtypeStruct((b, d), table.dtype), mesh=_sc_mesh(),
        scratch_types=[pltpu.VMEM((n_chunks, SC_CHUNK), jnp.int32), pltpu.VMEM((SC_CHUNK, d), table.dtype)],
        name="moe_collect",
    )(table, idx3)


def _moe_grouped_kernel(te_ref, tv_ref, xs_ref, wgu_ref, wd_ref, ys_ref, wgu16, wd16):
    i = pl.program_id(0)
    valid = tv_ref[i]

    @pl.when((i == 0) | (te_ref[i] != te_ref[jnp.maximum(i - 1, 0)]))
    def _():
        wgu16[...] = wgu_ref[0].astype(BF16)
        wd16[...] = wd_ref[0].astype(BF16)

    @pl.when(valid > 0)
    def _():
        rows = lax.broadcasted_iota(jnp.int32, xs_ref.shape, 0)
        xw = jnp.where(rows < valid, xs_ref[...], 0)
        x = _unpack_bf16_pairs(xw).astype(BF16)
        gu = _dot(x, wgu16[...])
        act = _silu(gu[:, :D_EXPERT]) * gu[:, D_EXPERT:]
        ys_ref[...] = _pack_bf16_pairs(_dot(act.astype(BF16), wd16[...]))

    @pl.when(valid == 0)
    def _():
        ys_ref[...] = jnp.zeros(ys_ref.shape, ys_ref.dtype)


def _moe_grouped(xs, tile_expert, tile_valid, wgu, wd):
    r, half = xs.shape
    d = 2 * half
    grid_spec = pltpu.PrefetchScalarGridSpec(
        num_scalar_prefetch=2, grid=(r // MOE_TILE,),
        in_specs=[pl.BlockSpec((MOE_TILE, half), lambda i, te, nu: (i, 0)),
                  pl.BlockSpec((1, d, 2 * D_EXPERT), lambda i, te, nu: (te[i], 0, 0)),
                  pl.BlockSpec((1, D_EXPERT, d), lambda i, te, nu: (te[i], 0, 0))],
        out_specs=pl.BlockSpec((MOE_TILE, half), lambda i, te, nu: (i, 0)),
        scratch_shapes=[pltpu.VMEM((d, 2 * D_EXPERT), BF16), pltpu.VMEM((D_EXPERT, d), BF16)])
    return pl.pallas_call(
        _moe_grouped_kernel, grid_spec=grid_spec,
        out_shape=jax.ShapeDtypeStruct((r, half), jnp.int32),
        compiler_params=_params(1), name="moe_grouped",
    )(tile_expert, tile_valid, xs, wgu, wd)


def _moe_combine_kernel(h_ref, z0_ref, z1_ref, rec_ref, y_ref):
    tm = h_ref.shape[0]
    rec = jnp.concatenate([rec_ref[...], jnp.zeros((LANES - SUBLANES, tm), F32)], axis=0).T
    y_ref[...] = (h_ref[...] + rec[:, 2:3] * _unpack_bf16_pairs(z0_ref[...])
                  + rec[:, 3:4] * _unpack_bf16_pairs(z1_ref[...]))


def _moe_combine(h, z, rec, *, tm):
    m, d = h.shape
    nt = m // tm
    row = lambda i: (i, 0)
    return pl.pallas_call(
        _moe_combine_kernel, grid=(nt,),
        in_specs=[pl.BlockSpec((tm, d), row),
                  pl.BlockSpec((tm, d // 2), row),
                  pl.BlockSpec((tm, d // 2), lambda i: (i + nt, 0)),
                  pl.BlockSpec((SUBLANES, tm), lambda i: (0, i))],
        out_specs=pl.BlockSpec((tm, d), row),
        out_shape=jax.ShapeDtypeStruct((m, d), F32),
        compiler_params=_params(1), name="moe_combine",
    )(h, z, z, rec)


def _moe_route(rec, cnt):
    m = rec.shape[1]
    n_tiles = 2 * m // MOE_TILE + N_EXPERTS
    counts = cnt[:, 0].astype(jnp.int32)
    tiles = (counts + MOE_TILE - 1) // MOE_TILE
    tile_end = jnp.cumsum(tiles)
    row_start = (tile_end - tiles) * MOE_TILE
    eid = rec[0:2].astype(jnp.int32)
    rank = rec[4:6].astype(jnp.int32)
    pos = rank
    for e in range(N_EXPERTS):
        pos = pos + jnp.where(eid == e, row_start[e], 0)
    tile = jnp.arange(n_tiles, dtype=jnp.int32)
    tile_expert = jnp.sum((tile_end[None, :] <= jnp.minimum(tile, tile_end[-1] - 1)[:, None]).astype(jnp.int32),
                          axis=1)
    row_end = jnp.sum(jnp.where(tile_expert[:, None] == jnp.arange(N_EXPERTS)[None, :],
                                (row_start + counts)[None, :], 0), axis=1)
    tile_valid = jnp.clip(row_end - tile * MOE_TILE, 0, MOE_TILE).astype(jnp.int32)
    return pos, tile_expert, tile_valid, n_tiles * MOE_TILE


def _head_rows(v):
    rows = jnp.concatenate([jnp.zeros((B_HEADS,), F32), v.astype(F32)])
    return jnp.broadcast_to(rows[:, None], (2 * B_HEADS, LANES))


def kernel(x_prompt, x_sample, cache_win_k, cache_win_v, state_conv, state_ssm, norm1_w, w_in, qnorm_w, knorm_w, conv_w, a_log, dt_bias, onorm_w, w_out, norm2_w, w_group, b_group, w_expert_router, b_expert_router, w_gate_up, w_down):
    nb, seq, d = x_prompt.shape
    db, t_len, _ = x_sample.shape
    n_buf = cache_win_k.shape[1]
    assert n_buf == MAX_WINDOW and seq % (MAX_DILATION * BAND) == 0 and t_len == SUBLANES
    keep = min(MAX_WINDOW, seq)

    c_bg = 3 * A_WIDTH + CONV_DIM + B_WIDTH
    w_cat = jnp.concatenate([w_in, jnp.zeros((d, LANES - 2 * B_HEADS), w_in.dtype)], axis=1).astype(BF16)
    assert w_cat.shape[1] == c_bg + LANES
    n1w = norm1_w.reshape(1, d).astype(F32)
    qw = jnp.tile(qnorm_w.astype(F32), A_HEADS).reshape(1, A_WIDTH)
    kw = jnp.tile(knorm_w.astype(F32), A_HEADS).reshape(1, A_WIDTH)
    a_log_v = _head_rows(a_log)
    dtb_v = _head_rows(dt_bias)
    onw = onorm_w.reshape(1, B_VAL_DIM).astype(F32)
    wa = w_out[:A_WIDTH].astype(BF16)
    wb = w_out[A_WIDTH:].astype(BF16)
    n2w = norm2_w.reshape(1, d).astype(F32)
    wr = jnp.concatenate([w_group, jnp.transpose(w_expert_router, (1, 0, 2)).reshape(d, N_EXPERTS),
                          jnp.zeros((d, LANES - N_GROUPS - N_EXPERTS), F32)], axis=1).astype(F32)
    br = jnp.zeros((1, LANES), F32).at[0, :N_GROUPS].set(b_group).at[0, N_GROUPS:N_GROUPS + N_EXPERTS].set(
        b_expert_router.reshape(-1))
    wgu = w_gate_up.astype(F32)
    wd = w_down.astype(F32)
    cw = conv_w.astype(F32)

    xp2d = x_prompt.reshape(nb * seq, d)
    qf, kf, vf, kwin, vwin, bqkv, z, bg = _in_proj(xp2d, n1w, w_cat, qw, kw, seq=seq, keep=keep,
                                                   fold=True, tm=512)
    oa = _attn_prompt(qf, kf, vf).reshape(nb * seq, A_WIDTH)
    conv0 = jnp.zeros((nb, CONV_WIDTH - 1, CONV_DIM), F32)
    ssm0 = jnp.zeros((nb, B_HEADS, B_KEY_DIM, B_VAL_DIM), F32)
    ob, ssm_prompt = _gdn_prompt(bqkv, z, bg, cw, conv0, ssm0, a_log_v, dtb_v, onw,
                                 seq=seq, ts=512, block=128, chunk=CHUNK)
    assert (nb * seq) % (SC_WORKERS * SC_CHUNK) == 0 and nb * seq >= N_EXPERTS * MOE_TILE
    h_p, n2p, rec, cnt = _out_proj(xp2d, oa, ob, wa, wb, n2w, wr, br, tm=512, sparse=True)
    pos, tile_expert, tile_valid, n_rows = _moe_route(rec, cnt)
    xs = _sc_dispatch(n2p, pos, n_rows)
    win_k_prompt = jnp.transpose(kwin, (0, 3, 1, 2))
    win_v_prompt = jnp.transpose(vwin, (0, 3, 1, 2))
    conv_prompt = bqkv.reshape(nb, seq, CONV_DIM)[:, seq - (CONV_WIDTH - 1):]

    ms = db * t_len
    xs2d = x_sample.reshape(ms, d)
    tms = min(512, ms)
    qs, ksn, vsn, _, _, bqkv_s, z_s, bg_s = _in_proj(xs2d, n1w, w_cat, qw, kw, seq=ms, keep=ms,
                                                     fold=False, tm=tms)
    cache_kt = jnp.transpose(cache_win_k.astype(F32), (0, 2, 3, 1))
    cache_vt = jnp.transpose(cache_win_v.astype(F32), (0, 2, 3, 1))
    oa_s = _attn_sample(qs, ksn, vsn, cache_kt, cache_vt, t_len=t_len)
    xs, oa_s = lax.optimization_barrier((xs, oa_s))
    ys = _moe_grouped(xs, tile_expert, tile_valid, wgu, wd)
    zs = _sc_gather(ys, pos.reshape(-1))
    xpad = jnp.concatenate([state_conv.astype(F32), bqkv_s.reshape(db, t_len, CONV_DIM)], axis=1)
    nseq = 16 if db % 16 == 0 else 1
    ob_s, ssm_sample = _gdn_sample(xpad, z_s, bg_s, cw, state_ssm.astype(F32), a_log_v, dtb_v, onw,
                                   t_len=t_len, nseq=nseq)
    h_s, n2_s, gates_s = _out_proj(xs2d, oa_s, ob_s, wa, wb, n2w, wr, br, tm=tms, sparse=False)
    y_sample = _moe(h_s, n2_s, gates_s, wgu, wd, tm=ms)
    zs, y_sample = lax.optimization_barrier((zs, y_sample))
    y_prompt = _moe_combine(h_p, zs, rec, tm=MOE_TILE).reshape(nb, seq, d)
    y_sample = y_sample.reshape(db, t_len, d)
    win_k_sample = ksn.reshape(db, t_len, A_HEADS, A_HEAD_DIM)
    win_v_sample = vsn.reshape(db, t_len, A_HEADS, A_HEAD_DIM)
    conv_sample = xpad[:, t_len:]

    return (y_prompt, y_sample, win_k_prompt, win_v_prompt, conv_prompt, ssm_prompt,
            win_k_sample, win_v_sample, conv_sample, ssm_sample)
```

```python
import functools

import jax
import jax.numpy as jnp
from jax import lax
from jax.experimental import pallas as pl
from jax.experimental.pallas import tpu as pltpu
from jax.experimental.pallas import tpu_sc as plsc

F32 = jnp.float32
BF16 = jnp.bfloat16

A_HEADS = 8
A_HEAD_DIM = 64
A_WIDTH = A_HEADS * A_HEAD_DIM
DILATED_PATTERNS = ((128, 1), (512, 4), (2048, 16))
MAX_WINDOW = 2048
MAX_DILATION = 16
BAND = 128
ATTN_SCALE = A_HEAD_DIM ** -0.5
LOG2_E = 1.4426950408889634
B_HEADS = 4
B_KEY_DIM = 128
B_VAL_DIM = 128
B_WIDTH = B_HEADS * B_VAL_DIM
CONV_WIDTH = 4
CONV_DIM = B_HEADS * (2 * B_KEY_DIM + B_VAL_DIM)
CHUNK = 64
N_GROUPS = 4
EXPERTS_PER_GROUP = 4
N_EXPERTS = N_GROUPS * EXPERTS_PER_GROUP
D_EXPERT = 512
RMS_EPS = 1e-6
NEG_INF = -1e30

LANES = 128
SUBLANES = 8
VMEM_LIMIT = 56 * 1024 * 1024


def _params(n_axes, vmem=VMEM_LIMIT):
    return pltpu.CompilerParams(dimension_semantics=("arbitrary",) * n_axes, vmem_limit_bytes=vmem)


def _split2(x):
    hi = x.astype(BF16)
    lo = (x - hi.astype(F32)).astype(BF16)
    return hi, lo


def _split3(x):
    hi = x.astype(BF16)
    r = x - hi.astype(F32)
    mid = r.astype(BF16)
    lo = (r - mid.astype(F32)).astype(BF16)
    return hi, mid, lo


def _dot(a, b):
    return jnp.dot(a, b, preferred_element_type=F32)


def _dot_nt(a, b):
    return lax.dot_general(a, b, (((1,), (1,)), ((), ())), preferred_element_type=F32)


def _sigmoid(x):
    return 1.0 / (1.0 + jnp.exp2(x * -LOG2_E))


def _silu(x):
    return x * _sigmoid(x)


def _in_proj_kernel(x_ref, n1w_ref, w_ref, qw_ref, kw_ref,
                    q_ref, k_ref, v_ref, kwin_ref, vwin_ref, bqkv_ref, z_ref, bg_ref,
                    *scratch, fold):
    tm = x_ref.shape[0]
    x = x_ref[...]
    ms = jnp.mean(x * x, axis=-1, keepdims=True)
    n1 = (x * lax.rsqrt(ms + RMS_EPS) * n1w_ref[...]).astype(BF16)

    first_head = lax.broadcasted_iota(jnp.int32, (tm, LANES), 1) < A_HEAD_DIM

    def head_norm(t, w):
        cols = []
        for c in range(A_WIDTH // LANES):
            tc = t[:, c * LANES:(c + 1) * LANES]
            sq = tc * tc
            s0 = jnp.sum(jnp.where(first_head, sq, 0.0), axis=-1, keepdims=True)
            s1 = jnp.sum(jnp.where(first_head, 0.0, sq), axis=-1, keepdims=True)
            ms = jnp.where(first_head, s0, s1) * (1.0 / A_HEAD_DIM)
            cols.append(tc * lax.rsqrt(ms + RMS_EPS))
        return jnp.concatenate(cols, axis=1) * w

    q_scale = ATTN_SCALE * LOG2_E if fold else ATTN_SCALE
    q = head_norm(_dot(n1, w_ref[:, 0:A_WIDTH]), qw_ref[...]) * q_scale
    k = head_norm(_dot(n1, w_ref[:, A_WIDTH:2 * A_WIDTH]), kw_ref[...])
    v = _dot(n1, w_ref[:, 2 * A_WIDTH:3 * A_WIDTH])
    if fold:
        kwin_ref[0] = k.T.reshape(A_HEADS, A_HEAD_DIM, tm)
        vwin_ref[0] = v.T.reshape(A_HEADS, A_HEAD_DIM, tm)

        (scr,) = scratch
        rows = tm // MAX_DILATION
        for val, out in ((q, q_ref), (k, k_ref), (v, v_ref)):
            for c in range(A_WIDTH // LANES):
                cs = slice(c * LANES, (c + 1) * LANES)
                scr[c] = val[:, cs]
                for r in range(MAX_DILATION):
                    out[0, r, :, cs] = scr[c, pl.ds(r, rows, stride=MAX_DILATION), :]
    else:
        kwin_ref[...] = k
        vwin_ref[...] = v
        q_ref[...] = q
        k_ref[...] = k
        v_ref[...] = v
    c0 = 3 * A_WIDTH
    bqkv_ref[...] = _dot(n1, w_ref[:, c0:c0 + CONV_DIM])
    c1 = c0 + CONV_DIM
    z_ref[...] = _dot(n1, w_ref[:, c1:c1 + B_WIDTH]).astype(BF16)
    c2 = c1 + B_WIDTH
    bg_ref[...] = _dot(n1, w_ref[:, c2:c2 + LANES])


def _in_proj(x2d, n1w, w_cat, qw, kw, *, seq, keep, fold, tm):
    m, d = x2d.shape
    nb = m // seq
    tiles_per_seq = seq // tm
    skip = (seq - keep) // tm
    keep_tiles = keep // tm
    n_cols = w_cat.shape[1]

    def win_map(i):
        return (i // tiles_per_seq) * keep_tiles + jnp.maximum(i % tiles_per_seq - skip, 0), 0

    row = lambda i: (i, 0)
    const = lambda i: (0, 0)
    if fold:
        rows = tm // MAX_DILATION
        qkv_shape = jax.ShapeDtypeStruct((nb, MAX_DILATION, seq // MAX_DILATION, A_WIDTH), F32)
        qkv_spec = pl.BlockSpec((1, MAX_DILATION, rows, A_WIDTH),
                                lambda i: (i // tiles_per_seq, 0, i % tiles_per_seq, 0))
        scratch = [pltpu.VMEM((A_WIDTH // LANES, tm, LANES), F32)]
        win_shape = jax.ShapeDtypeStruct((nb, A_HEADS, A_HEAD_DIM, keep), F32)
        win_spec = pl.BlockSpec((1, A_HEADS, A_HEAD_DIM, tm),
                                lambda i: (i // tiles_per_seq, 0, 0, jnp.maximum(i % tiles_per_seq - skip, 0)))
    else:
        qkv_shape = jax.ShapeDtypeStruct((m, A_WIDTH), F32)
        qkv_spec = pl.BlockSpec((tm, A_WIDTH), row)
        scratch = []
        win_shape = jax.ShapeDtypeStruct((nb * keep, A_WIDTH), F32)
        win_spec = pl.BlockSpec((tm, A_WIDTH), win_map)
    out_shape = (qkv_shape, qkv_shape, qkv_shape, win_shape, win_shape,
                 jax.ShapeDtypeStruct((m, CONV_DIM), F32),
                 jax.ShapeDtypeStruct((m, B_WIDTH), BF16),
                 jax.ShapeDtypeStruct((m, LANES), F32))
    out_specs = (qkv_spec, qkv_spec, qkv_spec, win_spec, win_spec,
                 pl.BlockSpec((tm, CONV_DIM), row),
                 pl.BlockSpec((tm, B_WIDTH), row),
                 pl.BlockSpec((tm, LANES), row))
    in_specs = [pl.BlockSpec((tm, d), row),
                pl.BlockSpec((1, d), const),
                pl.BlockSpec((d, n_cols), const),
                pl.BlockSpec((1, A_WIDTH), const),
                pl.BlockSpec((1, A_WIDTH), const)]
    return pl.pallas_call(
        functools.partial(_in_proj_kernel, fold=fold),
        grid=(m // tm,), in_specs=in_specs, out_specs=out_specs, out_shape=out_shape,
        scratch_shapes=scratch, compiler_params=_params(1), name="in_proj",
    )(x2d, n1w, w_cat, qw, kw)


def _band_mask(pieces):
    sub = BAND // pieces
    r = lax.broadcasted_iota(jnp.int32, (BAND, 2 * BAND), 0)
    c = lax.broadcasted_iota(jnp.int32, (BAND, 2 * BAND), 1)
    qpos = (r % sub) * pieces + r // sub + BAND
    cc = c % BAND
    kpos = (cc % sub) * pieces + cc // sub + (c // BAND) * BAND
    dist = qpos - kpos
    return (dist >= 0) & (dist <= BAND), c >= BAND


ATTN_UNROLL = 16


def _attn_prompt_kernel(q_ref, k_ref, v_ref, o_ref, p_scr, op_scr, mx_scr, l_scr, nat_scr):
    u_len = q_ref.shape[2]
    lane = lax.broadcasted_iota(jnp.int32, (BAND, LANES), 1)
    head0 = lane < A_HEAD_DIM

    def per_head(col):
        return jnp.where(head0, jnp.broadcast_to(col[:BAND], (BAND, LANES)),
                         jnp.broadcast_to(col[BAND:], (BAND, LANES)))

    for pi, pieces in enumerate((16, 4, 1)):
        sub = BAND // pieces
        n_res = MAX_DILATION // pieces
        n_blk = u_len // sub
        band, _ = _band_mask(pieces)
        bias = jnp.where(band, 0.0, NEG_INF)
        bias2 = jnp.concatenate([bias, bias], axis=0)
        bias2_cur = bias2[:, BAND:]

        def locate(blk, n_res=n_res, sub=sub):
            m = blk // n_res
            cur = pl.multiple_of(m * sub, SUBLANES)
            prv = pl.multiple_of(jnp.maximum(m - 1, 0) * sub, SUBLANES)
            return blk % n_res, cur, prv

        def gather(ref, res, start, pieces=pieces, n_res=n_res, sub=sub):
            parts = [ref[0, a * n_res + res, pl.ds(start, sub), :] for a in range(pieces)]
            return parts[0] if pieces == 1 else jnp.concatenate(parts, axis=0)

        def scatter(ref, res, start, val, pi=pi, pieces=pieces, n_res=n_res, sub=sub):
            for a in range(pieces):
                ref[pi, a * n_res + res, pl.ds(start, sub), :] = val[a * sub:(a + 1) * sub]

        def probs(it, carry, first, bias2=bias2, bias2_cur=bias2_cur):
            for j, is_first in enumerate(first):
                blk = it * ATTN_UNROLL + j
                res, cur, prv = locate(blk)
                qb = gather(q_ref, res, cur)
                q2 = jnp.concatenate([jnp.where(head0, qb, 0.0), jnp.where(head0, 0.0, qb)], axis=0).astype(BF16)
                kc = gather(k_ref, res, cur)
                if is_first:
                    s = _dot_nt(q2, kc.astype(BF16)) + bias2_cur
                else:
                    kb = jnp.concatenate([gather(k_ref, res, prv), kc], axis=0).astype(BF16)
                    s = _dot_nt(q2, kb) + bias2
                mx = jnp.max(s, axis=-1, keepdims=True)
                p = jnp.exp2(s - mx)
                if is_first:
                    p_scr[blk, :, BAND:] = p.astype(BF16)
                else:
                    p_scr[blk] = p.astype(BF16)
                scatter(mx_scr, res, cur, per_head(mx))
            return carry

        def values(it, carry, first):
            for j, is_first in enumerate(first):
                blk = it * ATTN_UNROLL + j
                res, cur, prv = locate(blk)
                vc = gather(v_ref, res, cur)
                if is_first:
                    p = p_scr[blk, :, BAND:]
                    vb = vc.astype(BF16)
                else:
                    p = p_scr[blk]
                    vb = jnp.concatenate([gather(v_ref, res, prv), vc], axis=0).astype(BF16)
                o2 = _dot(p, jnp.concatenate([vb, jnp.ones(vb.shape, BF16)], axis=1))
                scatter(op_scr, res, cur, jnp.where(head0, o2[:BAND, :LANES], o2[BAND:, :LANES]))
                scatter(l_scr, res, cur, jnp.where(head0, o2[:BAND, LANES:], o2[BAND:, LANES:]))
            return carry

        n_it = n_res * n_blk // ATTN_UNROLL
        flags = [tuple(g * ATTN_UNROLL + j < n_res for j in range(ATTN_UNROLL)) for g in range(n_it)]
        segments = []
        for g, f in enumerate(flags):
            if segments and segments[-1][2] == f:
                segments[-1][1] = g + 1
            else:
                segments.append([g, g + 1, f])
        for phase in (probs, values):
            for lo, hi, f in segments:
                lax.fori_loop(lo, hi, functools.partial(phase, first=f), 0)

    for r in range(MAX_DILATION):
        m0, m1, m2 = mx_scr[0, r], mx_scr[1, r], mx_scr[2, r]
        big = jnp.maximum(jnp.maximum(m0, m1), m2)
        e0, e1, e2 = jnp.exp2(m0 - big), jnp.exp2(m1 - big), jnp.exp2(m2 - big)
        num = e0 * op_scr[0, r] + e1 * op_scr[1, r] + e2 * op_scr[2, r]
        den = e0 * l_scr[0, r] + e1 * l_scr[1, r] + e2 * l_scr[2, r]
        nat_scr[pl.ds(r, u_len, stride=MAX_DILATION), :] = num / den
    o_ref[0] = nat_scr[...].astype(BF16)


def _attn_prompt(qf, kf, vf):
    nb, _, u_len, _ = qf.shape
    seq = u_len * MAX_DILATION
    n_pairs = A_WIDTH // LANES
    spec = pl.BlockSpec((1, MAX_DILATION, u_len, LANES), lambda b, p: (b, 0, 0, p))
    return pl.pallas_call(
        _attn_prompt_kernel,
        grid=(nb, n_pairs),
        in_specs=[spec, spec, spec],
        out_specs=pl.BlockSpec((1, seq, LANES), lambda b, p: (b, 0, p)),
        out_shape=jax.ShapeDtypeStruct((nb, seq, A_WIDTH), BF16),
        scratch_shapes=[pltpu.VMEM((seq // BAND, 2 * BAND, 2 * BAND), BF16),
                        pltpu.VMEM((3, MAX_DILATION, u_len, LANES), F32),
                        pltpu.VMEM((3, MAX_DILATION, u_len, LANES), F32),
                        pltpu.VMEM((3, MAX_DILATION, u_len, LANES), F32),
                        pltpu.VMEM((seq, LANES), F32)],
        compiler_params=_params(2), name="attn_prompt",
    )(qf, kf, vf)


def _multiplicity(t, dd):
    cnt = jnp.zeros(dd.shape, F32)
    for w, d in DILATED_PATTERNS:
        ok = (dd >= 0) & (dd <= w) & (dd % d == 0)
        cnt = cnt + jnp.where(ok, 1.0, 0.0)
    return cnt


def _attn_sample_kernel(q_ref, kn_ref, vn_ref, kt_ref, vt_ref, o_ref):
    t_len = q_ref.shape[0]
    n_buf = kt_ref.shape[3]
    q = q_ref[...]
    kn = kn_ref[...]
    vn = vn_ref[...]
    tq = n_buf + lax.broadcasted_iota(jnp.int32, (t_len, n_buf), 0)
    cnt = _multiplicity(None, tq - lax.broadcasted_iota(jnp.int32, (t_len, n_buf), 1))
    cnt_new = _multiplicity(None, lax.broadcasted_iota(jnp.int32, (t_len, t_len), 0)
                            - lax.broadcasted_iota(jnp.int32, (t_len, t_len), 1))
    outs = []
    for h in range(A_HEADS):
        hs = slice(h * A_HEAD_DIM, (h + 1) * A_HEAD_DIM)
        qh = q[:, hs].astype(BF16)
        s = jnp.where(cnt > 0, _dot(qh, kt_ref[0, h].astype(BF16)), NEG_INF)
        sn = jnp.where(cnt_new > 0, _dot_nt(qh, kn[:, hs].astype(BF16)), NEG_INF)
        m = jnp.maximum(jnp.max(s, axis=-1, keepdims=True), jnp.max(sn, axis=-1, keepdims=True))
        p = cnt * jnp.exp(s - m)
        pn = cnt_new * jnp.exp(sn - m)
        l = jnp.sum(p, axis=-1, keepdims=True) + jnp.sum(pn, axis=-1, keepdims=True)
        o = _dot_nt(p.astype(BF16), vt_ref[0, h].astype(BF16)) + _dot(pn.astype(BF16), vn[:, hs].astype(BF16))
        outs.append(o / l)
    o_ref[...] = jnp.concatenate(outs, axis=-1)


def _attn_sample(q2d, kn2d, vn2d, cache_kt, cache_vt, *, t_len, first, count):
    n_buf = cache_kt.shape[3]
    tok = pl.BlockSpec((t_len, A_WIDTH), lambda b: (b + first, 0))
    win = pl.BlockSpec((1, A_HEADS, A_HEAD_DIM, n_buf), lambda b: (b + first, 0, 0, 0))
    return pl.pallas_call(
        _attn_sample_kernel,
        grid=(count,),
        in_specs=[tok, tok, tok, win, win],
        out_specs=pl.BlockSpec((t_len, A_WIDTH), lambda b: (b, 0)),
        out_shape=jax.ShapeDtypeStruct((count * t_len, A_WIDTH), F32),
        compiler_params=_params(1), name="attn_sample",
    )(q2d, kn2d, vn2d, cache_kt, cache_vt)


def _gdn_prep(c, bg, a_col, dt_col, chunk):
    n = c.shape[0]
    bgt = bg.T[0:2 * B_HEADS]
    row = lax.broadcasted_iota(jnp.int32, bgt.shape, 0)
    xg = bgt + dt_col
    sp = jnp.maximum(xg, 0.0) + jnp.log1p(jnp.exp(-jnp.abs(xg)))
    gt = jnp.where(row >= B_HEADS, -a_col * sp, 0.0)
    ri = lax.broadcasted_iota(jnp.int32, (n, n), 0)
    ci = lax.broadcasted_iota(jnp.int32, (n, n), 1)
    tri = jnp.where(((ri // chunk) == (ci // chunk)) & (ri >= ci), 1.0, 0.0).astype(BF16)
    th, tm_, tl = _split3(gt)
    drow = _dot_nt(th, tri) + _dot_nt(tm_, tri) + _dot_nt(tl, tri)
    both = jnp.where(row >= B_HEADS, drow, _sigmoid(bgt))
    cols = jnp.concatenate([both, jnp.zeros((LANES - 2 * B_HEADS, n), F32)], axis=0).T
    beta_all = cols
    dcol = cols
    heads = []
    kd_ = B_KEY_DIM
    for h in range(B_HEADS):
        qh = c[:, h * kd_:(h + 1) * kd_]
        kh = c[:, B_HEADS * kd_ + h * kd_:B_HEADS * kd_ + (h + 1) * kd_]
        vh = c[:, 2 * B_HEADS * kd_ + h * B_VAL_DIM:2 * B_HEADS * kd_ + (h + 1) * B_VAL_DIM]
        qh = qh * lax.rsqrt(jnp.sum(qh * qh, axis=-1, keepdims=True) + RMS_EPS) * (B_KEY_DIM ** -0.5)
        kh = kh * lax.rsqrt(jnp.sum(kh * kh, axis=-1, keepdims=True) + RMS_EPS)
        heads.append((qh, kh, vh, beta_all[:, h:h + 1], dcol[:, B_HEADS + h:B_HEADS + h + 1],
                      drow[B_HEADS + h:B_HEADS + h + 1, :]))
    return heads


def _gdn_out(o, onw, zh):
    ms = jnp.mean(o * o, axis=-1, keepdims=True)
    return o * lax.rsqrt(ms + RMS_EPS) * onw * _silu(zh)


def _gdn_prompt_kernel(x_ref, z_ref, bg_ref, cw_ref, cst_ref, sst_ref, avec_ref, dtb_ref, onw_ref,
                       o_ref, snew_ref, xp_scr, s_scr, *, block, chunk):
    ts = x_ref.shape[0]
    t = pl.program_id(1)
    pad = SUBLANES
    hist = CONV_WIDTH - 1

    @pl.when(t == 0)
    def _():
        xp_scr[...] = jnp.zeros((pad, CONV_DIM), F32)
        xp_scr[pad - hist:pad, :] = cst_ref[0]
        s_scr[...] = sst_ref[0]

    assert CONV_WIDTH == 4
    x = x_ref[...]
    tail = xp_scr[...]
    sub = lax.broadcasted_iota(jnp.int32, (pad, CONV_DIM), 0)

    def shift(cur, prev_tail, s):
        rolled = pltpu.roll(cur, s, axis=0)
        head = jnp.where(sub < s, pltpu.roll(prev_tail, s, axis=0), rolled[0:pad])
        return jnp.concatenate([head, rolled[pad:]], axis=0)

    w0, w1, w2, w3 = (cw_ref[i:i + 1, :] for i in range(CONV_WIDTH))
    x1 = shift(x, tail, 1)
    a = x * w1 + x1 * w0
    a_tail = tail * w1 + pltpu.roll(tail, 1, axis=0) * w0
    y = x * w3 + x1 * w2 + shift(a, a_tail, 2)
    xp_scr[...] = x[ts - pad:ts]
    c = _silu(y)
    avec = jnp.exp(avec_ref[...])
    onw = onw_ref[...]
    n_chunks = block // chunk
    n_sb = ts // block
    ri = lax.broadcasted_iota(jnp.int32, (block, block), 0)
    ci = lax.broadcasted_iota(jnp.int32, (block, block), 1)
    same = (ri // chunk) == (ci // chunk)
    lower = same & (ri >= ci)
    strict = same & (ri > ci)
    eye = jnp.where(ri == ci, 1.0, 0.0).astype(F32)

    s = [s_scr[h] for h in range(B_HEADS)]
    for g0 in range(0, n_sb, GDN_GROUP):
        units = []
        for sb in range(g0, g0 + GDN_GROUP):
            rs = slice(sb * block, (sb + 1) * block)
            heads = _gdn_prep(c[rs], bg_ref[rs, :], avec, dtb_ref[...], chunk)
            for h, (qh, kh, vh, beta, dcol, drow) in enumerate(heads):
                units.append(dict(sb=sb, h=h, q=qh, k=kh, v=vh, beta=beta, dcol=dcol, drow=drow))
        s = _gdn_group(units, s, z_ref, o_ref, onw, lower, strict, eye, block, chunk)
    for h in range(B_HEADS):
        s_scr[h] = s[h]

    @pl.when(t == pl.num_programs(1) - 1)
    def _():
        snew_ref[0] = s_scr[...]


GDN_GROUP = 4


def _gdn_group(units, s, z_ref, o_ref, onw, lower, strict, eye, block, chunk, state_refs=None):
    n_chunks = block // chunk
    for u in units:
        gam = jnp.exp(jnp.where(lower, u["dcol"] - u["drow"], NEG_INF))
        kb = u["k"] * u["beta"]
        kq = _dot_nt(jnp.concatenate([kb, u["q"]], axis=0).astype(BF16), u["k"].astype(BF16))
        u["pw"] = jnp.where(strict, kq[:block] * gam, 0.0)
        u["t"] = eye - u["pw"]
        u["attn"] = jnp.where(lower, kq[block:] * gam, 0.0).astype(BF16)
        ed = jnp.exp(u["dcol"])
        u["rhs"] = jnp.concatenate([u["v"] * u["beta"], kb * ed], axis=1).astype(BF16)
        u["qd"] = u["q"] * ed
        u["kt"] = u["k"].T

    kk = 2
    while kk < chunk:
        for u in units:
            p16 = u["pw"].astype(BF16)
            u["pw"] = _dot(p16, p16)
        for u in units:
            u["t"] = u["t"] + _dot(u["t"].astype(BF16), u["pw"].astype(BF16))
        kk *= 2

    aligned = chunk % (2 * SUBLANES) == 0

    def chunk_rows(u, name, cs):
        return u[name + "16"][cs] if aligned else u[name][cs].astype(BF16)

    for u in units:
        u["uw"] = _dot(u["t"].astype(BF16), u["rhs"])
        u["uw16"] = u["uw"].astype(BF16)
    for u in units:
        au_aw = _dot(u["attn"], u["uw16"])
        u["op"] = au_aw[:, :B_VAL_DIM]
        u["qp"] = u["qd"] - au_aw[:, B_VAL_DIM:]
        u["qp16"] = u["qp"].astype(BF16)
    for u in units:
        u["n"], u["mw"], u["e"] = [], [], []
        for cc in range(n_chunks):
            cs = slice(cc * chunk, (cc + 1) * chunk)
            last = u["drow"][:, (cc + 1) * chunk - 1:(cc + 1) * chunk]
            kdt = (u["kt"][:, cs] * jnp.exp(last - u["drow"][:, cs])).astype(BF16)
            nm = _dot(kdt, chunk_rows(u, "uw", cs))
            u["n"].append(nm[:, :B_VAL_DIM])
            u["mw"].append(nm[:, B_VAL_DIM:].astype(BF16))
            u["e"].append(jnp.exp(last))

    s = None if s is None else list(s)
    for b0 in range(0, len(units), B_HEADS):
        sb = units[b0]["sb"]
        rs = slice(sb * block, (sb + 1) * block)
        outs = [[] for _ in range(B_HEADS)]
        for cc in range(n_chunks):
            cs = slice(cc * chunk, (cc + 1) * chunk)
            for h in range(B_HEADS):
                u = units[b0 + h]
                s_in = s[h] if state_refs is None else state_refs[0][cc, h]
                s16 = s_in.astype(BF16)
                if aligned:
                    r = _dot(jnp.concatenate([u["mw"][cc], chunk_rows(u, "qp", cs)], axis=0), s16)
                    r_m, r_q = r[:B_KEY_DIM], r[B_KEY_DIM:]
                else:
                    r_m, r_q = _dot(u["mw"][cc], s16), _dot(chunk_rows(u, "qp", cs), s16)
                outs[h].append(r_q + u["op"][cs])
                s_out = s_in * u["e"][cc] - r_m + u["n"][cc]
                if state_refs is None:
                    s[h] = s_out
                else:
                    state_refs[1][cc, h] = s_out
        for h in range(B_HEADS):
            o_all = jnp.concatenate(outs[h], axis=0) if n_chunks > 1 else outs[h][0]
            zh = z_ref[rs, h * B_VAL_DIM:(h + 1) * B_VAL_DIM].astype(F32)
            o_ref[rs, h * B_VAL_DIM:(h + 1) * B_VAL_DIM] = _gdn_out(o_all, onw, zh).astype(o_ref.dtype)
    return s


def _gdn_prompt(bqkv, z, bg, conv_w, conv_state, ssm_state, a_log_v, dtb_v, onw, *, seq, ts, block, chunk):
    m = bqkv.shape[0]
    nb = m // seq
    tps = seq // ts
    row = lambda b, t: (b * tps + t, 0)
    const = lambda b, t: (0, 0)
    return pl.pallas_call(
        functools.partial(_gdn_prompt_kernel, block=block, chunk=chunk),
        grid=(nb, tps),
        in_specs=[pl.BlockSpec((ts, CONV_DIM), row),
                  pl.BlockSpec((ts, B_WIDTH), row),
                  pl.BlockSpec((ts, LANES), row),
                  pl.BlockSpec((CONV_WIDTH, CONV_DIM), const),
                  pl.BlockSpec((1, CONV_WIDTH - 1, CONV_DIM), lambda b, t: (b, 0, 0)),
                  pl.BlockSpec((1, B_HEADS, B_KEY_DIM, B_VAL_DIM), lambda b, t: (b, 0, 0, 0)),
                  pl.BlockSpec((2 * B_HEADS, LANES), const),
                  pl.BlockSpec((2 * B_HEADS, LANES), const),
                  pl.BlockSpec((1, B_VAL_DIM), const)],
        out_specs=(pl.BlockSpec((ts, B_WIDTH), row),
                   pl.BlockSpec((1, B_HEADS, B_KEY_DIM, B_VAL_DIM), lambda b, t: (b, 0, 0, 0))),
        out_shape=(jax.ShapeDtypeStruct((m, B_WIDTH), BF16),
                   jax.ShapeDtypeStruct(ssm_state.shape, F32)),
        scratch_shapes=[pltpu.VMEM((SUBLANES, CONV_DIM), F32),
                        pltpu.VMEM((B_HEADS, B_KEY_DIM, B_VAL_DIM), F32)],
        compiler_params=_params(2), name="gdn_prompt",
    )(bqkv, z, bg, conv_w, conv_state, ssm_state, a_log_v, dtb_v, onw)


def _gdn_sample_kernel(xp_ref, z_ref, bg_ref, cw_ref, sst_ref, avec_ref, dtb_ref, onw_ref,
                       o_ref, snew_ref, *, t_len):
    nseq = xp_ref.shape[0]
    hist = CONV_WIDTH - 1
    ys = []
    for j in range(nseq):
        y = xp_ref[j, 0:t_len, :] * cw_ref[0:1, :]
        for i in range(1, CONV_WIDTH):
            y = y + xp_ref[j, i:i + t_len, :] * cw_ref[i:i + 1, :]
        ys.append(y)
    c = _silu(jnp.concatenate(ys, axis=0))
    avec = jnp.exp(avec_ref[...])
    onw = onw_ref[...]
    block = nseq * t_len
    ri = lax.broadcasted_iota(jnp.int32, (block, block), 0)
    ci = lax.broadcasted_iota(jnp.int32, (block, block), 1)
    same = (ri // t_len) == (ci // t_len)
    heads = _gdn_prep(c, bg_ref[...], avec, dtb_ref[...], t_len)
    units = [dict(sb=0, h=h, q=qh, k=kh, v=vh, beta=beta, dcol=dcol, drow=drow)
             for h, (qh, kh, vh, beta, dcol, drow) in enumerate(heads)]
    _gdn_group(units, None, z_ref, o_ref, onw, same & (ri >= ci), same & (ri > ci),
               jnp.where(ri == ci, 1.0, 0.0).astype(F32), block, t_len, state_refs=(sst_ref, snew_ref))


def _gdn_sample(xp, z, bg, conv_w, ssm_state, a_log_v, dtb_v, onw, *, t_len, nseq):
    nb = xp.shape[0]
    rows = nseq * t_len
    row = lambda i: (i, 0)
    const = lambda i: (0, 0)
    return pl.pallas_call(
        functools.partial(_gdn_sample_kernel, t_len=t_len),
        grid=(nb // nseq,),
        in_specs=[pl.BlockSpec((nseq, xp.shape[1], CONV_DIM), lambda i: (i, 0, 0)),
                  pl.BlockSpec((rows, B_WIDTH), row),
                  pl.BlockSpec((rows, LANES), row),
                  pl.BlockSpec((CONV_WIDTH, CONV_DIM), const),
                  pl.BlockSpec((nseq, B_HEADS, B_KEY_DIM, B_VAL_DIM), lambda i: (i, 0, 0, 0)),
                  pl.BlockSpec((2 * B_HEADS, LANES), const),
                  pl.BlockSpec((2 * B_HEADS, LANES), const),
                  pl.BlockSpec((1, B_VAL_DIM), const)],
        out_specs=(pl.BlockSpec((rows, B_WIDTH), row),
                   pl.BlockSpec((nseq, B_HEADS, B_KEY_DIM, B_VAL_DIM), lambda i: (i, 0, 0, 0))),
        out_shape=(jax.ShapeDtypeStruct((nb * t_len, B_WIDTH), BF16),
                   jax.ShapeDtypeStruct(ssm_state.shape, F32)),
        compiler_params=_params(1), name="gdn_sample",
    )(xp, z, bg, conv_w, ssm_state, a_log_v, dtb_v, onw)


def _pack_bf16_pairs(x):
    n = x.shape[1] // 2
    bits = pltpu.bitcast(x.astype(BF16).astype(F32), jnp.int32)
    return lax.shift_right_logical(bits[:, :n], 16) | (bits[:, n:] & jnp.int32(-65536))


def _unpack_bf16_pairs(w):
    lo = pltpu.bitcast(lax.shift_left(w, 16), F32)
    hi = pltpu.bitcast(w & jnp.int32(-65536), F32)
    return jnp.concatenate([lo, hi], axis=1)


def _out_proj_kernel(x_ref, oa_ref, ob_ref, wa_ref, wb_ref, n2w_ref, wr_ref, br_ref, *rest, sparse):
    if sparse:
        tri_ref, h_ref, n2_ref, gates_ref, cnt_ref = rest
    else:
        h_ref, n2_ref, gates_ref = rest
    h = x_ref[...] + _dot(oa_ref[...].astype(BF16), wa_ref[...]) + _dot(ob_ref[...].astype(BF16), wb_ref[...])
    h_ref[...] = h
    ms = jnp.mean(h * h, axis=-1, keepdims=True)
    n2 = h * lax.rsqrt(ms + RMS_EPS) * n2w_ref[...]
    if sparse:
        n2_ref[...] = _pack_bf16_pairs(n2)
    else:
        n2_ref[...] = n2.astype(BF16)
    nh, nl = _split2(n2)
    wh, wl = _split2(wr_ref[...])
    tm = n2.shape[0]
    parts = _dot(jnp.concatenate([nh, nl], axis=0), jnp.concatenate([wh, wl], axis=1))
    logits = parts[:tm, :LANES] + parts[:tm, LANES:] + parts[tm:, :LANES] + br_ref[...]
    n_rows = 3 * SUBLANES
    lt = logits.T[0:n_rows]
    row = lax.broadcasted_iota(jnp.int32, lt.shape, 0).astype(F32)
    big = 1e9
    gl = jnp.where(row < N_GROUPS, lt, NEG_INF)
    gmax = jnp.max(gl, axis=0, keepdims=True)
    gi = jnp.min(jnp.where(gl == gmax, row, big), axis=0, keepdims=True)
    g_sel = 1.0 / jnp.sum(jnp.exp(gl - gmax), axis=0, keepdims=True)
    lo = N_GROUPS + EXPERTS_PER_GROUP * gi
    el = jnp.where((row >= lo) & (row < lo + EXPERTS_PER_GROUP), lt, NEG_INF)
    v1 = jnp.max(el, axis=0, keepdims=True)
    i1 = jnp.min(jnp.where(el == v1, row, big), axis=0, keepdims=True)
    el2 = jnp.where(row == i1, NEG_INF, el)
    v2 = jnp.max(el2, axis=0, keepdims=True)
    i2 = jnp.min(jnp.where(el2 == v2, row, big), axis=0, keepdims=True)
    e2 = jnp.exp(v2 - v1)
    w1 = g_sel / (1.0 + e2)
    w2 = g_sel * e2 / (1.0 + e2)
    e0 = i1 - N_GROUPS
    e1 = i2 - N_GROUPS
    erow = lax.broadcasted_iota(jnp.int32, (N_EXPERTS, tm), 0).astype(F32)
    if not sparse:
        gt = jnp.where(erow == e0, w1, 0.0) + jnp.where(erow == e1, w2, 0.0)
        gates_ref[...] = jnp.concatenate([gt, jnp.zeros((LANES - N_EXPERTS, tm), F32)], axis=0).T
        return
    @pl.when(pl.program_id(0) == 0)
    def _():
        cnt_ref[...] = jnp.zeros(cnt_ref.shape, F32)

    cnt = cnt_ref[...]
    onehot = jnp.where(erow == e0, 1.0, 0.0) + jnp.where(erow == e1, 1.0, 0.0)
    before = _dot(onehot.astype(BF16), tri_ref[...]) + cnt[:, 0:1]
    r0 = jnp.sum(jnp.where(erow == e0, before, 0.0), axis=0, keepdims=True)
    r1 = jnp.sum(jnp.where(erow == e1, before, 0.0), axis=0, keepdims=True)
    cnt_ref[...] = cnt + jnp.sum(onehot, axis=1, keepdims=True)
    gates_ref[...] = jnp.concatenate([e0, e1, w1, w2, r0, r1, jnp.zeros((2, tm), F32)], axis=0)


def _out_proj(x2d, oa, ob, wa, wb, n2w, wr, br, *, tm, sparse):
    m, d = x2d.shape
    row = lambda i: (i, 0)
    const = lambda i: (0, 0)
    in_specs = [pl.BlockSpec((tm, d), row),
                pl.BlockSpec((tm, A_WIDTH), row),
                pl.BlockSpec((tm, B_WIDTH), row),
                pl.BlockSpec((A_WIDTH, d), const),
                pl.BlockSpec((B_WIDTH, d), const),
                pl.BlockSpec((1, d), const),
                pl.BlockSpec((d, LANES), const),
                pl.BlockSpec((1, LANES), const)]
    args = [x2d, oa, ob, wa, wb, n2w, wr, br]
    if sparse:
        idx = jnp.arange(tm)
        args.append((idx[:, None] < idx[None, :]).astype(BF16))
        in_specs.append(pl.BlockSpec((tm, tm), const))
        out_specs = (pl.BlockSpec((tm, d), row), pl.BlockSpec((tm, d // 2), row),
                     pl.BlockSpec((SUBLANES, tm), lambda i: (0, i)), pl.BlockSpec((N_EXPERTS, LANES), const))
        out_shape = (jax.ShapeDtypeStruct((m, d), F32), jax.ShapeDtypeStruct((m, d // 2), jnp.int32),
                     jax.ShapeDtypeStruct((SUBLANES, m), F32), jax.ShapeDtypeStruct((N_EXPERTS, LANES), F32))
    else:
        out_specs = (pl.BlockSpec((tm, d), row), pl.BlockSpec((tm, d), row), pl.BlockSpec((tm, LANES), row))
        out_shape = (jax.ShapeDtypeStruct((m, d), F32), jax.ShapeDtypeStruct((m, d), BF16),
                     jax.ShapeDtypeStruct((m, LANES), F32))
    return pl.pallas_call(
        functools.partial(_out_proj_kernel, sparse=sparse),
        grid=(m // tm,), in_specs=in_specs, out_specs=out_specs, out_shape=out_shape,
        compiler_params=_params(1), name="out_proj",
    )(*args)


def _moe_kernel(h_ref, n2_ref, gates_ref, wgu_ref, wd_ref, y_ref):
    e = pl.program_id(1)

    @pl.when(e == 0)
    def _():
        y_ref[...] = h_ref[...]

    n2 = n2_ref[...]
    gu = _dot(n2, wgu_ref[0].astype(BF16))
    act = _silu(gu[:, :D_EXPERT]) * gu[:, D_EXPERT:]
    lane = lax.broadcasted_iota(jnp.int32, gates_ref.shape, 1)
    gate = jnp.sum(jnp.where(lane == e, gates_ref[...], 0.0), axis=-1, keepdims=True)
    y_ref[...] += gate * _dot(act.astype(BF16), wd_ref[0].astype(BF16))


def _moe(h, n2, gates, wgu, wd, *, tm):
    m, d = h.shape
    row = lambda i, e: (i, 0)
    return pl.pallas_call(
        _moe_kernel,
        grid=(m // tm, N_EXPERTS),
        in_specs=[pl.BlockSpec((tm, d), row),
                  pl.BlockSpec((tm, d), row),
                  pl.BlockSpec((tm, LANES), row),
                  pl.BlockSpec((1, d, 2 * D_EXPERT), lambda i, e: (e, 0, 0)),
                  pl.BlockSpec((1, D_EXPERT, d), lambda i, e: (e, 0, 0))],
        out_specs=pl.BlockSpec((tm, d), row),
        out_shape=jax.ShapeDtypeStruct((m, d), F32),
        compiler_params=_params(2), name="moe",
    )(h, n2, gates, wgu, wd)


SC_CORES = 2
SC_SUBCORES = 16
SC_WORKERS = SC_CORES * SC_SUBCORES
SC_CHUNK = 64
MOE_TILE = 512


def _sc_mesh():
    return plsc.VectorSubcoreMesh(core_axis_name="c", subcore_axis_name="s",
                                  num_cores=SC_CORES, num_subcores=SC_SUBCORES)


def _sc_dispatch(src, pos, n_rows):
    m, d = src.shape
    n_chunks = m // (SC_WORKERS * SC_CHUNK)
    pos4 = pos.reshape(2, SC_WORKERS, n_chunks, SC_CHUNK).transpose(1, 0, 2, 3)

    def body(src_hbm, pos_hbm, out_hbm, idx_v, rows_v):
        wid = lax.axis_index("s") * SC_CORES + lax.axis_index("c")
        pltpu.sync_copy(pos_hbm.at[wid], idx_v)

        @pl.loop(0, n_chunks)
        def _(j):
            start = pl.multiple_of((wid * n_chunks + j) * SC_CHUNK, SC_CHUNK)
            pltpu.sync_copy(src_hbm.at[pl.ds(start, SC_CHUNK)], rows_v)
            pltpu.sync_copy(rows_v, out_hbm.at[idx_v.at[0, j]])
            pltpu.sync_copy(rows_v, out_hbm.at[idx_v.at[1, j]])

    return pl.kernel(
        body, out_type=jax.ShapeDtypeStruct((n_rows, d), src.dtype), mesh=_sc_mesh(),
        scratch_types=[pltpu.VMEM((2, n_chunks, SC_CHUNK), jnp.int32), pltpu.VMEM((SC_CHUNK, d), src.dtype)],
        name="moe_dispatch",
    )(src, pos4)


def _sc_gather(table, idx):
    b = idx.shape[0]
    d = table.shape[1]
    n_chunks = b // (SC_WORKERS * SC_CHUNK)
    idx3 = idx.reshape(SC_WORKERS, n_chunks, SC_CHUNK)

    def body(table_hbm, idx_hbm, out_hbm, idx_v, rows_v):
        wid = lax.axis_index("s") * SC_CORES + lax.axis_index("c")
        pltpu.sync_copy(idx_hbm.at[wid], idx_v)

        @pl.loop(0, n_chunks)
        def _(j):
            start = pl.multiple_of((wid * n_chunks + j) * SC_CHUNK, SC_CHUNK)
            pltpu.sync_copy(table_hbm.at[idx_v.at[j]], rows_v)
            pltpu.sync_copy(rows_v, out_hbm.at[pl.ds(start, SC_CHUNK)])

    return pl.kernel(
        body, out_type=jax.ShapeDtypeStruct((b, d), table.dtype), mesh=_sc_mesh(),
        scratch_types=[pltpu.VMEM((n_chunks, SC_CHUNK), jnp.int32), pltpu.VMEM((SC_CHUNK, d), table.dtype)],
        name="moe_collect",
    )(table, idx3)


def _moe_grouped_kernel(te_ref, tv_ref, xs_ref, wgu_ref, wd_ref, ys_ref, wgu16, wd16):
    i = pl.program_id(0)
    valid = tv_ref[i]

    @pl.when((i == 0) | (te_ref[i] != te_ref[jnp.maximum(i - 1, 0)]))
    def _():
        wgu16[...] = wgu_ref[0].astype(BF16)
        wd16[...] = wd_ref[0].astype(BF16)

    @pl.when(valid > 0)
    def _():
        rows = lax.broadcasted_iota(jnp.int32, xs_ref.shape, 0)
        xw = jnp.where(rows < valid, xs_ref[...], 0)
        x = _unpack_bf16_pairs(xw).astype(BF16)
        gu = _dot(x, wgu16[...])
        act = _silu(gu[:, :D_EXPERT]) * gu[:, D_EXPERT:]
        ys_ref[...] = _pack_bf16_pairs(_dot(act.astype(BF16), wd16[...]))

    @pl.when(valid == 0)
    def _():
        ys_ref[...] = jnp.zeros(ys_ref.shape, ys_ref.dtype)


def _moe_grouped(xs, tile_expert, tile_valid, wgu, wd):
    r, half = xs.shape
    d = 2 * half
    grid_spec = pltpu.PrefetchScalarGridSpec(
        num_scalar_prefetch=2, grid=(r // MOE_TILE,),
        in_specs=[pl.BlockSpec((MOE_TILE, half), lambda i, te, nu: (i, 0)),
                  pl.BlockSpec((1, d, 2 * D_EXPERT), lambda i, te, nu: (te[i], 0, 0)),
                  pl.BlockSpec((1, D_EXPERT, d), lambda i, te, nu: (te[i], 0, 0))],
        out_specs=pl.BlockSpec((MOE_TILE, half), lambda i, te, nu: (i, 0)),
        scratch_shapes=[pltpu.VMEM((d, 2 * D_EXPERT), BF16), pltpu.VMEM((D_EXPERT, d), BF16)])
    return pl.pallas_call(
        _moe_grouped_kernel, grid_spec=grid_spec,
        out_shape=jax.ShapeDtypeStruct((r, half), jnp.int32),
        compiler_params=_params(1), name="moe_grouped",
    )(tile_expert, tile_valid, xs, wgu, wd)


def _moe_combine_kernel(h_ref, z0_ref, z1_ref, rec_ref, y_ref):
    tm = h_ref.shape[0]
    rec = jnp.concatenate([rec_ref[...], jnp.zeros((LANES - SUBLANES, tm), F32)], axis=0).T
    y_ref[...] = (h_ref[...] + rec[:, 2:3] * _unpack_bf16_pairs(z0_ref[...])
                  + rec[:, 3:4] * _unpack_bf16_pairs(z1_ref[...]))


def _moe_combine(h, z, rec, *, tm):
    m, d = h.shape
    nt = m // tm
    row = lambda i: (i, 0)
    return pl.pallas_call(
        _moe_combine_kernel, grid=(nt,),
        in_specs=[pl.BlockSpec((tm, d), row),
                  pl.BlockSpec((tm, d // 2), row),
                  pl.BlockSpec((tm, d // 2), lambda i: (i + nt, 0)),
                  pl.BlockSpec((SUBLANES, tm), lambda i: (0, i))],
        out_specs=pl.BlockSpec((tm, d), row),
        out_shape=jax.ShapeDtypeStruct((m, d), F32),
        compiler_params=_params(1), name="moe_combine",
    )(h, z, z, rec)


def _moe_route(rec, cnt):
    m = rec.shape[1]
    n_tiles = 2 * m // MOE_TILE + N_EXPERTS
    counts = cnt[:, 0].astype(jnp.int32)
    tiles = (counts + MOE_TILE - 1) // MOE_TILE
    tile_end = jnp.cumsum(tiles)
    row_start = (tile_end - tiles) * MOE_TILE
    eid = rec[0:2].astype(jnp.int32)
    rank = rec[4:6].astype(jnp.int32)
    pos = rank
    for e in range(N_EXPERTS):
        pos = pos + jnp.where(eid == e, row_start[e], 0)
    tile = jnp.arange(n_tiles, dtype=jnp.int32)
    tile_expert = jnp.sum((tile_end[None, :] <= jnp.minimum(tile, tile_end[-1] - 1)[:, None]).astype(jnp.int32),
                          axis=1)
    row_end = jnp.sum(jnp.where(tile_expert[:, None] == jnp.arange(N_EXPERTS)[None, :],
                                (row_start + counts)[None, :], 0), axis=1)
    tile_valid = jnp.clip(row_end - tile * MOE_TILE, 0, MOE_TILE).astype(jnp.int32)
    return pos, tile_expert, tile_valid, n_tiles * MOE_TILE


def _head_rows(v):
    rows = jnp.concatenate([jnp.zeros((B_HEADS,), F32), v.astype(F32)])
    return jnp.broadcast_to(rows[:, None], (2 * B_HEADS, LANES))


def kernel(x_prompt, x_sample, cache_win_k, cache_win_v, state_conv, state_ssm, norm1_w, w_in, qnorm_w, knorm_w, conv_w, a_log, dt_bias, onorm_w, w_out, norm2_w, w_group, b_group, w_expert_router, b_expert_router, w_gate_up, w_down):
    nb, seq, d = x_prompt.shape
    db, t_len, _ = x_sample.shape
    n_buf = cache_win_k.shape[1]
    assert n_buf == MAX_WINDOW and seq % (MAX_DILATION * BAND) == 0 and t_len == SUBLANES
    keep = min(MAX_WINDOW, seq)

    c_bg = 3 * A_WIDTH + CONV_DIM + B_WIDTH
    w_cat = jnp.concatenate([w_in, jnp.zeros((d, LANES - 2 * B_HEADS), w_in.dtype)], axis=1).astype(BF16)
    assert w_cat.shape[1] == c_bg + LANES
    n1w = norm1_w.reshape(1, d).astype(F32)
    qw = jnp.tile(qnorm_w.astype(F32), A_HEADS).reshape(1, A_WIDTH)
    kw = jnp.tile(knorm_w.astype(F32), A_HEADS).reshape(1, A_WIDTH)
    a_log_v = _head_rows(a_log)
    dtb_v = _head_rows(dt_bias)
    onw = onorm_w.reshape(1, B_VAL_DIM).astype(F32)
    wa = w_out[:A_WIDTH].astype(BF16)
    wb = w_out[A_WIDTH:].astype(BF16)
    n2w = norm2_w.reshape(1, d).astype(F32)
    wr = jnp.concatenate([w_group, jnp.transpose(w_expert_router, (1, 0, 2)).reshape(d, N_EXPERTS),
                          jnp.zeros((d, LANES - N_GROUPS - N_EXPERTS), F32)], axis=1).astype(F32)
    br = jnp.zeros((1, LANES), F32).at[0, :N_GROUPS].set(b_group).at[0, N_GROUPS:N_GROUPS + N_EXPERTS].set(
        b_expert_router.reshape(-1))
    wgu = w_gate_up.astype(F32)
    wd = w_down.astype(F32)
    cw = conv_w.astype(F32)

    xp2d = x_prompt.reshape(nb * seq, d)
    qf, kf, vf, kwin, vwin, bqkv, z, bg = _in_proj(xp2d, n1w, w_cat, qw, kw, seq=seq, keep=keep,
                                                   fold=True, tm=512)
    oa = _attn_prompt(qf, kf, vf).reshape(nb * seq, A_WIDTH)
    conv0 = jnp.zeros((nb, CONV_WIDTH - 1, CONV_DIM), F32)
    ssm0 = jnp.zeros((nb, B_HEADS, B_KEY_DIM, B_VAL_DIM), F32)
    ob, ssm_prompt = _gdn_prompt(bqkv, z, bg, cw, conv0, ssm0, a_log_v, dtb_v, onw,
                                 seq=seq, ts=512, block=128, chunk=CHUNK)
    assert (nb * seq) % (SC_WORKERS * SC_CHUNK) == 0 and nb * seq >= N_EXPERTS * MOE_TILE
    h_p, n2p, rec, cnt = _out_proj(xp2d, oa, ob, wa, wb, n2w, wr, br, tm=512, sparse=True)
    pos, tile_expert, tile_valid, n_rows = _moe_route(rec, cnt)
    xs = _sc_dispatch(n2p, pos, n_rows)
    win_k_prompt = jnp.transpose(kwin, (0, 3, 1, 2))
    win_v_prompt = jnp.transpose(vwin, (0, 3, 1, 2))
    conv_prompt = bqkv.reshape(nb, seq, CONV_DIM)[:, seq - (CONV_WIDTH - 1):]

    ms = db * t_len
    xs2d = x_sample.reshape(ms, d)
    tms = min(512, ms)
    qs, ksn, vsn, _, _, bqkv_s, z_s, bg_s = _in_proj(xs2d, n1w, w_cat, qw, kw, seq=ms, keep=ms,
                                                     fold=False, tm=tms)
    cache_kt = jnp.transpose(cache_win_k.astype(F32), (0, 2, 3, 1))
    cache_vt = jnp.transpose(cache_win_v.astype(F32), (0, 2, 3, 1))
    half = db // 2
    oa_s0 = _attn_sample(qs, ksn, vsn, cache_kt, cache_vt, t_len=t_len, first=0, count=half)
    xs, oa_s0 = lax.optimization_barrier((xs, oa_s0))
    ys = _moe_grouped(xs, tile_expert, tile_valid, wgu, wd)
    ys, qs1 = lax.optimization_barrier((ys, qs))
    zs = _sc_gather(ys, pos.reshape(-1))
    oa_s1 = _attn_sample(qs1, ksn, vsn, cache_kt, cache_vt, t_len=t_len, first=half, count=db - half)
    oa_s = jnp.concatenate([oa_s0, oa_s1], axis=0)
    xpad = jnp.concatenate([state_conv.astype(F32), bqkv_s.reshape(db, t_len, CONV_DIM)], axis=1)
    nseq = 16 if db % 16 == 0 else 1
    ob_s, ssm_sample = _gdn_sample(xpad, z_s, bg_s, cw, state_ssm.astype(F32), a_log_v, dtb_v, onw,
                                   t_len=t_len, nseq=nseq)
    h_s, n2_s, gates_s = _out_proj(xs2d, oa_s, ob_s, wa, wb, n2w, wr, br, tm=tms, sparse=False)
    y_sample = _moe(h_s, n2_s, gates_s, wgu, wd, tm=ms)
    zs, y_sample = lax.optimization_barrier((zs, y_sample))
    y_prompt = _moe_combine(h_p, zs, rec, tm=MOE_TILE).reshape(nb, seq, d)
    y_sample = y_sample.reshape(db, t_len, d)
    win_k_sample = ksn.reshape(db, t_len, A_HEADS, A_HEAD_DIM)
    win_v_sample = vsn.reshape(db, t_len, A_HEADS, A_HEAD_DIM)
    conv_sample = xpad[:, t_len:]

    return (y_prompt, y_sample, win_k_prompt, win_v_prompt, conv_prompt, ssm_prompt,
            win_k_sample, win_v_sample, conv_sample, ssm_sample)
```

```python
import functools

import jax
import jax.numpy as jnp
from jax import lax
from jax.experimental import pallas as pl
from jax.experimental.pallas import tpu as pltpu
from jax.experimental.pallas import tpu_sc as plsc

F32 = jnp.float32
BF16 = jnp.bfloat16

A_HEADS = 8
A_HEAD_DIM = 64
A_WIDTH = A_HEADS * A_HEAD_DIM
DILATED_PATTERNS = ((128, 1), (512, 4), (2048, 16))
MAX_WINDOW = 2048
MAX_DILATION = 16
BAND = 128
ATTN_SCALE = A_HEAD_DIM ** -0.5
LOG2_E = 1.4426950408889634
B_HEADS = 4
B_KEY_DIM = 128
B_VAL_DIM = 128
B_WIDTH = B_HEADS * B_VAL_DIM
CONV_WIDTH = 4
CONV_DIM = B_HEADS * (2 * B_KEY_DIM + B_VAL_DIM)
CHUNK = 64
N_GROUPS = 4
EXPERTS_PER_GROUP = 4
N_EXPERTS = N_GROUPS * EXPERTS_PER_GROUP
D_EXPERT = 512
RMS_EPS = 1e-6
NEG_INF = -1e30

LANES = 128
SUBLANES = 8
VMEM_LIMIT = 56 * 1024 * 1024

PROJ_TILE = 512
GDN_TILE = 512
GDN_BLOCK = 128


def _params(n_axes, vmem=VMEM_LIMIT):
    return pltpu.CompilerParams(dimension_semantics=("arbitrary",) * n_axes, vmem_limit_bytes=vmem)


def _split2(x):
    hi = x.astype(BF16)
    lo = (x - hi.astype(F32)).astype(BF16)
    return hi, lo


def _split3(x):
    hi = x.astype(BF16)
    r = x - hi.astype(F32)
    mid = r.astype(BF16)
    lo = (r - mid.astype(F32)).astype(BF16)
    return hi, mid, lo


def _dot(a, b):
    return jnp.dot(a, b, preferred_element_type=F32)


def _dot_nt(a, b):
    return lax.dot_general(a, b, (((1,), (1,)), ((), ())), preferred_element_type=F32)


def _sigmoid(x):
    return 1.0 / (1.0 + jnp.exp2(x * -LOG2_E))


def _silu(x):
    return x * _sigmoid(x)


def _in_proj_kernel(x_ref, n1w_ref, w_ref, qw_ref, kw_ref,
                    q_ref, k_ref, v_ref, kwin_ref, vwin_ref, bqkv_ref, z_ref, bg_ref,
                    *scratch, fold):
    tm = x_ref.shape[0]
    x = x_ref[...]
    ms = jnp.mean(x * x, axis=-1, keepdims=True)
    n1 = (x * lax.rsqrt(ms + RMS_EPS) * n1w_ref[...]).astype(BF16)

    first_head = lax.broadcasted_iota(jnp.int32, (tm, LANES), 1) < A_HEAD_DIM

    def head_norm(t, w):
        cols = []
        for c in range(A_WIDTH // LANES):
            tc = t[:, c * LANES:(c + 1) * LANES]
            sq = tc * tc
            s0 = jnp.sum(jnp.where(first_head, sq, 0.0), axis=-1, keepdims=True)
            s1 = jnp.sum(jnp.where(first_head, 0.0, sq), axis=-1, keepdims=True)
            ms = jnp.where(first_head, s0, s1) * (1.0 / A_HEAD_DIM)
            cols.append(tc * lax.rsqrt(ms + RMS_EPS))
        return jnp.concatenate(cols, axis=1) * w

    q_scale = ATTN_SCALE * LOG2_E if fold else ATTN_SCALE
    q = head_norm(_dot(n1, w_ref[:, 0:A_WIDTH]), qw_ref[...]) * q_scale
    k = head_norm(_dot(n1, w_ref[:, A_WIDTH:2 * A_WIDTH]), kw_ref[...])
    v = _dot(n1, w_ref[:, 2 * A_WIDTH:3 * A_WIDTH])
    if fold:
        kwin_ref[0] = k.T.reshape(A_HEADS, A_HEAD_DIM, tm)
        vwin_ref[0] = v.T.reshape(A_HEADS, A_HEAD_DIM, tm)

        (scr,) = scratch
        rows = tm // MAX_DILATION
        for val, out in ((q, q_ref), (k, k_ref), (v, v_ref)):
            for c in range(A_WIDTH // LANES):
                cs = slice(c * LANES, (c + 1) * LANES)
                scr[c] = val[:, cs]
                for r in range(MAX_DILATION):
                    out[0, r, :, cs] = scr[c, pl.ds(r, rows, stride=MAX_DILATION), :]
    else:
        kwin_ref[...] = k
        vwin_ref[...] = v
        q_ref[...] = q
        k_ref[...] = k
        v_ref[...] = v
    c0 = 3 * A_WIDTH
    bqkv_ref[...] = _dot(n1, w_ref[:, c0:c0 + CONV_DIM])
    c1 = c0 + CONV_DIM
    z_ref[...] = _dot(n1, w_ref[:, c1:c1 + B_WIDTH]).astype(BF16)
    c2 = c1 + B_WIDTH
    bg_ref[...] = _dot(n1, w_ref[:, c2:c2 + LANES])


def _in_proj(x2d, n1w, w_cat, qw, kw, *, seq, keep, fold, tm):
    m, d = x2d.shape
    nb = m // seq
    tiles_per_seq = seq // tm
    skip = (seq - keep) // tm
    keep_tiles = keep // tm
    n_cols = w_cat.shape[1]

    def win_map(i):
        return (i // tiles_per_seq) * keep_tiles + jnp.maximum(i % tiles_per_seq - skip, 0), 0

    row = lambda i: (i, 0)
    const = lambda i: (0, 0)
    if fold:
        rows = tm // MAX_DILATION
        qkv_shape = jax.ShapeDtypeStruct((nb, MAX_DILATION, seq // MAX_DILATION, A_WIDTH), F32)
        qkv_spec = pl.BlockSpec((1, MAX_DILATION, rows, A_WIDTH),
                                lambda i: (i // tiles_per_seq, 0, i % tiles_per_seq, 0))
        scratch = [pltpu.VMEM((A_WIDTH // LANES, tm, LANES), F32)]
        win_shape = jax.ShapeDtypeStruct((nb, A_HEADS, A_HEAD_DIM, keep), F32)
        win_spec = pl.BlockSpec((1, A_HEADS, A_HEAD_DIM, tm),
                                lambda i: (i // tiles_per_seq, 0, 0, jnp.maximum(i % tiles_per_seq - skip, 0)))
    else:
        qkv_shape = jax.ShapeDtypeStruct((m, A_WIDTH), F32)
        qkv_spec = pl.BlockSpec((tm, A_WIDTH), row)
        scratch = []
        win_shape = jax.ShapeDtypeStruct((nb * keep, A_WIDTH), F32)
        win_spec = pl.BlockSpec((tm, A_WIDTH), win_map)
    out_shape = (qkv_shape, qkv_shape, qkv_shape, win_shape, win_shape,
                 jax.ShapeDtypeStruct((m, CONV_DIM), F32),
                 jax.ShapeDtypeStruct((m, B_WIDTH), BF16),
                 jax.ShapeDtypeStruct((m, LANES), F32))
    out_specs = (qkv_spec, qkv_spec, qkv_spec, win_spec, win_spec,
                 pl.BlockSpec((tm, CONV_DIM), row),
                 pl.BlockSpec((tm, B_WIDTH), row),
                 pl.BlockSpec((tm, LANES), row))
    in_specs = [pl.BlockSpec((tm, d), row),
                pl.BlockSpec((1, d), const),
                pl.BlockSpec((d, n_cols), const),
                pl.BlockSpec((1, A_WIDTH), const),
                pl.BlockSpec((1, A_WIDTH), const)]
    return pl.pallas_call(
        functools.partial(_in_proj_kernel, fold=fold),
        grid=(m // tm,), in_specs=in_specs, out_specs=out_specs, out_shape=out_shape,
        scratch_shapes=scratch, compiler_params=_params(1), name="in_proj",
    )(x2d, n1w, w_cat, qw, kw)


def _band_mask(pieces):
    sub = BAND // pieces
    r = lax.broadcasted_iota(jnp.int32, (BAND, 2 * BAND), 0)
    c = lax.broadcasted_iota(jnp.int32, (BAND, 2 * BAND), 1)
    qpos = (r % sub) * pieces + r // sub + BAND
    cc = c % BAND
    kpos = (cc % sub) * pieces + cc // sub + (c // BAND) * BAND
    dist = qpos - kpos
    return (dist >= 0) & (dist <= BAND), c >= BAND


ATTN_UNROLL = 16


def _attn_prompt_kernel(q_ref, k_ref, v_ref, o_ref, p_scr, op_scr, mx_scr, l_scr, nat_scr):
    u_len = q_ref.shape[2]
    lane = lax.broadcasted_iota(jnp.int32, (BAND, LANES), 1)
    head0 = lane < A_HEAD_DIM

    def per_head(col):
        return jnp.where(head0, jnp.broadcast_to(col[:BAND], (BAND, LANES)),
                         jnp.broadcast_to(col[BAND:], (BAND, LANES)))

    for pi, pieces in enumerate((16, 4, 1)):
        sub = BAND // pieces
        n_res = MAX_DILATION // pieces
        n_blk = u_len // sub
        band, _ = _band_mask(pieces)
        bias = jnp.where(band, 0.0, NEG_INF)
        bias2 = jnp.concatenate([bias, bias], axis=0)
        bias2_cur = bias2[:, BAND:]

        def locate(blk, n_res=n_res, sub=sub):
            m = blk // n_res
            cur = pl.multiple_of(m * sub, SUBLANES)
            prv = pl.multiple_of(jnp.maximum(m - 1, 0) * sub, SUBLANES)
            return blk % n_res, cur, prv

        def gather(ref, res, start, pieces=pieces, n_res=n_res, sub=sub):
            parts = [ref[0, a * n_res + res, pl.ds(start, sub), :] for a in range(pieces)]
            return parts[0] if pieces == 1 else jnp.concatenate(parts, axis=0)

        def scatter(ref, res, start, val, pi=pi, pieces=pieces, n_res=n_res, sub=sub):
            for a in range(pieces):
                ref[pi, a * n_res + res, pl.ds(start, sub), :] = val[a * sub:(a + 1) * sub]

        def probs(it, carry, first, bias2=bias2, bias2_cur=bias2_cur):
            for j, is_first in enumerate(first):
                blk = it * ATTN_UNROLL + j
                res, cur, prv = locate(blk)
                qb = gather(q_ref, res, cur)
                q2 = jnp.concatenate([jnp.where(head0, qb, 0.0), jnp.where(head0, 0.0, qb)], axis=0).astype(BF16)
                kc = gather(k_ref, res, cur)
                if is_first:
                    s = _dot_nt(q2, kc.astype(BF16)) + bias2_cur
                else:
                    kb = jnp.concatenate([gather(k_ref, res, prv), kc], axis=0).astype(BF16)
                    s = _dot_nt(q2, kb) + bias2
                mx = jnp.max(s, axis=-1, keepdims=True)
                p = jnp.exp2(s - mx)
                if is_first:
                    p_scr[blk, :, BAND:] = p.astype(BF16)
                else:
                    p_scr[blk] = p.astype(BF16)
                scatter(mx_scr, res, cur, per_head(mx))
            return carry

        def values(it, carry, first):
            for j, is_first in enumerate(first):
                blk = it * ATTN_UNROLL + j
                res, cur, prv = locate(blk)
                vc = gather(v_ref, res, cur)
                if is_first:
                    p = p_scr[blk, :, BAND:]
                    vb = vc.astype(BF16)
                else:
                    p = p_scr[blk]
                    vb = jnp.concatenate([gather(v_ref, res, prv), vc], axis=0).astype(BF16)
                o2 = _dot(p, jnp.concatenate([vb, jnp.ones(vb.shape, BF16)], axis=1))
                scatter(op_scr, res, cur, jnp.where(head0, o2[:BAND, :LANES], o2[BAND:, :LANES]))
                scatter(l_scr, res, cur, jnp.where(head0, o2[:BAND, LANES:], o2[BAND:, LANES:]))
            return carry

        n_it = n_res * n_blk // ATTN_UNROLL
        flags = [tuple(g * ATTN_UNROLL + j < n_res for j in range(ATTN_UNROLL)) for g in range(n_it)]
        segments = []
        for g, f in enumerate(flags):
            if segments and segments[-1][2] == f:
                segments[-1][1] = g + 1
            else:
                segments.append([g, g + 1, f])
        for phase in (probs, values):
            for lo, hi, f in segments:
                lax.fori_loop(lo, hi, functools.partial(phase, first=f), 0)

    for r in range(MAX_DILATION):
        m0, m1, m2 = mx_scr[0, r], mx_scr[1, r], mx_scr[2, r]
        big = jnp.maximum(jnp.maximum(m0, m1), m2)
        e0, e1, e2 = jnp.exp2(m0 - big), jnp.exp2(m1 - big), jnp.exp2(m2 - big)
        num = e0 * op_scr[0, r] + e1 * op_scr[1, r] + e2 * op_scr[2, r]
        den = e0 * l_scr[0, r] + e1 * l_scr[1, r] + e2 * l_scr[2, r]
        nat_scr[pl.ds(r, u_len, stride=MAX_DILATION), :] = num / den
    o_ref[0] = nat_scr[...].astype(BF16)


def _attn_prompt(qf, kf, vf):
    nb, _, u_len, _ = qf.shape
    seq = u_len * MAX_DILATION
    n_pairs = A_WIDTH // LANES
    spec = pl.BlockSpec((1, MAX_DILATION, u_len, LANES), lambda b, p: (b, 0, 0, p))
    return pl.pallas_call(
        _attn_prompt_kernel,
        grid=(nb, n_pairs),
        in_specs=[spec, spec, spec],
        out_specs=pl.BlockSpec((1, seq, LANES), lambda b, p: (b, 0, p)),
        out_shape=jax.ShapeDtypeStruct((nb, seq, A_WIDTH), BF16),
        scratch_shapes=[pltpu.VMEM((seq // BAND, 2 * BAND, 2 * BAND), BF16),
                        pltpu.VMEM((3, MAX_DILATION, u_len, LANES), F32),
                        pltpu.VMEM((3, MAX_DILATION, u_len, LANES), F32),
                        pltpu.VMEM((3, MAX_DILATION, u_len, LANES), F32),
                        pltpu.VMEM((seq, LANES), F32)],
        compiler_params=_params(2), name="attn_prompt",
    )(qf, kf, vf)


def _multiplicity(dd):
    cnt = jnp.zeros(dd.shape, F32)
    for w, d in DILATED_PATTERNS:
        ok = (dd >= 0) & (dd <= w) & (dd % d == 0)
        cnt = cnt + jnp.where(ok, 1.0, 0.0)
    return cnt


def _attn_sample_kernel(q_ref, kn_ref, vn_ref, kt_ref, vt_ref, o_ref):
    t_len = q_ref.shape[0]
    n_buf = kt_ref.shape[3]
    q = q_ref[...]
    kn = kn_ref[...]
    vn = vn_ref[...]
    tq = n_buf + lax.broadcasted_iota(jnp.int32, (t_len, n_buf), 0)
    cnt = _multiplicity(tq - lax.broadcasted_iota(jnp.int32, (t_len, n_buf), 1))
    cnt_new = _multiplicity(lax.broadcasted_iota(jnp.int32, (t_len, t_len), 0)
                            - lax.broadcasted_iota(jnp.int32, (t_len, t_len), 1))
    outs = []
    for h in range(A_HEADS):
        hs = slice(h * A_HEAD_DIM, (h + 1) * A_HEAD_DIM)
        qh = q[:, hs].astype(BF16)
        s = jnp.where(cnt > 0, _dot(qh, kt_ref[0, h].astype(BF16)), NEG_INF)
        sn = jnp.where(cnt_new > 0, _dot_nt(qh, kn[:, hs].astype(BF16)), NEG_INF)
        m = jnp.maximum(jnp.max(s, axis=-1, keepdims=True), jnp.max(sn, axis=-1, keepdims=True))
        p = cnt * jnp.exp(s - m)
        pn = cnt_new * jnp.exp(sn - m)
        l = jnp.sum(p, axis=-1, keepdims=True) + jnp.sum(pn, axis=-1, keepdims=True)
        o = _dot_nt(p.astype(BF16), vt_ref[0, h].astype(BF16)) + _dot(pn.astype(BF16), vn[:, hs].astype(BF16))
        outs.append(o / l)
    o_ref[...] = jnp.concatenate(outs, axis=-1)


def _attn_sample(q2d, kn2d, vn2d, cache_kt, cache_vt, *, t_len):
    nb, _, _, n_buf = cache_kt.shape
    tok = pl.BlockSpec((t_len, A_WIDTH), lambda b: (b, 0))
    win = pl.BlockSpec((1, A_HEADS, A_HEAD_DIM, n_buf), lambda b: (b, 0, 0, 0))
    return pl.pallas_call(
        _attn_sample_kernel,
        grid=(nb,),
        in_specs=[tok, tok, tok, win, win],
        out_specs=tok,
        out_shape=jax.ShapeDtypeStruct((nb * t_len, A_WIDTH), F32),
        compiler_params=_params(1), name="attn_sample",
    )(q2d, kn2d, vn2d, cache_kt, cache_vt)


def _gdn_prep(c, bg, a_col, dt_col, chunk):
    n = c.shape[0]
    bgt = bg.T[0:2 * B_HEADS]
    row = lax.broadcasted_iota(jnp.int32, bgt.shape, 0)
    xg = bgt + dt_col
    sp = jnp.maximum(xg, 0.0) + jnp.log1p(jnp.exp(-jnp.abs(xg)))
    gt = jnp.where(row >= B_HEADS, -a_col * sp, 0.0)
    ri = lax.broadcasted_iota(jnp.int32, (n, n), 0)
    ci = lax.broadcasted_iota(jnp.int32, (n, n), 1)
    tri = jnp.where(((ri // chunk) == (ci // chunk)) & (ri >= ci), 1.0, 0.0).astype(BF16)
    th, tm_, tl = _split3(gt)
    drow = _dot_nt(th, tri) + _dot_nt(tm_, tri) + _dot_nt(tl, tri)
    both = jnp.where(row >= B_HEADS, drow, _sigmoid(bgt))
    cols = jnp.concatenate([both, jnp.zeros((LANES - 2 * B_HEADS, n), F32)], axis=0).T
    beta_all = cols
    dcol = cols
    heads = []
    kd_ = B_KEY_DIM
    for h in range(B_HEADS):
        qh = c[:, h * kd_:(h + 1) * kd_]
        kh = c[:, B_HEADS * kd_ + h * kd_:B_HEADS * kd_ + (h + 1) * kd_]
        vh = c[:, 2 * B_HEADS * kd_ + h * B_VAL_DIM:2 * B_HEADS * kd_ + (h + 1) * B_VAL_DIM]
        qh = qh * lax.rsqrt(jnp.sum(qh * qh, axis=-1, keepdims=True) + RMS_EPS) * (B_KEY_DIM ** -0.5)
        kh = kh * lax.rsqrt(jnp.sum(kh * kh, axis=-1, keepdims=True) + RMS_EPS)
        heads.append((qh, kh, vh, beta_all[:, h:h + 1], dcol[:, B_HEADS + h:B_HEADS + h + 1],
                      drow[B_HEADS + h:B_HEADS + h + 1, :]))
    return heads


def _gdn_out(o, onw, zh):
    ms = jnp.mean(o * o, axis=-1, keepdims=True)
    return o * lax.rsqrt(ms + RMS_EPS) * onw * _silu(zh)


def _gdn_prompt_kernel(x_ref, z_ref, bg_ref, cw_ref, cst_ref, sst_ref, avec_ref, dtb_ref, onw_ref,
                       o_ref, snew_ref, xp_scr, s_scr, *, block, chunk):
    ts = x_ref.shape[0]
    t = pl.program_id(1)
    pad = SUBLANES
    hist = CONV_WIDTH - 1

    @pl.when(t == 0)
    def _():
        xp_scr[...] = jnp.zeros((pad, CONV_DIM), F32)
        xp_scr[pad - hist:pad, :] = cst_ref[0]
        s_scr[...] = sst_ref[0]

    assert CONV_WIDTH == 4
    x = x_ref[...]
    tail = xp_scr[...]
    sub = lax.broadcasted_iota(jnp.int32, (pad, CONV_DIM), 0)

    def shift(cur, prev_tail, s):
        rolled = pltpu.roll(cur, s, axis=0)
        head = jnp.where(sub < s, pltpu.roll(prev_tail, s, axis=0), rolled[0:pad])
        return jnp.concatenate([head, rolled[pad:]], axis=0)

    w0, w1, w2, w3 = (cw_ref[i:i + 1, :] for i in range(CONV_WIDTH))
    x1 = shift(x, tail, 1)
    a = x * w1 + x1 * w0
    a_tail = tail * w1 + pltpu.roll(tail, 1, axis=0) * w0
    y = x * w3 + x1 * w2 + shift(a, a_tail, 2)
    xp_scr[...] = x[ts - pad:ts]
    c = _silu(y)
    avec = jnp.exp(avec_ref[...])
    onw = onw_ref[...]
    n_chunks = block // chunk
    n_sb = ts // block
    ri = lax.broadcasted_iota(jnp.int32, (block, block), 0)
    ci = lax.broadcasted_iota(jnp.int32, (block, block), 1)
    same = (ri // chunk) == (ci // chunk)
    lower = same & (ri >= ci)
    strict = same & (ri > ci)
    eye = jnp.where(ri == ci, 1.0, 0.0).astype(F32)

    s = [s_scr[h] for h in range(B_HEADS)]
    for g0 in range(0, n_sb, GDN_GROUP):
        units = []
        for sb in range(g0, g0 + GDN_GROUP):
            rs = slice(sb * block, (sb + 1) * block)
            heads = _gdn_prep(c[rs], bg_ref[rs, :], avec, dtb_ref[...], chunk)
            for h, (qh, kh, vh, beta, dcol, drow) in enumerate(heads):
                units.append(dict(sb=sb, h=h, q=qh, k=kh, v=vh, beta=beta, dcol=dcol, drow=drow))
        s = _gdn_group(units, s, z_ref, o_ref, onw, lower, strict, eye, block, chunk)
    for h in range(B_HEADS):
        s_scr[h] = s[h]

    @pl.when(t == pl.num_programs(1) - 1)
    def _():
        snew_ref[0] = s_scr[...]


GDN_GROUP = 4


def _gdn_group(units, s, z_ref, o_ref, onw, lower, strict, eye, block, chunk, state_refs=None):
    n_chunks = block // chunk
    for u in units:
        gam = jnp.exp(jnp.where(lower, u["dcol"] - u["drow"], NEG_INF))
        kb = u["k"] * u["beta"]
        kq = _dot_nt(jnp.concatenate([kb, u["q"]], axis=0).astype(BF16), u["k"].astype(BF16))
        u["pw"] = jnp.where(strict, kq[:block] * gam, 0.0)
        u["t"] = eye - u["pw"]
        u["attn"] = jnp.where(lower, kq[block:] * gam, 0.0).astype(BF16)
        ed = jnp.exp(u["dcol"])
        u["rhs"] = jnp.concatenate([u["v"] * u["beta"], kb * ed], axis=1).astype(BF16)
        u["qd"] = u["q"] * ed
        u["kt"] = u["k"].T

    kk = 2
    while kk < chunk:
        for u in units:
            p16 = u["pw"].astype(BF16)
            u["pw"] = _dot(p16, p16)
        for u in units:
            u["t"] = u["t"] + _dot(u["t"].astype(BF16), u["pw"].astype(BF16))
        kk *= 2

    aligned = chunk % (2 * SUBLANES) == 0

    def chunk_rows(u, name, cs):
        return u[name + "16"][cs] if aligned else u[name][cs].astype(BF16)

    for u in units:
        u["uw"] = _dot(u["t"].astype(BF16), u["rhs"])
        u["uw16"] = u["uw"].astype(BF16)
    for u in units:
        au_aw = _dot(u["attn"], u["uw16"])
        u["op"] = au_aw[:, :B_VAL_DIM]
        u["qp"] = u["qd"] - au_aw[:, B_VAL_DIM:]
        u["qp16"] = u["qp"].astype(BF16)
    for u in units:
        u["n"], u["mw"], u["e"] = [], [], []
        for cc in range(n_chunks):
            cs = slice(cc * chunk, (cc + 1) * chunk)
            last = u["drow"][:, (cc + 1) * chunk - 1:(cc + 1) * chunk]
            kdt = (u["kt"][:, cs] * jnp.exp(last - u["drow"][:, cs])).astype(BF16)
            nm = _dot(kdt, chunk_rows(u, "uw", cs))
            u["n"].append(nm[:, :B_VAL_DIM])
            u["mw"].append(nm[:, B_VAL_DIM:].astype(BF16))
            u["e"].append(jnp.exp(last))

    s = None if s is None else list(s)
    for b0 in range(0, len(units), B_HEADS):
        sb = units[b0]["sb"]
        rs = slice(sb * block, (sb + 1) * block)
        outs = [[] for _ in range(B_HEADS)]
        for cc in range(n_chunks):
            cs = slice(cc * chunk, (cc + 1) * chunk)
            for h in range(B_HEADS):
                u = units[b0 + h]
                s_in = s[h] if state_refs is None else state_refs[0][cc, h]
                s16 = s_in.astype(BF16)
                if aligned:
                    r = _dot(jnp.concatenate([u["mw"][cc], chunk_rows(u, "qp", cs)], axis=0), s16)
                    r_m, r_q = r[:B_KEY_DIM], r[B_KEY_DIM:]
                else:
                    r_m, r_q = _dot(u["mw"][cc], s16), _dot(chunk_rows(u, "qp", cs), s16)
                outs[h].append(r_q + u["op"][cs])
                s_out = s_in * u["e"][cc] - r_m + u["n"][cc]
                if state_refs is None:
                    s[h] = s_out
                else:
                    state_refs[1][cc, h] = s_out
        for h in range(B_HEADS):
            o_all = jnp.concatenate(outs[h], axis=0) if n_chunks > 1 else outs[h][0]
            zh = z_ref[rs, h * B_VAL_DIM:(h + 1) * B_VAL_DIM].astype(F32)
            o_ref[rs, h * B_VAL_DIM:(h + 1) * B_VAL_DIM] = _gdn_out(o_all, onw, zh).astype(o_ref.dtype)
    return s


def _gdn_prompt(bqkv, z, bg, conv_w, conv_state, ssm_state, a_log_v, dtb_v, onw, *, seq, ts, block, chunk):
    m = bqkv.shape[0]
    nb = m // seq
    tps = seq // ts
    row = lambda b, t: (b * tps + t, 0)
    const = lambda b, t: (0, 0)
    return pl.pallas_call(
        functools.partial(_gdn_prompt_kernel, block=block, chunk=chunk),
        grid=(nb, tps),
        in_specs=[pl.BlockSpec((ts, CONV_DIM), row),
                  pl.BlockSpec((ts, B_WIDTH), row),
                  pl.BlockSpec((ts, LANES), row),
                  pl.BlockSpec((CONV_WIDTH, CONV_DIM), const),
                  pl.BlockSpec((1, CONV_WIDTH - 1, CONV_DIM), lambda b, t: (b, 0, 0)),
                  pl.BlockSpec((1, B_HEADS, B_KEY_DIM, B_VAL_DIM), lambda b, t: (b, 0, 0, 0)),
                  pl.BlockSpec((2 * B_HEADS, LANES), const),
                  pl.BlockSpec((2 * B_HEADS, LANES), const),
                  pl.BlockSpec((1, B_VAL_DIM), const)],
        out_specs=(pl.BlockSpec((ts, B_WIDTH), row),
                   pl.BlockSpec((1, B_HEADS, B_KEY_DIM, B_VAL_DIM), lambda b, t: (b, 0, 0, 0))),
        out_shape=(jax.ShapeDtypeStruct((m, B_WIDTH), BF16),
                   jax.ShapeDtypeStruct(ssm_state.shape, F32)),
        scratch_shapes=[pltpu.VMEM((SUBLANES, CONV_DIM), F32),
                        pltpu.VMEM((B_HEADS, B_KEY_DIM, B_VAL_DIM), F32)],
        compiler_params=_params(2), name="gdn_prompt",
    )(bqkv, z, bg, conv_w, conv_state, ssm_state, a_log_v, dtb_v, onw)


def _gdn_sample_kernel(xp_ref, z_ref, bg_ref, cw_ref, sst_ref, avec_ref, dtb_ref, onw_ref,
                       o_ref, snew_ref, *, t_len):
    nseq = xp_ref.shape[0]
    hist = CONV_WIDTH - 1
    ys = []
    for j in range(nseq):
        y = xp_ref[j, 0:t_len, :] * cw_ref[0:1, :]
        for i in range(1, CONV_WIDTH):
            y = y + xp_ref[j, i:i + t_len, :] * cw_ref[i:i + 1, :]
        ys.append(y)
    c = _silu(jnp.concatenate(ys, axis=0))
    avec = jnp.exp(avec_ref[...])
    onw = onw_ref[...]
    block = nseq * t_len
    ri = lax.broadcasted_iota(jnp.int32, (block, block), 0)
    ci = lax.broadcasted_iota(jnp.int32, (block, block), 1)
    same = (ri // t_len) == (ci // t_len)
    heads = _gdn_prep(c, bg_ref[...], avec, dtb_ref[...], t_len)
    units = [dict(sb=0, h=h, q=qh, k=kh, v=vh, beta=beta, dcol=dcol, drow=drow)
             for h, (qh, kh, vh, beta, dcol, drow) in enumerate(heads)]
    _gdn_group(units, None, z_ref, o_ref, onw, same & (ri >= ci), same & (ri > ci),
               jnp.where(ri == ci, 1.0, 0.0).astype(F32), block, t_len, state_refs=(sst_ref, snew_ref))


def _gdn_sample(xp, z, bg, conv_w, ssm_state, a_log_v, dtb_v, onw, *, t_len, nseq):
    nb = xp.shape[0]
    rows = nseq * t_len
    row = lambda i: (i, 0)
    const = lambda i: (0, 0)
    return pl.pallas_call(
        functools.partial(_gdn_sample_kernel, t_len=t_len),
        grid=(nb // nseq,),
        in_specs=[pl.BlockSpec((nseq, xp.shape[1], CONV_DIM), lambda i: (i, 0, 0)),
                  pl.BlockSpec((rows, B_WIDTH), row),
                  pl.BlockSpec((rows, LANES), row),
                  pl.BlockSpec((CONV_WIDTH, CONV_DIM), const),
                  pl.BlockSpec((nseq, B_HEADS, B_KEY_DIM, B_VAL_DIM), lambda i: (i, 0, 0, 0)),
                  pl.BlockSpec((2 * B_HEADS, LANES), const),
                  pl.BlockSpec((2 * B_HEADS, LANES), const),
                  pl.BlockSpec((1, B_VAL_DIM), const)],
        out_specs=(pl.BlockSpec((rows, B_WIDTH), row),
                   pl.BlockSpec((nseq, B_HEADS, B_KEY_DIM, B_VAL_DIM), lambda i: (i, 0, 0, 0))),
        out_shape=(jax.ShapeDtypeStruct((nb * t_len, B_WIDTH), BF16),
                   jax.ShapeDtypeStruct(ssm_state.shape, F32)),
        compiler_params=_params(1), name="gdn_sample",
    )(xp, z, bg, conv_w, ssm_state, a_log_v, dtb_v, onw)


def _pack_bf16_pairs(x):
    n = x.shape[1] // 2
    bits = pltpu.bitcast(x.astype(BF16).astype(F32), jnp.int32)
    return lax.shift_right_logical(bits[:, :n], 16) | (bits[:, n:] & jnp.int32(-65536))


def _unpack_bf16_pairs(w):
    lo = pltpu.bitcast(lax.shift_left(w, 16), F32)
    hi = pltpu.bitcast(w & jnp.int32(-65536), F32)
    return jnp.concatenate([lo, hi], axis=1)


def _out_proj_kernel(x_ref, oa_ref, ob_ref, wa_ref, wb_ref, n2w_ref, wr_ref, br_ref, *rest, sparse):
    if sparse:
        tri_ref, h_ref, n2_ref, gates_ref, cnt_ref = rest
    else:
        h_ref, n2_ref, gates_ref = rest
    h = x_ref[...] + _dot(oa_ref[...].astype(BF16), wa_ref[...]) + _dot(ob_ref[...].astype(BF16), wb_ref[...])
    h_ref[...] = h
    ms = jnp.mean(h * h, axis=-1, keepdims=True)
    n2 = h * lax.rsqrt(ms + RMS_EPS) * n2w_ref[...]
    if sparse:
        n2_ref[...] = _pack_bf16_pairs(n2)
    else:
        n2_ref[...] = n2.astype(BF16)
    nh, nl = _split2(n2)
    wh, wl = _split2(wr_ref[...])
    tm = n2.shape[0]
    parts = _dot(jnp.concatenate([nh, nl], axis=0), jnp.concatenate([wh, wl], axis=1))
    logits = parts[:tm, :LANES] + parts[:tm, LANES:] + parts[tm:, :LANES] + br_ref[...]
    n_rows = 3 * SUBLANES
    lt = logits.T[0:n_rows]
    row = lax.broadcasted_iota(jnp.int32, lt.shape, 0).astype(F32)
    big = 1e9
    gl = jnp.where(row < N_GROUPS, lt, NEG_INF)
    gmax = jnp.max(gl, axis=0, keepdims=True)
    gi = jnp.min(jnp.where(gl == gmax, row, big), axis=0, keepdims=True)
    g_sel = 1.0 / jnp.sum(jnp.exp(gl - gmax), axis=0, keepdims=True)
    lo = N_GROUPS + EXPERTS_PER_GROUP * gi
    el = jnp.where((row >= lo) & (row < lo + EXPERTS_PER_GROUP), lt, NEG_INF)
    v1 = jnp.max(el, axis=0, keepdims=True)
    i1 = jnp.min(jnp.where(el == v1, row, big), axis=0, keepdims=True)
    el2 = jnp.where(row == i1, NEG_INF, el)
    v2 = jnp.max(el2, axis=0, keepdims=True)
    i2 = jnp.min(jnp.where(el2 == v2, row, big), axis=0, keepdims=True)
    e2 = jnp.exp(v2 - v1)
    w1 = g_sel / (1.0 + e2)
    w2 = g_sel * e2 / (1.0 + e2)
    e0 = i1 - N_GROUPS
    e1 = i2 - N_GROUPS
    erow = lax.broadcasted_iota(jnp.int32, (N_EXPERTS, tm), 0).astype(F32)
    if not sparse:
        gt = jnp.where(erow == e0, w1, 0.0) + jnp.where(erow == e1, w2, 0.0)
        gates_ref[...] = jnp.concatenate([gt, jnp.zeros((LANES - N_EXPERTS, tm), F32)], axis=0).T
        return
    @pl.when(pl.program_id(0) == 0)
    def _():
        cnt_ref[...] = jnp.zeros(cnt_ref.shape, F32)

    cnt = cnt_ref[...]
    onehot = jnp.where(erow == e0, 1.0, 0.0) + jnp.where(erow == e1, 1.0, 0.0)
    before = _dot(onehot.astype(BF16), tri_ref[...]) + cnt[:, 0:1]
    r0 = jnp.sum(jnp.where(erow == e0, before, 0.0), axis=0, keepdims=True)
    r1 = jnp.sum(jnp.where(erow == e1, before, 0.0), axis=0, keepdims=True)
    cnt_ref[...] = cnt + jnp.sum(onehot, axis=1, keepdims=True)
    gates_ref[...] = jnp.concatenate([e0, e1, w1, w2, r0, r1, jnp.zeros((2, tm), F32)], axis=0)


def _out_proj(x2d, oa, ob, wa, wb, n2w, wr, br, *, tm, sparse):
    m, d = x2d.shape
    row = lambda i: (i, 0)
    const = lambda i: (0, 0)
    in_specs = [pl.BlockSpec((tm, d), row),
                pl.BlockSpec((tm, A_WIDTH), row),
                pl.BlockSpec((tm, B_WIDTH), row),
                pl.BlockSpec((A_WIDTH, d), const),
                pl.BlockSpec((B_WIDTH, d), const),
                pl.BlockSpec((1, d), const),
                pl.BlockSpec((d, LANES), const),
                pl.BlockSpec((1, LANES), const)]
    args = [x2d, oa, ob, wa, wb, n2w, wr, br]
    if sparse:
        idx = jnp.arange(tm)
        args.append((idx[:, None] < idx[None, :]).astype(BF16))
        in_specs.append(pl.BlockSpec((tm, tm), const))
        out_specs = (pl.BlockSpec((tm, d), row), pl.BlockSpec((tm, d // 2), row),
                     pl.BlockSpec((SUBLANES, tm), lambda i: (0, i)), pl.BlockSpec((N_EXPERTS, LANES), const))
        out_shape = (jax.ShapeDtypeStruct((m, d), F32), jax.ShapeDtypeStruct((m, d // 2), jnp.int32),
                     jax.ShapeDtypeStruct((SUBLANES, m), F32), jax.ShapeDtypeStruct((N_EXPERTS, LANES), F32))
    else:
        out_specs = (pl.BlockSpec((tm, d), row), pl.BlockSpec((tm, d), row), pl.BlockSpec((tm, LANES), row))
        out_shape = (jax.ShapeDtypeStruct((m, d), F32), jax.ShapeDtypeStruct((m, d), BF16),
                     jax.ShapeDtypeStruct((m, LANES), F32))
    return pl.pallas_call(
        functools.partial(_out_proj_kernel, sparse=sparse),
        grid=(m // tm,), in_specs=in_specs, out_specs=out_specs, out_shape=out_shape,
        compiler_params=_params(1), name="out_proj",
    )(*args)


def _moe_kernel(h_ref, n2_ref, gates_ref, wgu_ref, wd_ref, y_ref):
    e = pl.program_id(1)

    @pl.when(e == 0)
    def _():
        y_ref[...] = h_ref[...]

    n2 = n2_ref[...]
    gu = _dot(n2, wgu_ref[0].astype(BF16))
    act = _silu(gu[:, :D_EXPERT]) * gu[:, D_EXPERT:]
    lane = lax.broadcasted_iota(jnp.int32, gates_ref.shape, 1)
    gate = jnp.sum(jnp.where(lane == e, gates_ref[...], 0.0), axis=-1, keepdims=True)
    y_ref[...] += gate * _dot(act.astype(BF16), wd_ref[0].astype(BF16))


def _moe(h, n2, gates, wgu, wd, *, tm):
    m, d = h.shape
    row = lambda i, e: (i, 0)
    return pl.pallas_call(
        _moe_kernel,
        grid=(m // tm, N_EXPERTS),
        in_specs=[pl.BlockSpec((tm, d), row),
                  pl.BlockSpec((tm, d), row),
                  pl.BlockSpec((tm, LANES), row),
                  pl.BlockSpec((1, d, 2 * D_EXPERT), lambda i, e: (e, 0, 0)),
                  pl.BlockSpec((1, D_EXPERT, d), lambda i, e: (e, 0, 0))],
        out_specs=pl.BlockSpec((tm, d), row),
        out_shape=jax.ShapeDtypeStruct((m, d), F32),
        compiler_params=_params(2), name="moe",
    )(h, n2, gates, wgu, wd)


SC_CORES = 2
SC_SUBCORES = 16
SC_WORKERS = SC_CORES * SC_SUBCORES
SC_CHUNK = 64
MOE_TILE = 512


def _sc_mesh():
    return plsc.VectorSubcoreMesh(core_axis_name="c", subcore_axis_name="s",
                                  num_cores=SC_CORES, num_subcores=SC_SUBCORES)


def _sc_dispatch(src, pos, n_rows):
    m, d = src.shape
    n_chunks = m // (SC_WORKERS * SC_CHUNK)
    pos4 = pos.reshape(2, SC_WORKERS, n_chunks, SC_CHUNK).transpose(1, 0, 2, 3)

    def body(src_hbm, pos_hbm, out_hbm, idx_v, rows_v):
        wid = lax.axis_index("s") * SC_CORES + lax.axis_index("c")
        pltpu.sync_copy(pos_hbm.at[wid], idx_v)

        @pl.loop(0, n_chunks)
        def _(j):
            start = pl.multiple_of((wid * n_chunks + j) * SC_CHUNK, SC_CHUNK)
            pltpu.sync_copy(src_hbm.at[pl.ds(start, SC_CHUNK)], rows_v)
            pltpu.sync_copy(rows_v, out_hbm.at[idx_v.at[0, j]])
            pltpu.sync_copy(rows_v, out_hbm.at[idx_v.at[1, j]])

    return pl.kernel(
        body, out_type=jax.ShapeDtypeStruct((n_rows, d), src.dtype), mesh=_sc_mesh(),
        scratch_types=[pltpu.VMEM((2, n_chunks, SC_CHUNK), jnp.int32), pltpu.VMEM((SC_CHUNK, d), src.dtype)],
        name="moe_dispatch",
    )(src, pos4)


def _sc_gather(table, idx):
    b = idx.shape[0]
    d = table.shape[1]
    n_chunks = b // (SC_WORKERS * SC_CHUNK)
    idx3 = idx.reshape(SC_WORKERS, n_chunks, SC_CHUNK)

    def body(table_hbm, idx_hbm, out_hbm, idx_v, rows_v):
        wid = lax.axis_index("s") * SC_CORES + lax.axis_index("c")
        pltpu.sync_copy(idx_hbm.at[wid], idx_v)

        @pl.loop(0, n_chunks)
        def _(j):
            start = pl.multiple_of((wid * n_chunks + j) * SC_CHUNK, SC_CHUNK)
            pltpu.sync_copy(table_hbm.at[idx_v.at[j]], rows_v)
            pltpu.sync_copy(rows_v, out_hbm.at[pl.ds(start, SC_CHUNK)])

    return pl.kernel(
        body, out_type=jax.ShapeDtypeStruct((b, d), table.dtype), mesh=_sc_mesh(),
        scratch_types=[pltpu.VMEM((n_chunks, SC_CHUNK), jnp.int32), pltpu.VMEM((SC_CHUNK, d), table.dtype)],
        name="moe_collect",
    )(table, idx3)


def _moe_grouped_kernel(te_ref, tv_ref, xs_ref, wgu_ref, wd_ref, ys_ref, wgu16, wd16):
    i = pl.program_id(0)
    valid = tv_ref[i]

    @pl.when((i == 0) | (te_ref[i] != te_ref[jnp.maximum(i - 1, 0)]))
    def _():
        wgu16[...] = wgu_ref[0].astype(BF16)
        wd16[...] = wd_ref[0].astype(BF16)

    @pl.when(valid > 0)
    def _():
        rows = lax.broadcasted_iota(jnp.int32, xs_ref.shape, 0)
        xw = jnp.where(rows < valid, xs_ref[...], 0)
        x = _unpack_bf16_pairs(xw).astype(BF16)
        gu = _dot(x, wgu16[...])
        act = _silu(gu[:, :D_EXPERT]) * gu[:, D_EXPERT:]
        ys_ref[...] = _pack_bf16_pairs(_dot(act.astype(BF16), wd16[...]))

    @pl.when(valid == 0)
    def _():
        ys_ref[...] = jnp.zeros(ys_ref.shape, ys_ref.dtype)


def _moe_grouped(xs, tile_expert, tile_valid, wgu, wd):
    r, half = xs.shape
    d = 2 * half
    grid_spec = pltpu.PrefetchScalarGridSpec(
        num_scalar_prefetch=2, grid=(r // MOE_TILE,),
        in_specs=[pl.BlockSpec((MOE_TILE, half), lambda i, te, nu: (i, 0)),
                  pl.BlockSpec((1, d, 2 * D_EXPERT), lambda i, te, nu: (te[i], 0, 0)),
                  pl.BlockSpec((1, D_EXPERT, d), lambda i, te, nu: (te[i], 0, 0))],
        out_specs=pl.BlockSpec((MOE_TILE, half), lambda i, te, nu: (i, 0)),
        scratch_shapes=[pltpu.VMEM((d, 2 * D_EXPERT), BF16), pltpu.VMEM((D_EXPERT, d), BF16)])
    return pl.pallas_call(
        _moe_grouped_kernel, grid_spec=grid_spec,
        out_shape=jax.ShapeDtypeStruct((r, half), jnp.int32),
        compiler_params=_params(1), name="moe_grouped",
    )(tile_expert, tile_valid, xs, wgu, wd)


def _moe_combine_kernel(h_ref, z0_ref, z1_ref, rec_ref, y_ref):
    tm = h_ref.shape[0]
    rec = jnp.concatenate([rec_ref[...], jnp.zeros((LANES - SUBLANES, tm), F32)], axis=0).T
    y_ref[...] = (h_ref[...] + rec[:, 2:3] * _unpack_bf16_pairs(z0_ref[...])
                  + rec[:, 3:4] * _unpack_bf16_pairs(z1_ref[...]))


def _moe_combine(h, z, rec, *, tm):
    m, d = h.shape
    nt = m // tm
    row = lambda i: (i, 0)
    return pl.pallas_call(
        _moe_combine_kernel, grid=(nt,),
        in_specs=[pl.BlockSpec((tm, d), row),
                  pl.BlockSpec((tm, d // 2), row),
                  pl.BlockSpec((tm, d // 2), lambda i: (i + nt, 0)),
                  pl.BlockSpec((SUBLANES, tm), lambda i: (0, i))],
        out_specs=pl.BlockSpec((tm, d), row),
        out_shape=jax.ShapeDtypeStruct((m, d), F32),
        compiler_params=_params(1), name="moe_combine",
    )(h, z, z, rec)


def _moe_route(rec, cnt):
    m = rec.shape[1]
    n_tiles = 2 * m // MOE_TILE + N_EXPERTS
    counts = cnt[:, 0].astype(jnp.int32)
    tiles = (counts + MOE_TILE - 1) // MOE_TILE
    tile_end = jnp.cumsum(tiles)
    row_start = (tile_end - tiles) * MOE_TILE
    eid = rec[0:2].astype(jnp.int32)
    rank = rec[4:6].astype(jnp.int32)
    pos = rank
    for e in range(N_EXPERTS):
        pos = pos + jnp.where(eid == e, row_start[e], 0)
    tile = jnp.arange(n_tiles, dtype=jnp.int32)
    tile_expert = jnp.sum((tile_end[None, :] <= jnp.minimum(tile, tile_end[-1] - 1)[:, None]).astype(jnp.int32),
                          axis=1)
    row_end = jnp.sum(jnp.where(tile_expert[:, None] == jnp.arange(N_EXPERTS)[None, :],
                                (row_start + counts)[None, :], 0), axis=1)
    tile_valid = jnp.clip(row_end - tile * MOE_TILE, 0, MOE_TILE).astype(jnp.int32)
    return pos, tile_expert, tile_valid, n_tiles * MOE_TILE


def _head_rows(v):
    rows = jnp.concatenate([jnp.zeros((B_HEADS,), F32), v.astype(F32)])
    return jnp.broadcast_to(rows[:, None], (2 * B_HEADS, LANES))


def kernel(x_prompt, x_sample, cache_win_k, cache_win_v, state_conv, state_ssm, norm1_w, w_in, qnorm_w, knorm_w, conv_w, a_log, dt_bias, onorm_w, w_out, norm2_w, w_group, b_group, w_expert_router, b_expert_router, w_gate_up, w_down):
    nb, seq, d = x_prompt.shape
    db, t_len, _ = x_sample.shape
    n_buf = cache_win_k.shape[1]
    assert n_buf == MAX_WINDOW and seq % (MAX_DILATION * BAND) == 0 and t_len == SUBLANES
    keep = min(MAX_WINDOW, seq)

    c_bg = 3 * A_WIDTH + CONV_DIM + B_WIDTH
    w_cat = jnp.concatenate([w_in, jnp.zeros((d, LANES - 2 * B_HEADS), w_in.dtype)], axis=1).astype(BF16)
    assert w_cat.shape[1] == c_bg + LANES
    n1w = norm1_w.reshape(1, d).astype(F32)
    qw = jnp.tile(qnorm_w.astype(F32), A_HEADS).reshape(1, A_WIDTH)
    kw = jnp.tile(knorm_w.astype(F32), A_HEADS).reshape(1, A_WIDTH)
    a_log_v = _head_rows(a_log)
    dtb_v = _head_rows(dt_bias)
    onw = onorm_w.reshape(1, B_VAL_DIM).astype(F32)
    wa = w_out[:A_WIDTH].astype(BF16)
    wb = w_out[A_WIDTH:].astype(BF16)
    n2w = norm2_w.reshape(1, d).astype(F32)
    wr = jnp.concatenate([w_group, jnp.transpose(w_expert_router, (1, 0, 2)).reshape(d, N_EXPERTS),
                          jnp.zeros((d, LANES - N_GROUPS - N_EXPERTS), F32)], axis=1).astype(F32)
    br = jnp.zeros((1, LANES), F32).at[0, :N_GROUPS].set(b_group).at[0, N_GROUPS:N_GROUPS + N_EXPERTS].set(
        b_expert_router.reshape(-1))
    wgu = w_gate_up.astype(F32)
    wd = w_down.astype(F32)
    cw = conv_w.astype(F32)

    xp2d = x_prompt.reshape(nb * seq, d)
    qf, kf, vf, kwin, vwin, bqkv, z, bg = _in_proj(xp2d, n1w, w_cat, qw, kw, seq=seq, keep=keep,
                                                   fold=True, tm=PROJ_TILE)
    oa = _attn_prompt(qf, kf, vf).reshape(nb * seq, A_WIDTH)
    conv0 = jnp.zeros((nb, CONV_WIDTH - 1, CONV_DIM), F32)
    ssm0 = jnp.zeros((nb, B_HEADS, B_KEY_DIM, B_VAL_DIM), F32)
    ob, ssm_prompt = _gdn_prompt(bqkv, z, bg, cw, conv0, ssm0, a_log_v, dtb_v, onw,
                                 seq=seq, ts=GDN_TILE, block=GDN_BLOCK, chunk=CHUNK)
    assert (nb * seq) % (SC_WORKERS * SC_CHUNK) == 0 and nb * seq >= N_EXPERTS * MOE_TILE
    h_p, n2p, rec, cnt = _out_proj(xp2d, oa, ob, wa, wb, n2w, wr, br, tm=PROJ_TILE, sparse=True)
    pos, tile_expert, tile_valid, n_rows = _moe_route(rec, cnt)
    xs = _sc_dispatch(n2p, pos, n_rows)
    win_k_prompt = jnp.transpose(kwin, (0, 3, 1, 2))
    win_v_prompt = jnp.transpose(vwin, (0, 3, 1, 2))
    conv_prompt = bqkv.reshape(nb, seq, CONV_DIM)[:, seq - (CONV_WIDTH - 1):]

    ms = db * t_len
    xs2d = x_sample.reshape(ms, d)
    tms = min(PROJ_TILE, ms)
    qs, ksn, vsn, _, _, bqkv_s, z_s, bg_s = _in_proj(xs2d, n1w, w_cat, qw, kw, seq=ms, keep=ms,
                                                     fold=False, tm=tms)
    cache_kt = jnp.transpose(cache_win_k.astype(F32), (0, 2, 3, 1))
    cache_vt = jnp.transpose(cache_win_v.astype(F32), (0, 2, 3, 1))
    oa_s = _attn_sample(qs, ksn, vsn, cache_kt, cache_vt, t_len=t_len)
    xs, oa_s = lax.optimization_barrier((xs, oa_s))
    ys = _moe_grouped(xs, tile_expert, tile_valid, wgu, wd)
    zs = _sc_gather(ys, pos.reshape(-1))
    xpad = jnp.concatenate([state_conv.astype(F32), bqkv_s.reshape(db, t_len, CONV_DIM)], axis=1)
    nseq = 16 if db % 16 == 0 else 1
    ob_s, ssm_sample = _gdn_sample(xpad, z_s, bg_s, cw, state_ssm.astype(F32), a_log_v, dtb_v, onw,
                                   t_len=t_len, nseq=nseq)
    h_s, n2_s, gates_s = _out_proj(xs2d, oa_s, ob_s, wa, wb, n2w, wr, br, tm=tms, sparse=False)
    y_sample = _moe(h_s, n2_s, gates_s, wgu, wd, tm=ms)
    zs, y_sample = lax.optimization_barrier((zs, y_sample))
    y_prompt = _moe_combine(h_p, zs, rec, tm=MOE_TILE).reshape(nb, seq, d)
    y_sample = y_sample.reshape(db, t_len, d)
    win_k_sample = ksn.reshape(db, t_len, A_HEADS, A_HEAD_DIM)
    win_v_sample = vsn.reshape(db, t_len, A_HEADS, A_HEAD_DIM)
    conv_sample = xpad[:, t_len:]

    return (y_prompt, y_sample, win_k_prompt, win_v_prompt, conv_prompt, ssm_prompt,
            win_k_sample, win_v_sample, conv_sample, ssm_sample)
```

```python
import functools

import jax
import jax.numpy as jnp
from jax import lax
from jax.experimental import pallas as pl
from jax.experimental.pallas import tpu as pltpu
from jax.experimental.pallas import tpu_sc as plsc

F32 = jnp.float32
BF16 = jnp.bfloat16

A_HEADS = 8
A_HEAD_DIM = 64
A_WIDTH = A_HEADS * A_HEAD_DIM
DILATED_PATTERNS = ((128, 1), (512, 4), (2048, 16))
MAX_WINDOW = 2048
MAX_DILATION = 16
BAND = 128
ATTN_SCALE = A_HEAD_DIM ** -0.5
LOG2_E = 1.4426950408889634
B_HEADS = 4
B_KEY_DIM = 128
B_VAL_DIM = 128
B_WIDTH = B_HEADS * B_VAL_DIM
CONV_WIDTH = 4
CONV_DIM = B_HEADS * (2 * B_KEY_DIM + B_VAL_DIM)
CHUNK = 64
N_GROUPS = 4
EXPERTS_PER_GROUP = 4
N_EXPERTS = N_GROUPS * EXPERTS_PER_GROUP
D_EXPERT = 512
RMS_EPS = 1e-6
NEG_INF = -1e30

LANES = 128
SUBLANES = 8
VMEM_LIMIT = 56 * 1024 * 1024

PROJ_TILE = 512
GDN_TILE = 512
GDN_BLOCK = 128


def _params(n_axes, vmem=VMEM_LIMIT):
    return pltpu.CompilerParams(dimension_semantics=("arbitrary",) * n_axes, vmem_limit_bytes=vmem)


def _split2(x):
    hi = x.astype(BF16)
    lo = (x - hi.astype(F32)).astype(BF16)
    return hi, lo


def _split3(x):
    hi = x.astype(BF16)
    r = x - hi.astype(F32)
    mid = r.astype(BF16)
    lo = (r - mid.astype(F32)).astype(BF16)
    return hi, mid, lo


def _dot(a, b):
    return jnp.dot(a, b, preferred_element_type=F32)


def _dot_nt(a, b):
    return lax.dot_general(a, b, (((1,), (1,)), ((), ())), preferred_element_type=F32)


def _sigmoid(x):
    return 1.0 / (1.0 + jnp.exp2(x * -LOG2_E))


def _silu(x):
    return x * _sigmoid(x)


def _in_proj_kernel(x_ref, n1w_ref, w_ref, qw_ref, kw_ref,
                    q_ref, k_ref, v_ref, kwin_ref, vwin_ref, bqkv_ref, z_ref, bg_ref,
                    *scratch, fold):
    tm = x_ref.shape[0]
    x = x_ref[...]
    ms = jnp.mean(x * x, axis=-1, keepdims=True)
    n1 = (x * lax.rsqrt(ms + RMS_EPS) * n1w_ref[...]).astype(BF16)

    first_head = lax.broadcasted_iota(jnp.int32, (tm, LANES), 1) < A_HEAD_DIM

    def head_norm(t, w):
        cols = []
        for c in range(A_WIDTH // LANES):
            tc = t[:, c * LANES:(c + 1) * LANES]
            sq = tc * tc
            s0 = jnp.sum(jnp.where(first_head, sq, 0.0), axis=-1, keepdims=True)
            s1 = jnp.sum(jnp.where(first_head, 0.0, sq), axis=-1, keepdims=True)
            ms = jnp.where(first_head, s0, s1) * (1.0 / A_HEAD_DIM)
            cols.append(tc * lax.rsqrt(ms + RMS_EPS))
        return jnp.concatenate(cols, axis=1) * w

    q_scale = ATTN_SCALE * LOG2_E if fold else ATTN_SCALE
    q = head_norm(_dot(n1, w_ref[:, 0:A_WIDTH]), qw_ref[...]) * q_scale
    k = head_norm(_dot(n1, w_ref[:, A_WIDTH:2 * A_WIDTH]), kw_ref[...])
    v = _dot(n1, w_ref[:, 2 * A_WIDTH:3 * A_WIDTH])
    if fold:
        kwin_ref[0] = k.T.reshape(A_HEADS, A_HEAD_DIM, tm)
        vwin_ref[0] = v.T.reshape(A_HEADS, A_HEAD_DIM, tm)

        (scr,) = scratch
        rows = tm // MAX_DILATION
        for val, out in ((q, q_ref), (k, k_ref), (v, v_ref)):
            for c in range(A_WIDTH // LANES):
                cs = slice(c * LANES, (c + 1) * LANES)
                scr[c] = val[:, cs]
                for r in range(MAX_DILATION):
                    out[0, r, :, cs] = scr[c, pl.ds(r, rows, stride=MAX_DILATION), :]
    else:
        kwin_ref[...] = k
        vwin_ref[...] = v
        q_ref[...] = q
        k_ref[...] = k
        v_ref[...] = v
    c0 = 3 * A_WIDTH
    bqkv_ref[...] = _dot(n1, w_ref[:, c0:c0 + CONV_DIM])
    c1 = c0 + CONV_DIM
    z_ref[...] = _dot(n1, w_ref[:, c1:c1 + B_WIDTH]).astype(BF16)
    c2 = c1 + B_WIDTH
    bg_ref[...] = _dot(n1, w_ref[:, c2:c2 + LANES])


def _in_proj(x2d, n1w, w_cat, qw, kw, *, seq, keep, fold, tm):
    m, d = x2d.shape
    nb = m // seq
    tiles_per_seq = seq // tm
    skip = (seq - keep) // tm
    keep_tiles = keep // tm
    n_cols = w_cat.shape[1]

    def win_map(i):
        return (i // tiles_per_seq) * keep_tiles + jnp.maximum(i % tiles_per_seq - skip, 0), 0

    row = lambda i: (i, 0)
    const = lambda i: (0, 0)
    if fold:
        rows = tm // MAX_DILATION
        qkv_shape = jax.ShapeDtypeStruct((nb, MAX_DILATION, seq // MAX_DILATION, A_WIDTH), F32)
        qkv_spec = pl.BlockSpec((1, MAX_DILATION, rows, A_WIDTH),
                                lambda i: (i // tiles_per_seq, 0, i % tiles_per_seq, 0))
        scratch = [pltpu.VMEM((A_WIDTH // LANES, tm, LANES), F32)]
        win_shape = jax.ShapeDtypeStruct((nb, A_HEADS, A_HEAD_DIM, keep), F32)
        win_spec = pl.BlockSpec((1, A_HEADS, A_HEAD_DIM, tm),
                                lambda i: (i // tiles_per_seq, 0, 0, jnp.maximum(i % tiles_per_seq - skip, 0)))
    else:
        qkv_shape = jax.ShapeDtypeStruct((m, A_WIDTH), F32)
        qkv_spec = pl.BlockSpec((tm, A_WIDTH), row)
        scratch = []
        win_shape = jax.ShapeDtypeStruct((nb * keep, A_WIDTH), F32)
        win_spec = pl.BlockSpec((tm, A_WIDTH), win_map)
    out_shape = (qkv_shape, qkv_shape, qkv_shape, win_shape, win_shape,
                 jax.ShapeDtypeStruct((m, CONV_DIM), F32),
                 jax.ShapeDtypeStruct((m, B_WIDTH), BF16),
                 jax.ShapeDtypeStruct((m, LANES), F32))
    out_specs = (qkv_spec, qkv_spec, qkv_spec, win_spec, win_spec,
                 pl.BlockSpec((tm, CONV_DIM), row),
                 pl.BlockSpec((tm, B_WIDTH), row),
                 pl.BlockSpec((tm, LANES), row))
    in_specs = [pl.BlockSpec((tm, d), row),
                pl.BlockSpec((1, d), const),
                pl.BlockSpec((d, n_cols), const),
                pl.BlockSpec((1, A_WIDTH), const),
                pl.BlockSpec((1, A_WIDTH), const)]
    return pl.pallas_call(
        functools.partial(_in_proj_kernel, fold=fold),
        grid=(m // tm,), in_specs=in_specs, out_specs=out_specs, out_shape=out_shape,
        scratch_shapes=scratch, compiler_params=_params(1), name="in_proj",
    )(x2d, n1w, w_cat, qw, kw)


def _band_mask(pieces):
    sub = BAND // pieces
    r = lax.broadcasted_iota(jnp.int32, (BAND, 2 * BAND), 0)
    c = lax.broadcasted_iota(jnp.int32, (BAND, 2 * BAND), 1)
    qpos = (r % sub) * pieces + r // sub + BAND
    cc = c % BAND
    kpos = (cc % sub) * pieces + cc // sub + (c // BAND) * BAND
    dist = qpos - kpos
    return (dist >= 0) & (dist <= BAND), c >= BAND


ATTN_UNROLL = 16


def _attn_prompt_kernel(q_ref, k_ref, v_ref, o_ref, p_scr, op_scr, mx_scr, l_scr, nat_scr):
    u_len = q_ref.shape[2]
    lane = lax.broadcasted_iota(jnp.int32, (BAND, LANES), 1)
    head0 = lane < A_HEAD_DIM

    def per_head(col):
        return jnp.where(head0, jnp.broadcast_to(col[:BAND], (BAND, LANES)),
                         jnp.broadcast_to(col[BAND:], (BAND, LANES)))

    for pi, pieces in enumerate((16, 4, 1)):
        sub = BAND // pieces
        n_res = MAX_DILATION // pieces
        n_blk = u_len // sub
        band, _ = _band_mask(pieces)
        bias = jnp.where(band, 0.0, NEG_INF)
        bias2 = jnp.concatenate([bias, bias], axis=0)
        bias2_cur = bias2[:, BAND:]

        def locate(blk, n_res=n_res, sub=sub):
            m = blk // n_res
            cur = pl.multiple_of(m * sub, SUBLANES)
            prv = pl.multiple_of(jnp.maximum(m - 1, 0) * sub, SUBLANES)
            return blk % n_res, cur, prv

        def gather(ref, res, start, pieces=pieces, n_res=n_res, sub=sub):
            parts = [ref[0, a * n_res + res, pl.ds(start, sub), :] for a in range(pieces)]
            return parts[0] if pieces == 1 else jnp.concatenate(parts, axis=0)

        def scatter(ref, res, start, val, pi=pi, pieces=pieces, n_res=n_res, sub=sub):
            for a in range(pieces):
                ref[pi, a * n_res + res, pl.ds(start, sub), :] = val[a * sub:(a + 1) * sub]

        def probs(it, carry, first, bias2=bias2, bias2_cur=bias2_cur):
            for j, is_first in enumerate(first):
                blk = it * ATTN_UNROLL + j
                res, cur, prv = locate(blk)
                qb = gather(q_ref, res, cur)
                q2 = jnp.concatenate([jnp.where(head0, qb, 0.0), jnp.where(head0, 0.0, qb)], axis=0).astype(BF16)
                kc = gather(k_ref, res, cur)
                if is_first:
                    s = _dot_nt(q2, kc.astype(BF16)) + bias2_cur
                else:
                    kb = jnp.concatenate([gather(k_ref, res, prv), kc], axis=0).astype(BF16)
                    s = _dot_nt(q2, kb) + bias2
                mx = jnp.max(s, axis=-1, keepdims=True)
                p = jnp.exp2(s - mx)
                if is_first:
                    p_scr[blk, :, BAND:] = p.astype(BF16)
                else:
                    p_scr[blk] = p.astype(BF16)
                scatter(mx_scr, res, cur, per_head(mx))
            return carry

        def values(it, carry, first):
            for j, is_first in enumerate(first):
                blk = it * ATTN_UNROLL + j
                res, cur, prv = locate(blk)
                vc = gather(v_ref, res, cur)
                if is_first:
                    p = p_scr[blk, :, BAND:]
                    vb = vc.astype(BF16)
                else:
                    p = p_scr[blk]
                    vb = jnp.concatenate([gather(v_ref, res, prv), vc], axis=0).astype(BF16)
                o2 = _dot(p, jnp.concatenate([vb, jnp.ones(vb.shape, BF16)], axis=1))
                scatter(op_scr, res, cur, jnp.where(head0, o2[:BAND, :LANES], o2[BAND:, :LANES]))
                scatter(l_scr, res, cur, jnp.where(head0, o2[:BAND, LANES:], o2[BAND:, LANES:]))
            return carry

        n_it = n_res * n_blk // ATTN_UNROLL
        flags = [tuple(g * ATTN_UNROLL + j < n_res for j in range(ATTN_UNROLL)) for g in range(n_it)]
        segments = []
        for g, f in enumerate(flags):
            if segments and segments[-1][2] == f:
                segments[-1][1] = g + 1
            else:
                segments.append([g, g + 1, f])
        for phase in (probs, values):
            for lo, hi, f in segments:
                lax.fori_loop(lo, hi, functools.partial(phase, first=f), 0)

    for r in range(MAX_DILATION):
        m0, m1, m2 = mx_scr[0, r], mx_scr[1, r], mx_scr[2, r]
        big = jnp.maximum(jnp.maximum(m0, m1), m2)
        e0, e1, e2 = jnp.exp2(m0 - big), jnp.exp2(m1 - big), jnp.exp2(m2 - big)
        num = e0 * op_scr[0, r] + e1 * op_scr[1, r] + e2 * op_scr[2, r]
        den = e0 * l_scr[0, r] + e1 * l_scr[1, r] + e2 * l_scr[2, r]
        nat_scr[pl.ds(r, u_len, stride=MAX_DILATION), :] = num / den
    o_ref[0] = nat_scr[...].astype(BF16)


def _attn_prompt(qf, kf, vf):
    nb, _, u_len, _ = qf.shape
    seq = u_len * MAX_DILATION
    n_pairs = A_WIDTH // LANES
    spec = pl.BlockSpec((1, MAX_DILATION, u_len, LANES), lambda b, p: (b, 0, 0, p))
    return pl.pallas_call(
        _attn_prompt_kernel,
        grid=(nb, n_pairs),
        in_specs=[spec, spec, spec],
        out_specs=pl.BlockSpec((1, seq, LANES), lambda b, p: (b, 0, p)),
        out_shape=jax.ShapeDtypeStruct((nb, seq, A_WIDTH), BF16),
        scratch_shapes=[pltpu.VMEM((seq // BAND, 2 * BAND, 2 * BAND), BF16),
                        pltpu.VMEM((3, MAX_DILATION, u_len, LANES), F32),
                        pltpu.VMEM((3, MAX_DILATION, u_len, LANES), F32),
                        pltpu.VMEM((3, MAX_DILATION, u_len, LANES), F32),
                        pltpu.VMEM((seq, LANES), F32)],
        compiler_params=_params(2), name="attn_prompt",
    )(qf, kf, vf)


def _multiplicity(dd):
    cnt = jnp.zeros(dd.shape, F32)
    for w, d in DILATED_PATTERNS:
        ok = (dd >= 0) & (dd <= w) & (dd % d == 0)
        cnt = cnt + jnp.where(ok, 1.0, 0.0)
    return cnt


def _attn_sample_kernel(q_ref, kn_ref, vn_ref, kt_ref, vt_ref, o_ref):
    t_len = q_ref.shape[0]
    n_buf = kt_ref.shape[3]
    q = q_ref[...]
    kn = kn_ref[...]
    vn = vn_ref[...]
    tq = n_buf + lax.broadcasted_iota(jnp.int32, (t_len, n_buf), 0)
    cnt = _multiplicity(tq - lax.broadcasted_iota(jnp.int32, (t_len, n_buf), 1))
    cnt_new = _multiplicity(lax.broadcasted_iota(jnp.int32, (t_len, t_len), 0)
                            - lax.broadcasted_iota(jnp.int32, (t_len, t_len), 1))
    outs = []
    for h in range(A_HEADS):
        hs = slice(h * A_HEAD_DIM, (h + 1) * A_HEAD_DIM)
        qh = q[:, hs].astype(BF16)
        s = jnp.where(cnt > 0, _dot(qh, kt_ref[0, h].astype(BF16)), NEG_INF)
        sn = jnp.where(cnt_new > 0, _dot_nt(qh, kn[:, hs].astype(BF16)), NEG_INF)
        m = jnp.maximum(jnp.max(s, axis=-1, keepdims=True), jnp.max(sn, axis=-1, keepdims=True))
        p = cnt * jnp.exp(s - m)
        pn = cnt_new * jnp.exp(sn - m)
        l = jnp.sum(p, axis=-1, keepdims=True) + jnp.sum(pn, axis=-1, keepdims=True)
        o = _dot_nt(p.astype(BF16), vt_ref[0, h].astype(BF16)) + _dot(pn.astype(BF16), vn[:, hs].astype(BF16))
        outs.append(o / l)
    o_ref[...] = jnp.concatenate(outs, axis=-1)


def _attn_sample(q2d, kn2d, vn2d, cache_kt, cache_vt, *, t_len):
    nb, _, _, n_buf = cache_kt.shape
    tok = pl.BlockSpec((t_len, A_WIDTH), lambda b: (b, 0))
    win = pl.BlockSpec((1, A_HEADS, A_HEAD_DIM, n_buf), lambda b: (b, 0, 0, 0))
    return pl.pallas_call(
        _attn_sample_kernel,
        grid=(nb,),
        in_specs=[tok, tok, tok, win, win],
        out_specs=tok,
        out_shape=jax.ShapeDtypeStruct((nb * t_len, A_WIDTH), F32),
        compiler_params=_params(1), name="attn_sample",
    )(q2d, kn2d, vn2d, cache_kt, cache_vt)


def _gdn_prep(c, bg, a_col, dt_col, chunk):
    n = c.shape[0]
    bgt = bg.T[0:2 * B_HEADS]
    row = lax.broadcasted_iota(jnp.int32, bgt.shape, 0)
    xg = bgt + dt_col
    sp = jnp.maximum(xg, 0.0) + jnp.log1p(jnp.exp(-jnp.abs(xg)))
    gt = jnp.where(row >= B_HEADS, -a_col * sp, 0.0)
    ri = lax.broadcasted_iota(jnp.int32, (n, n), 0)
    ci = lax.broadcasted_iota(jnp.int32, (n, n), 1)
    tri = jnp.where(((ri // chunk) == (ci // chunk)) & (ri >= ci), 1.0, 0.0).astype(BF16)
    th, tm_, tl = _split3(gt)
    drow = _dot_nt(th, tri) + _dot_nt(tm_, tri) + _dot_nt(tl, tri)
    both = jnp.where(row >= B_HEADS, drow, _sigmoid(bgt))
    cols = jnp.concatenate([both, jnp.zeros((LANES - 2 * B_HEADS, n), F32)], axis=0).T
    beta_all = cols
    dcol = cols
    heads = []
    kd_ = B_KEY_DIM
    for h in range(B_HEADS):
        qh = c[:, h * kd_:(h + 1) * kd_]
        kh = c[:, B_HEADS * kd_ + h * kd_:B_HEADS * kd_ + (h + 1) * kd_]
        vh = c[:, 2 * B_HEADS * kd_ + h * B_VAL_DIM:2 * B_HEADS * kd_ + (h + 1) * B_VAL_DIM]
        qh = qh * lax.rsqrt(jnp.sum(qh * qh, axis=-1, keepdims=True) + RMS_EPS) * (B_KEY_DIM ** -0.5)
        kh = kh * lax.rsqrt(jnp.sum(kh * kh, axis=-1, keepdims=True) + RMS_EPS)
        heads.append((qh, kh, vh, beta_all[:, h:h + 1], dcol[:, B_HEADS + h:B_HEADS + h + 1],
                      drow[B_HEADS + h:B_HEADS + h + 1, :]))
    return heads


def _gdn_out(o, onw, zh):
    ms = jnp.mean(o * o, axis=-1, keepdims=True)
    return o * lax.rsqrt(ms + RMS_EPS) * onw * _silu(zh)


def _gdn_prompt_kernel(x_ref, z_ref, bg_ref, cw_ref, cst_ref, sst_ref, avec_ref, dtb_ref, onw_ref,
                       o_ref, snew_ref, xp_scr, s_scr, *, block, chunk):
    ts = x_ref.shape[0]
    t = pl.program_id(1)
    pad = SUBLANES
    hist = CONV_WIDTH - 1

    @pl.when(t == 0)
    def _():
        xp_scr[...] = jnp.zeros((pad, CONV_DIM), F32)
        xp_scr[pad - hist:pad, :] = cst_ref[0]
        s_scr[...] = sst_ref[0]

    assert CONV_WIDTH == 4
    x = x_ref[...]
    tail = xp_scr[...]
    sub = lax.broadcasted_iota(jnp.int32, (pad, CONV_DIM), 0)

    def shift(cur, prev_tail, s):
        rolled = pltpu.roll(cur, s, axis=0)
        head = jnp.where(sub < s, pltpu.roll(prev_tail, s, axis=0), rolled[0:pad])
        return jnp.concatenate([head, rolled[pad:]], axis=0)

    w0, w1, w2, w3 = (cw_ref[i:i + 1, :] for i in range(CONV_WIDTH))
    x1 = shift(x, tail, 1)
    a = x * w1 + x1 * w0
    a_tail = tail * w1 + pltpu.roll(tail, 1, axis=0) * w0
    y = x * w3 + x1 * w2 + shift(a, a_tail, 2)
    xp_scr[...] = x[ts - pad:ts]
    c = _silu(y)
    avec = jnp.exp(avec_ref[...])
    onw = onw_ref[...]
    n_chunks = block // chunk
    n_sb = ts // block
    ri = lax.broadcasted_iota(jnp.int32, (block, block), 0)
    ci = lax.broadcasted_iota(jnp.int32, (block, block), 1)
    same = (ri // chunk) == (ci // chunk)
    lower = same & (ri >= ci)
    strict = same & (ri > ci)
    eye = jnp.where(ri == ci, 1.0, 0.0).astype(F32)

    s = [s_scr[h] for h in range(B_HEADS)]
    for g0 in range(0, n_sb, GDN_GROUP):
        units = []
        for sb in range(g0, g0 + GDN_GROUP):
            rs = slice(sb * block, (sb + 1) * block)
            heads = _gdn_prep(c[rs], bg_ref[rs, :], avec, dtb_ref[...], chunk)
            for h, (qh, kh, vh, beta, dcol, drow) in enumerate(heads):
                units.append(dict(sb=sb, h=h, q=qh, k=kh, v=vh, beta=beta, dcol=dcol, drow=drow))
        s = _gdn_group(units, s, z_ref, o_ref, onw, lower, strict, eye, block, chunk)
    for h in range(B_HEADS):
        s_scr[h] = s[h]

    @pl.when(t == pl.num_programs(1) - 1)
    def _():
        snew_ref[0] = s_scr[...]


GDN_GROUP = 4


def _gdn_group(units, s, z_ref, o_ref, onw, lower, strict, eye, block, chunk, state_refs=None):
    n_chunks = block // chunk
    for u in units:
        gam = jnp.exp(jnp.where(lower, u["dcol"] - u["drow"], NEG_INF))
        kb = u["k"] * u["beta"]
        kq = _dot_nt(jnp.concatenate([kb, u["q"]], axis=0).astype(BF16), u["k"].astype(BF16))
        u["pw"] = jnp.where(strict, kq[:block] * gam, 0.0)
        u["t"] = eye - u["pw"]
        u["attn"] = jnp.where(lower, kq[block:] * gam, 0.0).astype(BF16)
        ed = jnp.exp(u["dcol"])
        u["rhs"] = jnp.concatenate([u["v"] * u["beta"], kb * ed], axis=1).astype(BF16)
        u["qd"] = u["q"] * ed
        u["kt"] = u["k"].T

    kk = 2
    while kk < chunk:
        for u in units:
            p16 = u["pw"].astype(BF16)
            u["pw"] = _dot(p16, p16)
        for u in units:
            u["t"] = u["t"] + _dot(u["t"].astype(BF16), u["pw"].astype(BF16))
        kk *= 2

    aligned = chunk % (2 * SUBLANES) == 0

    def chunk_rows(u, name, cs):
        return u[name + "16"][cs] if aligned else u[name][cs].astype(BF16)

    for u in units:
        u["uw"] = _dot(u["t"].astype(BF16), u["rhs"])
        u["uw16"] = u["uw"].astype(BF16)
    for u in units:
        au_aw = _dot(u["attn"], u["uw16"])
        u["op"] = au_aw[:, :B_VAL_DIM]
        u["qp"] = u["qd"] - au_aw[:, B_VAL_DIM:]
        u["qp16"] = u["qp"].astype(BF16)
    for u in units:
        u["n"], u["mw"], u["e"] = [], [], []
        for cc in range(n_chunks):
            cs = slice(cc * chunk, (cc + 1) * chunk)
            last = u["drow"][:, (cc + 1) * chunk - 1:(cc + 1) * chunk]
            kdt = (u["kt"][:, cs] * jnp.exp(last - u["drow"][:, cs])).astype(BF16)
            nm = _dot(kdt, chunk_rows(u, "uw", cs))
            u["n"].append(nm[:, :B_VAL_DIM])
            u["mw"].append(nm[:, B_VAL_DIM:].astype(BF16))
            u["e"].append(jnp.exp(last))

    s = None if s is None else list(s)
    for b0 in range(0, len(units), B_HEADS):
        sb = units[b0]["sb"]
        rs = slice(sb * block, (sb + 1) * block)
        outs = [[] for _ in range(B_HEADS)]
        for cc in range(n_chunks):
            cs = slice(cc * chunk, (cc + 1) * chunk)
            for h in range(B_HEADS):
                u = units[b0 + h]
                s_in = s[h] if state_refs is None else state_refs[0][cc, h]
                s16 = s_in.astype(BF16)
                if aligned:
                    r = _dot(jnp.concatenate([u["mw"][cc], chunk_rows(u, "qp", cs)], axis=0), s16)
                    r_m, r_q = r[:B_KEY_DIM], r[B_KEY_DIM:]
                else:
                    r_m, r_q = _dot(u["mw"][cc], s16), _dot(chunk_rows(u, "qp", cs), s16)
                outs[h].append(r_q + u["op"][cs])
                s_out = s_in * u["e"][cc] - r_m + u["n"][cc]
                if state_refs is None:
                    s[h] = s_out
                else:
                    state_refs[1][cc, h] = s_out
        for h in range(B_HEADS):
            o_all = jnp.concatenate(outs[h], axis=0) if n_chunks > 1 else outs[h][0]
            zh = z_ref[rs, h * B_VAL_DIM:(h + 1) * B_VAL_DIM].astype(F32)
            o_ref[rs, h * B_VAL_DIM:(h + 1) * B_VAL_DIM] = _gdn_out(o_all, onw, zh).astype(o_ref.dtype)
    return s


def _gdn_prompt(bqkv, z, bg, conv_w, conv_state, ssm_state, a_log_v, dtb_v, onw, *, seq, ts, block, chunk):
    m = bqkv.shape[0]
    nb = m // seq
    tps = seq // ts
    row = lambda b, t: (b * tps + t, 0)
    const = lambda b, t: (0, 0)
    return pl.pallas_call(
        functools.partial(_gdn_prompt_kernel, block=block, chunk=chunk),
        grid=(nb, tps),
        in_specs=[pl.BlockSpec((ts, CONV_DIM), row),
                  pl.BlockSpec((ts, B_WIDTH), row),
                  pl.BlockSpec((ts, LANES), row),
                  pl.BlockSpec((CONV_WIDTH, CONV_DIM), const),
                  pl.BlockSpec((1, CONV_WIDTH - 1, CONV_DIM), lambda b, t: (b, 0, 0)),
                  pl.BlockSpec((1, B_HEADS, B_KEY_DIM, B_VAL_DIM), lambda b, t: (b, 0, 0, 0)),
                  pl.BlockSpec((2 * B_HEADS, LANES), const),
                  pl.BlockSpec((2 * B_HEADS, LANES), const),
                  pl.BlockSpec((1, B_VAL_DIM), const)],
        out_specs=(pl.BlockSpec((ts, B_WIDTH), row),
                   pl.BlockSpec((1, B_HEADS, B_KEY_DIM, B_VAL_DIM), lambda b, t: (b, 0, 0, 0))),
        out_shape=(jax.ShapeDtypeStruct((m, B_WIDTH), BF16),
                   jax.ShapeDtypeStruct(ssm_state.shape, F32)),
        scratch_shapes=[pltpu.VMEM((SUBLANES, CONV_DIM), F32),
                        pltpu.VMEM((B_HEADS, B_KEY_DIM, B_VAL_DIM), F32)],
        compiler_params=_params(2), name="gdn_prompt",
    )(bqkv, z, bg, conv_w, conv_state, ssm_state, a_log_v, dtb_v, onw)


def _gdn_sample_kernel(xp_ref, z_ref, bg_ref, cw_ref, sst_ref, avec_ref, dtb_ref, onw_ref,
                       o_ref, snew_ref, *, t_len):
    nseq = xp_ref.shape[0]
    hist = CONV_WIDTH - 1
    ys = []
    for j in range(nseq):
        y = xp_ref[j, 0:t_len, :] * cw_ref[0:1, :]
        for i in range(1, CONV_WIDTH):
            y = y + xp_ref[j, i:i + t_len, :] * cw_ref[i:i + 1, :]
        ys.append(y)
    c = _silu(jnp.concatenate(ys, axis=0))
    avec = jnp.exp(avec_ref[...])
    onw = onw_ref[...]
    block = nseq * t_len
    ri = lax.broadcasted_iota(jnp.int32, (block, block), 0)
    ci = lax.broadcasted_iota(jnp.int32, (block, block), 1)
    same = (ri // t_len) == (ci // t_len)
    heads = _gdn_prep(c, bg_ref[...], avec, dtb_ref[...], t_len)
    units = [dict(sb=0, h=h, q=qh, k=kh, v=vh, beta=beta, dcol=dcol, drow=drow)
             for h, (qh, kh, vh, beta, dcol, drow) in enumerate(heads)]
    _gdn_group(units, None, z_ref, o_ref, onw, same & (ri >= ci), same & (ri > ci),
               jnp.where(ri == ci, 1.0, 0.0).astype(F32), block, t_len, state_refs=(sst_ref, snew_ref))


def _gdn_sample(xp, z, bg, conv_w, ssm_state, a_log_v, dtb_v, onw, *, t_len, nseq):
    nb = xp.shape[0]
    rows = nseq * t_len
    row = lambda i: (i, 0)
    const = lambda i: (0, 0)
    return pl.pallas_call(
        functools.partial(_gdn_sample_kernel, t_len=t_len),
        grid=(nb // nseq,),
        in_specs=[pl.BlockSpec((nseq, xp.shape[1], CONV_DIM), lambda i: (i, 0, 0)),
                  pl.BlockSpec((rows, B_WIDTH), row),
                  pl.BlockSpec((rows, LANES), row),
                  pl.BlockSpec((CONV_WIDTH, CONV_DIM), const),
                  pl.BlockSpec((nseq, B_HEADS, B_KEY_DIM, B_VAL_DIM), lambda i: (i, 0, 0, 0)),
                  pl.BlockSpec((2 * B_HEADS, LANES), const),
                  pl.BlockSpec((2 * B_HEADS, LANES), const),
                  pl.BlockSpec((1, B_VAL_DIM), const)],
        out_specs=(pl.BlockSpec((rows, B_WIDTH), row),
                   pl.BlockSpec((nseq, B_HEADS, B_KEY_DIM, B_VAL_DIM), lambda i: (i, 0, 0, 0))),
        out_shape=(jax.ShapeDtypeStruct((nb * t_len, B_WIDTH), BF16),
                   jax.ShapeDtypeStruct(ssm_state.shape, F32)),
        compiler_params=_params(1), name="gdn_sample",
    )(xp, z, bg, conv_w, ssm_state, a_log_v, dtb_v, onw)


def _pack_bf16_pairs(x):
    n = x.shape[1] // 2
    bits = pltpu.bitcast(x.astype(BF16).astype(F32), jnp.int32)
    return lax.shift_right_logical(bits[:, :n], 16) | (bits[:, n:] & jnp.int32(-65536))


def _unpack_bf16_pairs(w):
    lo = pltpu.bitcast(lax.shift_left(w, 16), F32)
    hi = pltpu.bitcast(w & jnp.int32(-65536), F32)
    return jnp.concatenate([lo, hi], axis=1)


def _out_proj_kernel(x_ref, oa_ref, ob_ref, wa_ref, wb_ref, n2w_ref, wr_ref, br_ref, *rest, sparse):
    if sparse:
        tri_ref, h_ref, n2_ref, gates_ref, cnt_ref = rest
    else:
        h_ref, n2_ref, gates_ref = rest
    h = x_ref[...] + _dot(oa_ref[...].astype(BF16), wa_ref[...]) + _dot(ob_ref[...].astype(BF16), wb_ref[...])
    h_ref[...] = h
    ms = jnp.mean(h * h, axis=-1, keepdims=True)
    n2 = h * lax.rsqrt(ms + RMS_EPS) * n2w_ref[...]
    if sparse:
        n2_ref[...] = _pack_bf16_pairs(n2)
    else:
        n2_ref[...] = n2.astype(BF16)
    nh, nl = _split2(n2)
    wh, wl = _split2(wr_ref[...])
    tm = n2.shape[0]
    parts = _dot(jnp.concatenate([nh, nl], axis=0), jnp.concatenate([wh, wl], axis=1))
    logits = parts[:tm, :LANES] + parts[:tm, LANES:] + parts[tm:, :LANES] + br_ref[...]
    n_rows = 3 * SUBLANES
    lt = logits.T[0:n_rows]
    row = lax.broadcasted_iota(jnp.int32, lt.shape, 0).astype(F32)
    big = 1e9
    gl = jnp.where(row < N_GROUPS, lt, NEG_INF)
    gmax = jnp.max(gl, axis=0, keepdims=True)
    gi = jnp.min(jnp.where(gl == gmax, row, big), axis=0, keepdims=True)
    g_sel = 1.0 / jnp.sum(jnp.exp(gl - gmax), axis=0, keepdims=True)
    lo = N_GROUPS + EXPERTS_PER_GROUP * gi
    el = jnp.where((row >= lo) & (row < lo + EXPERTS_PER_GROUP), lt, NEG_INF)
    v1 = jnp.max(el, axis=0, keepdims=True)
    i1 = jnp.min(jnp.where(el == v1, row, big), axis=0, keepdims=True)
    el2 = jnp.where(row == i1, NEG_INF, el)
    v2 = jnp.max(el2, axis=0, keepdims=True)
    i2 = jnp.min(jnp.where(el2 == v2, row, big), axis=0, keepdims=True)
    e2 = jnp.exp(v2 - v1)
    w1 = g_sel / (1.0 + e2)
    w2 = g_sel * e2 / (1.0 + e2)
    e0 = i1 - N_GROUPS
    e1 = i2 - N_GROUPS
    erow = lax.broadcasted_iota(jnp.int32, (N_EXPERTS, tm), 0).astype(F32)
    if not sparse:
        gt = jnp.where(erow == e0, w1, 0.0) + jnp.where(erow == e1, w2, 0.0)
        gates_ref[...] = jnp.concatenate([gt, jnp.zeros((LANES - N_EXPERTS, tm), F32)], axis=0).T
        return
    @pl.when(pl.program_id(0) == 0)
    def _():
        cnt_ref[...] = jnp.zeros(cnt_ref.shape, F32)

    cnt = cnt_ref[...]
    onehot = jnp.where(erow == e0, 1.0, 0.0) + jnp.where(erow == e1, 1.0, 0.0)
    before = _dot(onehot.astype(BF16), tri_ref[...]) + cnt[:, 0:1]
    r0 = jnp.sum(jnp.where(erow == e0, before, 0.0), axis=0, keepdims=True)
    r1 = jnp.sum(jnp.where(erow == e1, before, 0.0), axis=0, keepdims=True)
    cnt_ref[...] = cnt + jnp.sum(onehot, axis=1, keepdims=True)
    gates_ref[...] = jnp.concatenate([e0, e1, w1, w2, r0, r1, jnp.zeros((2, tm), F32)], axis=0)


def _out_proj(x2d, oa, ob, wa, wb, n2w, wr, br, *, tm, sparse):
    m, d = x2d.shape
    row = lambda i: (i, 0)
    const = lambda i: (0, 0)
    in_specs = [pl.BlockSpec((tm, d), row),
                pl.BlockSpec((tm, A_WIDTH), row),
                pl.BlockSpec((tm, B_WIDTH), row),
                pl.BlockSpec((A_WIDTH, d), const),
                pl.BlockSpec((B_WIDTH, d), const),
                pl.BlockSpec((1, d), const),
                pl.BlockSpec((d, LANES), const),
                pl.BlockSpec((1, LANES), const)]
    args = [x2d, oa, ob, wa, wb, n2w, wr, br]
    if sparse:
        idx = jnp.arange(tm)
        args.append((idx[:, None] < idx[None, :]).astype(BF16))
        in_specs.append(pl.BlockSpec((tm, tm), const))
        out_specs = (pl.BlockSpec((tm, d), row), pl.BlockSpec((tm, d // 2), row),
                     pl.BlockSpec((SUBLANES, tm), lambda i: (0, i)), pl.BlockSpec((N_EXPERTS, LANES), const))
        out_shape = (jax.ShapeDtypeStruct((m, d), F32), jax.ShapeDtypeStruct((m, d // 2), jnp.int32),
                     jax.ShapeDtypeStruct((SUBLANES, m), F32), jax.ShapeDtypeStruct((N_EXPERTS, LANES), F32))
    else:
        out_specs = (pl.BlockSpec((tm, d), row), pl.BlockSpec((tm, d), row), pl.BlockSpec((tm, LANES), row))
        out_shape = (jax.ShapeDtypeStruct((m, d), F32), jax.ShapeDtypeStruct((m, d), BF16),
                     jax.ShapeDtypeStruct((m, LANES), F32))
    return pl.pallas_call(
        functools.partial(_out_proj_kernel, sparse=sparse),
        grid=(m // tm,), in_specs=in_specs, out_specs=out_specs, out_shape=out_shape,
        compiler_params=_params(1), name="out_proj",
    )(*args)


def _moe_kernel(h_ref, n2_ref, gates_ref, wgu_ref, wd_ref, y_ref):
    e = pl.program_id(1)

    @pl.when(e == 0)
    def _():
        y_ref[...] = h_ref[...]

    n2 = n2_ref[...]
    gu = _dot(n2, wgu_ref[0].astype(BF16))
    act = _silu(gu[:, :D_EXPERT]) * gu[:, D_EXPERT:]
    lane = lax.broadcasted_iota(jnp.int32, gates_ref.shape, 1)
    gate = jnp.sum(jnp.where(lane == e, gates_ref[...], 0.0), axis=-1, keepdims=True)
    y_ref[...] += gate * _dot(act.astype(BF16), wd_ref[0].astype(BF16))


def _moe(h, n2, gates, wgu, wd, *, tm):
    m, d = h.shape
    row = lambda i, e: (i, 0)
    return pl.pallas_call(
        _moe_kernel,
        grid=(m // tm, N_EXPERTS),
        in_specs=[pl.BlockSpec((tm, d), row),
                  pl.BlockSpec((tm, d), row),
                  pl.BlockSpec((tm, LANES), row),
                  pl.BlockSpec((1, d, 2 * D_EXPERT), lambda i, e: (e, 0, 0)),
                  pl.BlockSpec((1, D_EXPERT, d), lambda i, e: (e, 0, 0))],
        out_specs=pl.BlockSpec((tm, d), row),
        out_shape=jax.ShapeDtypeStruct((m, d), F32),
        compiler_params=_params(2), name="moe",
    )(h, n2, gates, wgu, wd)


SC_CORES = 2
SC_SUBCORES = 16
SC_WORKERS = SC_CORES * SC_SUBCORES
SC_CHUNK = 64
MOE_TILE = 1024


def _sc_mesh():
    return plsc.VectorSubcoreMesh(core_axis_name="c", subcore_axis_name="s",
                                  num_cores=SC_CORES, num_subcores=SC_SUBCORES)


def _sc_dispatch(src, pos, n_rows):
    m, d = src.shape
    n_chunks = m // (SC_WORKERS * SC_CHUNK)
    pos4 = pos.reshape(2, SC_WORKERS, n_chunks, SC_CHUNK).transpose(1, 0, 2, 3)

    def body(src_hbm, pos_hbm, out_hbm, idx_v, rows_v):
        wid = lax.axis_index("s") * SC_CORES + lax.axis_index("c")
        pltpu.sync_copy(pos_hbm.at[wid], idx_v)

        @pl.loop(0, n_chunks)
        def _(j):
            start = pl.multiple_of((wid * n_chunks + j) * SC_CHUNK, SC_CHUNK)
            pltpu.sync_copy(src_hbm.at[pl.ds(start, SC_CHUNK)], rows_v)
            pltpu.sync_copy(rows_v, out_hbm.at[idx_v.at[0, j]])
            pltpu.sync_copy(rows_v, out_hbm.at[idx_v.at[1, j]])

    return pl.kernel(
        body, out_type=jax.ShapeDtypeStruct((n_rows, d), src.dtype), mesh=_sc_mesh(),
        scratch_types=[pltpu.VMEM((2, n_chunks, SC_CHUNK), jnp.int32), pltpu.VMEM((SC_CHUNK, d), src.dtype)],
        name="moe_dispatch",
    )(src, pos4)


def _sc_gather(table, idx):
    b = idx.shape[0]
    d = table.shape[1]
    n_chunks = b // (SC_WORKERS * SC_CHUNK)
    idx3 = idx.reshape(SC_WORKERS, n_chunks, SC_CHUNK)

    def body(table_hbm, idx_hbm, out_hbm, idx_v, rows_v):
        wid = lax.axis_index("s") * SC_CORES + lax.axis_index("c")
        pltpu.sync_copy(idx_hbm.at[wid], idx_v)

        @pl.loop(0, n_chunks)
        def _(j):
            start = pl.multiple_of((wid * n_chunks + j) * SC_CHUNK, SC_CHUNK)
            pltpu.sync_copy(table_hbm.at[idx_v.at[j]], rows_v)
            pltpu.sync_copy(rows_v, out_hbm.at[pl.ds(start, SC_CHUNK)])

    return pl.kernel(
        body, out_type=jax.ShapeDtypeStruct((b, d), table.dtype), mesh=_sc_mesh(),
        scratch_types=[pltpu.VMEM((n_chunks, SC_CHUNK), jnp.int32), pltpu.VMEM((SC_CHUNK, d), table.dtype)],
        name="moe_collect",
    )(table, idx3)


def _moe_grouped_kernel(te_ref, tv_ref, xs_ref, wgu_ref, wd_ref, ys_ref, wgu16, wd16):
    i = pl.program_id(0)
    valid = tv_ref[i]

    @pl.when((i == 0) | (te_ref[i] != te_ref[jnp.maximum(i - 1, 0)]))
    def _():
        wgu16[...] = wgu_ref[0].astype(BF16)
        wd16[...] = wd_ref[0].astype(BF16)

    @pl.when(valid > 0)
    def _():
        rows = lax.broadcasted_iota(jnp.int32, xs_ref.shape, 0)
        xw = jnp.where(rows < valid, xs_ref[...], 0)
        x = _unpack_bf16_pairs(xw).astype(BF16)
        gu = _dot(x, wgu16[...])
        act = _silu(gu[:, :D_EXPERT]) * gu[:, D_EXPERT:]
        ys_ref[...] = _pack_bf16_pairs(_dot(act.astype(BF16), wd16[...]))

    @pl.when(valid == 0)
    def _():
        ys_ref[...] = jnp.zeros(ys_ref.shape, ys_ref.dtype)


def _moe_grouped(xs, tile_expert, tile_valid, wgu, wd):
    r, half = xs.shape
    d = 2 * half
    grid_spec = pltpu.PrefetchScalarGridSpec(
        num_scalar_prefetch=2, grid=(r // MOE_TILE,),
        in_specs=[pl.BlockSpec((MOE_TILE, half), lambda i, te, nu: (i, 0)),
                  pl.BlockSpec((1, d, 2 * D_EXPERT), lambda i, te, nu: (te[i], 0, 0)),
                  pl.BlockSpec((1, D_EXPERT, d), lambda i, te, nu: (te[i], 0, 0))],
        out_specs=pl.BlockSpec((MOE_TILE, half), lambda i, te, nu: (i, 0)),
        scratch_shapes=[pltpu.VMEM((d, 2 * D_EXPERT), BF16), pltpu.VMEM((D_EXPERT, d), BF16)])
    return pl.pallas_call(
        _moe_grouped_kernel, grid_spec=grid_spec,
        out_shape=jax.ShapeDtypeStruct((r, half), jnp.int32),
        compiler_params=_params(1), name="moe_grouped",
    )(tile_expert, tile_valid, xs, wgu, wd)


def _moe_combine_kernel(h_ref, z0_ref, z1_ref, rec_ref, y_ref):
    tm = h_ref.shape[0]
    rec = jnp.concatenate([rec_ref[...], jnp.zeros((LANES - SUBLANES, tm), F32)], axis=0).T
    y_ref[...] = (h_ref[...] + rec[:, 2:3] * _unpack_bf16_pairs(z0_ref[...])
                  + rec[:, 3:4] * _unpack_bf16_pairs(z1_ref[...]))


def _moe_combine(h, z, rec, *, tm):
    m, d = h.shape
    nt = m // tm
    row = lambda i: (i, 0)
    return pl.pallas_call(
        _moe_combine_kernel, grid=(nt,),
        in_specs=[pl.BlockSpec((tm, d), row),
                  pl.BlockSpec((tm, d // 2), row),
                  pl.BlockSpec((tm, d // 2), lambda i: (i + nt, 0)),
                  pl.BlockSpec((SUBLANES, tm), lambda i: (0, i))],
        out_specs=pl.BlockSpec((tm, d), row),
        out_shape=jax.ShapeDtypeStruct((m, d), F32),
        compiler_params=_params(1), name="moe_combine",
    )(h, z, z, rec)


def _moe_route(rec, cnt):
    m = rec.shape[1]
    n_tiles = 2 * m // MOE_TILE + N_EXPERTS
    counts = cnt[:, 0].astype(jnp.int32)
    tiles = (counts + MOE_TILE - 1) // MOE_TILE
    tile_end = jnp.cumsum(tiles)
    row_start = (tile_end - tiles) * MOE_TILE
    eid = rec[0:2].astype(jnp.int32)
    rank = rec[4:6].astype(jnp.int32)
    pos = rank
    for e in range(N_EXPERTS):
        pos = pos + jnp.where(eid == e, row_start[e], 0)
    tile = jnp.arange(n_tiles, dtype=jnp.int32)
    tile_expert = jnp.sum((tile_end[None, :] <= jnp.minimum(tile, tile_end[-1] - 1)[:, None]).astype(jnp.int32),
                          axis=1)
    row_end = jnp.sum(jnp.where(tile_expert[:, None] == jnp.arange(N_EXPERTS)[None, :],
                                (row_start + counts)[None, :], 0), axis=1)
    tile_valid = jnp.clip(row_end - tile * MOE_TILE, 0, MOE_TILE).astype(jnp.int32)
    return pos, tile_expert, tile_valid, n_tiles * MOE_TILE


def _head_rows(v):
    rows = jnp.concatenate([jnp.zeros((B_HEADS,), F32), v.astype(F32)])
    return jnp.broadcast_to(rows[:, None], (2 * B_HEADS, LANES))


def kernel(x_prompt, x_sample, cache_win_k, cache_win_v, state_conv, state_ssm, norm1_w, w_in, qnorm_w, knorm_w, conv_w, a_log, dt_bias, onorm_w, w_out, norm2_w, w_group, b_group, w_expert_router, b_expert_router, w_gate_up, w_down):
    nb, seq, d = x_prompt.shape
    db, t_len, _ = x_sample.shape
    n_buf = cache_win_k.shape[1]
    assert n_buf == MAX_WINDOW and seq % (MAX_DILATION * BAND) == 0 and t_len == SUBLANES
    keep = min(MAX_WINDOW, seq)

    c_bg = 3 * A_WIDTH + CONV_DIM + B_WIDTH
    w_cat = jnp.concatenate([w_in, jnp.zeros((d, LANES - 2 * B_HEADS), w_in.dtype)], axis=1).astype(BF16)
    assert w_cat.shape[1] == c_bg + LANES
    n1w = norm1_w.reshape(1, d).astype(F32)
    qw = jnp.tile(qnorm_w.astype(F32), A_HEADS).reshape(1, A_WIDTH)
    kw = jnp.tile(knorm_w.astype(F32), A_HEADS).reshape(1, A_WIDTH)
    a_log_v = _head_rows(a_log)
    dtb_v = _head_rows(dt_bias)
    onw = onorm_w.reshape(1, B_VAL_DIM).astype(F32)
    wa = w_out[:A_WIDTH].astype(BF16)
    wb = w_out[A_WIDTH:].astype(BF16)
    n2w = norm2_w.reshape(1, d).astype(F32)
    wr = jnp.concatenate([w_group, jnp.transpose(w_expert_router, (1, 0, 2)).reshape(d, N_EXPERTS),
                          jnp.zeros((d, LANES - N_GROUPS - N_EXPERTS), F32)], axis=1).astype(F32)
    br = jnp.zeros((1, LANES), F32).at[0, :N_GROUPS].set(b_group).at[0, N_GROUPS:N_GROUPS + N_EXPERTS].set(
        b_expert_router.reshape(-1))
    wgu = w_gate_up.astype(F32)
    wd = w_down.astype(F32)
    cw = conv_w.astype(F32)

    xp2d = x_prompt.reshape(nb * seq, d)
    qf, kf, vf, kwin, vwin, bqkv, z, bg = _in_proj(xp2d, n1w, w_cat, qw, kw, seq=seq, keep=keep,
                                                   fold=True, tm=PROJ_TILE)
    oa = _attn_prompt(qf, kf, vf).reshape(nb * seq, A_WIDTH)
    conv0 = jnp.zeros((nb, CONV_WIDTH - 1, CONV_DIM), F32)
    ssm0 = jnp.zeros((nb, B_HEADS, B_KEY_DIM, B_VAL_DIM), F32)
    ob, ssm_prompt = _gdn_prompt(bqkv, z, bg, cw, conv0, ssm0, a_log_v, dtb_v, onw,
                                 seq=seq, ts=GDN_TILE, block=GDN_BLOCK, chunk=CHUNK)
    assert (nb * seq) % (SC_WORKERS * SC_CHUNK) == 0 and nb * seq >= N_EXPERTS * MOE_TILE
    h_p, n2p, rec, cnt = _out_proj(xp2d, oa, ob, wa, wb, n2w, wr, br, tm=PROJ_TILE, sparse=True)
    pos, tile_expert, tile_valid, n_rows = _moe_route(rec, cnt)
    xs = _sc_dispatch(n2p, pos, n_rows)
    win_k_prompt = jnp.transpose(kwin, (0, 3, 1, 2))
    win_v_prompt = jnp.transpose(vwin, (0, 3, 1, 2))
    conv_prompt = bqkv.reshape(nb, seq, CONV_DIM)[:, seq - (CONV_WIDTH - 1):]

    ms = db * t_len
    xs2d = x_sample.reshape(ms, d)
    tms = min(PROJ_TILE, ms)
    qs, ksn, vsn, _, _, bqkv_s, z_s, bg_s = _in_proj(xs2d, n1w, w_cat, qw, kw, seq=ms, keep=ms,
                                                     fold=False, tm=tms)
    cache_kt = jnp.transpose(cache_win_k.astype(F32), (0, 2, 3, 1))
    cache_vt = jnp.transpose(cache_win_v.astype(F32), (0, 2, 3, 1))
    oa_s = _attn_sample(qs, ksn, vsn, cache_kt, cache_vt, t_len=t_len)
    xs, oa_s = lax.optimization_barrier((xs, oa_s))
    ys = _moe_grouped(xs, tile_expert, tile_valid, wgu, wd)
    zs = _sc_gather(ys, pos.reshape(-1))
    xpad = jnp.concatenate([state_conv.astype(F32), bqkv_s.reshape(db, t_len, CONV_DIM)], axis=1)
    nseq = 16 if db % 16 == 0 else 1
    ob_s, ssm_sample = _gdn_sample(xpad, z_s, bg_s, cw, state_ssm.astype(F32), a_log_v, dtb_v, onw,
                                   t_len=t_len, nseq=nseq)
    h_s, n2_s, gates_s = _out_proj(xs2d, oa_s, ob_s, wa, wb, n2w, wr, br, tm=tms, sparse=False)
    y_sample = _moe(h_s, n2_s, gates_s, wgu, wd, tm=ms)
    zs, y_sample = lax.optimization_barrier((zs, y_sample))
    y_prompt = _moe_combine(h_p, zs, rec, tm=MOE_TILE).reshape(nb, seq, d)
    y_sample = y_sample.reshape(db, t_len, d)
    win_k_sample = ksn.reshape(db, t_len, A_HEADS, A_HEAD_DIM)
    win_v_sample = vsn.reshape(db, t_len, A_HEADS, A_HEAD_DIM)
    conv_sample = xpad[:, t_len:]

    return (y_prompt, y_sample, win_k_prompt, win_v_prompt, conv_prompt, ssm_prompt,
            win_k_sample, win_v_sample, conv_sample, ssm_sample)
```
